```python
import math
import jax, jax.numpy as jnp
from jax import lax
import numpy as np

D_MODEL = 1024
BATCH = 1
SEQ = 16384
DEPTH = 2

HEAD_DIM = 64
MOBA_HEADS = 8
DSA_HEADS = 8
N_HEADS = MOBA_HEADS + DSA_HEADS
MIX_WIDTH = N_HEADS * HEAD_DIM
MOBA_WIDTH = MOBA_HEADS * HEAD_DIM
DSA_WIDTH = DSA_HEADS * HEAD_DIM
MOBA_BLOCK = 256
MOBA_TOPK = 3
IDX_HEADS = 4
IDX_DIM = 64
DSA_TOPK_MAX = 256
N_BUCKETS = 32
MAX_DISTANCE = 4096
PEER_HEADS = 8
PEER_NKEYS = 128
PEER_EXPERTS = PEER_NKEYS * PEER_NKEYS
PEER_DKEY = 256
PEER_TOPK = 16
Q_CHUNK = 128
T_CHUNK = 128
EPS = 1e-6
PROJ_SIZES = (MOBA_WIDTH, MOBA_WIDTH, MOBA_WIDTH, DSA_WIDTH, DSA_WIDTH, DSA_WIDTH,
              IDX_HEADS * IDX_DIM, IDX_DIM, IDX_HEADS)
PROJ_WIDTH = 3 * MOBA_WIDTH + 3 * DSA_WIDTH + IDX_HEADS * IDX_DIM + IDX_DIM + IDX_HEADS

kernel_name = 'hybrid_moba_dsa_peer_adaln'


def rms_norm(x, g):
    xf = x.astype(jnp.float32)
    y = xf * lax.rsqrt(jnp.mean(xf * xf, axis=-1, keepdims=True) + EPS)
    return (y * g).astype(x.dtype)


def modulate(h, shift, scale):
    return h * (1 + scale[:, None, :]) + shift[:, None, :]


def t5_bucket(dist):
    max_exact = N_BUCKETS // 2
    d = jnp.maximum(dist, 0)
    df = jnp.maximum(d, 1).astype(jnp.float32)
    large = max_exact + (jnp.log(df / max_exact) / math.log(MAX_DISTANCE / max_exact)
                         * (N_BUCKETS - max_exact)).astype(jnp.int32)
    return jnp.where(d < max_exact, d, jnp.minimum(large, N_BUCKETS - 1))


def moba_attention(q, k, v, tab):
    B, H, S, Dh = q.shape
    nb = -(-S // MOBA_BLOCK)
    pad = nb * MOBA_BLOCK - S
    kp = jnp.pad(k, ((0, 0), (0, 0), (0, pad), (0, 0))).reshape(B, H, nb, MOBA_BLOCK, Dh)
    vp = jnp.pad(v, ((0, 0), (0, 0), (0, pad), (0, 0))).reshape(B, H, nb, MOBA_BLOCK, Dh)
    k_mean = jnp.mean(kp.astype(jnp.float32), axis=3)
    n_sel = max(min(MOBA_TOPK, nb - 1), 1)
    scale = HEAD_DIM ** -0.5
    bi = jnp.arange(B)[:, None, None, None]
    hi = jnp.arange(H)[None, :, None, None]
    blk = jnp.arange(MOBA_BLOCK)

    def one_chunk(ci):
        t0 = ci * Q_CHUNK
        t = t0 + jnp.arange(Q_CHUNK)
        b0 = t0 // MOBA_BLOCK
        qc = lax.dynamic_slice_in_dim(q, t0, Q_CHUNK, axis=2)
        gate = jnp.einsum('bhqd,bhnd->bhqn', qc.astype(jnp.float32), k_mean)
        gate = jnp.where(jnp.arange(nb) < b0, gate, -jnp.inf)
        _, sel = lax.top_k(gate, n_sel)
        sel_ok = sel < b0
        kg = kp[bi, hi, sel]
        vg = vp[bi, hi, sel]
        s_sel = jnp.einsum('bhqd,bhqjkd->bhqjk', qc, kg).astype(jnp.float32) * scale
        pos_sel = sel[..., None] * MOBA_BLOCK + blk
        s_sel = s_sel + tab[hi[..., None], t5_bucket(t[:, None, None] - pos_sel)]
        s_sel = jnp.where(sel_ok[..., None], s_sel, -jnp.inf).reshape(B, H, Q_CHUNK, n_sel * MOBA_BLOCK)
        ko = lax.dynamic_index_in_dim(kp, b0, axis=2, keepdims=False)
        vo = lax.dynamic_index_in_dim(vp, b0, axis=2, keepdims=False)
        pos_own = b0 * MOBA_BLOCK + blk
        s_own = jnp.einsum('bhqd,bhkd->bhqk', qc, ko).astype(jnp.float32) * scale
        s_own = s_own + jnp.take(tab, t5_bucket(t[:, None] - pos_own[None, :]), axis=1)[None]
        s_own = jnp.where(pos_own[None, :] <= t[:, None], s_own, -jnp.inf)
        p = jax.nn.softmax(jnp.concatenate([s_sel, s_own], axis=-1), axis=-1).astype(v.dtype)
        p_sel = p[..., :n_sel * MOBA_BLOCK].reshape(B, H, Q_CHUNK, n_sel, MOBA_BLOCK)
        p_own = p[..., n_sel * MOBA_BLOCK:]
        return (jnp.einsum('bhqjk,bhqjkd->bhqd', p_sel, vg)
                + jnp.einsum('bhqk,bhkd->bhqd', p_own, vo))

    out = lax.map(one_chunk, jnp.arange(S // Q_CHUNK))
    return jnp.transpose(out, (1, 2, 0, 3, 4)).reshape(B, H, S, Dh)


def dsa_attention(q, k, v, q_idx, k_idx, w_idx, tab):
    B, S, H, Dh = q.shape
    topk = min(DSA_TOPK_MAX, S // 4)
    scale = HEAD_DIM ** -0.5
    keys = jnp.arange(S)
    bi = jnp.arange(B)[:, None, None]

    def one_chunk(ci):
        t0 = ci * Q_CHUNK
        t = t0 + jnp.arange(Q_CHUNK)
        qi = lax.dynamic_slice_in_dim(q_idx, t0, Q_CHUNK, axis=1)
        wi = lax.dynamic_slice_in_dim(w_idx, t0, Q_CHUNK, axis=1)
        qc = lax.dynamic_slice_in_dim(q, t0, Q_CHUNK, axis=1)
        rel = jax.nn.relu(jnp.einsum('bqhd,bsd->bqhs', qi, k_idx).astype(jnp.float32))
        score = jnp.einsum('bqh,bqhs->bqs', wi.astype(jnp.float32), rel)
        score = jnp.where(keys[None, None, :] <= t[None, :, None], score, -jnp.inf)
        _, idx = lax.top_k(score, topk)
        kg = k[bi, idx]
        vg = v[bi, idx]
        s = jnp.einsum('bqhd,bqkhd->bhqk', qc, kg).astype(jnp.float32) * scale
        bias = jnp.moveaxis(jnp.take(tab, t5_bucket(t[None, :, None] - idx), axis=1), 0, 1)
        s = jnp.where((idx <= t[None, :, None])[:, None], s + bias, -jnp.inf)
        p = jax.nn.softmax(s, axis=-1).astype(v.dtype)
        return jnp.einsum('bhqk,bqkhd->bqhd', p, vg)

    out = lax.map(one_chunk, jnp.arange(S // Q_CHUNK))
    return jnp.transpose(out, (1, 0, 2, 3, 4)).reshape(B, S, H * Dh)


def hybrid_mixer(h, w_in, w_out, rel_bias):
    B, S, _ = h.shape
    proj = h @ w_in
    parts = []
    off = 0
    for size in PROJ_SIZES:
        parts.append(proj[..., off:off + size])
        off += size
    mq, mk, mv, dq, dk, dv, iq, ik, iw = parts
    to_bhsd = lambda t, nh: jnp.transpose(t.reshape(B, S, nh, HEAD_DIM), (0, 2, 1, 3))
    to_bshd = lambda t, nh: t.reshape(B, S, nh, HEAD_DIM)
    y_moba = moba_attention(to_bhsd(mq, MOBA_HEADS), to_bhsd(mk, MOBA_HEADS),
                            to_bhsd(mv, MOBA_HEADS), rel_bias[:MOBA_HEADS])
    y_moba = jnp.transpose(y_moba, (0, 2, 1, 3)).reshape(B, S, MOBA_WIDTH)
    ikf = ik.astype(jnp.float32)
    ik_n = (ikf * lax.rsqrt(jnp.mean(ikf * ikf, axis=-1, keepdims=True) + EPS)).astype(ik.dtype)
    y_dsa = dsa_attention(to_bshd(dq, DSA_HEADS), to_bshd(dk, DSA_HEADS), to_bshd(dv, DSA_HEADS),
                          iq.reshape(B, S, IDX_HEADS, IDX_DIM), ik_n, iw * IDX_HEADS ** -0.5,
                          rel_bias[MOBA_HEADS:])
    y = jnp.concatenate([y_moba, y_dsa], axis=-1)
    return y @ w_out


def peer_ffn(h, w_query, sub_keys, u_emb, v_emb):
    B, S, D = h.shape

    def one_chunk(ci):
        hc = lax.dynamic_slice_in_dim(h, ci * T_CHUNK, T_CHUNK, axis=1)
        q = (hc @ w_query).reshape(B, T_CHUNK, PEER_HEADS, 2, PEER_DKEY // 2)
        s = jnp.einsum('bthpd,hpnd->bthpn', q, sub_keys).astype(jnp.float32)
        s_top, i_top = lax.top_k(s, PEER_TOPK)
        cand = (s_top[..., 0, :, None] + s_top[..., 1, None, :]).reshape(B, T_CHUNK, PEER_HEADS, PEER_TOPK * PEER_TOPK)
        cand_idx = (i_top[..., 0, :, None] * PEER_NKEYS + i_top[..., 1, None, :]).reshape(cand.shape)
        g_s, pos = lax.top_k(cand, PEER_TOPK)
        e_idx = jnp.take_along_axis(cand_idx, pos, axis=-1)
        g = jax.nn.softmax(g_s, axis=-1)
        u = u_emb[e_idx]
        act = jax.nn.gelu(jnp.einsum('btd,bthkd->bthk', hc, u).astype(jnp.float32), approximate=False)
        return jnp.einsum('bthk,bthkd->btd', (g * act).astype(h.dtype), v_emb[e_idx])

    out = lax.map(one_chunk, jnp.arange(S // T_CHUNK))
    return jnp.transpose(out, (1, 0, 2, 3)).reshape(B, S, D)


def setup_inputs(seed: int = 0) -> dict:
    key = jax.random.key(seed)
    ks = jax.random.split(key, 14)
    f32 = jnp.float32
    nrm = lambda k, shape, s: jax.random.normal(k, shape, f32) * s
    return {
        'x': nrm(ks[0], (BATCH, SEQ, D_MODEL), 1.0),
        'c': nrm(ks[1], (BATCH, D_MODEL), 1.0),
        'w_ada': nrm(ks[2], (DEPTH, D_MODEL, 6 * D_MODEL), 0.5 * D_MODEL ** -0.5),
        'b_ada': nrm(ks[3], (DEPTH, 6 * D_MODEL), 0.02),
        'norm_attn': 1.0 + nrm(ks[4], (DEPTH, D_MODEL), 0.02),
        'norm_ffn': 1.0 + nrm(ks[5], (DEPTH, D_MODEL), 0.02),
        'w_in': nrm(ks[6], (DEPTH, D_MODEL, PROJ_WIDTH), D_MODEL ** -0.5),
        'w_out': nrm(ks[7], (DEPTH, MIX_WIDTH, D_MODEL), MIX_WIDTH ** -0.5),
        'rel_bias': nrm(ks[8], (N_HEADS, N_BUCKETS), 0.5),
        'peer_wq': nrm(ks[9], (DEPTH, D_MODEL, PEER_HEADS * PEER_DKEY), D_MODEL ** -0.5),
        'peer_subkeys': nrm(ks[10], (DEPTH, PEER_HEADS, 2, PEER_NKEYS, PEER_DKEY // 2), (PEER_DKEY // 2) ** -0.5),
        'peer_u': nrm(ks[11], (DEPTH, PEER_EXPERTS, D_MODEL), D_MODEL ** -0.5),
        'peer_v': nrm(ks[12], (DEPTH, PEER_EXPERTS, D_MODEL), PEER_HEADS ** -0.5),
        'norm_final': 1.0 + nrm(ks[13], (D_MODEL,), 0.02),
    }


def reference(x, c, w_ada, b_ada, norm_attn, norm_ffn, w_in, w_out, rel_bias,
              peer_wq, peer_subkeys, peer_u, peer_v, norm_final):
    mod = jnp.einsum('bd,lde->lbe', jax.nn.silu(c), w_ada) + b_ada[:, None, :]
    for l in range(DEPTH):
        sh1, sc1, g1, sh2, sc2, g2 = jnp.split(mod[l], 6, axis=-1)
        h = modulate(rms_norm(x, norm_attn[l]), sh1, sc1)
        x = x + g1[:, None, :] * hybrid_mixer(h, w_in[l], w_out[l], rel_bias)
        h = modulate(rms_norm(x, norm_ffn[l]), sh2, sc2)
        x = x + g2[:, None, :] * peer_ffn(h, peer_wq[l], peer_subkeys[l], peer_u[l], peer_v[l])
    return rms_norm(x, norm_final)
```

```python
import functools
import math

import jax
import jax.numpy as jnp
from jax import lax
from jax.experimental import pallas as pl
from jax.experimental.pallas import tpu as pltpu

F32 = jnp.float32
BF16 = jnp.bfloat16
I32 = jnp.int32

HEAD_DIM = 64
MOBA_HEADS = 8
DSA_HEADS = 8
N_HEADS = MOBA_HEADS + DSA_HEADS
MOBA_WIDTH = MOBA_HEADS * HEAD_DIM
DSA_WIDTH = DSA_HEADS * HEAD_DIM
MOBA_BLOCK = 256
MOBA_TOPK = 3
IDX_HEADS = 4
IDX_DIM = 64
DSA_TOPK_MAX = 256
N_BUCKETS = 32
MAX_DISTANCE = 4096
PEER_HEADS = 8
PEER_NKEYS = 128
PEER_DKEY = 256
PEER_TOPK = 16
EPS = 1e-6
PROJ_SIZES = (MOBA_WIDTH, MOBA_WIDTH, MOBA_WIDTH, DSA_WIDTH, DSA_WIDTH, DSA_WIDTH,
              IDX_HEADS * IDX_DIM, IDX_DIM, IDX_HEADS)
PROJ_WIDTH = sum(PROJ_SIZES)

NEG = -1e30
ATT_TILE = MOBA_BLOCK
LANES = 128
VMEM_LIMIT = 56 * 1024 * 1024

_NT = (((1,), (1,)), ((), ()))


def _far_bias_tiles(tile):
    max_exact = N_BUCKETS // 2
    sat = int(math.ceil(max_exact * (MAX_DISTANCE / max_exact) ** ((N_BUCKETS - max_exact - 1) / (N_BUCKETS - max_exact)))) + 4
    return -(-(sat + tile - 1) // tile)


def _cparams(sem):
    return pltpu.CompilerParams(dimension_semantics=sem, vmem_limit_bytes=VMEM_LIMIT)


def _ada_kernel(c_ref, w_ref, b_ref, o_ref):
    c = c_ref[...]
    sc = c * jax.nn.sigmoid(c)
    o_ref[0] = jnp.dot(sc, w_ref[0], preferred_element_type=F32,
                       precision=lax.Precision.HIGHEST) + b_ref[0]


def _ada_mod(c, w_ada, b_ada):
    depth, d, n = w_ada.shape
    tn = 1536
    c8 = jnp.broadcast_to(c, (8, d))
    out = pl.pallas_call(
        _ada_kernel,
        grid=(depth, n // tn),
        in_specs=[pl.BlockSpec((8, d), lambda l, j: (0, 0)),
                  pl.BlockSpec((1, d, tn), lambda l, j: (l, 0, j)),
                  pl.BlockSpec((1, 1, tn), lambda l, j: (l, 0, j))],
        out_specs=pl.BlockSpec((1, 8, tn), lambda l, j: (l, 0, j)),
        out_shape=jax.ShapeDtypeStruct((depth, 8, n), F32),
        compiler_params=_cparams(("arbitrary", "arbitrary")),
    )(c8, w_ada, b_ada.reshape(depth, 1, n))
    return out[:, 0, :]


def _nmm_kernel(x_ref, g_ref, sh_ref, sc_ref, w_ref, h_ref, o_ref):
    x = x_ref[...]
    y = x * lax.rsqrt(jnp.mean(x * x, axis=-1, keepdims=True) + EPS) * g_ref[...]
    hb = (y * (1.0 + sc_ref[...]) + sh_ref[...]).astype(BF16)
    h_ref[...] = hb
    o_ref[...] = jnp.dot(hb, w_ref[...], preferred_element_type=F32)


def _norm_mod_matmul(x, g, shift, scale, w_bf16, tm=256):
    s, d = x.shape
    n = w_bf16.shape[1]
    row = lambda i: (0, 0)
    return pl.pallas_call(
        _nmm_kernel,
        grid=(s // tm,),
        in_specs=[pl.BlockSpec((tm, d), lambda i: (i, 0)),
                  pl.BlockSpec((1, d), row), pl.BlockSpec((1, d), row), pl.BlockSpec((1, d), row),
                  pl.BlockSpec((d, n), row)],
        out_specs=[pl.BlockSpec((tm, d), lambda i: (i, 0)),
                   pl.BlockSpec((tm, n), lambda i: (i, 0))],
        out_shape=[jax.ShapeDtypeStruct((s, d), BF16), jax.ShapeDtypeStruct((s, n), F32)],
        compiler_params=_cparams(("arbitrary",)),
    )(x, g.reshape(1, d), shift.reshape(1, d), scale.reshape(1, d), w_bf16)


def _t5_bucket(d):
    max_exact = N_BUCKETS // 2
    d = jnp.maximum(d, 0)
    df = jnp.maximum(d, 1).astype(F32)
    large = max_exact + (jnp.log(df / max_exact) / math.log(MAX_DISTANCE / max_exact)
                         * (N_BUCKETS - max_exact)).astype(I32)
    return jnp.where(d < max_exact, d, jnp.minimum(large, N_BUCKETS - 1))


def _bias_tile_kernel(tab_ref, o_ref, *, tile, n_off):
    h = pl.program_id(0)
    off = pl.program_id(1)
    a = lax.broadcasted_iota(I32, (tile, tile), 0)
    b = lax.broadcasted_iota(I32, (tile, tile), 1)
    d = off * tile + a - b
    bucket = jnp.where(off == n_off - 1, N_BUCKETS - 1, _t5_bucket(d))
    val = jnp.zeros((tile, tile), F32)
    for k in range(N_BUCKETS):
        val = jnp.where(bucket == k, tab_ref[h, k], val)
    o_ref[0, 0] = jnp.where(d < 0, NEG, val)


def _bias_tiles(rel_bias, tile):
    n_off = _far_bias_tiles(tile) + 1
    nh = rel_bias.shape[0]
    return pl.pallas_call(
        functools.partial(_bias_tile_kernel, tile=tile, n_off=n_off),
        grid=(nh, n_off),
        in_specs=[pl.BlockSpec(memory_space=pltpu.SMEM)],
        out_specs=pl.BlockSpec((1, 1, tile, tile), lambda h, o: (h, o, 0, 0)),
        out_shape=jax.ShapeDtypeStruct((nh, n_off, tile, tile), F32),
        compiler_params=_cparams(("arbitrary", "arbitrary")),
    )(rel_bias)


def _kmean_kernel(k_ref, o_ref, *, nb):
    k = k_ref[0]
    km = jnp.mean(k.reshape(nb, MOBA_BLOCK, HEAD_DIM), axis=1)
    o_ref[0] = jnp.zeros(o_ref.shape[1:], F32)
    o_ref[0, :nb, :] = km


def _moba_kmean(k_hsd):
    nh, s, dh = k_hsd.shape
    nb = s // MOBA_BLOCK
    return pl.pallas_call(
        functools.partial(_kmean_kernel, nb=nb),
        grid=(nh,),
        in_specs=[pl.BlockSpec((1, s, dh), lambda h: (h, 0, 0))],
        out_specs=pl.BlockSpec((1, LANES, dh), lambda h: (h, 0, 0)),
        out_shape=jax.ShapeDtypeStruct((nh, LANES, dh), F32),
        compiler_params=_cparams(("arbitrary",)),
    )(k_hsd)


def _moba_sel_kernel(q_ref, km_ref, o_ref, *, tq, n_sel):
    q = q_ref[0]
    gate = lax.dot_general(q, km_ref[0], _NT, preferred_element_type=F32,
                           precision=lax.Precision.HIGHEST)
    t = pl.program_id(1) * tq + lax.broadcasted_iota(I32, (tq, LANES), 0)
    b0 = t // MOBA_BLOCK
    n = lax.broadcasted_iota(I32, (tq, LANES), 1)
    avail = n < b0
    taken = n == b0
    for _ in range(n_sel):
        g = jnp.where(avail, gate, -jnp.inf)
        m = jnp.max(g, axis=1, keepdims=True)
        idx = jnp.min(jnp.where((g == m) & avail, n, 1 << 20), axis=1, keepdims=True)
        pick = n == idx
        taken = taken | pick
        avail = avail & jnp.logical_not(pick)
    o_ref[0] = jnp.where(taken, 0.0, NEG)[:, :HEAD_DIM].astype(BF16)


def _moba_select(q_hsd, kmean, tq=512):
    nh, s, dh = q_hsd.shape
    nb = s // MOBA_BLOCK
    assert nb <= HEAD_DIM, "block-selection mask rides in HEAD_DIM spare contraction columns"
    n_sel = max(min(MOBA_TOPK, nb - 1), 1)
    tq = min(tq, s)
    return pl.pallas_call(
        functools.partial(_moba_sel_kernel, tq=tq, n_sel=n_sel),
        grid=(nh, s // tq),
        in_specs=[pl.BlockSpec((1, tq, dh), lambda h, i: (h, i, 0)),
                  pl.BlockSpec((1, LANES, dh), lambda h, i: (h, 0, 0))],
        out_specs=pl.BlockSpec((1, tq, dh), lambda h, i: (h, i, 0)),
        out_shape=jax.ShapeDtypeStruct((nh, s, dh), BF16),
        compiler_params=_cparams(("arbitrary", "arbitrary")),
    )(q_hsd, kmean)


def _attn_kernel(qi_ref, kj_ref, *refs, nh, has_mask):
    if has_mask:
        q_ref, k_ref, v_ref, bt_ref, mb_ref, o_ref, m_scr, l_scr, acc_scr = refs
    else:
        q_ref, k_ref, v_ref, bt_ref, o_ref, m_scr, l_scr, acc_scr = refs
    step = pl.program_id(0)
    qi = qi_ref[step]
    kj = kj_ref[step]

    @pl.when(kj == 0)
    def _():
        m_scr[...] = jnp.full(m_scr.shape, NEG, F32)
        l_scr[...] = jnp.zeros(l_scr.shape, F32)
        acc_scr[...] = jnp.zeros(acc_scr.shape, F32)

    if has_mask:
        mb = mb_ref[0].astype(F32)
    for h in range(nh):
        s = lax.dot_general(q_ref[h], k_ref[h], _NT, preferred_element_type=F32) + bt_ref[h, 0]
        if has_mask:
            s = s + mb
        m_old = m_scr[h][:, :1]
        l_old = l_scr[h][:, :1]
        m_new = jnp.maximum(m_old, jnp.max(s, axis=1, keepdims=True))
        alpha = jnp.exp(m_old - m_new)
        p = jnp.exp(s - m_new)
        l_new = alpha * l_old + jnp.sum(p, axis=1, keepdims=True)
        acc_scr[h] = alpha * acc_scr[h] + jnp.dot(p.astype(BF16), v_ref[h], preferred_element_type=F32)
        m_scr[h] = jnp.broadcast_to(m_new, m_scr.shape[1:])
        l_scr[h] = jnp.broadcast_to(l_new, l_scr.shape[1:])

    @pl.when(kj == qi)
    def _():
        for h in range(nh):
            o_ref[h] = acc_scr[h] / l_scr[h][:, :1]


def _attention(q, k, v, bias_tiles, head_group, mask_bias=None):
    nh, s, dk = q.shape
    dv = v.shape[-1]
    t = ATT_TILE
    nq = s // t
    n_off = bias_tiles.shape[1]
    qi_list = [i for i in range(nq) for _ in range(i + 1)]
    kj_list = [j for i in range(nq) for j in range(i + 1)]
    qi_arr = jnp.asarray(qi_list, I32)
    kj_arr = jnp.asarray(kj_list, I32)
    in_specs = [pl.BlockSpec((nh, t, dk), lambda st, qi, kj: (0, qi[st], 0)),
                pl.BlockSpec((nh, t, dk), lambda st, qi, kj: (0, kj[st], 0)),
                pl.BlockSpec((nh, t, dv), lambda st, qi, kj: (0, kj[st], 0)),
                pl.BlockSpec((nh, 1, t, t),
                             lambda st, qi, kj: (head_group, jnp.minimum(qi[st] - kj[st], n_off - 1), 0, 0))]
    args = [q, k, v, bias_tiles]
    if mask_bias is not None:
        in_specs.append(pl.BlockSpec((1, t, t), lambda st, qi, kj: (kj[st], qi[st], 0)))
        args.append(mask_bias)
    grid_spec = pltpu.PrefetchScalarGridSpec(
        num_scalar_prefetch=2,
        grid=(len(qi_list),),
        in_specs=in_specs,
        out_specs=pl.BlockSpec((nh, t, dv), lambda st, qi, kj: (0, qi[st], 0)),
        scratch_shapes=[pltpu.VMEM((nh, t, LANES), F32), pltpu.VMEM((nh, t, LANES), F32),
                        pltpu.VMEM((nh, t, dv), F32)])
    return pl.pallas_call(
        functools.partial(_attn_kernel, nh=nh, has_mask=mask_bias is not None),
        grid_spec=grid_spec,
        out_shape=jax.ShapeDtypeStruct((nh, s, dv), F32),
        compiler_params=_cparams(("arbitrary",)),
    )(qi_arr, kj_arr, *args)


def _sortable(x):
    b = pltpu.bitcast(jnp.where(x == 0.0, 0.0, x), I32)
    return b ^ ((b >> 31) & 0x7FFFFFFF)


def _dsa_sel_kernel(qi_ref, w_ref, kn_ref, tri_ref, o_ref, keys_scr, *, tq, ch, topk):
    n_chunks = kn_ref.shape[0]
    t0 = pl.program_id(0) * tq
    n_live = (t0 + tq + ch - 1) // ch
    t = t0 + lax.broadcasted_iota(I32, (tq, ch), 0)
    col = lax.broadcasted_iota(I32, (tq, ch), 1)
    q_heads = [qi_ref[:, h * IDX_DIM:(h + 1) * IDX_DIM] for h in range(IDX_HEADS)]
    w = w_ref[...]
    w_heads = [w[:, h:h + 1] for h in range(IDX_HEADS)]

    def score_chunk(c, carry):
        kc = kn_ref[c]
        sc = jnp.zeros((tq, ch), F32)
        for h in range(IDX_HEADS):
            r = lax.dot_general(q_heads[h], kc, _NT, preferred_element_type=F32)
            sc = sc + w_heads[h] * jnp.maximum(r, 0.0)
        sc = jnp.where(c * ch + col <= t, sc, -jnp.inf)
        keys_scr[c] = _sortable(sc)
        return carry

    lax.fori_loop(0, n_live, score_chunk, 0)

    def count(pred_fn):
        def body(c, cnt):
            kc = keys_scr[c]
            for j in range(ch // LANES):
                cnt = cnt + pred_fn(kc[:, j * LANES:(j + 1) * LANES]).astype(I32)
            return cnt
        cnt = lax.fori_loop(0, n_live, body, jnp.zeros((tq, LANES), I32))
        return jnp.sum(cnt, axis=1, keepdims=True)

    def bit_step(i, ans):
        cand = ans + lax.shift_left(jnp.int32(1), 31 - i)
        total = count(lambda kc: kc >= cand)
        return jnp.where(total >= topk, cand, ans)

    thr = lax.fori_loop(0, 32, bit_step, jnp.full((tq, LANES), -2 ** 31, I32))
    n_gt = count(lambda kc: kc > thr)
    need = (topk - n_gt).astype(F32)
    thr_c = jnp.broadcast_to(thr[:, :1], (tq, ch))

    def emit_chunk(c, seen):
        kc = keys_scr[c]
        eq = kc == thr_c
        eqf = eq.astype(F32)
        rank = seen + jnp.dot(eqf.astype(BF16), tri_ref[...], preferred_element_type=F32)
        sel = ((kc > thr_c) | (eq & (rank < need))) & (c * ch + col <= t)
        o_ref[c] = jnp.where(sel, 0.0, NEG).astype(BF16)
        return seen + jnp.sum(eqf, axis=1, keepdims=True)

    lax.fori_loop(0, n_live, emit_chunk, jnp.zeros((tq, 1), F32))

    def fill_chunk(c, carry):
        o_ref[c] = jnp.full((tq, ch), NEG, BF16)
        return carry

    lax.fori_loop(n_live, n_chunks, fill_chunk, 0)


def _dsa_select(q_idx_bf16, w_idx, kn_bf16, topk, tq=128):
    s = q_idx_bf16.shape[0]
    ch = ATT_TILE
    n_chunks = s // ch
    kn3 = kn_bf16.reshape(n_chunks, ch, IDX_DIM)
    tri = (lax.broadcasted_iota(I32, (ch, ch), 0) < lax.broadcasted_iota(I32, (ch, ch), 1)).astype(BF16)
    return pl.pallas_call(
        functools.partial(_dsa_sel_kernel, tq=tq, ch=ch, topk=topk),
        grid=(s // tq,),
        in_specs=[pl.BlockSpec((tq, IDX_HEADS * IDX_DIM), lambda i: (i, 0)),
                  pl.BlockSpec((tq, IDX_HEADS), lambda i: (i, 0)),
                  pl.BlockSpec((n_chunks, ch, IDX_DIM), lambda i: (0, 0, 0)),
                  pl.BlockSpec((ch, ch), lambda i: (0, 0))],
        out_specs=pl.BlockSpec((n_chunks, tq, ch), lambda i: (0, i, 0)),
        out_shape=jax.ShapeDtypeStruct((n_chunks, s, ch), BF16),
        scratch_shapes=[pltpu.VMEM((n_chunks, tq, ch), I32)],
        compiler_params=_cparams(("arbitrary",)),
    )(q_idx_bf16, w_idx, kn3, tri)


def _out_proj_kernel(y_ref, w_ref, x_ref, g_ref, o_ref):
    o_ref[...] = x_ref[...] + g_ref[...] * jnp.dot(y_ref[...], w_ref[...], preferred_element_type=F32)


def _out_proj(y_bf16, w_bf16, x, gate, tm=512):
    s, d = x.shape
    k = y_bf16.shape[1]
    return pl.pallas_call(
        _out_proj_kernel,
        grid=(s // tm,),
        in_specs=[pl.BlockSpec((tm, k), lambda i: (i, 0)),
                  pl.BlockSpec((k, d), lambda i: (0, 0)),
                  pl.BlockSpec((tm, d), lambda i: (i, 0)),
                  pl.BlockSpec((1, d), lambda i: (0, 0))],
        out_specs=pl.BlockSpec((tm, d), lambda i: (i, 0)),
        out_shape=jax.ShapeDtypeStruct((s, d), F32),
        compiler_params=_cparams(("arbitrary",)),
    )(y_bf16, w_bf16, x, gate.reshape(1, d))


def _cmp_exchange(v, i, j):
    hi = jnp.maximum(v[i], v[j])
    lo = jnp.minimum(v[i], v[j])
    v[i], v[j] = hi, lo


def _bitonic_sort_desc(v):
    n = len(v)
    k = 2
    while k <= n:
        j = k // 2
        while j >= 1:
            for i in range(n):
                l = i ^ j
                if l > i:
                    if (i & k) == 0:
                        _cmp_exchange(v, i, l)
                    else:
                        _cmp_exchange(v, l, i)
            j //= 2
        k *= 2


def _bitonic_merge_desc(v):
    n = len(v)
    j = n // 2
    while j >= 1:
        for i in range(n):
            l = i ^ j
            if l > i:
                _cmp_exchange(v, i, l)
        j //= 2


def _top16_rows(x):
    tt = x.shape[1]
    v = [x[8 * i:8 * i + 8, :] for i in range(PEER_NKEYS // 8)]
    _bitonic_sort_desc(v)
    for shift in (4, 2, 1):
        other = [pltpu.roll(a, shift, 0) for a in v]
        v = [jnp.maximum(v[i], other[len(v) - 1 - i]) for i in range(len(v))]
        _bitonic_merge_desc(v)
    return v


def _peer_route_kernel(q_ref, sk_ref, s0_ref, s1_ref, e0_ref, e1_ref, tau_ref, *, tt):
    sub = lax.broadcasted_iota(I32, (8, tt), 0)
    tops = [[], []]
    for h in range(PEER_HEADS):
        for p in range(2):
            lo = (2 * h + p) * (PEER_DKEY // 2)
            qhp = q_ref[:, lo:lo + PEER_DKEY // 2].astype(BF16)
            sc = lax.dot_general(sk_ref[2 * h + p], qhp, _NT, preferred_element_type=F32)
            (s0_ref if p == 0 else s1_ref)[h] = sc
            tops[p].append(_top16_rows(sc))
    a = [sum(jnp.where(sub == h, tops[0][h][k], 0.0) for h in range(PEER_HEADS)) for k in range(PEER_TOPK)]
    b = [sum(jnp.where(sub == h, tops[1][h][k], 0.0) for h in range(PEER_HEADS)) for k in range(PEER_TOPK)]
    cand = [a[i] + b[j] for i in range(PEER_TOPK) for j in range(PEER_TOPK) if (i + 1) * (j + 1) <= PEER_TOPK]
    cand = cand + [jnp.full((8, tt), -jnp.inf, F32)] * (64 - len(cand))
    _bitonic_sort_desc(cand)
    tau = cand[PEER_TOPK - 1]
    z = sum(jnp.exp(cand[k] - cand[0]) for k in range(PEER_TOPK))
    tau_ref[...] = tau
    inv_z = 1.0 / z
    for h in range(PEER_HEADS):
        e0_ref[h] = jnp.exp(s0_ref[h] - a[0][h:h + 1, :]) * inv_z[h:h + 1, :]
        e1_ref[h] = jnp.exp(s1_ref[h] - b[0][h:h + 1, :])


def _peer_route(q, subkeys_bf16, tt=256):
    s = q.shape[0]
    big = jax.ShapeDtypeStruct((PEER_HEADS, PEER_NKEYS, s), F32)
    big_spec = pl.BlockSpec((PEER_HEADS, PEER_NKEYS, tt), lambda i: (0, 0, i))
    return pl.pallas_call(
        functools.partial(_peer_route_kernel, tt=tt),
        grid=(s // tt,),
        in_specs=[pl.BlockSpec((tt, PEER_HEADS * PEER_DKEY), lambda i: (i, 0)),
                  pl.BlockSpec((2 * PEER_HEADS, PEER_NKEYS, PEER_DKEY // 2), lambda i: (0, 0, 0))],
        out_specs=[big_spec, big_spec, big_spec, big_spec,
                   pl.BlockSpec((PEER_HEADS, tt), lambda i: (0, i))],
        out_shape=[big, big, big, big, jax.ShapeDtypeStruct((PEER_HEADS, s), F32)],
        compiler_params=_cparams(("arbitrary",)),
    )(q, subkeys_bf16)


def _peer_dense_kernel(h_ref, u_ref, vt_ref, s0_ref, s1_ref, e0_ref, e1_ref, tau_ref,
                       x_ref, g_ref, gf_ref, o_ref, acc_scr, *, rows_per_step, final_norm):
    k = pl.program_id(1)

    @pl.when(k == 0)
    def _():
        acc_scr[...] = jnp.zeros(acc_scr.shape, F32)

    act = lax.dot_general(u_ref[...], h_ref[...], _NT, preferred_element_type=F32)
    tt = act.shape[1]
    parts = []
    for ii in range(rows_per_step):
        i = k * rows_per_step + ii
        wgt = jnp.zeros((PEER_NKEYS, tt), F32)
        for h in range(PEER_HEADS):
            ssum = s0_ref[h, pl.ds(i, 1), :] + s1_ref[h]
            prod = e0_ref[h, pl.ds(i, 1), :] * e1_ref[h]
            wgt = wgt + jnp.where(ssum >= tau_ref[h:h + 1, :], prod, 0.0)
        a = act[ii * PEER_NKEYS:(ii + 1) * PEER_NKEYS, :]
        gelu = 0.5 * a * (1.0 + lax.erf(a * math.sqrt(0.5)))
        parts.append((wgt * gelu).astype(BF16))
    p = jnp.concatenate(parts, axis=0)
    acc_scr[...] += jnp.dot(vt_ref[...], p, preferred_element_type=F32)

    @pl.when(k == pl.num_programs(1) - 1)
    def _():
        y = x_ref[...] + g_ref[...] * acc_scr[...].T
        if final_norm:
            y = y * lax.rsqrt(jnp.mean(y * y, axis=-1, keepdims=True) + EPS) * gf_ref[...]
        o_ref[...] = y


def _peer_dense(h_bf16, u_bf16, vt_bf16, s0, s1, e0, e1, tau, x, gate, g_final, final_norm,
                tt=512, rows_per_step=4):
    s, d = x.shape
    n_exp = u_bf16.shape[0]
    et = rows_per_step * PEER_NKEYS
    tok = lambda j, k: (0, 0, j)
    big_spec = pl.BlockSpec((PEER_HEADS, PEER_NKEYS, tt), tok)
    return pl.pallas_call(
        functools.partial(_peer_dense_kernel, rows_per_step=rows_per_step, final_norm=final_norm),
        grid=(s // tt, n_exp // et),
        in_specs=[pl.BlockSpec((tt, d), lambda j, k: (j, 0)),
                  pl.BlockSpec((et, d), lambda j, k: (k, 0)),
                  pl.BlockSpec((d, et), lambda j, k: (0, k)),
                  big_spec, big_spec, big_spec, big_spec,
                  pl.BlockSpec((PEER_HEADS, tt), lambda j, k: (0, j)),
                  pl.BlockSpec((tt, d), lambda j, k: (j, 0)),
                  pl.BlockSpec((1, d), lambda j, k: (0, 0)),
                  pl.BlockSpec((1, d), lambda j, k: (0, 0))],
        out_specs=pl.BlockSpec((tt, d), lambda j, k: (j, 0)),
        out_shape=jax.ShapeDtypeStruct((s, d), F32),
        scratch_shapes=[pltpu.VMEM((d, tt), F32)],
        compiler_params=_cparams(("arbitrary", "arbitrary")),
    )(h_bf16, u_bf16, vt_bf16, s0, s1, e0, e1, tau, x, gate.reshape(1, d), g_final.reshape(1, d))


def _heads_first(t, nh):
    s = t.shape[0]
    return jnp.transpose(t.reshape(s, nh, HEAD_DIM), (1, 0, 2))


def _mixer(h_proj, w_out_bf16, bias_tiles, x, gate):
    s = h_proj.shape[0]
    parts = []
    off = 0
    for size in PROJ_SIZES:
        parts.append(h_proj[:, off:off + size])
        off += size
    mq, mk, mv, dq, dk, dv, iq, ik, iw = parts
    scale = HEAD_DIM ** -0.5

    mq_h, mk_h, mv_h = (_heads_first(a, MOBA_HEADS) for a in (mq, mk, mv))
    selb = _moba_select(mq_h, _moba_kmean(mk_h))
    blk = jnp.arange(s, dtype=I32) // MOBA_BLOCK
    onehot = (blk[:, None] == jnp.arange(HEAD_DIM, dtype=I32)[None, :]).astype(BF16)
    q_aug = jnp.concatenate([(mq_h * scale).astype(BF16), selb], axis=-1)
    k_aug = jnp.concatenate([mk_h.astype(BF16), jnp.broadcast_to(onehot, (MOBA_HEADS, s, HEAD_DIM))], axis=-1)
    y_moba = _attention(q_aug, k_aug, mv_h.astype(BF16), bias_tiles, 0)

    ikf = ik.astype(F32)
    ik_n = ikf * lax.rsqrt(jnp.mean(ikf * ikf, axis=-1, keepdims=True) + EPS)
    topk = min(DSA_TOPK_MAX, s // 4)
    mask_bias = _dsa_select(iq.astype(BF16), iw * IDX_HEADS ** -0.5, ik_n.astype(BF16), topk)
    dq_h, dk_h, dv_h = (_heads_first(a, DSA_HEADS) for a in (dq, dk, dv))
    zeros = jnp.zeros((DSA_HEADS, s, HEAD_DIM), BF16)
    q_pad = jnp.concatenate([(dq_h * scale).astype(BF16), zeros], axis=-1)
    k_pad = jnp.concatenate([dk_h.astype(BF16), zeros], axis=-1)
    y_dsa = _attention(q_pad, k_pad, dv_h.astype(BF16), bias_tiles, 1, mask_bias)

    y = jnp.concatenate([jnp.transpose(y_moba, (1, 0, 2)).reshape(s, MOBA_WIDTH),
                         jnp.transpose(y_dsa, (1, 0, 2)).reshape(s, DSA_WIDTH)], axis=-1)
    return _out_proj(y.astype(BF16), w_out_bf16, x, gate)


def kernel(x, c, w_ada, b_ada, norm_attn, norm_ffn, w_in, w_out, rel_bias,
           peer_wq, peer_subkeys, peer_u, peer_v, norm_final):
    batch, s, d = x.shape
    assert batch == 1 and s % (2 * ATT_TILE) == 0
    depth = w_ada.shape[0]
    x = x[0]
    mod = _ada_mod(c, w_ada, b_ada)
    bias_tiles = _bias_tiles(rel_bias, ATT_TILE)
    n_pad = -(-PROJ_WIDTH // LANES) * LANES
    for l in range(depth):
        sh1, sc1, g1, sh2, sc2, g2 = jnp.split(mod[l], 6)
        w_in_l = jnp.pad(w_in[l], ((0, 0), (0, n_pad - PROJ_WIDTH))).astype(BF16)
        _, proj = _norm_mod_matmul(x, norm_attn[l], sh1, sc1, w_in_l)
        x = _mixer(proj, w_out[l].astype(BF16), bias_tiles, x, g1)
        h2, q = _norm_mod_matmul(x, norm_ffn[l], sh2, sc2, peer_wq[l].astype(BF16))
        subkeys = peer_subkeys[l].reshape(2 * PEER_HEADS, PEER_NKEYS, PEER_DKEY // 2).astype(BF16)
        s0, s1, e0, e1, tau = _peer_route(q, subkeys)
        x = _peer_dense(h2, peer_u[l].astype(BF16), peer_v[l].T.astype(BF16), s0, s1, e0, e1, tau,
                        x, g2, norm_final, final_norm=(l == depth - 1))
    return x[None]
```

```python
import functools
import math

import jax
import jax.numpy as jnp
from jax import lax
from jax.experimental import pallas as pl
from jax.experimental.pallas import tpu as pltpu

F32 = jnp.float32
BF16 = jnp.bfloat16
I32 = jnp.int32

HEAD_DIM = 64
MOBA_HEADS = 8
DSA_HEADS = 8
N_HEADS = MOBA_HEADS + DSA_HEADS
MOBA_WIDTH = MOBA_HEADS * HEAD_DIM
DSA_WIDTH = DSA_HEADS * HEAD_DIM
MOBA_BLOCK = 256
MOBA_TOPK = 3
IDX_HEADS = 4
IDX_DIM = 64
DSA_TOPK_MAX = 256
N_BUCKETS = 32
MAX_DISTANCE = 4096
PEER_HEADS = 8
PEER_NKEYS = 128
PEER_DKEY = 256
PEER_TOPK = 16
EPS = 1e-6
PROJ_SIZES = (MOBA_WIDTH, MOBA_WIDTH, MOBA_WIDTH, DSA_WIDTH, DSA_WIDTH, DSA_WIDTH,
              IDX_HEADS * IDX_DIM, IDX_DIM, IDX_HEADS)
PROJ_WIDTH = sum(PROJ_SIZES)

NEG = -1e30
LOG2E = math.log2(math.e)
ATT_TILE = MOBA_BLOCK
LANES = 128
MXU_DEPTH = 256
VMEM_LIMIT = 56 * 1024 * 1024

_NT = (((1,), (1,)), ((), ()))


def _far_bias_tiles(tile):
    max_exact = N_BUCKETS // 2
    sat = int(math.ceil(max_exact * (MAX_DISTANCE / max_exact) ** ((N_BUCKETS - max_exact - 1) / (N_BUCKETS - max_exact)))) + 4
    return -(-(sat + tile - 1) // tile)


def _cparams(sem):
    return pltpu.CompilerParams(dimension_semantics=sem, vmem_limit_bytes=VMEM_LIMIT)


def _ada_kernel(c_ref, w_ref, b_ref, o_ref):
    c = c_ref[...]
    sc = c * jax.nn.sigmoid(c)
    o_ref[0] = jnp.dot(sc, w_ref[0], preferred_element_type=F32,
                       precision=lax.Precision.HIGHEST) + b_ref[0]


def _ada_mod(c, w_ada, b_ada):
    depth, d, n = w_ada.shape
    tn = 1536
    c8 = jnp.broadcast_to(c, (8, d))
    out = pl.pallas_call(
        _ada_kernel,
        grid=(depth, n // tn),
        in_specs=[pl.BlockSpec((8, d), lambda l, j: (0, 0)),
                  pl.BlockSpec((1, d, tn), lambda l, j: (l, 0, j)),
                  pl.BlockSpec((1, 1, tn), lambda l, j: (l, 0, j))],
        out_specs=pl.BlockSpec((1, 8, tn), lambda l, j: (l, 0, j)),
        out_shape=jax.ShapeDtypeStruct((depth, 8, n), F32),
        compiler_params=_cparams(("arbitrary", "arbitrary")),
    )(c8, w_ada, b_ada.reshape(depth, 1, n))
    return out[:, 0, :]


def _nmm_kernel(x_ref, g_ref, sh_ref, sc_ref, w_ref, h_ref, o_ref):
    x = x_ref[...]
    y = x * lax.rsqrt(jnp.mean(x * x, axis=-1, keepdims=True) + EPS) * g_ref[...]
    hb = (y * (1.0 + sc_ref[...]) + sh_ref[...]).astype(BF16)
    h_ref[...] = hb
    o_ref[...] = jnp.dot(hb, w_ref[...], preferred_element_type=F32)


def _norm_mod_matmul(x, g, shift, scale, w_bf16, tm=256):
    s, d = x.shape
    n = w_bf16.shape[1]
    row = lambda i: (0, 0)
    return pl.pallas_call(
        _nmm_kernel,
        grid=(s // tm,),
        in_specs=[pl.BlockSpec((tm, d), lambda i: (i, 0)),
                  pl.BlockSpec((1, d), row), pl.BlockSpec((1, d), row), pl.BlockSpec((1, d), row),
                  pl.BlockSpec((d, n), row)],
        out_specs=[pl.BlockSpec((tm, d), lambda i: (i, 0)),
                   pl.BlockSpec((tm, n), lambda i: (i, 0))],
        out_shape=[jax.ShapeDtypeStruct((s, d), BF16), jax.ShapeDtypeStruct((s, n), F32)],
        compiler_params=_cparams(("arbitrary",)),
    )(x, g.reshape(1, d), shift.reshape(1, d), scale.reshape(1, d), w_bf16)


def _t5_bucket(d):
    max_exact = N_BUCKETS // 2
    d = jnp.maximum(d, 0)
    df = jnp.maximum(d, 1).astype(F32)
    large = max_exact + (jnp.log(df / max_exact) / math.log(MAX_DISTANCE / max_exact)
                         * (N_BUCKETS - max_exact)).astype(I32)
    return jnp.where(d < max_exact, d, jnp.minimum(large, N_BUCKETS - 1))


def _bias_tile_kernel(tab_ref, o_ref, *, tile):
    h = pl.program_id(0)
    off = pl.program_id(1)
    b = lax.broadcasted_iota(I32, (tile, tile), 0)
    a = lax.broadcasted_iota(I32, (tile, tile), 1)
    d = off * tile + a - b
    bucket = _t5_bucket(d)
    val = jnp.zeros((tile, tile), F32)
    for k in range(N_BUCKETS):
        val = jnp.where(bucket == k, tab_ref[h, k], val)
    val = (val - tab_ref[h, N_BUCKETS - 1]) * LOG2E
    o_ref[0, 0] = jnp.where(d < 0, NEG, val)


def _bias_tiles(rel_bias, tile):
    n_near = _far_bias_tiles(tile)
    nh = rel_bias.shape[0]
    return pl.pallas_call(
        functools.partial(_bias_tile_kernel, tile=tile),
        grid=(nh, n_near),
        in_specs=[pl.BlockSpec(memory_space=pltpu.SMEM)],
        out_specs=pl.BlockSpec((1, 1, tile, tile), lambda h, o: (h, o, 0, 0)),
        out_shape=jax.ShapeDtypeStruct((nh, n_near, tile, tile), F32),
        compiler_params=_cparams(("arbitrary", "arbitrary")),
    )(rel_bias)


def _kmean_kernel(k_ref, o_ref, *, nb):
    k = k_ref[0]
    km = jnp.mean(k.reshape(nb, MOBA_BLOCK, HEAD_DIM), axis=1)
    o_ref[0] = jnp.zeros(o_ref.shape[1:], F32)
    o_ref[0, :nb, :] = km


def _moba_kmean(k_hsd):
    nh, s, dh = k_hsd.shape
    nb = s // MOBA_BLOCK
    return pl.pallas_call(
        functools.partial(_kmean_kernel, nb=nb),
        grid=(nh,),
        in_specs=[pl.BlockSpec((1, s, dh), lambda h: (h, 0, 0))],
        out_specs=pl.BlockSpec((1, LANES, dh), lambda h: (h, 0, 0)),
        out_shape=jax.ShapeDtypeStruct((nh, LANES, dh), F32),
        compiler_params=_cparams(("arbitrary",)),
    )(k_hsd)


def _moba_sel_kernel(q_ref, km_ref, o_ref, *, tq, n_sel):
    q = q_ref[0]
    gate = lax.dot_general(q, km_ref[0], _NT, preferred_element_type=F32,
                           precision=lax.Precision.HIGHEST)
    t = pl.program_id(1) * tq + lax.broadcasted_iota(I32, (tq, LANES), 0)
    b0 = t // MOBA_BLOCK
    n = lax.broadcasted_iota(I32, (tq, LANES), 1)
    avail = n < b0
    taken = n == b0
    for _ in range(n_sel):
        g = jnp.where(avail, gate, -jnp.inf)
        m = jnp.max(g, axis=1, keepdims=True)
        idx = jnp.min(jnp.where((g == m) & avail, n, 1 << 20), axis=1, keepdims=True)
        pick = n == idx
        taken = taken | pick
        avail = avail & jnp.logical_not(pick)
    o_ref[0] = jnp.where(taken, 0.0, NEG)[:, :HEAD_DIM].astype(BF16)


def _moba_select(q_hsd, kmean, tq=512):
    nh, s, dh = q_hsd.shape
    nb = s // MOBA_BLOCK
    assert nb <= HEAD_DIM, "block-selection mask rides in HEAD_DIM spare contraction columns"
    n_sel = max(min(MOBA_TOPK, nb - 1), 1)
    tq = min(tq, s)
    return pl.pallas_call(
        functools.partial(_moba_sel_kernel, tq=tq, n_sel=n_sel),
        grid=(nh, s // tq),
        in_specs=[pl.BlockSpec((1, tq, dh), lambda h, i: (h, i, 0)),
                  pl.BlockSpec((1, LANES, dh), lambda h, i: (h, 0, 0))],
        out_specs=pl.BlockSpec((1, tq, dh), lambda h, i: (h, i, 0)),
        out_shape=jax.ShapeDtypeStruct((nh, s, dh), BF16),
        compiler_params=_cparams(("arbitrary", "arbitrary")),
    )(q_hsd, kmean)


def _attn_kernel(qi_ref, kj_ref, *refs, nh, n_near, has_mask):
    if has_mask:
        q_ref, k_ref, vt_ref, bt_ref, mb_ref, o_ref, m_scr, l_scr, acc_scr = refs
    else:
        q_ref, k_ref, vt_ref, bt_ref, o_ref, m_scr, l_scr, acc_scr = refs
    step = pl.program_id(0)
    qi = qi_ref[step]
    kj = kj_ref[step]
    off = qi - kj

    @pl.when(kj == 0)
    def _():
        m_scr[...] = jnp.full(m_scr.shape, NEG, F32)
        l_scr[...] = jnp.zeros(l_scr.shape, F32)
        acc_scr[...] = jnp.zeros(acc_scr.shape, F32)

    def tile_update(with_bias):
        if has_mask:
            mb = mb_ref[0, 0].astype(F32)
        scores = []
        for h in range(nh):
            s = lax.dot_general(k_ref[h], q_ref[h], _NT, preferred_element_type=F32)
            if with_bias:
                s = s + bt_ref[h, 0]
            if has_mask:
                s = s + mb
            scores.append(s)
        probs, alphas = [], []
        for h in range(nh):
            m_old = m_scr[h]
            m_new = jnp.maximum(m_old, jnp.max(scores[h], axis=0, keepdims=True))
            alpha = jnp.exp2(m_old - m_new)
            p = jnp.exp2(scores[h] - m_new)
            l_scr[h] = alpha * l_scr[h] + jnp.sum(p, axis=0, keepdims=True)
            m_scr[h] = m_new
            probs.append(p.astype(BF16))
            alphas.append(alpha)
        for h in range(nh):
            acc_scr[h] = alphas[h] * acc_scr[h] + jnp.dot(vt_ref[h], probs[h], preferred_element_type=F32)

    pl.when(off < n_near)(functools.partial(tile_update, True))
    pl.when(off >= n_near)(functools.partial(tile_update, False))

    @pl.when(kj == qi)
    def _():
        for h in range(nh):
            o_ref[h] = acc_scr[h] / l_scr[h]


def _attention(q, k, vt, bias_tiles, head_group, mask_bias=None):
    nh, s, dk = q.shape
    dv = vt.shape[1]
    t = ATT_TILE
    nq = s // t
    n_near = bias_tiles.shape[1]
    qi_list = [i for i in range(nq) for _ in range(i + 1)]
    kj_list = [j for i in range(nq) for j in range(i + 1)]
    qi_arr = jnp.asarray(qi_list, I32)
    kj_arr = jnp.asarray(kj_list, I32)
    in_specs = [pl.BlockSpec((nh, t, dk), lambda st, qi, kj: (0, qi[st], 0)),
                pl.BlockSpec((nh, t, dk), lambda st, qi, kj: (0, kj[st], 0)),
                pl.BlockSpec((nh, dv, t), lambda st, qi, kj: (0, 0, kj[st])),
                pl.BlockSpec((nh, 1, t, t),
                             lambda st, qi, kj: (head_group, jnp.minimum(qi[st] - kj[st], n_near - 1), 0, 0))]
    args = [q, k, vt, bias_tiles]
    if mask_bias is not None:
        in_specs.append(pl.BlockSpec((1, 1, t, t), lambda st, qi, kj: (qi[st], kj[st], 0, 0)))
        args.append(mask_bias)
    grid_spec = pltpu.PrefetchScalarGridSpec(
        num_scalar_prefetch=2,
        grid=(len(qi_list),),
        in_specs=in_specs,
        out_specs=pl.BlockSpec((nh, dv, t), lambda st, qi, kj: (0, 0, qi[st])),
        scratch_shapes=[pltpu.VMEM((nh, 1, t), F32), pltpu.VMEM((nh, 1, t), F32),
                        pltpu.VMEM((nh, dv, t), F32)])
    return pl.pallas_call(
        functools.partial(_attn_kernel, nh=nh, n_near=n_near, has_mask=mask_bias is not None),
        grid_spec=grid_spec,
        out_shape=jax.ShapeDtypeStruct((nh, dv, s), F32),
        compiler_params=_cparams(("arbitrary",)),
    )(qi_arr, kj_arr, *args)


def _sortable(x):
    b = pltpu.bitcast(jnp.where(x == 0.0, 0.0, x), I32)
    return b ^ ((b >> 31) & 0x7FFFFFFF)


def _dsa_sel_kernel(qi_ref, wt_ref, kn_ref, tri_ref, o_ref, keys_scr, *, tile, topk):
    n_chunks = kn_ref.shape[0]
    i = pl.program_id(0)
    n_live = i + 1
    t = i * tile + lax.broadcasted_iota(I32, (tile, tile), 1)
    row = lax.broadcasted_iota(I32, (tile, tile), 0)
    q_heads = [qi_ref[:, h * IDX_DIM:(h + 1) * IDX_DIM] for h in range(IDX_HEADS)]
    w_heads = [wt_ref[h:h + 1, :] for h in range(IDX_HEADS)]

    def score_chunk(c, carry):
        kc = kn_ref[c]
        sc = jnp.zeros((tile, tile), F32)
        for h in range(IDX_HEADS):
            r = lax.dot_general(kc, q_heads[h], _NT, preferred_element_type=F32)
            sc = sc + w_heads[h] * jnp.maximum(r, 0.0)
        sc = jnp.where(c * tile + row <= t, sc, -jnp.inf)
        keys_scr[c] = _sortable(sc)
        return carry

    lax.fori_loop(0, n_live, score_chunk, 0)

    def count(pred_fn):
        def body(c, cnt):
            for g in range(tile // 8):
                cnt = cnt + pred_fn(keys_scr[c, 8 * g:8 * g + 8, :]).astype(I32)
            return cnt
        cnt = lax.fori_loop(0, n_live, body, jnp.zeros((8, tile), I32))
        return jnp.sum(cnt, axis=0, keepdims=True)

    def bit_step(b, ans):
        cand = ans + lax.shift_left(jnp.int32(1), 31 - b)
        cand8 = jnp.broadcast_to(cand, (8, tile))
        total = count(lambda kc: kc >= cand8)
        return jnp.where(total >= topk, cand, ans)

    thr = lax.fori_loop(0, 32, bit_step, jnp.full((1, tile), -2 ** 31, I32))
    thr8 = jnp.broadcast_to(thr, (8, tile))
    n_gt = count(lambda kc: kc > thr8)
    need = (topk - n_gt).astype(F32)

    def emit_chunk(c, seen):
        kc = keys_scr[c]
        eq = kc == thr
        eqf = eq.astype(F32)
        rank = seen + jnp.dot(tri_ref[...], eqf.astype(BF16), preferred_element_type=F32)
        sel = ((kc > thr) | (eq & (rank < need))) & (c * tile + row <= t)
        o_ref[0, c] = jnp.where(sel, 0.0, NEG).astype(BF16)
        return seen + jnp.sum(eqf, axis=0, keepdims=True)

    lax.fori_loop(0, n_live, emit_chunk, jnp.zeros((1, tile), F32))

    def fill_chunk(c, carry):
        o_ref[0, c] = jnp.full((tile, tile), NEG, BF16)
        return carry

    lax.fori_loop(n_live, n_chunks, fill_chunk, 0)


def _dsa_select(q_idx_bf16, w_idx, kn_bf16, topk):
    s = q_idx_bf16.shape[0]
    t = ATT_TILE
    n_chunks = s // t
    kn3 = kn_bf16.reshape(n_chunks, t, IDX_DIM)
    wt = jnp.pad(w_idx.T, ((0, 8 - IDX_HEADS), (0, 0)))
    tri = (lax.broadcasted_iota(I32, (t, t), 1) < lax.broadcasted_iota(I32, (t, t), 0)).astype(BF16)
    return pl.pallas_call(
        functools.partial(_dsa_sel_kernel, tile=t, topk=topk),
        grid=(n_chunks,),
        in_specs=[pl.BlockSpec((t, IDX_HEADS * IDX_DIM), lambda i: (i, 0)),
                  pl.BlockSpec((8, t), lambda i: (0, i)),
                  pl.BlockSpec((n_chunks, t, IDX_DIM), lambda i: (0, 0, 0)),
                  pl.BlockSpec((t, t), lambda i: (0, 0))],
        out_specs=pl.BlockSpec((1, n_chunks, t, t), lambda i: (i, 0, 0, 0)),
        out_shape=jax.ShapeDtypeStruct((n_chunks, n_chunks, t, t), BF16),
        scratch_shapes=[pltpu.VMEM((n_chunks, t, t), I32)],
        compiler_params=_cparams(("arbitrary",)),
    )(q_idx_bf16, wt, kn3, tri)


def _out_proj_kernel(y_ref, w_ref, x_ref, g_ref, o_ref):
    o_ref[...] = x_ref[...] + g_ref[...] * jnp.dot(y_ref[...], w_ref[...], preferred_element_type=F32)


def _out_proj(y_bf16, w_bf16, x, gate, tm=512):
    s, d = x.shape
    k = y_bf16.shape[1]
    return pl.pallas_call(
        _out_proj_kernel,
        grid=(s // tm,),
        in_specs=[pl.BlockSpec((tm, k), lambda i: (i, 0)),
                  pl.BlockSpec((k, d), lambda i: (0, 0)),
                  pl.BlockSpec((tm, d), lambda i: (i, 0)),
                  pl.BlockSpec((1, d), lambda i: (0, 0))],
        out_specs=pl.BlockSpec((tm, d), lambda i: (i, 0)),
        out_shape=jax.ShapeDtypeStruct((s, d), F32),
        compiler_params=_cparams(("arbitrary",)),
    )(y_bf16, w_bf16, x, gate.reshape(1, d))


def _cmp_exchange(v, i, j):
    hi = jnp.maximum(v[i], v[j])
    lo = jnp.minimum(v[i], v[j])
    v[i], v[j] = hi, lo


def _bitonic_sort_desc(v):
    n = len(v)
    k = 2
    while k <= n:
        j = k // 2
        while j >= 1:
            for i in range(n):
                l = i ^ j
                if l > i:
                    if (i & k) == 0:
                        _cmp_exchange(v, i, l)
                    else:
                        _cmp_exchange(v, l, i)
            j //= 2
        k *= 2


def _bitonic_merge_desc(v):
    n = len(v)
    j = n // 2
    while j >= 1:
        for i in range(n):
            l = i ^ j
            if l > i:
                _cmp_exchange(v, i, l)
        j //= 2


def _top16_rows(x):
    tt = x.shape[1]
    v = [x[8 * i:8 * i + 8, :] for i in range(PEER_NKEYS // 8)]
    _bitonic_sort_desc(v)
    for shift in (4, 2, 1):
        other = [pltpu.roll(a, shift, 0) for a in v]
        v = [jnp.maximum(v[i], other[len(v) - 1 - i]) for i in range(len(v))]
        _bitonic_merge_desc(v)
    return v


def _peer_route_kernel(q_ref, sk_ref, r1_ref, e1_ref, e0_ref, ns_ref, s0_scr, s1_scr, *, tt):
    sub = lax.broadcasted_iota(I32, (8, tt), 0)
    tops = [[], []]
    for h in range(PEER_HEADS):
        for p in range(2):
            lo = (2 * h + p) * (PEER_DKEY // 2)
            qhp = q_ref[:, lo:lo + PEER_DKEY // 2].astype(BF16)
            sc = lax.dot_general(sk_ref[2 * h + p], qhp, _NT, preferred_element_type=F32)
            (s0_scr if p == 0 else s1_scr)[h] = sc
            tops[p].append(_top16_rows(sc))
    a = [sum(jnp.where(sub == h, tops[0][h][k], 0.0) for h in range(PEER_HEADS)) for k in range(PEER_TOPK)]
    b = [sum(jnp.where(sub == h, tops[1][h][k], 0.0) for h in range(PEER_HEADS)) for k in range(PEER_TOPK)]
    cand = [a[i] + b[j] for i in range(PEER_TOPK) for j in range(PEER_TOPK) if (i + 1) * (j + 1) <= PEER_TOPK]
    cand = cand + [jnp.full((8, tt), -jnp.inf, F32)] * (64 - len(cand))
    _bitonic_sort_desc(cand)
    tau = cand[PEER_TOPK - 1]
    z = sum(jnp.exp(cand[k] - cand[0]) for k in range(PEER_TOPK))
    half_inv_z = 0.5 / z
    for h in range(PEER_HEADS):
        s0 = s0_scr[h]
        s1 = s1_scr[h]
        tau_h = tau[h:h + 1, :]
        nsel = jnp.zeros((PEER_NKEYS, tt), F32)
        rank1 = jnp.zeros((PEER_NKEYS, tt), F32)
        for k in range(PEER_TOPK):
            bk = b[k][h:h + 1, :]
            nsel = nsel + jnp.where(s0 + bk >= tau_h, 1.0, 0.0)
            rank1 = rank1 + jnp.where(bk > s1, 1.0, 0.0)
        ns_ref[h] = nsel
        r1_ref[h] = rank1.astype(BF16)
        e0_ref[h] = jnp.exp(s0 - a[0][h:h + 1, :]) * half_inv_z[h:h + 1, :]
        e1_ref[h] = jnp.exp(s1 - b[0][h:h + 1, :]).astype(BF16)


def _peer_route(q, subkeys_bf16, tt=256):
    s = q.shape[0]
    big = lambda dt: jax.ShapeDtypeStruct((PEER_HEADS, PEER_NKEYS, s), dt)
    big_spec = pl.BlockSpec((PEER_HEADS, PEER_NKEYS, tt), lambda i: (0, 0, i))
    return pl.pallas_call(
        functools.partial(_peer_route_kernel, tt=tt),
        grid=(s // tt,),
        in_specs=[pl.BlockSpec((tt, PEER_HEADS * PEER_DKEY), lambda i: (i, 0)),
                  pl.BlockSpec((2 * PEER_HEADS, PEER_NKEYS, PEER_DKEY // 2), lambda i: (0, 0, 0))],
        out_specs=[big_spec, big_spec, big_spec, big_spec],
        out_shape=[big(BF16), big(BF16), big(F32), big(F32)],
        scratch_shapes=[pltpu.VMEM((PEER_HEADS, PEER_NKEYS, tt), F32),
                        pltpu.VMEM((PEER_HEADS, PEER_NKEYS, tt), F32)],
        compiler_params=_cparams(("arbitrary",)),
    )(q, subkeys_bf16)


def _peer_dense_kernel(h_ref, u_ref, vt_ref, r1_ref, e1_ref, e0_ref, ns_ref,
                       x_ref, g_ref, gf_ref, o_ref, acc_scr, *, rows_per_step, final_norm):
    k = pl.program_id(1)

    @pl.when(k == 0)
    def _():
        acc_scr[...] = jnp.zeros(acc_scr.shape, F32)

    act = lax.dot_general(u_ref[...], h_ref[...], _NT, preferred_element_type=F32)
    tt = act.shape[1]
    pack_rows = 16
    zero = jnp.zeros((pack_rows, tt), BF16)

    def expert_rows(ii):
        i = k * rows_per_step + ii
        bcast = lambda ref, h: jnp.broadcast_to(ref[h, pl.ds(i, 1), :], (pack_rows, tt)).astype(BF16)
        e0_rows = [bcast(e0_ref, h) for h in range(PEER_HEADS)]
        ns_rows = [bcast(ns_ref, h) for h in range(PEER_HEADS)]
        a = act[ii * PEER_NKEYS:(ii + 1) * PEER_NKEYS, :]
        gelu2 = (a * (1.0 + lax.erf(a * math.sqrt(0.5)))).astype(BF16)
        out = []
        for g in range(PEER_NKEYS // pack_rows):
            rows = slice(g * pack_rows, (g + 1) * pack_rows)
            wgt = None
            for h in range(PEER_HEADS):
                term = jnp.where(r1_ref[h, rows, :] < ns_rows[h], e1_ref[h, rows, :], zero) * e0_rows[h]
                wgt = term if wgt is None else wgt + term
            out.append(wgt * gelu2[rows, :])
        return out

    depth_rows = MXU_DEPTH // PEER_NKEYS
    mixed = None
    for c in range(rows_per_step // depth_rows):
        p = jnp.concatenate([r for ii in range(c * depth_rows, (c + 1) * depth_rows) for r in expert_rows(ii)], axis=0)
        d = jnp.dot(vt_ref[:, c * MXU_DEPTH:(c + 1) * MXU_DEPTH], p, preferred_element_type=F32)
        mixed = d if mixed is None else mixed + d
    acc_scr[...] += mixed

    @pl.when(k == pl.num_programs(1) - 1)
    def _():
        y = x_ref[...] + g_ref[...] * acc_scr[...].T
        if final_norm:
            y = y * lax.rsqrt(jnp.mean(y * y, axis=-1, keepdims=True) + EPS) * gf_ref[...]
        o_ref[...] = y


def _peer_dense(h_bf16, u_bf16, vt_bf16, rank1, e1, e0, nsel, x, gate, g_final, final_norm,
                tt=512, rows_per_step=8):
    s, d = x.shape
    n_exp = u_bf16.shape[0]
    et = rows_per_step * PEER_NKEYS
    tok = lambda j, k: (0, 0, j)
    big_spec = pl.BlockSpec((PEER_HEADS, PEER_NKEYS, tt), tok)
    return pl.pallas_call(
        functools.partial(_peer_dense_kernel, rows_per_step=rows_per_step, final_norm=final_norm),
        grid=(s // tt, n_exp // et),
        in_specs=[pl.BlockSpec((tt, d), lambda j, k: (j, 0)),
                  pl.BlockSpec((et, d), lambda j, k: (k, 0)),
                  pl.BlockSpec((d, et), lambda j, k: (0, k)),
                  big_spec, big_spec, big_spec, big_spec,
                  pl.BlockSpec((tt, d), lambda j, k: (j, 0)),
                  pl.BlockSpec((1, d), lambda j, k: (0, 0)),
                  pl.BlockSpec((1, d), lambda j, k: (0, 0))],
        out_specs=pl.BlockSpec((tt, d), lambda j, k: (j, 0)),
        out_shape=jax.ShapeDtypeStruct((s, d), F32),
        scratch_shapes=[pltpu.VMEM((d, tt), F32)],
        compiler_params=_cparams(("arbitrary", "arbitrary")),
    )(h_bf16, u_bf16, vt_bf16, rank1, e1, e0, nsel, x, gate.reshape(1, d), g_final.reshape(1, d))


def _heads_first(t, nh):
    s = t.shape[0]
    return jnp.transpose(t.reshape(s, nh, HEAD_DIM), (1, 0, 2))


def _mixer(h_proj, w_out_bf16, bias_tiles, x, gate):
    s = h_proj.shape[0]
    parts = []
    off = 0
    for size in PROJ_SIZES:
        parts.append(h_proj[:, off:off + size])
        off += size
    mq, mk, mv, dq, dk, dv, iq, ik, iw = parts
    scale = HEAD_DIM ** -0.5 * LOG2E
    dims_first = lambda a, nh: a.T.reshape(nh, HEAD_DIM, s).astype(BF16)

    mq_h, mk_h = _heads_first(mq, MOBA_HEADS), _heads_first(mk, MOBA_HEADS)
    selb = _moba_select(mq_h, _moba_kmean(mk_h))
    blk = jnp.arange(s, dtype=I32) // MOBA_BLOCK
    onehot = (blk[:, None] == jnp.arange(HEAD_DIM, dtype=I32)[None, :]).astype(BF16)
    q_aug = jnp.concatenate([(mq_h * scale).astype(BF16), selb], axis=-1)
    k_aug = jnp.concatenate([mk_h.astype(BF16), jnp.broadcast_to(onehot, (MOBA_HEADS, s, HEAD_DIM))], axis=-1)
    y_moba = _attention(q_aug, k_aug, dims_first(mv, MOBA_HEADS), bias_tiles, 0)

    ikf = ik.astype(F32)
    ik_n = ikf * lax.rsqrt(jnp.mean(ikf * ikf, axis=-1, keepdims=True) + EPS)
    topk = min(DSA_TOPK_MAX, s // 4)
    mask_bias = _dsa_select(iq.astype(BF16), iw * IDX_HEADS ** -0.5, ik_n.astype(BF16), topk)
    dq_h, dk_h = _heads_first(dq, DSA_HEADS), _heads_first(dk, DSA_HEADS)
    zeros = jnp.zeros((DSA_HEADS, s, HEAD_DIM), BF16)
    q_pad = jnp.concatenate([(dq_h * scale).astype(BF16), zeros], axis=-1)
    k_pad = jnp.concatenate([dk_h.astype(BF16), zeros], axis=-1)
    y_dsa = _attention(q_pad, k_pad, dims_first(dv, DSA_HEADS), bias_tiles, 1, mask_bias)

    y = jnp.concatenate([y_moba.reshape(MOBA_WIDTH, s), y_dsa.reshape(DSA_WIDTH, s)], axis=0).T
    return _out_proj(y.astype(BF16), w_out_bf16, x, gate)


def kernel(x, c, w_ada, b_ada, norm_attn, norm_ffn, w_in, w_out, rel_bias,
           peer_wq, peer_subkeys, peer_u, peer_v, norm_final):
    batch, s, d = x.shape
    assert batch == 1 and s % (2 * ATT_TILE) == 0
    depth = w_ada.shape[0]
    x = x[0]
    mod = _ada_mod(c, w_ada, b_ada)
    bias_tiles = _bias_tiles(rel_bias, ATT_TILE)
    n_pad = -(-PROJ_WIDTH // LANES) * LANES
    for l in range(depth):
        sh1, sc1, g1, sh2, sc2, g2 = jnp.split(mod[l], 6)
        w_in_l = jnp.pad(w_in[l], ((0, 0), (0, n_pad - PROJ_WIDTH))).astype(BF16)
        _, proj = _norm_mod_matmul(x, norm_attn[l], sh1, sc1, w_in_l)
        x = _mixer(proj, w_out[l].astype(BF16), bias_tiles, x, g1)
        h2, q = _norm_mod_matmul(x, norm_ffn[l], sh2, sc2, peer_wq[l].astype(BF16))
        subkeys = peer_subkeys[l].reshape(2 * PEER_HEADS, PEER_NKEYS, PEER_DKEY // 2).astype(BF16)
        rank1, e1, e0, nsel = _peer_route(q, subkeys)
        x = _peer_dense(h2, peer_u[l].astype(BF16), peer_v[l].T.astype(BF16), rank1, e1, e0, nsel,
                        x, g2, norm_final, final_norm=(l == depth - 1))
    return x[None]
```

```python
import functools
import math

import jax
import jax.numpy as jnp
from jax import lax
from jax.experimental import pallas as pl
from jax.experimental.pallas import tpu as pltpu

F32 = jnp.float32
BF16 = jnp.bfloat16
I32 = jnp.int32

HEAD_DIM = 64
MOBA_HEADS = 8
DSA_HEADS = 8
N_HEADS = MOBA_HEADS + DSA_HEADS
MOBA_WIDTH = MOBA_HEADS * HEAD_DIM
DSA_WIDTH = DSA_HEADS * HEAD_DIM
MOBA_BLOCK = 256
MOBA_TOPK = 3
IDX_HEADS = 4
IDX_DIM = 64
DSA_TOPK_MAX = 256
N_BUCKETS = 32
MAX_DISTANCE = 4096
PEER_HEADS = 8
PEER_NKEYS = 128
PEER_DKEY = 256
PEER_TOPK = 16
EPS = 1e-6
PROJ_SIZES = (MOBA_WIDTH, MOBA_WIDTH, MOBA_WIDTH, DSA_WIDTH, DSA_WIDTH, DSA_WIDTH,
              IDX_HEADS * IDX_DIM, IDX_DIM, IDX_HEADS)
PROJ_WIDTH = sum(PROJ_SIZES)

NEG = -1e30
LOG2E = math.log2(math.e)
ATT_TILE = MOBA_BLOCK
ATT_KEY_TILES_PER_STEP = 4
LANES = 128
MXU_DEPTH = 256
VMEM_LIMIT = 56 * 1024 * 1024

_NT = (((1,), (1,)), ((), ()))


def _far_bias_tiles(tile):
    max_exact = N_BUCKETS // 2
    sat = int(math.ceil(max_exact * (MAX_DISTANCE / max_exact) ** ((N_BUCKETS - max_exact - 1) / (N_BUCKETS - max_exact)))) + 4
    return -(-(sat + tile - 1) // tile)


def _cparams(sem):
    return pltpu.CompilerParams(dimension_semantics=sem, vmem_limit_bytes=VMEM_LIMIT)


def _ada_kernel(c_ref, w_ref, b_ref, o_ref):
    c = c_ref[...]
    sc = c * jax.nn.sigmoid(c)
    o_ref[0] = jnp.dot(sc, w_ref[0], preferred_element_type=F32,
                       precision=lax.Precision.HIGHEST) + b_ref[0]


def _ada_mod(c, w_ada, b_ada):
    depth, d, n = w_ada.shape
    tn = 1536
    c8 = jnp.broadcast_to(c, (8, d))
    out = pl.pallas_call(
        _ada_kernel,
        grid=(depth, n // tn),
        in_specs=[pl.BlockSpec((8, d), lambda l, j: (0, 0)),
                  pl.BlockSpec((1, d, tn), lambda l, j: (l, 0, j)),
                  pl.BlockSpec((1, 1, tn), lambda l, j: (l, 0, j))],
        out_specs=pl.BlockSpec((1, 8, tn), lambda l, j: (l, 0, j)),
        out_shape=jax.ShapeDtypeStruct((depth, 8, n), F32),
        compiler_params=_cparams(("arbitrary", "arbitrary")),
    )(c8, w_ada, b_ada.reshape(depth, 1, n))
    return out[:, 0, :]


def _nmm_kernel(x_ref, g_ref, sh_ref, sc_ref, w_ref, h_ref, o_ref):
    x = x_ref[...]
    y = x * lax.rsqrt(jnp.mean(x * x, axis=-1, keepdims=True) + EPS) * g_ref[...]
    hb = (y * (1.0 + sc_ref[...]) + sh_ref[...]).astype(BF16)
    h_ref[...] = hb
    o_ref[...] = jnp.dot(hb, w_ref[...], preferred_element_type=F32)


def _norm_mod_matmul(x, g, shift, scale, w_bf16, tm=256):
    s, d = x.shape
    n = w_bf16.shape[1]
    row = lambda i: (0, 0)
    return pl.pallas_call(
        _nmm_kernel,
        grid=(s // tm,),
        in_specs=[pl.BlockSpec((tm, d), lambda i: (i, 0)),
                  pl.BlockSpec((1, d), row), pl.BlockSpec((1, d), row), pl.BlockSpec((1, d), row),
                  pl.BlockSpec((d, n), row)],
        out_specs=[pl.BlockSpec((tm, d), lambda i: (i, 0)),
                   pl.BlockSpec((tm, n), lambda i: (i, 0))],
        out_shape=[jax.ShapeDtypeStruct((s, d), BF16), jax.ShapeDtypeStruct((s, n), F32)],
        compiler_params=_cparams(("arbitrary",)),
    )(x, g.reshape(1, d), shift.reshape(1, d), scale.reshape(1, d), w_bf16)


def _t5_bucket(d):
    max_exact = N_BUCKETS // 2
    d = jnp.maximum(d, 0)
    df = jnp.maximum(d, 1).astype(F32)
    large = max_exact + (jnp.log(df / max_exact) / math.log(MAX_DISTANCE / max_exact)
                         * (N_BUCKETS - max_exact)).astype(I32)
    return jnp.where(d < max_exact, d, jnp.minimum(large, N_BUCKETS - 1))


def _bias_tile_kernel(tab_ref, o_ref, *, tile):
    h = pl.program_id(0)
    off = pl.program_id(1)
    b = lax.broadcasted_iota(I32, (tile, tile), 0)
    a = lax.broadcasted_iota(I32, (tile, tile), 1)
    d = off * tile + a - b
    bucket = _t5_bucket(d)
    val = jnp.zeros((tile, tile), F32)
    for k in range(N_BUCKETS):
        val = jnp.where(bucket == k, tab_ref[h, k], val)
    val = (val - tab_ref[h, N_BUCKETS - 1]) * LOG2E
    o_ref[0, 0] = jnp.where(d < 0, NEG, val)


def _bias_tiles(rel_bias, tile):
    n_near = _far_bias_tiles(tile)
    nh = rel_bias.shape[0]
    return pl.pallas_call(
        functools.partial(_bias_tile_kernel, tile=tile),
        grid=(nh, n_near),
        in_specs=[pl.BlockSpec(memory_space=pltpu.SMEM)],
        out_specs=pl.BlockSpec((1, 1, tile, tile), lambda h, o: (h, o, 0, 0)),
        out_shape=jax.ShapeDtypeStruct((nh, n_near, tile, tile), F32),
        compiler_params=_cparams(("arbitrary", "arbitrary")),
    )(rel_bias)


def _kmean_kernel(k_ref, o_ref, *, nb):
    k = k_ref[0]
    km = jnp.mean(k.reshape(nb, MOBA_BLOCK, HEAD_DIM), axis=1)
    o_ref[0] = jnp.zeros(o_ref.shape[1:], F32)
    o_ref[0, :nb, :] = km


def _moba_kmean(k_hsd):
    nh, s, dh = k_hsd.shape
    nb = s // MOBA_BLOCK
    return pl.pallas_call(
        functools.partial(_kmean_kernel, nb=nb),
        grid=(nh,),
        in_specs=[pl.BlockSpec((1, s, dh), lambda h: (h, 0, 0))],
        out_specs=pl.BlockSpec((1, LANES, dh), lambda h: (h, 0, 0)),
        out_shape=jax.ShapeDtypeStruct((nh, LANES, dh), F32),
        compiler_params=_cparams(("arbitrary",)),
    )(k_hsd)


def _moba_sel_kernel(q_ref, km_ref, o_ref, *, tq, n_sel):
    q = q_ref[0]
    gate = lax.dot_general(q, km_ref[0], _NT, preferred_element_type=F32,
                           precision=lax.Precision.HIGHEST)
    t = pl.program_id(1) * tq + lax.broadcasted_iota(I32, (tq, LANES), 0)
    b0 = t // MOBA_BLOCK
    n = lax.broadcasted_iota(I32, (tq, LANES), 1)
    avail = n < b0
    taken = n == b0
    for _ in range(n_sel):
        g = jnp.where(avail, gate, -jnp.inf)
        m = jnp.max(g, axis=1, keepdims=True)
        idx = jnp.min(jnp.where((g == m) & avail, n, 1 << 20), axis=1, keepdims=True)
        pick = n == idx
        taken = taken | pick
        avail = avail & jnp.logical_not(pick)
    o_ref[0] = jnp.where(taken, 0.0, NEG)[:, :HEAD_DIM].astype(BF16)


def _moba_select(q_hsd, kmean, tq=512):
    nh, s, dh = q_hsd.shape
    nb = s // MOBA_BLOCK
    assert nb <= HEAD_DIM, "block-selection mask rides in HEAD_DIM spare contraction columns"
    n_sel = max(min(MOBA_TOPK, nb - 1), 1)
    tq = min(tq, s)
    return pl.pallas_call(
        functools.partial(_moba_sel_kernel, tq=tq, n_sel=n_sel),
        grid=(nh, s // tq),
        in_specs=[pl.BlockSpec((1, tq, dh), lambda h, i: (h, i, 0)),
                  pl.BlockSpec((1, LANES, dh), lambda h, i: (h, 0, 0))],
        out_specs=pl.BlockSpec((1, tq, dh), lambda h, i: (h, i, 0)),
        out_shape=jax.ShapeDtypeStruct((nh, s, dh), BF16),
        compiler_params=_cparams(("arbitrary", "arbitrary")),
    )(q_hsd, kmean)


def _attn_kernel(qi_ref, kg_ref, *refs, nh, n_near, group, has_mask):
    if has_mask:
        q_ref, k_ref, vt_ref, bt_ref, mb_ref, o_ref, m_scr, l_scr, acc_scr = refs
    else:
        q_ref, k_ref, vt_ref, bt_ref, o_ref, m_scr, l_scr, acc_scr = refs
    t = ATT_TILE
    step = pl.program_id(0)
    qi = qi_ref[step]
    kg = kg_ref[step]

    @pl.when(kg == 0)
    def _():
        m_scr[...] = jnp.full(m_scr.shape, NEG, F32)
        l_scr[...] = jnp.zeros(l_scr.shape, F32)
        acc_scr[...] = jnp.zeros(acc_scr.shape, F32)

    def tile_update(g, off, with_bias):
        keys = slice(g * t, (g + 1) * t)
        if has_mask:
            mb = mb_ref[0, g].astype(F32)
        scores = []
        for h in range(nh):
            s = lax.dot_general(k_ref[h, keys, :], q_ref[h], _NT, preferred_element_type=F32)
            if with_bias:
                s = s + bt_ref[h, off]
            if has_mask:
                s = s + mb
            scores.append(s)
        probs, alphas = [], []
        for h in range(nh):
            m_old = m_scr[h]
            m_new = jnp.maximum(m_old, jnp.max(scores[h], axis=0, keepdims=True))
            alpha = jnp.exp2(m_old - m_new)
            p = jnp.exp2(scores[h] - m_new)
            l_scr[h] = alpha * l_scr[h] + jnp.sum(p, axis=0, keepdims=True)
            m_scr[h] = m_new
            probs.append(p.astype(BF16))
            alphas.append(alpha)
        for h in range(nh):
            acc_scr[h] = alphas[h] * acc_scr[h] + jnp.dot(vt_ref[h, :, keys], probs[h], preferred_element_type=F32)

    for g in range(group):
        off = qi - (kg * group + g)
        pl.when((off >= 0) & (off < n_near))(functools.partial(tile_update, g, jnp.minimum(off, n_near - 1), True))
        pl.when(off >= n_near)(functools.partial(tile_update, g, off, False))

    @pl.when(kg == qi // group)
    def _():
        for h in range(nh):
            o_ref[h] = acc_scr[h] / l_scr[h]


def _attention(q, k, vt, bias_tiles, head_group, mask_bias=None):
    nh, s, dk = q.shape
    dv = vt.shape[1]
    t = ATT_TILE
    group = ATT_KEY_TILES_PER_STEP
    nq = s // t
    n_near = bias_tiles.shape[1]
    qi_list = [i for i in range(nq) for _ in range(i // group + 1)]
    kg_list = [j for i in range(nq) for j in range(i // group + 1)]
    qi_arr = jnp.asarray(qi_list, I32)
    kg_arr = jnp.asarray(kg_list, I32)
    in_specs = [pl.BlockSpec((nh, t, dk), lambda st, qi, kg: (0, qi[st], 0)),
                pl.BlockSpec((nh, group * t, dk), lambda st, qi, kg: (0, kg[st], 0)),
                pl.BlockSpec((nh, dv, group * t), lambda st, qi, kg: (0, 0, kg[st])),
                pl.BlockSpec((nh, n_near, t, t), lambda st, qi, kg: (head_group, 0, 0, 0),
                             pipeline_mode=pl.Buffered(1))]
    args = [q, k, vt, bias_tiles]
    if mask_bias is not None:
        in_specs.append(pl.BlockSpec((1, group, t, t), lambda st, qi, kg: (qi[st], kg[st], 0, 0)))
        args.append(mask_bias)
    grid_spec = pltpu.PrefetchScalarGridSpec(
        num_scalar_prefetch=2,
        grid=(len(qi_list),),
        in_specs=in_specs,
        out_specs=pl.BlockSpec((nh, dv, t), lambda st, qi, kg: (0, 0, qi[st])),
        scratch_shapes=[pltpu.VMEM((nh, 1, t), F32), pltpu.VMEM((nh, 1, t), F32),
                        pltpu.VMEM((nh, dv, t), F32)])
    return pl.pallas_call(
        functools.partial(_attn_kernel, nh=nh, n_near=n_near, group=group, has_mask=mask_bias is not None),
        grid_spec=grid_spec,
        out_shape=jax.ShapeDtypeStruct((nh, dv, s), F32),
        compiler_params=_cparams(("arbitrary",)),
    )(qi_arr, kg_arr, *args)


def _sortable(x):
    b = pltpu.bitcast(jnp.where(x == 0.0, 0.0, x), I32)
    return b ^ ((b >> 31) & 0x7FFFFFFF)


def _dsa_sel_kernel(qi_ref, wt_ref, kn_ref, tri_ref, o_ref, keys_scr, *, tile, topk):
    n_chunks = kn_ref.shape[0]
    i = pl.program_id(0)
    n_live = i + 1
    t = i * tile + lax.broadcasted_iota(I32, (tile, tile), 1)
    row = lax.broadcasted_iota(I32, (tile, tile), 0)
    q_heads = [qi_ref[:, h * IDX_DIM:(h + 1) * IDX_DIM] for h in range(IDX_HEADS)]
    w_heads = [wt_ref[h:h + 1, :] for h in range(IDX_HEADS)]

    def score_chunk(c, carry):
        kc = kn_ref[c]
        sc = jnp.zeros((tile, tile), F32)
        for h in range(IDX_HEADS):
            r = lax.dot_general(kc, q_heads[h], _NT, preferred_element_type=F32)
            sc = sc + w_heads[h] * jnp.maximum(r, 0.0)
        sc = jnp.where(c * tile + row <= t, sc, -jnp.inf)
        keys_scr[c] = _sortable(sc)
        return carry

    lax.fori_loop(0, n_live, score_chunk, 0)

    def count(pred_fn):
        def body(c, cnt):
            for g in range(tile // 8):
                cnt = cnt + pred_fn(keys_scr[c, 8 * g:8 * g + 8, :]).astype(I32)
            return cnt
        cnt = lax.fori_loop(0, n_live, body, jnp.zeros((8, tile), I32))
        return jnp.sum(cnt, axis=0, keepdims=True)

    def bit_step(b, ans):
        cand = ans + lax.shift_left(jnp.int32(1), 31 - b)
        cand8 = jnp.broadcast_to(cand, (8, tile))
        total = count(lambda kc: kc >= cand8)
        return jnp.where(total >= topk, cand, ans)

    thr = lax.fori_loop(0, 32, bit_step, jnp.full((1, tile), -2 ** 31, I32))
    thr8 = jnp.broadcast_to(thr, (8, tile))
    n_gt = count(lambda kc: kc > thr8)
    need = (topk - n_gt).astype(F32)

    def emit_chunk(c, seen):
        kc = keys_scr[c]
        eq = kc == thr
        eqf = eq.astype(F32)
        rank = seen + jnp.dot(tri_ref[...], eqf.astype(BF16), preferred_element_type=F32)
        sel = ((kc > thr) | (eq & (rank < need))) & (c * tile + row <= t)
        o_ref[0, c] = jnp.where(sel, 0.0, NEG).astype(BF16)
        return seen + jnp.sum(eqf, axis=0, keepdims=True)

    lax.fori_loop(0, n_live, emit_chunk, jnp.zeros((1, tile), F32))

    def fill_chunk(c, carry):
        o_ref[0, c] = jnp.full((tile, tile), NEG, BF16)
        return carry

    lax.fori_loop(n_live, n_chunks, fill_chunk, 0)


def _dsa_select(q_idx_bf16, w_idx, kn_bf16, topk):
    s = q_idx_bf16.shape[0]
    t = ATT_TILE
    n_chunks = s // t
    kn3 = kn_bf16.reshape(n_chunks, t, IDX_DIM)
    wt = jnp.pad(w_idx.T, ((0, 8 - IDX_HEADS), (0, 0)))
    tri = (lax.broadcasted_iota(I32, (t, t), 1) < lax.broadcasted_iota(I32, (t, t), 0)).astype(BF16)
    return pl.pallas_call(
        functools.partial(_dsa_sel_kernel, tile=t, topk=topk),
        grid=(n_chunks,),
        in_specs=[pl.BlockSpec((t, IDX_HEADS * IDX_DIM), lambda i: (i, 0)),
                  pl.BlockSpec((8, t), lambda i: (0, i)),
                  pl.BlockSpec((n_chunks, t, IDX_DIM), lambda i: (0, 0, 0)),
                  pl.BlockSpec((t, t), lambda i: (0, 0))],
        out_specs=pl.BlockSpec((1, n_chunks, t, t), lambda i: (i, 0, 0, 0)),
        out_shape=jax.ShapeDtypeStruct((n_chunks, n_chunks, t, t), BF16),
        scratch_shapes=[pltpu.VMEM((n_chunks, t, t), I32)],
        compiler_params=_cparams(("arbitrary",)),
    )(q_idx_bf16, wt, kn3, tri)


def _out_proj_kernel(y_ref, w_ref, x_ref, g_ref, o_ref):
    o_ref[...] = x_ref[...] + g_ref[...] * jnp.dot(y_ref[...], w_ref[...], preferred_element_type=F32)


def _out_proj(y_bf16, w_bf16, x, gate, tm=512):
    s, d = x.shape
    k = y_bf16.shape[1]
    return pl.pallas_call(
        _out_proj_kernel,
        grid=(s // tm,),
        in_specs=[pl.BlockSpec((tm, k), lambda i: (i, 0)),
                  pl.BlockSpec((k, d), lambda i: (0, 0)),
                  pl.BlockSpec((tm, d), lambda i: (i, 0)),
                  pl.BlockSpec((1, d), lambda i: (0, 0))],
        out_specs=pl.BlockSpec((tm, d), lambda i: (i, 0)),
        out_shape=jax.ShapeDtypeStruct((s, d), F32),
        compiler_params=_cparams(("arbitrary",)),
    )(y_bf16, w_bf16, x, gate.reshape(1, d))


def _cmp_exchange(v, i, j):
    hi = jnp.maximum(v[i], v[j])
    lo = jnp.minimum(v[i], v[j])
    v[i], v[j] = hi, lo


def _bitonic_sort_desc(v):
    n = len(v)
    k = 2
    while k <= n:
        j = k // 2
        while j >= 1:
            for i in range(n):
                l = i ^ j
                if l > i:
                    if (i & k) == 0:
                        _cmp_exchange(v, i, l)
                    else:
                        _cmp_exchange(v, l, i)
            j //= 2
        k *= 2


def _bitonic_merge_desc(v):
    n = len(v)
    j = n // 2
    while j >= 1:
        for i in range(n):
            l = i ^ j
            if l > i:
                _cmp_exchange(v, i, l)
        j //= 2


def _top16_rows(x):
    tt = x.shape[1]
    v = [x[8 * i:8 * i + 8, :] for i in range(PEER_NKEYS // 8)]
    _bitonic_sort_desc(v)
    for shift in (4, 2, 1):
        other = [pltpu.roll(a, shift, 0) for a in v]
        v = [jnp.maximum(v[i], other[len(v) - 1 - i]) for i in range(len(v))]
        _bitonic_merge_desc(v)
    return v


def _peer_route_kernel(q_ref, sk_ref, r1_ref, e1_ref, e0_ref, ns_ref, s0_scr, s1_scr, *, tt):
    sub = lax.broadcasted_iota(I32, (8, tt), 0)
    tops = [[], []]
    for h in range(PEER_HEADS):
        for p in range(2):
            lo = (2 * h + p) * (PEER_DKEY // 2)
            qhp = q_ref[:, lo:lo + PEER_DKEY // 2].astype(BF16)
            sc = lax.dot_general(sk_ref[2 * h + p], qhp, _NT, preferred_element_type=F32)
            (s0_scr if p == 0 else s1_scr)[h] = sc
            tops[p].append(_top16_rows(sc))
    a = [sum(jnp.where(sub == h, tops[0][h][k], 0.0) for h in range(PEER_HEADS)) for k in range(PEER_TOPK)]
    b = [sum(jnp.where(sub == h, tops[1][h][k], 0.0) for h in range(PEER_HEADS)) for k in range(PEER_TOPK)]
    cand = [a[i] + b[j] for i in range(PEER_TOPK) for j in range(PEER_TOPK) if (i + 1) * (j + 1) <= PEER_TOPK]
    cand = cand + [jnp.full((8, tt), -jnp.inf, F32)] * (64 - len(cand))
    _bitonic_sort_desc(cand)
    tau = cand[PEER_TOPK - 1]
    z = sum(jnp.exp(cand[k] - cand[0]) for k in range(PEER_TOPK))
    half_inv_z = 0.5 / z
    for h in range(PEER_HEADS):
        s0 = s0_scr[h]
        s1 = s1_scr[h]
        tau_h = tau[h:h + 1, :]
        nsel = jnp.zeros((PEER_NKEYS, tt), F32)
        rank1 = jnp.zeros((PEER_NKEYS, tt), F32)
        for k in range(PEER_TOPK):
            bk = b[k][h:h + 1, :]
            nsel = nsel + jnp.where(s0 + bk >= tau_h, 1.0, 0.0)
            rank1 = rank1 + jnp.where(bk > s1, 1.0, 0.0)
        ns_ref[h] = nsel
        r1_ref[h] = rank1.astype(BF16)
        e0_ref[h] = jnp.exp(s0 - a[0][h:h + 1, :]) * half_inv_z[h:h + 1, :]
        e1_ref[h] = jnp.exp(s1 - b[0][h:h + 1, :]).astype(BF16)


def _peer_route(q, subkeys_bf16, tt=256):
    s = q.shape[0]
    big = lambda dt: jax.ShapeDtypeStruct((PEER_HEADS, PEER_NKEYS, s), dt)
    big_spec = pl.BlockSpec((PEER_HEADS, PEER_NKEYS, tt), lambda i: (0, 0, i))
    return pl.pallas_call(
        functools.partial(_peer_route_kernel, tt=tt),
        grid=(s // tt,),
        in_specs=[pl.BlockSpec((tt, PEER_HEADS * PEER_DKEY), lambda i: (i, 0)),
                  pl.BlockSpec((2 * PEER_HEADS, PEER_NKEYS, PEER_DKEY // 2), lambda i: (0, 0, 0))],
        out_specs=[big_spec, big_spec, big_spec, big_spec],
        out_shape=[big(BF16), big(BF16), big(F32), big(F32)],
        scratch_shapes=[pltpu.VMEM((PEER_HEADS, PEER_NKEYS, tt), F32),
                        pltpu.VMEM((PEER_HEADS, PEER_NKEYS, tt), F32)],
        compiler_params=_cparams(("arbitrary",)),
    )(q, subkeys_bf16)


def _peer_dense_kernel(h_ref, u_ref, vt_ref, r1_ref, e1_ref, e0_ref, ns_ref,
                       x_ref, g_ref, gf_ref, o_ref, acc_scr, *, rows_per_step, final_norm):
    k = pl.program_id(1)

    @pl.when(k == 0)
    def _():
        acc_scr[...] = jnp.zeros(acc_scr.shape, F32)

    act = lax.dot_general(u_ref[...], h_ref[...], _NT, preferred_element_type=F32)
    tt = act.shape[1]
    pack_rows = 16
    zero = jnp.zeros((pack_rows, tt), BF16)

    def expert_rows(ii):
        i = k * rows_per_step + ii
        bcast = lambda ref, h: jnp.broadcast_to(ref[h, pl.ds(i, 1), :], (pack_rows, tt)).astype(BF16)
        e0_rows = [bcast(e0_ref, h) for h in range(PEER_HEADS)]
        ns_rows = [bcast(ns_ref, h) for h in range(PEER_HEADS)]
        a = act[ii * PEER_NKEYS:(ii + 1) * PEER_NKEYS, :]
        gelu2 = (a * (1.0 + lax.erf(a * math.sqrt(0.5)))).astype(BF16)
        out = []
        for g in range(PEER_NKEYS // pack_rows):
            rows = slice(g * pack_rows, (g + 1) * pack_rows)
            wgt = None
            for h in range(PEER_HEADS):
                term = jnp.where(r1_ref[h, rows, :] < ns_rows[h], e1_ref[h, rows, :], zero) * e0_rows[h]
                wgt = term if wgt is None else wgt + term
            out.append(wgt * gelu2[rows, :])
        return out

    depth_rows = MXU_DEPTH // PEER_NKEYS
    mixed = None
    for c in range(rows_per_step // depth_rows):
        p = jnp.concatenate([r for ii in range(c * depth_rows, (c + 1) * depth_rows) for r in expert_rows(ii)], axis=0)
        d = jnp.dot(vt_ref[:, c * MXU_DEPTH:(c + 1) * MXU_DEPTH], p, preferred_element_type=F32)
        mixed = d if mixed is None else mixed + d
    acc_scr[...] += mixed

    @pl.when(k == pl.num_programs(1) - 1)
    def _():
        y = x_ref[...] + g_ref[...] * acc_scr[...].T
        if final_norm:
            y = y * lax.rsqrt(jnp.mean(y * y, axis=-1, keepdims=True) + EPS) * gf_ref[...]
        o_ref[...] = y


def _peer_dense(h_bf16, u_bf16, vt_bf16, rank1, e1, e0, nsel, x, gate, g_final, final_norm,
                tt=512, rows_per_step=8):
    s, d = x.shape
    n_exp = u_bf16.shape[0]
    et = rows_per_step * PEER_NKEYS
    tok = lambda j, k: (0, 0, j)
    big_spec = pl.BlockSpec((PEER_HEADS, PEER_NKEYS, tt), tok)
    return pl.pallas_call(
        functools.partial(_peer_dense_kernel, rows_per_step=rows_per_step, final_norm=final_norm),
        grid=(s // tt, n_exp // et),
        in_specs=[pl.BlockSpec((tt, d), lambda j, k: (j, 0)),
                  pl.BlockSpec((et, d), lambda j, k: (k, 0)),
                  pl.BlockSpec((d, et), lambda j, k: (0, k)),
                  big_spec, big_spec, big_spec, big_spec,
                  pl.BlockSpec((tt, d), lambda j, k: (j, 0)),
                  pl.BlockSpec((1, d), lambda j, k: (0, 0)),
                  pl.BlockSpec((1, d), lambda j, k: (0, 0))],
        out_specs=pl.BlockSpec((tt, d), lambda j, k: (j, 0)),
        out_shape=jax.ShapeDtypeStruct((s, d), F32),
        scratch_shapes=[pltpu.VMEM((d, tt), F32)],
        compiler_params=_cparams(("arbitrary", "arbitrary")),
    )(h_bf16, u_bf16, vt_bf16, rank1, e1, e0, nsel, x, gate.reshape(1, d), g_final.reshape(1, d))


def _heads_first(t, nh):
    s = t.shape[0]
    return jnp.transpose(t.reshape(s, nh, HEAD_DIM), (1, 0, 2))


def _mixer(h_proj, w_out_bf16, bias_tiles, x, gate):
    s = h_proj.shape[0]
    parts = []
    off = 0
    for size in PROJ_SIZES:
        parts.append(h_proj[:, off:off + size])
        off += size
    mq, mk, mv, dq, dk, dv, iq, ik, iw = parts
    scale = HEAD_DIM ** -0.5 * LOG2E
    dims_first = lambda a, nh: a.T.reshape(nh, HEAD_DIM, s).astype(BF16)

    mq_h, mk_h = _heads_first(mq, MOBA_HEADS), _heads_first(mk, MOBA_HEADS)
    selb = _moba_select(mq_h, _moba_kmean(mk_h))
    blk = jnp.arange(s, dtype=I32) // MOBA_BLOCK
    onehot = (blk[:, None] == jnp.arange(HEAD_DIM, dtype=I32)[None, :]).astype(BF16)
    q_aug = jnp.concatenate([(mq_h * scale).astype(BF16), selb], axis=-1)
    k_aug = jnp.concatenate([mk_h.astype(BF16), jnp.broadcast_to(onehot, (MOBA_HEADS, s, HEAD_DIM))], axis=-1)
    y_moba = _attention(q_aug, k_aug, dims_first(mv, MOBA_HEADS), bias_tiles, 0)

    ikf = ik.astype(F32)
    ik_n = ikf * lax.rsqrt(jnp.mean(ikf * ikf, axis=-1, keepdims=True) + EPS)
    topk = min(DSA_TOPK_MAX, s // 4)
    mask_bias = _dsa_select(iq.astype(BF16), iw * IDX_HEADS ** -0.5, ik_n.astype(BF16), topk)
    dq_h, dk_h = _heads_first(dq, DSA_HEADS), _heads_first(dk, DSA_HEADS)
    zeros = jnp.zeros((DSA_HEADS, s, HEAD_DIM), BF16)
    q_pad = jnp.concatenate([(dq_h * scale).astype(BF16), zeros], axis=-1)
    k_pad = jnp.concatenate([dk_h.astype(BF16), zeros], axis=-1)
    y_dsa = _attention(q_pad, k_pad, dims_first(dv, DSA_HEADS), bias_tiles, 1, mask_bias)

    y = jnp.concatenate([y_moba.reshape(MOBA_WIDTH, s), y_dsa.reshape(DSA_WIDTH, s)], axis=0).T
    return _out_proj(y.astype(BF16), w_out_bf16, x, gate)


def kernel(x, c, w_ada, b_ada, norm_attn, norm_ffn, w_in, w_out, rel_bias,
           peer_wq, peer_subkeys, peer_u, peer_v, norm_final):
    batch, s, d = x.shape
    assert batch == 1 and s % (ATT_KEY_TILES_PER_STEP * ATT_TILE) == 0
    depth = w_ada.shape[0]
    x = x[0]
    mod = _ada_mod(c, w_ada, b_ada)
    bias_tiles = _bias_tiles(rel_bias, ATT_TILE)
    n_pad = -(-PROJ_WIDTH // LANES) * LANES
    for l in range(depth):
        sh1, sc1, g1, sh2, sc2, g2 = jnp.split(mod[l], 6)
        w_in_l = jnp.pad(w_in[l], ((0, 0), (0, n_pad - PROJ_WIDTH))).astype(BF16)
        _, proj = _norm_mod_matmul(x, norm_attn[l], sh1, sc1, w_in_l)
        x = _mixer(proj, w_out[l].astype(BF16), bias_tiles, x, g1)
        h2, q = _norm_mod_matmul(x, norm_ffn[l], sh2, sc2, peer_wq[l].astype(BF16))
        subkeys = peer_subkeys[l].reshape(2 * PEER_HEADS, PEER_NKEYS, PEER_DKEY // 2).astype(BF16)
        rank1, e1, e0, nsel = _peer_route(q, subkeys)
        x = _peer_dense(h2, peer_u[l].astype(BF16), peer_v[l].T.astype(BF16), rank1, e1, e0, nsel,
                        x, g2, norm_final, final_norm=(l == depth - 1))
    return x[None]
```

```python
import functools
import math

import jax
import jax.numpy as jnp
from jax import lax
from jax.experimental import pallas as pl
from jax.experimental.pallas import tpu as pltpu

F32 = jnp.float32
BF16 = jnp.bfloat16
I32 = jnp.int32

HEAD_DIM = 64
MOBA_HEADS = 8
DSA_HEADS = 8
N_HEADS = MOBA_HEADS + DSA_HEADS
MOBA_WIDTH = MOBA_HEADS * HEAD_DIM
DSA_WIDTH = DSA_HEADS * HEAD_DIM
MOBA_BLOCK = 256
MOBA_TOPK = 3
MOBA_MAX_BLOCKS = 64
IDX_HEADS = 4
IDX_DIM = 64
DSA_TOPK_MAX = 256
N_BUCKETS = 32
MAX_DISTANCE = 4096
PEER_HEADS = 8
PEER_NKEYS = 128
PEER_DKEY = 256
PEER_TOPK = 16
EPS = 1e-6
PROJ_SIZES = (MOBA_WIDTH, MOBA_WIDTH, MOBA_WIDTH, DSA_WIDTH, DSA_WIDTH, DSA_WIDTH,
              IDX_HEADS * IDX_DIM, IDX_DIM, IDX_HEADS)
PROJ_WIDTH = sum(PROJ_SIZES)

NEG = -1e30
LOG2E = math.log2(math.e)
ATT_TILE = MOBA_BLOCK
ATT_KEY_TILES_PER_STEP = 4
LANES = 128
MXU_DEPTH = 256
VMEM_LIMIT = 56 * 1024 * 1024

_NT = (((1,), (1,)), ((), ()))


def _far_bias_tiles(tile):
    max_exact = N_BUCKETS // 2
    sat = int(math.ceil(max_exact * (MAX_DISTANCE / max_exact) ** ((N_BUCKETS - max_exact - 1) / (N_BUCKETS - max_exact)))) + 4
    return -(-(sat + tile - 1) // tile)


def _cparams(sem):
    return pltpu.CompilerParams(dimension_semantics=sem, vmem_limit_bytes=VMEM_LIMIT)


def _ada_kernel(c_ref, w_ref, b_ref, o_ref):
    c = c_ref[...]
    sc = c * jax.nn.sigmoid(c)
    o_ref[0] = jnp.dot(sc, w_ref[0], preferred_element_type=F32,
                       precision=lax.Precision.HIGHEST) + b_ref[0]


def _ada_mod(c, w_ada, b_ada):
    depth, d, n = w_ada.shape
    tn = 1536
    c8 = jnp.broadcast_to(c, (8, d))
    out = pl.pallas_call(
        _ada_kernel,
        grid=(depth, n // tn),
        in_specs=[pl.BlockSpec((8, d), lambda l, j: (0, 0)),
                  pl.BlockSpec((1, d, tn), lambda l, j: (l, 0, j)),
                  pl.BlockSpec((1, 1, tn), lambda l, j: (l, 0, j))],
        out_specs=pl.BlockSpec((1, 8, tn), lambda l, j: (l, 0, j)),
        out_shape=jax.ShapeDtypeStruct((depth, 8, n), F32),
        compiler_params=_cparams(("arbitrary", "arbitrary")),
    )(c8, w_ada, b_ada.reshape(depth, 1, n))
    return out[:, 0, :]


def _nmm_kernel(x_ref, g_ref, sh_ref, sc_ref, w_ref, h_ref, o_ref):
    x = x_ref[...]
    y = x * lax.rsqrt(jnp.mean(x * x, axis=-1, keepdims=True) + EPS) * g_ref[...]
    hb = (y * (1.0 + sc_ref[...]) + sh_ref[...]).astype(BF16)
    h_ref[...] = hb
    o_ref[...] = jnp.dot(hb, w_ref[...], preferred_element_type=F32)


def _norm_mod_matmul(x, g, shift, scale, w_bf16, tm=256):
    s, d = x.shape
    n = w_bf16.shape[1]
    row = lambda i: (0, 0)
    return pl.pallas_call(
        _nmm_kernel,
        grid=(s // tm,),
        in_specs=[pl.BlockSpec((tm, d), lambda i: (i, 0)),
                  pl.BlockSpec((1, d), row), pl.BlockSpec((1, d), row), pl.BlockSpec((1, d), row),
                  pl.BlockSpec((d, n), row)],
        out_specs=[pl.BlockSpec((tm, d), lambda i: (i, 0)),
                   pl.BlockSpec((tm, n), lambda i: (i, 0))],
        out_shape=[jax.ShapeDtypeStruct((s, d), BF16), jax.ShapeDtypeStruct((s, n), F32)],
        compiler_params=_cparams(("arbitrary",)),
    )(x, g.reshape(1, d), shift.reshape(1, d), scale.reshape(1, d), w_bf16)


def _t5_bucket(d):
    max_exact = N_BUCKETS // 2
    d = jnp.maximum(d, 0)
    df = jnp.maximum(d, 1).astype(F32)
    large = max_exact + (jnp.log(df / max_exact) / math.log(MAX_DISTANCE / max_exact)
                         * (N_BUCKETS - max_exact)).astype(I32)
    return jnp.where(d < max_exact, d, jnp.minimum(large, N_BUCKETS - 1))


def _bias_tile_kernel(tab_ref, o_ref, *, tile):
    h = pl.program_id(0)
    off = pl.program_id(1)
    b = lax.broadcasted_iota(I32, (tile, tile), 0)
    a = lax.broadcasted_iota(I32, (tile, tile), 1)
    d = off * tile + a - b
    bucket = _t5_bucket(d)
    val = jnp.zeros((tile, tile), F32)
    for k in range(N_BUCKETS):
        val = jnp.where(bucket == k, tab_ref[h, k], val)
    val = (val - tab_ref[h, N_BUCKETS - 1]) * LOG2E
    o_ref[0, 0] = jnp.where(d < 0, NEG, val)


def _bias_tiles(rel_bias, tile):
    n_near = _far_bias_tiles(tile)
    nh = rel_bias.shape[0]
    return pl.pallas_call(
        functools.partial(_bias_tile_kernel, tile=tile),
        grid=(nh, n_near),
        in_specs=[pl.BlockSpec(memory_space=pltpu.SMEM)],
        out_specs=pl.BlockSpec((1, 1, tile, tile), lambda h, o: (h, o, 0, 0)),
        out_shape=jax.ShapeDtypeStruct((nh, n_near, tile, tile), F32),
        compiler_params=_cparams(("arbitrary", "arbitrary")),
    )(rel_bias)


def _kmean_kernel(k_ref, o_ref, *, nb):
    k = k_ref[0]
    km = jnp.mean(k.reshape(nb, MOBA_BLOCK, HEAD_DIM), axis=1)
    o_ref[0] = jnp.zeros(o_ref.shape[1:], F32)
    o_ref[0, :nb, :] = km


def _moba_kmean(k_hsd):
    nh, s, dh = k_hsd.shape
    nb = s // MOBA_BLOCK
    return pl.pallas_call(
        functools.partial(_kmean_kernel, nb=nb),
        grid=(nh,),
        in_specs=[pl.BlockSpec((1, s, dh), lambda h: (h, 0, 0))],
        out_specs=pl.BlockSpec((1, MOBA_MAX_BLOCKS, dh), lambda h: (h, 0, 0)),
        out_shape=jax.ShapeDtypeStruct((nh, MOBA_MAX_BLOCKS, dh), F32),
        compiler_params=_cparams(("arbitrary",)),
    )(k_hsd)


def _moba_sel_kernel(q_ref, km_ref, o_ref, *, tq, n_sel):
    gate = lax.dot_general(km_ref[0], q_ref[0], _NT, preferred_element_type=F32,
                           precision=lax.Precision.HIGHEST)
    t = pl.program_id(1) * tq + lax.broadcasted_iota(I32, (MOBA_MAX_BLOCKS, tq), 1)
    b0 = t // MOBA_BLOCK
    n = lax.broadcasted_iota(I32, (MOBA_MAX_BLOCKS, tq), 0)
    avail = n < b0
    taken = n == b0
    for _ in range(n_sel):
        g = jnp.where(avail, gate, -jnp.inf)
        m = jnp.max(g, axis=0, keepdims=True)
        idx = jnp.min(jnp.where((g == m) & avail, n, 1 << 20), axis=0, keepdims=True)
        pick = n == idx
        taken = taken | pick
        avail = avail & jnp.logical_not(pick)
    o_ref[0] = jnp.where(taken, 0.0, NEG)


def _moba_select(q_hsd, kmean, tq=512):
    nh, s, dh = q_hsd.shape
    nb = s // MOBA_BLOCK
    assert nb <= MOBA_MAX_BLOCKS
    n_sel = max(min(MOBA_TOPK, nb - 1), 1)
    tq = min(tq, s)
    return pl.pallas_call(
        functools.partial(_moba_sel_kernel, tq=tq, n_sel=n_sel),
        grid=(nh, s // tq),
        in_specs=[pl.BlockSpec((1, tq, dh), lambda h, i: (h, i, 0)),
                  pl.BlockSpec((1, MOBA_MAX_BLOCKS, dh), lambda h, i: (h, 0, 0))],
        out_specs=pl.BlockSpec((1, MOBA_MAX_BLOCKS, tq), lambda h, i: (h, 0, i)),
        out_shape=jax.ShapeDtypeStruct((nh, MOBA_MAX_BLOCKS, s), F32),
        compiler_params=_cparams(("arbitrary", "arbitrary")),
    )(q_hsd, kmean)


def _attn_kernel(qi_ref, kg_ref, *refs, nh, n_near, group, has_mask, has_block_sel):
    q_ref, k_ref, vt_ref, bt_ref = refs[:4]
    extra_ref = refs[4] if (has_mask or has_block_sel) else None
    o_ref, m_scr, l_scr, acc_scr = refs[-4:]
    t = ATT_TILE
    step = pl.program_id(0)
    qi = qi_ref[step]
    kg = kg_ref[step]

    @pl.when(kg == 0)
    def _():
        m_scr[...] = jnp.full(m_scr.shape, NEG, F32)
        l_scr[...] = jnp.zeros(l_scr.shape, F32)
        acc_scr[...] = jnp.zeros(acc_scr.shape, F32)

    def tile_update(g, off, with_bias):
        keys = slice(g * t, (g + 1) * t)
        if has_mask:
            mb = extra_ref[0, g].astype(F32)
        scores = []
        for h in range(nh):
            s = lax.dot_general(k_ref[h, keys, :], q_ref[h], _NT, preferred_element_type=F32)
            if with_bias:
                s = s + bt_ref[h, off]
            if has_mask:
                s = s + mb
            if has_block_sel:
                s = s + extra_ref[h, pl.ds(kg * group + g, 1), :]
            scores.append(s)
        probs, alphas = [], []
        for h in range(nh):
            m_old = m_scr[h]
            m_new = jnp.maximum(m_old, jnp.max(scores[h], axis=0, keepdims=True))
            alpha = jnp.exp2(m_old - m_new)
            p = jnp.exp2(scores[h] - m_new)
            l_scr[h] = alpha * l_scr[h] + jnp.sum(p, axis=0, keepdims=True)
            m_scr[h] = m_new
            probs.append(p.astype(BF16))
            alphas.append(alpha)
        for h in range(nh):
            acc_scr[h] = alphas[h] * acc_scr[h] + jnp.dot(vt_ref[h, :, keys], probs[h], preferred_element_type=F32)

    for g in range(group):
        off = qi - (kg * group + g)
        pl.when((off >= 0) & (off < n_near))(functools.partial(tile_update, g, jnp.minimum(off, n_near - 1), True))
        pl.when(off >= n_near)(functools.partial(tile_update, g, off, False))

    @pl.when(kg == qi // group)
    def _():
        for h in range(nh):
            o_ref[h] = acc_scr[h] / l_scr[h]


def _attention(q, k, vt, bias_tiles, head_group, mask_bias=None, block_sel=None):
    assert (mask_bias is None) != (block_sel is None)
    nh, s, dk = q.shape
    dv = vt.shape[1]
    t = ATT_TILE
    group = ATT_KEY_TILES_PER_STEP
    nq = s // t
    n_near = bias_tiles.shape[1]
    qi_list = [i for i in range(nq) for _ in range(i // group + 1)]
    kg_list = [j for i in range(nq) for j in range(i // group + 1)]
    qi_arr = jnp.asarray(qi_list, I32)
    kg_arr = jnp.asarray(kg_list, I32)
    in_specs = [pl.BlockSpec((nh, t, dk), lambda st, qi, kg: (0, qi[st], 0)),
                pl.BlockSpec((nh, group * t, dk), lambda st, qi, kg: (0, kg[st], 0)),
                pl.BlockSpec((nh, dv, group * t), lambda st, qi, kg: (0, 0, kg[st])),
                pl.BlockSpec((nh, n_near, t, t), lambda st, qi, kg: (head_group, 0, 0, 0),
                             pipeline_mode=pl.Buffered(1))]
    args = [q, k, vt, bias_tiles]
    if mask_bias is not None:
        in_specs.append(pl.BlockSpec((1, group, t, t), lambda st, qi, kg: (qi[st], kg[st], 0, 0)))
        args.append(mask_bias)
    else:
        in_specs.append(pl.BlockSpec((nh, MOBA_MAX_BLOCKS, t), lambda st, qi, kg: (0, 0, qi[st])))
        args.append(block_sel)
    grid_spec = pltpu.PrefetchScalarGridSpec(
        num_scalar_prefetch=2,
        grid=(len(qi_list),),
        in_specs=in_specs,
        out_specs=pl.BlockSpec((nh, dv, t), lambda st, qi, kg: (0, 0, qi[st])),
        scratch_shapes=[pltpu.VMEM((nh, 1, t), F32), pltpu.VMEM((nh, 1, t), F32),
                        pltpu.VMEM((nh, dv, t), F32)])
    return pl.pallas_call(
        functools.partial(_attn_kernel, nh=nh, n_near=n_near, group=group,
                          has_mask=mask_bias is not None, has_block_sel=block_sel is not None),
        grid_spec=grid_spec,
        out_shape=jax.ShapeDtypeStruct((nh, dv, s), F32),
        compiler_params=_cparams(("arbitrary",)),
    )(qi_arr, kg_arr, *args)


def _sortable(x):
    b = pltpu.bitcast(jnp.where(x == 0.0, 0.0, x), I32)
    return b ^ ((b >> 31) & 0x7FFFFFFF)


def _dsa_sel_kernel(qi_ref, wt_ref, kn_ref, tri_ref, o_ref, keys_scr, *, tile, topk):
    n_chunks = kn_ref.shape[0]
    i = pl.program_id(0)
    n_live = i + 1
    causal = lax.broadcasted_iota(I32, (tile, tile), 0) <= lax.broadcasted_iota(I32, (tile, tile), 1)
    q_heads = [qi_ref[:, h * IDX_DIM:(h + 1) * IDX_DIM] for h in range(IDX_HEADS)]
    w_heads = [wt_ref[h:h + 1, :] for h in range(IDX_HEADS)]

    def score_chunk(c, diagonal):
        kc = kn_ref[c]
        sc = None
        for h in range(IDX_HEADS):
            r = lax.dot_general(kc, q_heads[h], _NT, preferred_element_type=F32)
            term = w_heads[h] * jnp.maximum(r, 0.0)
            sc = term if sc is None else sc + term
        if diagonal:
            sc = jnp.where(causal, sc, -jnp.inf)
        keys_scr[c] = _sortable(sc)

    def score_body(c, carry):
        score_chunk(c, False)
        return carry

    lax.fori_loop(0, i, score_body, 0)
    score_chunk(i, True)

    def count(pred_fn):
        def body(c, cnt):
            hits = [pred_fn(keys_scr[c, 8 * g:8 * g + 8, :]).astype(I32) for g in range(tile // 8)]
            while len(hits) > 1:
                hits = [hits[j] + hits[j + 1] for j in range(0, len(hits), 2)]
            return cnt + hits[0]
        cnt = lax.fori_loop(0, n_live, body, jnp.zeros((8, tile), I32))
        return jnp.sum(cnt, axis=0, keepdims=True)

    def bit_step(b, ans):
        cand = ans + lax.shift_left(jnp.int32(1), 31 - b)
        cand8 = jnp.broadcast_to(cand, (8, tile))
        total = count(lambda kc: kc >= cand8)
        return jnp.where(total >= topk, cand, ans)

    thr = lax.fori_loop(0, 32, bit_step, jnp.full((1, tile), -2 ** 31, I32))
    thr8 = jnp.broadcast_to(thr, (8, tile))
    n_gt = count(lambda kc: kc > thr8)
    need = (topk - n_gt).astype(F32)

    def emit_chunk(c, seen, diagonal):
        kc = keys_scr[c]
        eq = kc == thr
        eqf = eq.astype(F32)
        rank = seen + jnp.dot(tri_ref[...], eqf.astype(BF16), preferred_element_type=F32)
        sel = (kc > thr) | (eq & (rank < need))
        if diagonal:
            sel = sel & causal
        o_ref[0, c] = jnp.where(sel, 0.0, NEG).astype(BF16)
        return seen + jnp.sum(eqf, axis=0, keepdims=True)

    seen = lax.fori_loop(0, i, lambda c, seen: emit_chunk(c, seen, False), jnp.zeros((1, tile), F32))
    emit_chunk(i, seen, True)

    def fill_chunk(c, carry):
        o_ref[0, c] = jnp.full((tile, tile), NEG, BF16)
        return carry

    lax.fori_loop(n_live, n_chunks, fill_chunk, 0)


def _dsa_select(q_idx_bf16, w_idx, kn_bf16, topk):
    s = q_idx_bf16.shape[0]
    t = ATT_TILE
    n_chunks = s // t
    kn3 = kn_bf16.reshape(n_chunks, t, IDX_DIM)
    wt = jnp.pad(w_idx.T, ((0, 8 - IDX_HEADS), (0, 0)))
    tri = (lax.broadcasted_iota(I32, (t, t), 1) < lax.broadcasted_iota(I32, (t, t), 0)).astype(BF16)
    return pl.pallas_call(
        functools.partial(_dsa_sel_kernel, tile=t, topk=topk),
        grid=(n_chunks,),
        in_specs=[pl.BlockSpec((t, IDX_HEADS * IDX_DIM), lambda i: (i, 0)),
                  pl.BlockSpec((8, t), lambda i: (0, i)),
                  pl.BlockSpec((n_chunks, t, IDX_DIM), lambda i: (0, 0, 0)),
                  pl.BlockSpec((t, t), lambda i: (0, 0))],
        out_specs=pl.BlockSpec((1, n_chunks, t, t), lambda i: (i, 0, 0, 0)),
        out_shape=jax.ShapeDtypeStruct((n_chunks, n_chunks, t, t), BF16),
        scratch_shapes=[pltpu.VMEM((n_chunks, t, t), I32)],
        compiler_params=_cparams(("arbitrary",)),
    )(q_idx_bf16, wt, kn3, tri)


def _out_proj_kernel(y_ref, w_ref, x_ref, g_ref, o_ref):
    o_ref[...] = x_ref[...] + g_ref[...] * jnp.dot(y_ref[...], w_ref[...], preferred_element_type=F32)


def _out_proj(y_bf16, w_bf16, x, gate, tm=512):
    s, d = x.shape
    k = y_bf16.shape[1]
    return pl.pallas_call(
        _out_proj_kernel,
        grid=(s // tm,),
        in_specs=[pl.BlockSpec((tm, k), lambda i: (i, 0)),
                  pl.BlockSpec((k, d), lambda i: (0, 0)),
                  pl.BlockSpec((tm, d), lambda i: (i, 0)),
                  pl.BlockSpec((1, d), lambda i: (0, 0))],
        out_specs=pl.BlockSpec((tm, d), lambda i: (i, 0)),
        out_shape=jax.ShapeDtypeStruct((s, d), F32),
        compiler_params=_cparams(("arbitrary",)),
    )(y_bf16, w_bf16, x, gate.reshape(1, d))


def _cmp_exchange(v, i, j):
    hi = jnp.maximum(v[i], v[j])
    lo = jnp.minimum(v[i], v[j])
    v[i], v[j] = hi, lo


def _bitonic_sort_desc(v):
    n = len(v)
    k = 2
    while k <= n:
        j = k // 2
        while j >= 1:
            for i in range(n):
                l = i ^ j
                if l > i:
                    if (i & k) == 0:
                        _cmp_exchange(v, i, l)
                    else:
                        _cmp_exchange(v, l, i)
            j //= 2
        k *= 2


def _bitonic_merge_desc(v):
    n = len(v)
    j = n // 2
    while j >= 1:
        for i in range(n):
            l = i ^ j
            if l > i:
                _cmp_exchange(v, i, l)
        j //= 2


def _top16_rows(x):
    tt = x.shape[1]
    v = [x[8 * i:8 * i + 8, :] for i in range(PEER_NKEYS // 8)]
    _bitonic_sort_desc(v)
    for shift in (4, 2, 1):
        other = [pltpu.roll(a, shift, 0) for a in v]
        v = [jnp.maximum(v[i], other[len(v) - 1 - i]) for i in range(len(v))]
        _bitonic_merge_desc(v)
    return v


def _peer_route_kernel(q_ref, sk_ref, r1_ref, e1_ref, e0_ref, ns_ref, s0_scr, s1_scr, *, tt):
    sub = lax.broadcasted_iota(I32, (8, tt), 0)
    tops = [[], []]
    for h in range(PEER_HEADS):
        for p in range(2):
            lo = (2 * h + p) * (PEER_DKEY // 2)
            qhp = q_ref[:, lo:lo + PEER_DKEY // 2].astype(BF16)
            sc = lax.dot_general(sk_ref[2 * h + p], qhp, _NT, preferred_element_type=F32)
            (s0_scr if p == 0 else s1_scr)[h] = sc
            tops[p].append(_top16_rows(sc))
    a = [sum(jnp.where(sub == h, tops[0][h][k], 0.0) for h in range(PEER_HEADS)) for k in range(PEER_TOPK)]
    b = [sum(jnp.where(sub == h, tops[1][h][k], 0.0) for h in range(PEER_HEADS)) for k in range(PEER_TOPK)]
    cand = [a[i] + b[j] for i in range(PEER_TOPK) for j in range(PEER_TOPK) if (i + 1) * (j + 1) <= PEER_TOPK]
    cand = cand + [jnp.full((8, tt), -jnp.inf, F32)] * (64 - len(cand))
    _bitonic_sort_desc(cand)
    tau = cand[PEER_TOPK - 1]
    z = sum(jnp.exp(cand[k] - cand[0]) for k in range(PEER_TOPK))
    half_inv_z = 0.5 / z
    for h in range(PEER_HEADS):
        s0 = s0_scr[h]
        s1 = s1_scr[h]
        tau_h = tau[h:h + 1, :]
        nsel = jnp.zeros((PEER_NKEYS, tt), F32)
        rank1 = jnp.zeros((PEER_NKEYS, tt), F32)
        for k in range(PEER_TOPK):
            bk = b[k][h:h + 1, :]
            nsel = nsel + jnp.where(s0 + bk >= tau_h, 1.0, 0.0)
            rank1 = rank1 + jnp.where(bk > s1, 1.0, 0.0)
        ns_ref[h] = nsel
        r1_ref[h] = rank1.astype(BF16)
        e0_ref[h] = jnp.exp(s0 - a[0][h:h + 1, :]) * half_inv_z[h:h + 1, :]
        e1_ref[h] = jnp.exp(s1 - b[0][h:h + 1, :]).astype(BF16)


def _peer_route(q, subkeys_bf16, tt=256):
    s = q.shape[0]
    big = lambda dt: jax.ShapeDtypeStruct((PEER_HEADS, PEER_NKEYS, s), dt)
    big_spec = pl.BlockSpec((PEER_HEADS, PEER_NKEYS, tt), lambda i: (0, 0, i))
    return pl.pallas_call(
        functools.partial(_peer_route_kernel, tt=tt),
        grid=(s // tt,),
        in_specs=[pl.BlockSpec((tt, PEER_HEADS * PEER_DKEY), lambda i: (i, 0)),
                  pl.BlockSpec((2 * PEER_HEADS, PEER_NKEYS, PEER_DKEY // 2), lambda i: (0, 0, 0))],
        out_specs=[big_spec, big_spec, big_spec, big_spec],
        out_shape=[big(BF16), big(BF16), big(F32), big(F32)],
        scratch_shapes=[pltpu.VMEM((PEER_HEADS, PEER_NKEYS, tt), F32),
                        pltpu.VMEM((PEER_HEADS, PEER_NKEYS, tt), F32)],
        compiler_params=_cparams(("arbitrary",)),
    )(q, subkeys_bf16)


def _peer_dense_kernel(h_ref, u_ref, vt_ref, r1_ref, e1_ref, e0_ref, ns_ref,
                       x_ref, g_ref, gf_ref, o_ref, acc_scr, *, rows_per_step, final_norm):
    k = pl.program_id(1)

    @pl.when(k == 0)
    def _():
        acc_scr[...] = jnp.zeros(acc_scr.shape, F32)

    act = lax.dot_general(u_ref[...], h_ref[...], _NT, preferred_element_type=F32)
    tt = act.shape[1]
    pack_rows = 16
    zero = jnp.zeros((pack_rows, tt), BF16)

    def expert_rows(ii):
        i = k * rows_per_step + ii
        bcast = lambda ref, h: jnp.broadcast_to(ref[h, pl.ds(i, 1), :], (pack_rows, tt)).astype(BF16)
        e0_rows = [bcast(e0_ref, h) for h in range(PEER_HEADS)]
        ns_rows = [bcast(ns_ref, h) for h in range(PEER_HEADS)]
        a = act[ii * PEER_NKEYS:(ii + 1) * PEER_NKEYS, :]
        gelu2 = (a * (1.0 + lax.erf(a * math.sqrt(0.5)))).astype(BF16)
        out = []
        for g in range(PEER_NKEYS // pack_rows):
            rows = slice(g * pack_rows, (g + 1) * pack_rows)
            wgt = None
            for h in range(PEER_HEADS):
                term = jnp.where(r1_ref[h, rows, :] < ns_rows[h], e1_ref[h, rows, :], zero) * e0_rows[h]
                wgt = term if wgt is None else wgt + term
            out.append(wgt * gelu2[rows, :])
        return out

    depth_rows = MXU_DEPTH // PEER_NKEYS
    mixed = None
    for c in range(rows_per_step // depth_rows):
        p = jnp.concatenate([r for ii in range(c * depth_rows, (c + 1) * depth_rows) for r in expert_rows(ii)], axis=0)
        d = jnp.dot(vt_ref[:, c * MXU_DEPTH:(c + 1) * MXU_DEPTH], p, preferred_element_type=F32)
        mixed = d if mixed is None else mixed + d
    acc_scr[...] += mixed

    @pl.when(k == pl.num_programs(1) - 1)
    def _():
        y = x_ref[...] + g_ref[...] * acc_scr[...].T
        if final_norm:
            y = y * lax.rsqrt(jnp.mean(y * y, axis=-1, keepdims=True) + EPS) * gf_ref[...]
        o_ref[...] = y


def _peer_dense(h_bf16, u_bf16, vt_bf16, rank1, e1, e0, nsel, x, gate, g_final, final_norm,
                tt=512, rows_per_step=8):
    s, d = x.shape
    n_exp = u_bf16.shape[0]
    et = rows_per_step * PEER_NKEYS
    tok = lambda j, k: (0, 0, j)
    big_spec = pl.BlockSpec((PEER_HEADS, PEER_NKEYS, tt), tok)
    return pl.pallas_call(
        functools.partial(_peer_dense_kernel, rows_per_step=rows_per_step, final_norm=final_norm),
        grid=(s // tt, n_exp // et),
        in_specs=[pl.BlockSpec((tt, d), lambda j, k: (j, 0)),
                  pl.BlockSpec((et, d), lambda j, k: (k, 0)),
                  pl.BlockSpec((d, et), lambda j, k: (0, k)),
                  big_spec, big_spec, big_spec, big_spec,
                  pl.BlockSpec((tt, d), lambda j, k: (j, 0)),
                  pl.BlockSpec((1, d), lambda j, k: (0, 0)),
                  pl.BlockSpec((1, d), lambda j, k: (0, 0))],
        out_specs=pl.BlockSpec((tt, d), lambda j, k: (j, 0)),
        out_shape=jax.ShapeDtypeStruct((s, d), F32),
        scratch_shapes=[pltpu.VMEM((d, tt), F32)],
        compiler_params=_cparams(("arbitrary", "arbitrary")),
    )(h_bf16, u_bf16, vt_bf16, rank1, e1, e0, nsel, x, gate.reshape(1, d), g_final.reshape(1, d))


def _heads_first(t, nh):
    s = t.shape[0]
    return jnp.transpose(t.reshape(s, nh, HEAD_DIM), (1, 0, 2))


def _mixer(h_proj, w_out_bf16, bias_tiles, x, gate):
    s = h_proj.shape[0]
    parts = []
    off = 0
    for size in PROJ_SIZES:
        parts.append(h_proj[:, off:off + size])
        off += size
    mq, mk, mv, dq, dk, dv, iq, ik, iw = parts
    scale = HEAD_DIM ** -0.5 * LOG2E
    dims_first = lambda a, nh: a.T.reshape(nh, HEAD_DIM, s).astype(BF16)

    mq_h, mk_h = _heads_first(mq, MOBA_HEADS), _heads_first(mk, MOBA_HEADS)
    block_sel = _moba_select(mq_h, _moba_kmean(mk_h))
    y_moba = _attention((mq_h * scale).astype(BF16), mk_h.astype(BF16), dims_first(mv, MOBA_HEADS),
                        bias_tiles, 0, block_sel=block_sel)

    ikf = ik.astype(F32)
    ik_n = ikf * lax.rsqrt(jnp.mean(ikf * ikf, axis=-1, keepdims=True) + EPS)
    topk = min(DSA_TOPK_MAX, s // 4)
    mask_bias = _dsa_select(iq.astype(BF16), iw * IDX_HEADS ** -0.5, ik_n.astype(BF16), topk)
    dq_h, dk_h = _heads_first(dq, DSA_HEADS), _heads_first(dk, DSA_HEADS)
    y_dsa = _attention((dq_h * scale).astype(BF16), dk_h.astype(BF16), dims_first(dv, DSA_HEADS),
                       bias_tiles, 1, mask_bias=mask_bias)

    y = jnp.concatenate([y_moba.reshape(MOBA_WIDTH, s), y_dsa.reshape(DSA_WIDTH, s)], axis=0).T
    return _out_proj(y.astype(BF16), w_out_bf16, x, gate)


def kernel(x, c, w_ada, b_ada, norm_attn, norm_ffn, w_in, w_out, rel_bias,
           peer_wq, peer_subkeys, peer_u, peer_v, norm_final):
    batch, s, d = x.shape
    assert batch == 1 and s % (ATT_KEY_TILES_PER_STEP * ATT_TILE) == 0
    depth = w_ada.shape[0]
    x = x[0]
    mod = _ada_mod(c, w_ada, b_ada)
    bias_tiles = _bias_tiles(rel_bias, ATT_TILE)
    n_pad = -(-PROJ_WIDTH // LANES) * LANES
    for l in range(depth):
        sh1, sc1, g1, sh2, sc2, g2 = jnp.split(mod[l], 6)
        w_in_l = jnp.pad(w_in[l], ((0, 0), (0, n_pad - PROJ_WIDTH))).astype(BF16)
        _, proj = _norm_mod_matmul(x, norm_attn[l], sh1, sc1, w_in_l)
        x = _mixer(proj, w_out[l].astype(BF16), bias_tiles, x, g1)
        h2, q = _norm_mod_matmul(x, norm_ffn[l], sh2, sc2, peer_wq[l].astype(BF16))
        subkeys = peer_subkeys[l].reshape(2 * PEER_HEADS, PEER_NKEYS, PEER_DKEY // 2).astype(BF16)
        rank1, e1, e0, nsel = _peer_route(q, subkeys)
        x = _peer_dense(h2, peer_u[l].astype(BF16), peer_v[l].T.astype(BF16), rank1, e1, e0, nsel,
                        x, g2, norm_final, final_norm=(l == depth - 1))
    return x[None]
```

```python
import functools
import math

import jax
import jax.numpy as jnp
from jax import lax
from jax.experimental import pallas as pl
from jax.experimental.pallas import tpu as pltpu

F32 = jnp.float32
BF16 = jnp.bfloat16
I32 = jnp.int32

HEAD_DIM = 64
MOBA_HEADS = 8
DSA_HEADS = 8
N_HEADS = MOBA_HEADS + DSA_HEADS
MOBA_WIDTH = MOBA_HEADS * HEAD_DIM
DSA_WIDTH = DSA_HEADS * HEAD_DIM
MOBA_BLOCK = 256
MOBA_TOPK = 3
MOBA_MAX_BLOCKS = 64
IDX_HEADS = 4
IDX_DIM = 64
DSA_TOPK_MAX = 256
N_BUCKETS = 32
MAX_DISTANCE = 4096
PEER_HEADS = 8
PEER_NKEYS = 128
PEER_DKEY = 256
PEER_TOPK = 16
EPS = 1e-6
PROJ_SIZES = (MOBA_WIDTH, MOBA_WIDTH, MOBA_WIDTH, DSA_WIDTH, DSA_WIDTH, DSA_WIDTH,
              IDX_HEADS * IDX_DIM, IDX_DIM, IDX_HEADS)
PROJ_WIDTH = sum(PROJ_SIZES)

NEG = -1e30
LOG2E = math.log2(math.e)
ATT_TILE = MOBA_BLOCK
ATT_KEY_TILES_PER_STEP = 4
ATT_HEAD_SKEW = 4
LANES = 128
MXU_DEPTH = 256
VMEM_LIMIT = 56 * 1024 * 1024

_NT = (((1,), (1,)), ((), ()))


def _far_bias_tiles(tile):
    max_exact = N_BUCKETS // 2
    sat = int(math.ceil(max_exact * (MAX_DISTANCE / max_exact) ** ((N_BUCKETS - max_exact - 1) / (N_BUCKETS - max_exact)))) + 4
    return -(-(sat + tile - 1) // tile)


def _cparams(sem):
    return pltpu.CompilerParams(dimension_semantics=sem, vmem_limit_bytes=VMEM_LIMIT)


def _ada_kernel(c_ref, w_ref, b_ref, o_ref):
    c = c_ref[...]
    sc = c * jax.nn.sigmoid(c)
    o_ref[0] = jnp.dot(sc, w_ref[0], preferred_element_type=F32,
                       precision=lax.Precision.HIGHEST) + b_ref[0]


def _ada_mod(c, w_ada, b_ada):
    depth, d, n = w_ada.shape
    tn = 1536
    c8 = jnp.broadcast_to(c, (8, d))
    out = pl.pallas_call(
        _ada_kernel,
        grid=(depth, n // tn),
        in_specs=[pl.BlockSpec((8, d), lambda l, j: (0, 0)),
                  pl.BlockSpec((1, d, tn), lambda l, j: (l, 0, j)),
                  pl.BlockSpec((1, 1, tn), lambda l, j: (l, 0, j))],
        out_specs=pl.BlockSpec((1, 8, tn), lambda l, j: (l, 0, j)),
        out_shape=jax.ShapeDtypeStruct((depth, 8, n), F32),
        compiler_params=_cparams(("arbitrary", "arbitrary")),
    )(c8, w_ada, b_ada.reshape(depth, 1, n))
    return out[:, 0, :]


def _nmm_kernel(x_ref, g_ref, sh_ref, sc_ref, w_ref, h_ref, o_ref):
    x = x_ref[...]
    y = x * lax.rsqrt(jnp.mean(x * x, axis=-1, keepdims=True) + EPS) * g_ref[...]
    hb = (y * (1.0 + sc_ref[...]) + sh_ref[...]).astype(BF16)
    h_ref[...] = hb
    o_ref[...] = jnp.dot(hb, w_ref[...], preferred_element_type=F32)


def _norm_mod_matmul(x, g, shift, scale, w_bf16, tm=256):
    s, d = x.shape
    n = w_bf16.shape[1]
    row = lambda i: (0, 0)
    return pl.pallas_call(
        _nmm_kernel,
        grid=(s // tm,),
        in_specs=[pl.BlockSpec((tm, d), lambda i: (i, 0)),
                  pl.BlockSpec((1, d), row), pl.BlockSpec((1, d), row), pl.BlockSpec((1, d), row),
                  pl.BlockSpec((d, n), row)],
        out_specs=[pl.BlockSpec((tm, d), lambda i: (i, 0)),
                   pl.BlockSpec((tm, n), lambda i: (i, 0))],
        out_shape=[jax.ShapeDtypeStruct((s, d), BF16), jax.ShapeDtypeStruct((s, n), F32)],
        compiler_params=_cparams(("arbitrary",)),
    )(x, g.reshape(1, d), shift.reshape(1, d), scale.reshape(1, d), w_bf16)


def _t5_bucket(d):
    max_exact = N_BUCKETS // 2
    d = jnp.maximum(d, 0)
    df = jnp.maximum(d, 1).astype(F32)
    large = max_exact + (jnp.log(df / max_exact) / math.log(MAX_DISTANCE / max_exact)
                         * (N_BUCKETS - max_exact)).astype(I32)
    return jnp.where(d < max_exact, d, jnp.minimum(large, N_BUCKETS - 1))


def _bias_tile_kernel(tab_ref, o_ref, *, tile):
    h = pl.program_id(0)
    off = pl.program_id(1)
    b = lax.broadcasted_iota(I32, (tile, tile), 0)
    a = lax.broadcasted_iota(I32, (tile, tile), 1)
    d = off * tile + a - b
    bucket = _t5_bucket(d)
    val = jnp.zeros((tile, tile), F32)
    for k in range(N_BUCKETS):
        val = jnp.where(bucket == k, tab_ref[h, k], val)
    val = (val - tab_ref[h, N_BUCKETS - 1]) * LOG2E
    o_ref[0, 0] = jnp.where(d < 0, NEG, val)


def _bias_tiles(rel_bias, tile):
    n_near = _far_bias_tiles(tile)
    nh = rel_bias.shape[0]
    return pl.pallas_call(
        functools.partial(_bias_tile_kernel, tile=tile),
        grid=(nh, n_near),
        in_specs=[pl.BlockSpec(memory_space=pltpu.SMEM)],
        out_specs=pl.BlockSpec((1, 1, tile, tile), lambda h, o: (h, o, 0, 0)),
        out_shape=jax.ShapeDtypeStruct((nh, n_near, tile, tile), F32),
        compiler_params=_cparams(("arbitrary", "arbitrary")),
    )(rel_bias)


def _kmean_kernel(k_ref, o_ref, *, nb):
    k = k_ref[0]
    km = jnp.mean(k.reshape(nb, MOBA_BLOCK, HEAD_DIM), axis=1)
    o_ref[0] = jnp.zeros(o_ref.shape[1:], F32)
    o_ref[0, :nb, :] = km


def _moba_kmean(k_hsd):
    nh, s, dh = k_hsd.shape
    nb = s // MOBA_BLOCK
    return pl.pallas_call(
        functools.partial(_kmean_kernel, nb=nb),
        grid=(nh,),
        in_specs=[pl.BlockSpec((1, s, dh), lambda h: (h, 0, 0))],
        out_specs=pl.BlockSpec((1, MOBA_MAX_BLOCKS, dh), lambda h: (h, 0, 0)),
        out_shape=jax.ShapeDtypeStruct((nh, MOBA_MAX_BLOCKS, dh), F32),
        compiler_params=_cparams(("arbitrary",)),
    )(k_hsd)


def _moba_sel_kernel(q_ref, km_ref, o_ref, *, tq, n_sel):
    gate = lax.dot_general(km_ref[0], q_ref[0], _NT, preferred_element_type=F32,
                           precision=lax.Precision.HIGHEST)
    t = pl.program_id(1) * tq + lax.broadcasted_iota(I32, (MOBA_MAX_BLOCKS, tq), 1)
    b0 = t // MOBA_BLOCK
    n = lax.broadcasted_iota(I32, (MOBA_MAX_BLOCKS, tq), 0)
    avail = n < b0
    taken = n == b0
    for _ in range(n_sel):
        g = jnp.where(avail, gate, -jnp.inf)
        m = jnp.max(g, axis=0, keepdims=True)
        idx = jnp.min(jnp.where((g == m) & avail, n, 1 << 20), axis=0, keepdims=True)
        pick = n == idx
        taken = taken | pick
        avail = avail & jnp.logical_not(pick)
    o_ref[0] = jnp.where(taken, 0.0, NEG)


def _moba_select(q_hsd, kmean, tq=512):
    nh, s, dh = q_hsd.shape
    nb = s // MOBA_BLOCK
    assert nb <= MOBA_MAX_BLOCKS
    n_sel = max(min(MOBA_TOPK, nb - 1), 1)
    tq = min(tq, s)
    return pl.pallas_call(
        functools.partial(_moba_sel_kernel, tq=tq, n_sel=n_sel),
        grid=(nh, s // tq),
        in_specs=[pl.BlockSpec((1, tq, dh), lambda h, i: (h, i, 0)),
                  pl.BlockSpec((1, MOBA_MAX_BLOCKS, dh), lambda h, i: (h, 0, 0))],
        out_specs=pl.BlockSpec((1, MOBA_MAX_BLOCKS, tq), lambda h, i: (h, 0, i)),
        out_shape=jax.ShapeDtypeStruct((nh, MOBA_MAX_BLOCKS, s), F32),
        compiler_params=_cparams(("arbitrary", "arbitrary")),
    )(q_hsd, kmean)


def _attn_kernel(qi_ref, kg_ref, *refs, nh, n_near, group, has_mask, has_block_sel):
    q_ref, k_ref, vt_ref, bt_ref = refs[:4]
    extra_ref = refs[4] if (has_mask or has_block_sel) else None
    o_ref, m_scr, l_scr, acc_scr = refs[-4:]
    t = ATT_TILE
    step = pl.program_id(0)
    qi = qi_ref[step]
    kg = kg_ref[step]

    @pl.when(kg == 0)
    def _():
        m_scr[...] = jnp.full(m_scr.shape, NEG, F32)
        l_scr[...] = jnp.zeros(l_scr.shape, F32)
        acc_scr[...] = jnp.zeros(acc_scr.shape, F32)

    def tile_update(g, off, with_bias):
        keys = slice(g * t, (g + 1) * t)
        if has_mask:
            mb = extra_ref[0, g].astype(F32)
        def score(h):
            s = lax.dot_general(k_ref[h, keys, :], q_ref[h], _NT, preferred_element_type=F32)
            if with_bias:
                s = s + bt_ref[h, off]
            if has_mask:
                s = s + mb
            if has_block_sel:
                s = s + extra_ref[h, pl.ds(kg * group + g, 1), :]
            return s

        def softmax_update(h, s):
            m_old = m_scr[h]
            m_new = jnp.maximum(m_old, jnp.max(s, axis=0, keepdims=True))
            alpha = jnp.exp2(m_old - m_new)
            p = jnp.exp2(s - m_new)
            l_scr[h] = alpha * l_scr[h] + jnp.sum(p, axis=0, keepdims=True)
            m_scr[h] = m_new
            return p.astype(BF16), alpha

        def value_update(h, p, alpha):
            acc_scr[h] = alpha * acc_scr[h] + jnp.dot(vt_ref[h, :, keys], p, preferred_element_type=F32)

        scores, probs = {}, {}
        for i in range(nh + 2 * ATT_HEAD_SKEW):
            if i < nh:
                scores[i] = score(i)
            if 0 <= i - ATT_HEAD_SKEW < nh:
                probs[i - ATT_HEAD_SKEW] = softmax_update(i - ATT_HEAD_SKEW, scores.pop(i - ATT_HEAD_SKEW))
            if 0 <= i - 2 * ATT_HEAD_SKEW < nh:
                value_update(i - 2 * ATT_HEAD_SKEW, *probs.pop(i - 2 * ATT_HEAD_SKEW))

    for g in range(group):
        off = qi - (kg * group + g)
        pl.when((off >= 0) & (off < n_near))(functools.partial(tile_update, g, jnp.minimum(off, n_near - 1), True))
        pl.when(off >= n_near)(functools.partial(tile_update, g, off, False))

    @pl.when(kg == qi // group)
    def _():
        for h in range(nh):
            o_ref[h] = acc_scr[h] / l_scr[h]


def _attention(q, k, vt, bias_tiles, head_group, mask_bias=None, block_sel=None):
    assert mask_bias is None or block_sel is None
    nh, s, dk = q.shape
    dv = vt.shape[1]
    t = ATT_TILE
    group = ATT_KEY_TILES_PER_STEP
    nq = s // t
    n_near = bias_tiles.shape[1]
    qi_list = [i for i in range(nq) for _ in range(i // group + 1)]
    kg_list = [j for i in range(nq) for j in range(i // group + 1)]
    qi_arr = jnp.asarray(qi_list, I32)
    kg_arr = jnp.asarray(kg_list, I32)
    in_specs = [pl.BlockSpec((nh, t, dk), lambda st, qi, kg: (0, qi[st], 0)),
                pl.BlockSpec((nh, group * t, dk), lambda st, qi, kg: (0, kg[st], 0)),
                pl.BlockSpec((nh, dv, group * t), lambda st, qi, kg: (0, 0, kg[st])),
                pl.BlockSpec((nh, n_near, t, t), lambda st, qi, kg: (head_group, 0, 0, 0),
                             pipeline_mode=pl.Buffered(1))]
    args = [q, k, vt, bias_tiles]
    if mask_bias is not None:
        in_specs.append(pl.BlockSpec((1, group, t, t), lambda st, qi, kg: (qi[st], kg[st], 0, 0)))
        args.append(mask_bias)
    if block_sel is not None:
        in_specs.append(pl.BlockSpec((nh, MOBA_MAX_BLOCKS, t), lambda st, qi, kg: (0, 0, qi[st])))
        args.append(block_sel)
    grid_spec = pltpu.PrefetchScalarGridSpec(
        num_scalar_prefetch=2,
        grid=(len(qi_list),),
        in_specs=in_specs,
        out_specs=pl.BlockSpec((nh, dv, t), lambda st, qi, kg: (0, 0, qi[st])),
        scratch_shapes=[pltpu.VMEM((nh, 1, t), F32), pltpu.VMEM((nh, 1, t), F32),
                        pltpu.VMEM((nh, dv, t), F32)])
    return pl.pallas_call(
        functools.partial(_attn_kernel, nh=nh, n_near=n_near, group=group,
                          has_mask=mask_bias is not None, has_block_sel=block_sel is not None),
        grid_spec=grid_spec,
        out_shape=jax.ShapeDtypeStruct((nh, dv, s), F32),
        compiler_params=_cparams(("arbitrary",)),
    )(qi_arr, kg_arr, *args)


def _sortable(x):
    b = pltpu.bitcast(jnp.where(x == 0.0, 0.0, x), I32)
    return b ^ ((b >> 31) & 0x7FFFFFFF)


def _dsa_sel_kernel(qi_ref, wt_ref, kn_ref, tri_ref, o_ref, keys_scr, seen_scr, *, tile, topk):
    n_chunks = kn_ref.shape[0]
    i = pl.program_id(0)
    n_live = i + 1
    causal = lax.broadcasted_iota(I32, (tile, tile), 0) <= lax.broadcasted_iota(I32, (tile, tile), 1)
    q_heads = [qi_ref[:, h * IDX_DIM:(h + 1) * IDX_DIM] for h in range(IDX_HEADS)]
    w_heads = [wt_ref[h:h + 1, :] for h in range(IDX_HEADS)]

    def score_chunk(c, diagonal):
        kc = kn_ref[c]
        dots = [lax.dot_general(kc, q_heads[h], _NT, preferred_element_type=F32) for h in range(IDX_HEADS)]
        sc = None
        for h in range(IDX_HEADS):
            term = w_heads[h] * jnp.maximum(dots[h], 0.0)
            sc = term if sc is None else sc + term
        if diagonal:
            sc = jnp.where(causal, sc, -jnp.inf)
        keys_scr[c] = _sortable(sc)

    def for_past_chunks(fn):
        def pair(p, carry):
            fn(2 * p)
            fn(2 * p + 1)
            return carry
        lax.fori_loop(0, i // 2, pair, 0)
        pl.when(i % 2 == 1)(lambda: fn(i - 1))

    for_past_chunks(lambda c: score_chunk(c, False))
    score_chunk(i, True)

    def count(pred_fn):
        def body(c, cnt):
            hits = [pred_fn(keys_scr[c, 8 * g:8 * g + 8, :]).astype(I32) for g in range(tile // 8)]
            while len(hits) > 1:
                hits = [hits[j] + hits[j + 1] for j in range(0, len(hits), 2)]
            return cnt + hits[0]
        cnt = lax.fori_loop(0, n_live, body, jnp.zeros((8, tile), I32))
        return jnp.sum(cnt, axis=0, keepdims=True)

    def bit_step(b, ans):
        cand = ans + lax.shift_left(jnp.int32(1), 31 - b)
        cand8 = jnp.broadcast_to(cand, (8, tile))
        total = count(lambda kc: kc >= cand8)
        return jnp.where(total >= topk, cand, ans)

    thr = lax.fori_loop(0, 32, bit_step, jnp.full((1, tile), -2 ** 31, I32))
    thr8 = jnp.broadcast_to(thr, (8, tile))
    n_gt = count(lambda kc: kc > thr8)
    need = (topk - n_gt).astype(F32)

    seen_scr[...] = jnp.zeros(seen_scr.shape, F32)

    def emit_chunk(c, diagonal):
        kc = keys_scr[c]
        eq = kc == thr
        eqf = eq.astype(F32)
        seen = seen_scr[...]
        seen_scr[...] = seen + jnp.sum(eqf, axis=0, keepdims=True)
        rank = seen + jnp.dot(tri_ref[...], eqf.astype(BF16), preferred_element_type=F32)
        sel = (kc > thr) | (eq & (rank < need))
        if diagonal:
            sel = sel & causal
        o_ref[0, c] = jnp.where(sel, 0.0, NEG).astype(BF16)

    for_past_chunks(lambda c: emit_chunk(c, False))
    emit_chunk(i, True)

    def fill_chunk(c, carry):
        o_ref[0, c] = jnp.full((tile, tile), NEG, BF16)
        return carry

    lax.fori_loop(n_live, n_chunks, fill_chunk, 0)


def _dsa_select(q_idx_bf16, w_idx, kn_bf16, topk):
    s = q_idx_bf16.shape[0]
    t = ATT_TILE
    n_chunks = s // t
    kn3 = kn_bf16.reshape(n_chunks, t, IDX_DIM)
    wt = jnp.pad(w_idx.T, ((0, 8 - IDX_HEADS), (0, 0)))
    tri = (lax.broadcasted_iota(I32, (t, t), 1) < lax.broadcasted_iota(I32, (t, t), 0)).astype(BF16)
    return pl.pallas_call(
        functools.partial(_dsa_sel_kernel, tile=t, topk=topk),
        grid=(n_chunks,),
        in_specs=[pl.BlockSpec((t, IDX_HEADS * IDX_DIM), lambda i: (i, 0)),
                  pl.BlockSpec((8, t), lambda i: (0, i)),
                  pl.BlockSpec((n_chunks, t, IDX_DIM), lambda i: (0, 0, 0)),
                  pl.BlockSpec((t, t), lambda i: (0, 0))],
        out_specs=pl.BlockSpec((1, n_chunks, t, t), lambda i: (i, 0, 0, 0)),
        out_shape=jax.ShapeDtypeStruct((n_chunks, n_chunks, t, t), BF16),
        scratch_shapes=[pltpu.VMEM((n_chunks, t, t), I32), pltpu.VMEM((1, t), F32)],
        compiler_params=_cparams(("arbitrary",)),
    )(q_idx_bf16, wt, kn3, tri)


def _out_proj_kernel(y_ref, w_ref, x_ref, g_ref, o_ref):
    o_ref[...] = x_ref[...] + g_ref[...] * jnp.dot(y_ref[...], w_ref[...], preferred_element_type=F32)


def _out_proj(y_bf16, w_bf16, x, gate, tm=512):
    s, d = x.shape
    k = y_bf16.shape[1]
    return pl.pallas_call(
        _out_proj_kernel,
        grid=(s // tm,),
        in_specs=[pl.BlockSpec((tm, k), lambda i: (i, 0)),
                  pl.BlockSpec((k, d), lambda i: (0, 0)),
                  pl.BlockSpec((tm, d), lambda i: (i, 0)),
                  pl.BlockSpec((1, d), lambda i: (0, 0))],
        out_specs=pl.BlockSpec((tm, d), lambda i: (i, 0)),
        out_shape=jax.ShapeDtypeStruct((s, d), F32),
        compiler_params=_cparams(("arbitrary",)),
    )(y_bf16, w_bf16, x, gate.reshape(1, d))


def _cmp_exchange(v, i, j):
    hi = jnp.maximum(v[i], v[j])
    lo = jnp.minimum(v[i], v[j])
    v[i], v[j] = hi, lo


def _bitonic_sort_desc(v):
    n = len(v)
    k = 2
    while k <= n:
        j = k // 2
        while j >= 1:
            for i in range(n):
                l = i ^ j
                if l > i:
                    if (i & k) == 0:
                        _cmp_exchange(v, i, l)
                    else:
                        _cmp_exchange(v, l, i)
            j //= 2
        k *= 2


def _bitonic_merge_desc(v):
    n = len(v)
    j = n // 2
    while j >= 1:
        for i in range(n):
            l = i ^ j
            if l > i:
                _cmp_exchange(v, i, l)
        j //= 2


def _top16_rows(x):
    tt = x.shape[1]
    v = [x[8 * i:8 * i + 8, :] for i in range(PEER_NKEYS // 8)]
    _bitonic_sort_desc(v)
    for shift in (4, 2, 1):
        other = [pltpu.roll(a, shift, 0) for a in v]
        v = [jnp.maximum(v[i], other[len(v) - 1 - i]) for i in range(len(v))]
        _bitonic_merge_desc(v)
    return v


def _peer_route_kernel(q_ref, sk_ref, r1_ref, e1_ref, e0_ref, ns_ref, s0_scr, s1_scr, *, tt):
    sub = lax.broadcasted_iota(I32, (8, tt), 0)
    tops = [[], []]
    for h in range(PEER_HEADS):
        for p in range(2):
            lo = (2 * h + p) * (PEER_DKEY // 2)
            qhp = q_ref[:, lo:lo + PEER_DKEY // 2].astype(BF16)
            sc = lax.dot_general(sk_ref[2 * h + p], qhp, _NT, preferred_element_type=F32)
            (s0_scr if p == 0 else s1_scr)[h] = sc
            tops[p].append(_top16_rows(sc))
    a = [sum(jnp.where(sub == h, tops[0][h][k], 0.0) for h in range(PEER_HEADS)) for k in range(PEER_TOPK)]
    b = [sum(jnp.where(sub == h, tops[1][h][k], 0.0) for h in range(PEER_HEADS)) for k in range(PEER_TOPK)]
    cand = [a[i] + b[j] for i in range(PEER_TOPK) for j in range(PEER_TOPK) if (i + 1) * (j + 1) <= PEER_TOPK]
    cand = cand + [jnp.full((8, tt), -jnp.inf, F32)] * (64 - len(cand))
    _bitonic_sort_desc(cand)
    tau = cand[PEER_TOPK - 1]
    z = sum(jnp.exp(cand[k] - cand[0]) for k in range(PEER_TOPK))
    half_inv_z = 0.5 / z
    for h in range(PEER_HEADS):
        s0 = s0_scr[h]
        s1 = s1_scr[h]
        tau_h = tau[h:h + 1, :]
        nsel = jnp.zeros((PEER_NKEYS, tt), F32)
        rank1 = jnp.zeros((PEER_NKEYS, tt), F32)
        for k in range(PEER_TOPK):
            bk = b[k][h:h + 1, :]
            nsel = nsel + jnp.where(s0 + bk >= tau_h, 1.0, 0.0)
            rank1 = rank1 + jnp.where(bk > s1, 1.0, 0.0)
        ns_ref[h] = nsel
        r1_ref[h] = rank1.astype(BF16)
        e0_ref[h] = jnp.exp(s0 - a[0][h:h + 1, :]) * half_inv_z[h:h + 1, :]
        e1_ref[h] = jnp.exp(s1 - b[0][h:h + 1, :]).astype(BF16)


def _peer_route(q, subkeys_bf16, tt=256):
    s = q.shape[0]
    big = lambda dt: jax.ShapeDtypeStruct((PEER_HEADS, PEER_NKEYS, s), dt)
    big_spec = pl.BlockSpec((PEER_HEADS, PEER_NKEYS, tt), lambda i: (0, 0, i))
    return pl.pallas_call(
        functools.partial(_peer_route_kernel, tt=tt),
        grid=(s // tt,),
        in_specs=[pl.BlockSpec((tt, PEER_HEADS * PEER_DKEY), lambda i: (i, 0)),
                  pl.BlockSpec((2 * PEER_HEADS, PEER_NKEYS, PEER_DKEY // 2), lambda i: (0, 0, 0))],
        out_specs=[big_spec, big_spec, big_spec, big_spec],
        out_shape=[big(BF16), big(BF16), big(F32), big(F32)],
        scratch_shapes=[pltpu.VMEM((PEER_HEADS, PEER_NKEYS, tt), F32),
                        pltpu.VMEM((PEER_HEADS, PEER_NKEYS, tt), F32)],
        compiler_params=_cparams(("arbitrary",)),
    )(q, subkeys_bf16)


def _peer_dense_kernel(h_ref, u_ref, vt_ref, r1_ref, e1_ref, e0_ref, ns_ref,
                       x_ref, g_ref, gf_ref, o_ref, acc_scr, *, rows_per_step, final_norm):
    k = pl.program_id(1)

    @pl.when(k == 0)
    def _():
        acc_scr[...] = jnp.zeros(acc_scr.shape, F32)

    tt = h_ref.shape[0]
    pack_rows = 16
    zero = jnp.zeros((pack_rows, tt), BF16)

    def activations(c):
        return lax.dot_general(u_ref[c * MXU_DEPTH:(c + 1) * MXU_DEPTH, :], h_ref[...], _NT,
                               preferred_element_type=F32)

    def expert_rows(ii, a):
        i = k * rows_per_step + ii
        bcast = lambda ref, h: jnp.broadcast_to(ref[h, pl.ds(i, 1), :], (pack_rows, tt)).astype(BF16)
        e0_rows = [bcast(e0_ref, h) for h in range(PEER_HEADS)]
        ns_rows = [bcast(ns_ref, h) for h in range(PEER_HEADS)]
        gelu2 = (a * (1.0 + lax.erf(a * math.sqrt(0.5)))).astype(BF16)
        out = []
        for g in range(PEER_NKEYS // pack_rows):
            rows = slice(g * pack_rows, (g + 1) * pack_rows)
            wgt = None
            for h in range(PEER_HEADS):
                term = jnp.where(r1_ref[h, rows, :] < ns_rows[h], e1_ref[h, rows, :], zero) * e0_rows[h]
                wgt = term if wgt is None else wgt + term
            out.append(wgt * gelu2[rows, :])
        return out

    depth_rows = MXU_DEPTH // PEER_NKEYS
    n_chunks = rows_per_step // depth_rows
    mixed = None
    act = activations(0)
    for c in range(n_chunks):
        act_next = activations(c + 1) if c + 1 < n_chunks else None
        p = jnp.concatenate([r for j in range(depth_rows)
                             for r in expert_rows(c * depth_rows + j, act[j * PEER_NKEYS:(j + 1) * PEER_NKEYS, :])],
                            axis=0)
        d = jnp.dot(vt_ref[:, c * MXU_DEPTH:(c + 1) * MXU_DEPTH], p, preferred_element_type=F32)
        mixed = d if mixed is None else mixed + d
        act = act_next
    acc_scr[...] += mixed

    @pl.when(k == pl.num_programs(1) - 1)
    def _():
        y = x_ref[...] + g_ref[...] * acc_scr[...].T
        if final_norm:
            y = y * lax.rsqrt(jnp.mean(y * y, axis=-1, keepdims=True) + EPS) * gf_ref[...]
        o_ref[...] = y


def _peer_dense(h_bf16, u_bf16, vt_bf16, rank1, e1, e0, nsel, x, gate, g_final, final_norm,
                tt=512, rows_per_step=8):
    s, d = x.shape
    n_exp = u_bf16.shape[0]
    et = rows_per_step * PEER_NKEYS
    tok = lambda j, k: (0, 0, j)
    big_spec = pl.BlockSpec((PEER_HEADS, PEER_NKEYS, tt), tok)
    return pl.pallas_call(
        functools.partial(_peer_dense_kernel, rows_per_step=rows_per_step, final_norm=final_norm),
        grid=(s // tt, n_exp // et),
        in_specs=[pl.BlockSpec((tt, d), lambda j, k: (j, 0)),
                  pl.BlockSpec((et, d), lambda j, k: (k, 0)),
                  pl.BlockSpec((d, et), lambda j, k: (0, k)),
                  big_spec, big_spec, big_spec, big_spec,
                  pl.BlockSpec((tt, d), lambda j, k: (j, 0)),
                  pl.BlockSpec((1, d), lambda j, k: (0, 0)),
                  pl.BlockSpec((1, d), lambda j, k: (0, 0))],
        out_specs=pl.BlockSpec((tt, d), lambda j, k: (j, 0)),
        out_shape=jax.ShapeDtypeStruct((s, d), F32),
        scratch_shapes=[pltpu.VMEM((d, tt), F32)],
        compiler_params=_cparams(("arbitrary", "arbitrary")),
    )(h_bf16, u_bf16, vt_bf16, rank1, e1, e0, nsel, x, gate.reshape(1, d), g_final.reshape(1, d))


def _heads_first(t, nh):
    s = t.shape[0]
    return jnp.transpose(t.reshape(s, nh, HEAD_DIM), (1, 0, 2))


def _mixer(h_proj, w_out_bf16, bias_tiles, x, gate):
    s = h_proj.shape[0]
    parts = []
    off = 0
    for size in PROJ_SIZES:
        parts.append(h_proj[:, off:off + size])
        off += size
    mq, mk, mv, dq, dk, dv, iq, ik, iw = parts
    scale = HEAD_DIM ** -0.5 * LOG2E
    dims_first = lambda a, nh: a.T.reshape(nh, HEAD_DIM, s).astype(BF16)

    mq_h, mk_h = _heads_first(mq, MOBA_HEADS), _heads_first(mk, MOBA_HEADS)
    block_sel = _moba_select(mq_h, _moba_kmean(mk_h))
    blk = jnp.arange(s, dtype=I32) // MOBA_BLOCK
    onehot = (blk[:, None] == jnp.arange(MOBA_MAX_BLOCKS, dtype=I32)[None, :]).astype(BF16)
    q_aug = jnp.concatenate([(mq_h * scale).astype(BF16), jnp.transpose(block_sel, (0, 2, 1)).astype(BF16)], axis=-1)
    k_aug = jnp.concatenate([mk_h.astype(BF16), jnp.broadcast_to(onehot, (MOBA_HEADS, s, MOBA_MAX_BLOCKS))], axis=-1)
    y_moba = _attention(q_aug, k_aug, dims_first(mv, MOBA_HEADS), bias_tiles, 0)

    ikf = ik.astype(F32)
    ik_n = ikf * lax.rsqrt(jnp.mean(ikf * ikf, axis=-1, keepdims=True) + EPS)
    topk = min(DSA_TOPK_MAX, s // 4)
    mask_bias = _dsa_select(iq.astype(BF16), iw * IDX_HEADS ** -0.5, ik_n.astype(BF16), topk)
    dq_h, dk_h = _heads_first(dq, DSA_HEADS), _heads_first(dk, DSA_HEADS)
    y_dsa = _attention((dq_h * scale).astype(BF16), dk_h.astype(BF16), dims_first(dv, DSA_HEADS),
                       bias_tiles, 1, mask_bias=mask_bias)

    y = jnp.concatenate([y_moba.reshape(MOBA_WIDTH, s), y_dsa.reshape(DSA_WIDTH, s)], axis=0).T
    return _out_proj(y.astype(BF16), w_out_bf16, x, gate)


def kernel(x, c, w_ada, b_ada, norm_attn, norm_ffn, w_in, w_out, rel_bias,
           peer_wq, peer_subkeys, peer_u, peer_v, norm_final):
    batch, s, d = x.shape
    assert batch == 1 and s % (ATT_KEY_TILES_PER_STEP * ATT_TILE) == 0
    depth = w_ada.shape[0]
    x = x[0]
    mod = _ada_mod(c, w_ada, b_ada)
    bias_tiles = _bias_tiles(rel_bias, ATT_TILE)
    n_pad = -(-PROJ_WIDTH // LANES) * LANES
    for l in range(depth):
        sh1, sc1, g1, sh2, sc2, g2 = jnp.split(mod[l], 6)
        w_in_l = jnp.pad(w_in[l], ((0, 0), (0, n_pad - PROJ_WIDTH))).astype(BF16)
        _, proj = _norm_mod_matmul(x, norm_attn[l], sh1, sc1, w_in_l)
        x = _mixer(proj, w_out[l].astype(BF16), bias_tiles, x, g1)
        h2, q = _norm_mod_matmul(x, norm_ffn[l], sh2, sc2, peer_wq[l].astype(BF16))
        subkeys = peer_subkeys[l].reshape(2 * PEER_HEADS, PEER_NKEYS, PEER_DKEY // 2).astype(BF16)
        rank1, e1, e0, nsel = _peer_route(q, subkeys)
        x = _peer_dense(h2, peer_u[l].astype(BF16), peer_v[l].T.astype(BF16), rank1, e1, e0, nsel,
                        x, g2, norm_final, final_norm=(l == depth - 1))
    return x[None]
```

```python
import functools
import math

import jax
import jax.numpy as jnp
from jax import lax
from jax.experimental import pallas as pl
from jax.experimental.pallas import tpu as pltpu

F32 = jnp.float32
BF16 = jnp.bfloat16
I32 = jnp.int32

HEAD_DIM = 64
MOBA_HEADS = 8
DSA_HEADS = 8
N_HEADS = MOBA_HEADS + DSA_HEADS
MOBA_WIDTH = MOBA_HEADS * HEAD_DIM
DSA_WIDTH = DSA_HEADS * HEAD_DIM
MOBA_BLOCK = 256
MOBA_TOPK = 3
MOBA_MAX_BLOCKS = 64
IDX_HEADS = 4
IDX_DIM = 64
DSA_TOPK_MAX = 256
DSA_SORT_CHUNKS = 8
N_BUCKETS = 32
MAX_DISTANCE = 4096
PEER_HEADS = 8
PEER_NKEYS = 128
PEER_DKEY = 256
PEER_TOPK = 16
EPS = 1e-6
PROJ_SIZES = (MOBA_WIDTH, MOBA_WIDTH, MOBA_WIDTH, DSA_WIDTH, DSA_WIDTH, DSA_WIDTH,
              IDX_HEADS * IDX_DIM, IDX_DIM, IDX_HEADS)
PROJ_WIDTH = sum(PROJ_SIZES)

NEG = -1e30
LOG2E = math.log2(math.e)
ATT_TILE = MOBA_BLOCK
ATT_KEY_TILES_PER_STEP = 4
ATT_HEAD_SKEW = 4
LANES = 128
MXU_DEPTH = 256
VMEM_LIMIT = 56 * 1024 * 1024

_NT = (((1,), (1,)), ((), ()))


def _far_bias_tiles(tile):
    max_exact = N_BUCKETS // 2
    sat = int(math.ceil(max_exact * (MAX_DISTANCE / max_exact) ** ((N_BUCKETS - max_exact - 1) / (N_BUCKETS - max_exact)))) + 4
    return -(-(sat + tile - 1) // tile)


def _cparams(sem):
    return pltpu.CompilerParams(dimension_semantics=sem, vmem_limit_bytes=VMEM_LIMIT)


def _ada_kernel(c_ref, w_ref, b_ref, o_ref):
    c = c_ref[...]
    sc = c * jax.nn.sigmoid(c)
    o_ref[0] = jnp.dot(sc, w_ref[0], preferred_element_type=F32,
                       precision=lax.Precision.HIGHEST) + b_ref[0]


def _ada_mod(c, w_ada, b_ada):
    depth, d, n = w_ada.shape
    tn = 1536
    c8 = jnp.broadcast_to(c, (8, d))
    out = pl.pallas_call(
        _ada_kernel,
        grid=(depth, n // tn),
        in_specs=[pl.BlockSpec((8, d), lambda l, j: (0, 0)),
                  pl.BlockSpec((1, d, tn), lambda l, j: (l, 0, j)),
                  pl.BlockSpec((1, 1, tn), lambda l, j: (l, 0, j))],
        out_specs=pl.BlockSpec((1, 8, tn), lambda l, j: (l, 0, j)),
        out_shape=jax.ShapeDtypeStruct((depth, 8, n), F32),
        compiler_params=_cparams(("arbitrary", "arbitrary")),
    )(c8, w_ada, b_ada.reshape(depth, 1, n))
    return out[:, 0, :]


def _nmm_kernel(x_ref, g_ref, sh_ref, sc_ref, w_ref, h_ref, o_ref):
    x = x_ref[...]
    y = x * lax.rsqrt(jnp.mean(x * x, axis=-1, keepdims=True) + EPS) * g_ref[...]
    hb = (y * (1.0 + sc_ref[...]) + sh_ref[...]).astype(BF16)
    h_ref[...] = hb
    o_ref[...] = jnp.dot(hb, w_ref[...], preferred_element_type=F32)


def _norm_mod_matmul(x, g, shift, scale, w_bf16, tm=256):
    s, d = x.shape
    n = w_bf16.shape[1]
    row = lambda i: (0, 0)
    return pl.pallas_call(
        _nmm_kernel,
        grid=(s // tm,),
        in_specs=[pl.BlockSpec((tm, d), lambda i: (i, 0)),
                  pl.BlockSpec((1, d), row), pl.BlockSpec((1, d), row), pl.BlockSpec((1, d), row),
                  pl.BlockSpec((d, n), row)],
        out_specs=[pl.BlockSpec((tm, d), lambda i: (i, 0)),
                   pl.BlockSpec((tm, n), lambda i: (i, 0))],
        out_shape=[jax.ShapeDtypeStruct((s, d), BF16), jax.ShapeDtypeStruct((s, n), F32)],
        compiler_params=_cparams(("arbitrary",)),
    )(x, g.reshape(1, d), shift.reshape(1, d), scale.reshape(1, d), w_bf16)


def _t5_bucket(d):
    max_exact = N_BUCKETS // 2
    d = jnp.maximum(d, 0)
    df = jnp.maximum(d, 1).astype(F32)
    large = max_exact + (jnp.log(df / max_exact) / math.log(MAX_DISTANCE / max_exact)
                         * (N_BUCKETS - max_exact)).astype(I32)
    return jnp.where(d < max_exact, d, jnp.minimum(large, N_BUCKETS - 1))


def _bias_tile_kernel(tab_ref, o_ref, *, tile):
    h = pl.program_id(0)
    off = pl.program_id(1)
    b = lax.broadcasted_iota(I32, (tile, tile), 0)
    a = lax.broadcasted_iota(I32, (tile, tile), 1)
    d = off * tile + a - b
    bucket = _t5_bucket(d)
    val = jnp.zeros((tile, tile), F32)
    for k in range(N_BUCKETS):
        val = jnp.where(bucket == k, tab_ref[h, k], val)
    val = (val - tab_ref[h, N_BUCKETS - 1]) * LOG2E
    o_ref[0, 0] = jnp.where(d < 0, NEG, val)


def _bias_tiles(rel_bias, tile):
    n_near = _far_bias_tiles(tile)
    nh = rel_bias.shape[0]
    return pl.pallas_call(
        functools.partial(_bias_tile_kernel, tile=tile),
        grid=(nh, n_near),
        in_specs=[pl.BlockSpec(memory_space=pltpu.SMEM)],
        out_specs=pl.BlockSpec((1, 1, tile, tile), lambda h, o: (h, o, 0, 0)),
        out_shape=jax.ShapeDtypeStruct((nh, n_near, tile, tile), F32),
        compiler_params=_cparams(("arbitrary", "arbitrary")),
    )(rel_bias)


def _kmean_kernel(k_ref, o_ref, *, nb):
    k = k_ref[0]
    km = jnp.mean(k.reshape(nb, MOBA_BLOCK, HEAD_DIM), axis=1)
    o_ref[0] = jnp.zeros(o_ref.shape[1:], F32)
    o_ref[0, :nb, :] = km


def _moba_kmean(k_hsd):
    nh, s, dh = k_hsd.shape
    nb = s // MOBA_BLOCK
    return pl.pallas_call(
        functools.partial(_kmean_kernel, nb=nb),
        grid=(nh,),
        in_specs=[pl.BlockSpec((1, s, dh), lambda h: (h, 0, 0))],
        out_specs=pl.BlockSpec((1, MOBA_MAX_BLOCKS, dh), lambda h: (h, 0, 0)),
        out_shape=jax.ShapeDtypeStruct((nh, MOBA_MAX_BLOCKS, dh), F32),
        compiler_params=_cparams(("arbitrary",)),
    )(k_hsd)


def _moba_sel_kernel(q_ref, km_ref, o_ref, *, tq, n_sel):
    gate = lax.dot_general(km_ref[0], q_ref[0], _NT, preferred_element_type=F32,
                           precision=lax.Precision.HIGHEST)
    t = pl.program_id(1) * tq + lax.broadcasted_iota(I32, (MOBA_MAX_BLOCKS, tq), 1)
    b0 = t // MOBA_BLOCK
    n = lax.broadcasted_iota(I32, (MOBA_MAX_BLOCKS, tq), 0)
    avail = n < b0
    taken = n == b0
    for _ in range(n_sel):
        g = jnp.where(avail, gate, -jnp.inf)
        m = jnp.max(g, axis=0, keepdims=True)
        idx = jnp.min(jnp.where((g == m) & avail, n, 1 << 20), axis=0, keepdims=True)
        pick = n == idx
        taken = taken | pick
        avail = avail & jnp.logical_not(pick)
    o_ref[0] = jnp.where(taken, 0.0, NEG)


def _moba_select(q_hsd, kmean, tq=512):
    nh, s, dh = q_hsd.shape
    nb = s // MOBA_BLOCK
    assert nb <= MOBA_MAX_BLOCKS
    n_sel = max(min(MOBA_TOPK, nb - 1), 1)
    tq = min(tq, s)
    return pl.pallas_call(
        functools.partial(_moba_sel_kernel, tq=tq, n_sel=n_sel),
        grid=(nh, s // tq),
        in_specs=[pl.BlockSpec((1, tq, dh), lambda h, i: (h, i, 0)),
                  pl.BlockSpec((1, MOBA_MAX_BLOCKS, dh), lambda h, i: (h, 0, 0))],
        out_specs=pl.BlockSpec((1, MOBA_MAX_BLOCKS, tq), lambda h, i: (h, 0, i)),
        out_shape=jax.ShapeDtypeStruct((nh, MOBA_MAX_BLOCKS, s), F32),
        compiler_params=_cparams(("arbitrary", "arbitrary")),
    )(q_hsd, kmean)


def _attn_kernel(qi_ref, kg_ref, *refs, nh, n_near, group, has_mask, has_block_sel):
    q_ref, k_ref, vt_ref, bt_ref = refs[:4]
    extra_ref = refs[4] if (has_mask or has_block_sel) else None
    o_ref, m_scr, l_scr, acc_scr = refs[-4:]
    t = ATT_TILE
    step = pl.program_id(0)
    qi = qi_ref[step]
    kg = kg_ref[step]

    @pl.when(kg == 0)
    def _():
        m_scr[...] = jnp.full(m_scr.shape, NEG, F32)
        l_scr[...] = jnp.zeros(l_scr.shape, F32)
        acc_scr[...] = jnp.zeros(acc_scr.shape, F32)

    def tile_update(g, off, with_bias):
        keys = slice(g * t, (g + 1) * t)
        if has_mask:
            mb = extra_ref[0, g].astype(F32)
        def score(h):
            s = lax.dot_general(k_ref[h, keys, :], q_ref[h], _NT, preferred_element_type=F32)
            if with_bias:
                s = s + bt_ref[h, off]
            if has_mask:
                s = s + mb
            if has_block_sel:
                s = s + extra_ref[h, pl.ds(kg * group + g, 1), :]
            return s

        def softmax_update(h, s):
            m_old = m_scr[h]
            m_new = jnp.maximum(m_old, jnp.max(s, axis=0, keepdims=True))
            alpha = jnp.exp2(m_old - m_new)
            p = jnp.exp2(s - m_new)
            l_scr[h] = alpha * l_scr[h] + jnp.sum(p, axis=0, keepdims=True)
            m_scr[h] = m_new
            return p.astype(BF16), alpha

        def value_update(h, p, alpha):
            acc_scr[h] = alpha * acc_scr[h] + jnp.dot(vt_ref[h, :, keys], p, preferred_element_type=F32)

        scores, probs = {}, {}
        for i in range(nh + 2 * ATT_HEAD_SKEW):
            if i < nh:
                scores[i] = score(i)
            if 0 <= i - ATT_HEAD_SKEW < nh:
                probs[i - ATT_HEAD_SKEW] = softmax_update(i - ATT_HEAD_SKEW, scores.pop(i - ATT_HEAD_SKEW))
            if 0 <= i - 2 * ATT_HEAD_SKEW < nh:
                value_update(i - 2 * ATT_HEAD_SKEW, *probs.pop(i - 2 * ATT_HEAD_SKEW))

    for g in range(group):
        off = qi - (kg * group + g)
        pl.when((off >= 0) & (off < n_near))(functools.partial(tile_update, g, jnp.minimum(off, n_near - 1), True))
        pl.when(off >= n_near)(functools.partial(tile_update, g, off, False))

    @pl.when(kg == qi // group)
    def _():
        for h in range(nh):
            o_ref[h] = acc_scr[h] / l_scr[h]


def _attention(q, k, vt, bias_tiles, head_group, mask_bias=None, block_sel=None):
    assert mask_bias is None or block_sel is None
    nh, s, dk = q.shape
    dv = vt.shape[1]
    t = ATT_TILE
    group = ATT_KEY_TILES_PER_STEP
    nq = s // t
    n_near = bias_tiles.shape[1]
    qi_list = [i for i in range(nq) for _ in range(i // group + 1)]
    kg_list = [j for i in range(nq) for j in range(i // group + 1)]
    qi_arr = jnp.asarray(qi_list, I32)
    kg_arr = jnp.asarray(kg_list, I32)
    in_specs = [pl.BlockSpec((nh, t, dk), lambda st, qi, kg: (0, qi[st], 0)),
                pl.BlockSpec((nh, group * t, dk), lambda st, qi, kg: (0, kg[st], 0)),
                pl.BlockSpec((nh, dv, group * t), lambda st, qi, kg: (0, 0, kg[st])),
                pl.BlockSpec((nh, n_near, t, t), lambda st, qi, kg: (head_group, 0, 0, 0),
                             pipeline_mode=pl.Buffered(1))]
    args = [q, k, vt, bias_tiles]
    if mask_bias is not None:
        in_specs.append(pl.BlockSpec((1, group, t, t), lambda st, qi, kg: (qi[st], kg[st], 0, 0)))
        args.append(mask_bias)
    if block_sel is not None:
        in_specs.append(pl.BlockSpec((nh, MOBA_MAX_BLOCKS, t), lambda st, qi, kg: (0, 0, qi[st])))
        args.append(block_sel)
    grid_spec = pltpu.PrefetchScalarGridSpec(
        num_scalar_prefetch=2,
        grid=(len(qi_list),),
        in_specs=in_specs,
        out_specs=pl.BlockSpec((nh, dv, t), lambda st, qi, kg: (0, 0, qi[st])),
        scratch_shapes=[pltpu.VMEM((nh, 1, t), F32), pltpu.VMEM((nh, 1, t), F32),
                        pltpu.VMEM((nh, dv, t), F32)])
    return pl.pallas_call(
        functools.partial(_attn_kernel, nh=nh, n_near=n_near, group=group,
                          has_mask=mask_bias is not None, has_block_sel=block_sel is not None),
        grid_spec=grid_spec,
        out_shape=jax.ShapeDtypeStruct((nh, dv, s), F32),
        compiler_params=_cparams(("arbitrary",)),
    )(qi_arr, kg_arr, *args)


def _bitonic_stages(n, full_sort):
    levels = [2 ** p for p in range(1, n.bit_length())] if full_sort else [n]
    return [(k, k >> s) for k in levels for s in range(1, k.bit_length())]


def _plan_passes(stages, block):
    passes = []
    for k, j in stages:
        cross = j >= block
        if passes and passes[-1][2] == cross and (not cross or bin(passes[-1][0] | j).count("1") <= 3):
            passes[-1][0] |= j if cross else 0
            passes[-1][1].append((k, j))
        else:
            passes.append([j if cross else block - 1, [(k, j)], cross])
    return [(mask, sub) for mask, sub, _ in passes]


def _run_pass(load, store, n, mask, stages, regs=32):
    groups = {}
    for e in range(n):
        groups.setdefault(e & ~mask, []).append(e)
    groups = list(groups.values())
    per = max(1, regs // len(groups[0]))
    for g0 in range(0, len(groups), per):
        elems = sorted(e for grp in groups[g0:g0 + per] for e in grp)
        vals = {e: load(e) for e in elems}
        for k, j in stages:
            for e in elems:
                l = e ^ j
                if l > e:
                    hi = jnp.maximum(vals[e], vals[l])
                    lo = jnp.minimum(vals[e], vals[l])
                    vals[e], vals[l] = (hi, lo) if (e & k) == 0 else (lo, hi)
        for e in elems:
            store(e, vals[e])


def _sortable(x):
    return jnp.where(x == 0.0, 0.0, x)


def _dsa_sel_kernel(qi_ref, wt_ref, kn_ref, tri_ref, o_ref, keys_scr, seen_scr, work_scr, top_scr, *, tile, topk):
    n_chunks = kn_ref.shape[0]
    i = pl.program_id(0)
    n_live = i + 1
    groups_per_chunk = tile // 8
    causal = lax.broadcasted_iota(I32, (tile, tile), 0) <= lax.broadcasted_iota(I32, (tile, tile), 1)
    q_heads = [qi_ref[:, h * IDX_DIM:(h + 1) * IDX_DIM] for h in range(IDX_HEADS)]
    w_heads = [wt_ref[h:h + 1, :] for h in range(IDX_HEADS)]

    def score_chunk(c, diagonal):
        kc = kn_ref[c]
        dots = [lax.dot_general(kc, q_heads[h], _NT, preferred_element_type=F32) for h in range(IDX_HEADS)]
        sc = None
        for h in range(IDX_HEADS):
            term = w_heads[h] * jnp.maximum(dots[h], 0.0)
            sc = term if sc is None else sc + term
        if diagonal:
            sc = jnp.where(causal, sc, -jnp.inf)
        keys_scr[c] = _sortable(sc)

    def for_past_chunks(fn):
        def pair(p, carry):
            fn(2 * p)
            fn(2 * p + 1)
            return carry
        lax.fori_loop(0, i // 2, pair, 0)
        pl.when(i % 2 == 1)(lambda: fn(i - 1))

    for_past_chunks(lambda c: score_chunk(c, False))
    score_chunk(i, True)

    def count(pred_fn):
        def body(c, cnt):
            hits = [pred_fn(keys_scr[c, 8 * g:8 * g + 8, :]).astype(I32) for g in range(tile // 8)]
            while len(hits) > 1:
                hits = [hits[j] + hits[j + 1] for j in range(0, len(hits), 2)]
            return cnt + hits[0]
        cnt = lax.fori_loop(0, n_live, body, jnp.zeros((8, tile), I32))
        return jnp.sum(cnt, axis=0, keepdims=True)

    n_list = DSA_SORT_CHUNKS * groups_per_chunk
    n_batches = (n_live + DSA_SORT_CHUNKS - 1) // DSA_SORT_CHUNKS

    def pad_chunk(c, carry):
        keys_scr[c] = jnp.full((tile, tile), -jnp.inf, F32)
        return carry

    lax.fori_loop(n_live, n_batches * DSA_SORT_CHUNKS, pad_chunk, 0)
    top_scr[...] = jnp.full(top_scr.shape, -jnp.inf, F32)
    sort_passes = _plan_passes(_bitonic_stages(n_list, True), groups_per_chunk)
    merge_passes = _plan_passes(_bitonic_stages(n_list, False), groups_per_chunk)
    lane_halves = [slice(lo, lo + LANES) for lo in range(0, tile, LANES)]

    def run_network(passes, first_load, first_store, buf, lanes):
        def load(e):
            return buf[e, :, lanes]

        def store(e, v):
            buf[e, :, lanes] = v

        for n_pass, (mask, stages) in enumerate(passes):
            _run_pass(first_load if n_pass == 0 else load, first_store if n_pass == 0 else store,
                      n_list, mask, stages)

    def sort_batch(b, carry):
        for lanes in lane_halves:
            def load_keys(e, lanes=lanes):
                g = e % groups_per_chunk
                return keys_scr[b * DSA_SORT_CHUNKS + e // groups_per_chunk, 8 * g:8 * g + 8, lanes]

            def store_work(e, v, lanes=lanes):
                work_scr[e, :, lanes] = v

            run_network(sort_passes, load_keys, store_work, work_scr, lanes)

            def load_best(e, lanes=lanes):
                return jnp.maximum(top_scr[e, :, lanes], work_scr[n_list - 1 - e, :, lanes])

            def store_top(e, v, lanes=lanes):
                top_scr[e, :, lanes] = v

            run_network(merge_passes, load_best, store_top, top_scr, lanes)
        return carry

    lax.fori_loop(0, n_batches, sort_batch, 0)

    src, dst = top_scr, work_scr
    for shift in (4, 2, 1):
        for lanes in lane_halves:
            def load_pair(e, lanes=lanes, src=src, shift=shift):
                return jnp.maximum(src[e, :, lanes], pltpu.roll(src[n_list - 1 - e, :, lanes], shift, 0))

            def store_dst(e, v, lanes=lanes, dst=dst):
                dst[e, :, lanes] = v

            run_network(merge_passes, load_pair, store_dst, dst, lanes)
        src, dst = dst, src
    thr = src[topk - 1, 0:1, :]
    thr8 = jnp.broadcast_to(thr, (8, tile))
    n_gt = count(lambda kc: kc > thr8)
    need = (topk - n_gt).astype(F32)

    seen_scr[...] = jnp.zeros(seen_scr.shape, F32)

    def emit_chunk(c, diagonal):
        kc = keys_scr[c]
        eq = kc == thr
        eqf = eq.astype(F32)
        seen = seen_scr[...]
        seen_scr[...] = seen + jnp.sum(eqf, axis=0, keepdims=True)
        rank = seen + jnp.dot(tri_ref[...], eqf.astype(BF16), preferred_element_type=F32)
        sel = (kc > thr) | (eq & (rank < need))
        if diagonal:
            sel = sel & causal
        o_ref[0, c] = jnp.where(sel, 0.0, NEG).astype(BF16)

    for_past_chunks(lambda c: emit_chunk(c, False))
    emit_chunk(i, True)

    def fill_chunk(c, carry):
        o_ref[0, c] = jnp.full((tile, tile), NEG, BF16)
        return carry

    lax.fori_loop(n_live, n_chunks, fill_chunk, 0)


def _dsa_select(q_idx_bf16, w_idx, kn_bf16, topk):
    s = q_idx_bf16.shape[0]
    t = ATT_TILE
    n_chunks = s // t
    n_list = DSA_SORT_CHUNKS * (t // 8)
    assert topk <= n_list, "the sorting network keeps n_list best keys per query"
    kn3 = kn_bf16.reshape(n_chunks, t, IDX_DIM)
    wt = jnp.pad(w_idx.T, ((0, 8 - IDX_HEADS), (0, 0)))
    tri = (lax.broadcasted_iota(I32, (t, t), 1) < lax.broadcasted_iota(I32, (t, t), 0)).astype(BF16)
    return pl.pallas_call(
        functools.partial(_dsa_sel_kernel, tile=t, topk=topk),
        grid=(n_chunks,),
        in_specs=[pl.BlockSpec((t, IDX_HEADS * IDX_DIM), lambda i: (i, 0)),
                  pl.BlockSpec((8, t), lambda i: (0, i)),
                  pl.BlockSpec((n_chunks, t, IDX_DIM), lambda i: (0, 0, 0)),
                  pl.BlockSpec((t, t), lambda i: (0, 0))],
        out_specs=pl.BlockSpec((1, n_chunks, t, t), lambda i: (i, 0, 0, 0)),
        out_shape=jax.ShapeDtypeStruct((n_chunks, n_chunks, t, t), BF16),
        scratch_shapes=[pltpu.VMEM((max(n_chunks, DSA_SORT_CHUNKS), t, t), F32), pltpu.VMEM((1, t), F32),
                        pltpu.VMEM((n_list, 8, t), F32), pltpu.VMEM((n_list, 8, t), F32)],
        compiler_params=_cparams(("arbitrary",)),
    )(q_idx_bf16, wt, kn3, tri)


def _out_proj_kernel(y_ref, w_ref, x_ref, g_ref, o_ref):
    o_ref[...] = x_ref[...] + g_ref[...] * jnp.dot(y_ref[...], w_ref[...], preferred_element_type=F32)


def _out_proj(y_bf16, w_bf16, x, gate, tm=512):
    s, d = x.shape
    k = y_bf16.shape[1]
    return pl.pallas_call(
        _out_proj_kernel,
        grid=(s // tm,),
        in_specs=[pl.BlockSpec((tm, k), lambda i: (i, 0)),
                  pl.BlockSpec((k, d), lambda i: (0, 0)),
                  pl.BlockSpec((tm, d), lambda i: (i, 0)),
                  pl.BlockSpec((1, d), lambda i: (0, 0))],
        out_specs=pl.BlockSpec((tm, d), lambda i: (i, 0)),
        out_shape=jax.ShapeDtypeStruct((s, d), F32),
        compiler_params=_cparams(("arbitrary",)),
    )(y_bf16, w_bf16, x, gate.reshape(1, d))


def _cmp_exchange(v, i, j):
    hi = jnp.maximum(v[i], v[j])
    lo = jnp.minimum(v[i], v[j])
    v[i], v[j] = hi, lo


def _bitonic_sort_desc(v):
    n = len(v)
    k = 2
    while k <= n:
        j = k // 2
        while j >= 1:
            for i in range(n):
                l = i ^ j
                if l > i:
                    if (i & k) == 0:
                        _cmp_exchange(v, i, l)
                    else:
                        _cmp_exchange(v, l, i)
            j //= 2
        k *= 2


def _bitonic_merge_desc(v):
    n = len(v)
    j = n // 2
    while j >= 1:
        for i in range(n):
            l = i ^ j
            if l > i:
                _cmp_exchange(v, i, l)
        j //= 2


def _top16_rows(x):
    tt = x.shape[1]
    v = [x[8 * i:8 * i + 8, :] for i in range(PEER_NKEYS // 8)]
    _bitonic_sort_desc(v)
    for shift in (4, 2, 1):
        other = [pltpu.roll(a, shift, 0) for a in v]
        v = [jnp.maximum(v[i], other[len(v) - 1 - i]) for i in range(len(v))]
        _bitonic_merge_desc(v)
    return v


def _peer_route_kernel(q_ref, sk_ref, r1_ref, e1_ref, e0_ref, ns_ref, s0_scr, s1_scr, *, tt):
    sub = lax.broadcasted_iota(I32, (8, tt), 0)
    tops = [[], []]
    for h in range(PEER_HEADS):
        for p in range(2):
            lo = (2 * h + p) * (PEER_DKEY // 2)
            qhp = q_ref[:, lo:lo + PEER_DKEY // 2].astype(BF16)
            sc = lax.dot_general(sk_ref[2 * h + p], qhp, _NT, preferred_element_type=F32)
            (s0_scr if p == 0 else s1_scr)[h] = sc
            tops[p].append(_top16_rows(sc))
    a = [sum(jnp.where(sub == h, tops[0][h][k], 0.0) for h in range(PEER_HEADS)) for k in range(PEER_TOPK)]
    b = [sum(jnp.where(sub == h, tops[1][h][k], 0.0) for h in range(PEER_HEADS)) for k in range(PEER_TOPK)]
    cand = [a[i] + b[j] for i in range(PEER_TOPK) for j in range(PEER_TOPK) if (i + 1) * (j + 1) <= PEER_TOPK]
    cand = cand + [jnp.full((8, tt), -jnp.inf, F32)] * (64 - len(cand))
    _bitonic_sort_desc(cand)
    tau = cand[PEER_TOPK - 1]
    z = sum(jnp.exp(cand[k] - cand[0]) for k in range(PEER_TOPK))
    half_inv_z = 0.5 / z
    for h in range(PEER_HEADS):
        s0 = s0_scr[h]
        s1 = s1_scr[h]
        tau_h = tau[h:h + 1, :]
        nsel = jnp.zeros((PEER_NKEYS, tt), F32)
        rank1 = jnp.zeros((PEER_NKEYS, tt), F32)
        for k in range(PEER_TOPK):
            bk = b[k][h:h + 1, :]
            nsel = nsel + jnp.where(s0 + bk >= tau_h, 1.0, 0.0)
            rank1 = rank1 + jnp.where(bk > s1, 1.0, 0.0)
        ns_ref[h] = nsel
        r1_ref[h] = rank1.astype(BF16)
        e0_ref[h] = jnp.exp(s0 - a[0][h:h + 1, :]) * half_inv_z[h:h + 1, :]
        e1_ref[h] = jnp.exp(s1 - b[0][h:h + 1, :]).astype(BF16)


def _peer_route(q, subkeys_bf16, tt=256):
    s = q.shape[0]
    big = lambda dt: jax.ShapeDtypeStruct((PEER_HEADS, PEER_NKEYS, s), dt)
    big_spec = pl.BlockSpec((PEER_HEADS, PEER_NKEYS, tt), lambda i: (0, 0, i))
    return pl.pallas_call(
        functools.partial(_peer_route_kernel, tt=tt),
        grid=(s // tt,),
        in_specs=[pl.BlockSpec((tt, PEER_HEADS * PEER_DKEY), lambda i: (i, 0)),
                  pl.BlockSpec((2 * PEER_HEADS, PEER_NKEYS, PEER_DKEY // 2), lambda i: (0, 0, 0))],
        out_specs=[big_spec, big_spec, big_spec, big_spec],
        out_shape=[big(BF16), big(BF16), big(F32), big(F32)],
        scratch_shapes=[pltpu.VMEM((PEER_HEADS, PEER_NKEYS, tt), F32),
                        pltpu.VMEM((PEER_HEADS, PEER_NKEYS, tt), F32)],
        compiler_params=_cparams(("arbitrary",)),
    )(q, subkeys_bf16)


def _peer_dense_kernel(h_ref, u_ref, vt_ref, r1_ref, e1_ref, e0_ref, ns_ref,
                       x_ref, g_ref, gf_ref, o_ref, acc_scr, *, rows_per_step, final_norm):
    k = pl.program_id(1)

    @pl.when(k == 0)
    def _():
        acc_scr[...] = jnp.zeros(acc_scr.shape, F32)

    tt = h_ref.shape[0]
    pack_rows = 16
    zero = jnp.zeros((pack_rows, tt), BF16)

    def activations(c):
        return lax.dot_general(u_ref[c * MXU_DEPTH:(c + 1) * MXU_DEPTH, :], h_ref[...], _NT,
                               preferred_element_type=F32)

    def expert_rows(ii, a):
        i = k * rows_per_step + ii
        bcast = lambda ref, h: jnp.broadcast_to(ref[h, pl.ds(i, 1), :], (pack_rows, tt)).astype(BF16)
        e0_rows = [bcast(e0_ref, h) for h in range(PEER_HEADS)]
        ns_rows = [bcast(ns_ref, h) for h in range(PEER_HEADS)]
        gelu2 = (a * (1.0 + lax.erf(a * math.sqrt(0.5)))).astype(BF16)
        out = []
        for g in range(PEER_NKEYS // pack_rows):
            rows = slice(g * pack_rows, (g + 1) * pack_rows)
            wgt = None
            for h in range(PEER_HEADS):
                term = jnp.where(r1_ref[h, rows, :] < ns_rows[h], e1_ref[h, rows, :], zero) * e0_rows[h]
                wgt = term if wgt is None else wgt + term
            out.append(wgt * gelu2[rows, :])
        return out

    depth_rows = MXU_DEPTH // PEER_NKEYS
    n_chunks = rows_per_step // depth_rows
    mixed = None
    act = activations(0)
    for c in range(n_chunks):
        act_next = activations(c + 1) if c + 1 < n_chunks else None
        p = jnp.concatenate([r for j in range(depth_rows)
                             for r in expert_rows(c * depth_rows + j, act[j * PEER_NKEYS:(j + 1) * PEER_NKEYS, :])],
                            axis=0)
        d = jnp.dot(vt_ref[:, c * MXU_DEPTH:(c + 1) * MXU_DEPTH], p, preferred_element_type=F32)
        mixed = d if mixed is None else mixed + d
        act = act_next
    acc_scr[...] += mixed

    @pl.when(k == pl.num_programs(1) - 1)
    def _():
        y = x_ref[...] + g_ref[...] * acc_scr[...].T
        if final_norm:
            y = y * lax.rsqrt(jnp.mean(y * y, axis=-1, keepdims=True) + EPS) * gf_ref[...]
        o_ref[...] = y


def _peer_dense(h_bf16, u_bf16, vt_bf16, rank1, e1, e0, nsel, x, gate, g_final, final_norm,
                tt=512, rows_per_step=8):
    s, d = x.shape
    n_exp = u_bf16.shape[0]
    et = rows_per_step * PEER_NKEYS
    tok = lambda j, k: (0, 0, j)
    big_spec = pl.BlockSpec((PEER_HEADS, PEER_NKEYS, tt), tok)
    return pl.pallas_call(
        functools.partial(_peer_dense_kernel, rows_per_step=rows_per_step, final_norm=final_norm),
        grid=(s // tt, n_exp // et),
        in_specs=[pl.BlockSpec((tt, d), lambda j, k: (j, 0)),
                  pl.BlockSpec((et, d), lambda j, k: (k, 0)),
                  pl.BlockSpec((d, et), lambda j, k: (0, k)),
                  big_spec, big_spec, big_spec, big_spec,
                  pl.BlockSpec((tt, d), lambda j, k: (j, 0)),
                  pl.BlockSpec((1, d), lambda j, k: (0, 0)),
                  pl.BlockSpec((1, d), lambda j, k: (0, 0))],
        out_specs=pl.BlockSpec((tt, d), lambda j, k: (j, 0)),
        out_shape=jax.ShapeDtypeStruct((s, d), F32),
        scratch_shapes=[pltpu.VMEM((d, tt), F32)],
        compiler_params=_cparams(("arbitrary", "arbitrary")),
    )(h_bf16, u_bf16, vt_bf16, rank1, e1, e0, nsel, x, gate.reshape(1, d), g_final.reshape(1, d))


def _heads_first(t, nh):
    s = t.shape[0]
    return jnp.transpose(t.reshape(s, nh, HEAD_DIM), (1, 0, 2))


def _mixer(h_proj, w_out_bf16, bias_tiles, x, gate):
    s = h_proj.shape[0]
    parts = []
    off = 0
    for size in PROJ_SIZES:
        parts.append(h_proj[:, off:off + size])
        off += size
    mq, mk, mv, dq, dk, dv, iq, ik, iw = parts
    scale = HEAD_DIM ** -0.5 * LOG2E
    dims_first = lambda a, nh: a.T.reshape(nh, HEAD_DIM, s).astype(BF16)

    mq_h, mk_h = _heads_first(mq, MOBA_HEADS), _heads_first(mk, MOBA_HEADS)
    block_sel = _moba_select(mq_h, _moba_kmean(mk_h))
    blk = jnp.arange(s, dtype=I32) // MOBA_BLOCK
    onehot = (blk[:, None] == jnp.arange(MOBA_MAX_BLOCKS, dtype=I32)[None, :]).astype(BF16)
    q_aug = jnp.concatenate([(mq_h * scale).astype(BF16), jnp.transpose(block_sel, (0, 2, 1)).astype(BF16)], axis=-1)
    k_aug = jnp.concatenate([mk_h.astype(BF16), jnp.broadcast_to(onehot, (MOBA_HEADS, s, MOBA_MAX_BLOCKS))], axis=-1)
    y_moba = _attention(q_aug, k_aug, dims_first(mv, MOBA_HEADS), bias_tiles, 0)

    ikf = ik.astype(F32)
    ik_n = ikf * lax.rsqrt(jnp.mean(ikf * ikf, axis=-1, keepdims=True) + EPS)
    topk = min(DSA_TOPK_MAX, s // 4)
    mask_bias = _dsa_select(iq.astype(BF16), iw * IDX_HEADS ** -0.5, ik_n.astype(BF16), topk)
    dq_h, dk_h = _heads_first(dq, DSA_HEADS), _heads_first(dk, DSA_HEADS)
    y_dsa = _attention((dq_h * scale).astype(BF16), dk_h.astype(BF16), dims_first(dv, DSA_HEADS),
                       bias_tiles, 1, mask_bias=mask_bias)

    y = jnp.concatenate([y_moba.reshape(MOBA_WIDTH, s), y_dsa.reshape(DSA_WIDTH, s)], axis=0).T
    return _out_proj(y.astype(BF16), w_out_bf16, x, gate)


def kernel(x, c, w_ada, b_ada, norm_attn, norm_ffn, w_in, w_out, rel_bias,
           peer_wq, peer_subkeys, peer_u, peer_v, norm_final):
    batch, s, d = x.shape
    assert batch == 1 and s % (ATT_KEY_TILES_PER_STEP * ATT_TILE) == 0
    depth = w_ada.shape[0]
    x = x[0]
    mod = _ada_mod(c, w_ada, b_ada)
    bias_tiles = _bias_tiles(rel_bias, ATT_TILE)
    n_pad = -(-PROJ_WIDTH // LANES) * LANES
    for l in range(depth):
        sh1, sc1, g1, sh2, sc2, g2 = jnp.split(mod[l], 6)
        w_in_l = jnp.pad(w_in[l], ((0, 0), (0, n_pad - PROJ_WIDTH))).astype(BF16)
        _, proj = _norm_mod_matmul(x, norm_attn[l], sh1, sc1, w_in_l)
        x = _mixer(proj, w_out[l].astype(BF16), bias_tiles, x, g1)
        h2, q = _norm_mod_matmul(x, norm_ffn[l], sh2, sc2, peer_wq[l].astype(BF16))
        subkeys = peer_subkeys[l].reshape(2 * PEER_HEADS, PEER_NKEYS, PEER_DKEY // 2).astype(BF16)
        rank1, e1, e0, nsel = _peer_route(q, subkeys)
        x = _peer_dense(h2, peer_u[l].astype(BF16), peer_v[l].T.astype(BF16), rank1, e1, e0, nsel,
                        x, g2, norm_final, final_norm=(l == depth - 1))
    return x[None]
```

```python
import functools
import math

import jax
import jax.numpy as jnp
from jax import lax
from jax.experimental import pallas as pl
from jax.experimental.pallas import tpu as pltpu

F32 = jnp.float32
BF16 = jnp.bfloat16
I32 = jnp.int32

HEAD_DIM = 64
MOBA_HEADS = 8
DSA_HEADS = 8
N_HEADS = MOBA_HEADS + DSA_HEADS
MOBA_WIDTH = MOBA_HEADS * HEAD_DIM
DSA_WIDTH = DSA_HEADS * HEAD_DIM
MOBA_BLOCK = 256
MOBA_TOPK = 3
MOBA_MAX_BLOCKS = 64
IDX_HEADS = 4
IDX_DIM = 64
DSA_TOPK_MAX = 256
DSA_SORT_CHUNKS = 8
N_BUCKETS = 32
MAX_DISTANCE = 4096
PEER_HEADS = 8
PEER_NKEYS = 128
PEER_DKEY = 256
PEER_TOPK = 16
EPS = 1e-6
PROJ_SIZES = (MOBA_WIDTH, MOBA_WIDTH, MOBA_WIDTH, DSA_WIDTH, DSA_WIDTH, DSA_WIDTH,
              IDX_HEADS * IDX_DIM, IDX_DIM, IDX_HEADS)
PROJ_WIDTH = sum(PROJ_SIZES)

NEG = -1e30
LOG2E = math.log2(math.e)
ATT_TILE = MOBA_BLOCK
ATT_KEY_TILES_PER_STEP = 4
LANES = 128
MXU_DEPTH = 256
VMEM_LIMIT = 56 * 1024 * 1024

_NT = (((1,), (1,)), ((), ()))


def _far_bias_tiles(tile):
    max_exact = N_BUCKETS // 2
    sat = int(math.ceil(max_exact * (MAX_DISTANCE / max_exact) ** ((N_BUCKETS - max_exact - 1) / (N_BUCKETS - max_exact)))) + 4
    return -(-(sat + tile - 1) // tile)


def _cparams(sem):
    return pltpu.CompilerParams(dimension_semantics=sem, vmem_limit_bytes=VMEM_LIMIT)


def _ada_kernel(c_ref, w_ref, b_ref, o_ref):
    c = c_ref[...]
    sc = c * jax.nn.sigmoid(c)
    o_ref[0] = jnp.dot(sc, w_ref[0], preferred_element_type=F32,
                       precision=lax.Precision.HIGHEST) + b_ref[0]


def _ada_mod(c, w_ada, b_ada):
    depth, d, n = w_ada.shape
    tn = 1536
    c8 = jnp.broadcast_to(c, (8, d))
    out = pl.pallas_call(
        _ada_kernel,
        grid=(depth, n // tn),
        in_specs=[pl.BlockSpec((8, d), lambda l, j: (0, 0)),
                  pl.BlockSpec((1, d, tn), lambda l, j: (l, 0, j)),
                  pl.BlockSpec((1, 1, tn), lambda l, j: (l, 0, j))],
        out_specs=pl.BlockSpec((1, 8, tn), lambda l, j: (l, 0, j)),
        out_shape=jax.ShapeDtypeStruct((depth, 8, n), F32),
        compiler_params=_cparams(("arbitrary", "arbitrary")),
    )(c8, w_ada, b_ada.reshape(depth, 1, n))
    return out[:, 0, :]


def _nmm_kernel(x_ref, g_ref, sh_ref, sc_ref, w_ref, h_ref, o_ref):
    x = x_ref[...]
    y = x * lax.rsqrt(jnp.mean(x * x, axis=-1, keepdims=True) + EPS) * g_ref[...]
    hb = (y * (1.0 + sc_ref[...]) + sh_ref[...]).astype(BF16)
    h_ref[...] = hb
    o_ref[...] = jnp.dot(hb, w_ref[...], preferred_element_type=F32)


def _norm_mod_matmul(x, g, shift, scale, w_bf16, tm=256):
    s, d = x.shape
    n = w_bf16.shape[1]
    row = lambda i: (0, 0)
    return pl.pallas_call(
        _nmm_kernel,
        grid=(s // tm,),
        in_specs=[pl.BlockSpec((tm, d), lambda i: (i, 0)),
                  pl.BlockSpec((1, d), row), pl.BlockSpec((1, d), row), pl.BlockSpec((1, d), row),
                  pl.BlockSpec((d, n), row)],
        out_specs=[pl.BlockSpec((tm, d), lambda i: (i, 0)),
                   pl.BlockSpec((tm, n), lambda i: (i, 0))],
        out_shape=[jax.ShapeDtypeStruct((s, d), BF16), jax.ShapeDtypeStruct((s, n), F32)],
        compiler_params=_cparams(("arbitrary",)),
    )(x, g.reshape(1, d), shift.reshape(1, d), scale.reshape(1, d), w_bf16)


def _t5_bucket(d):
    max_exact = N_BUCKETS // 2
    d = jnp.maximum(d, 0)
    df = jnp.maximum(d, 1).astype(F32)
    large = max_exact + (jnp.log(df / max_exact) / math.log(MAX_DISTANCE / max_exact)
                         * (N_BUCKETS - max_exact)).astype(I32)
    return jnp.where(d < max_exact, d, jnp.minimum(large, N_BUCKETS - 1))


def _bias_tile_kernel(tab_ref, o_ref, *, tile, n_near):
    h = pl.program_id(0)
    off = pl.program_id(1)
    b = lax.broadcasted_iota(I32, (tile, tile), 0)
    a = lax.broadcasted_iota(I32, (tile, tile), 1)
    d = jnp.where(off > n_near, -1, off * tile + a - b)
    bucket = _t5_bucket(d)
    val = jnp.zeros((tile, tile), F32)
    for k in range(N_BUCKETS):
        val = jnp.where(bucket == k, tab_ref[h, k], val)
    val = (val - tab_ref[h, N_BUCKETS - 1]) * LOG2E
    o_ref[0, 0] = jnp.where(d < 0, NEG, val)


def _bias_tiles(rel_bias, tile):
    n_near = _far_bias_tiles(tile)
    nh = rel_bias.shape[0]
    return pl.pallas_call(
        functools.partial(_bias_tile_kernel, tile=tile, n_near=n_near),
        grid=(nh, n_near + 2),
        in_specs=[pl.BlockSpec(memory_space=pltpu.SMEM)],
        out_specs=pl.BlockSpec((1, 1, tile, tile), lambda h, o: (h, o, 0, 0)),
        out_shape=jax.ShapeDtypeStruct((nh, n_near + 2, tile, tile), F32),
        compiler_params=_cparams(("arbitrary", "arbitrary")),
    )(rel_bias)


def _kmean_kernel(k_ref, o_ref, *, nb):
    k = k_ref[0]
    km = jnp.mean(k.reshape(nb, MOBA_BLOCK, HEAD_DIM), axis=1)
    o_ref[0] = jnp.zeros(o_ref.shape[1:], F32)
    o_ref[0, :nb, :] = km


def _moba_kmean(k_hsd):
    nh, s, dh = k_hsd.shape
    nb = s // MOBA_BLOCK
    return pl.pallas_call(
        functools.partial(_kmean_kernel, nb=nb),
        grid=(nh,),
        in_specs=[pl.BlockSpec((1, s, dh), lambda h: (h, 0, 0))],
        out_specs=pl.BlockSpec((1, MOBA_MAX_BLOCKS, dh), lambda h: (h, 0, 0)),
        out_shape=jax.ShapeDtypeStruct((nh, MOBA_MAX_BLOCKS, dh), F32),
        compiler_params=_cparams(("arbitrary",)),
    )(k_hsd)


def _moba_sel_kernel(q_ref, km_ref, o_ref, *, tq, n_sel):
    gate = lax.dot_general(km_ref[0], q_ref[0], _NT, preferred_element_type=F32,
                           precision=lax.Precision.HIGHEST)
    t = pl.program_id(1) * tq + lax.broadcasted_iota(I32, (MOBA_MAX_BLOCKS, tq), 1)
    b0 = t // MOBA_BLOCK
    n = lax.broadcasted_iota(I32, (MOBA_MAX_BLOCKS, tq), 0)
    avail = n < b0
    taken = n == b0
    for _ in range(n_sel):
        g = jnp.where(avail, gate, -jnp.inf)
        m = jnp.max(g, axis=0, keepdims=True)
        idx = jnp.min(jnp.where((g == m) & avail, n, 1 << 20), axis=0, keepdims=True)
        pick = n == idx
        taken = taken | pick
        avail = avail & jnp.logical_not(pick)
    o_ref[0] = jnp.where(taken, 0.0, NEG)


def _moba_select(q_hsd, kmean, tq=512):
    nh, s, dh = q_hsd.shape
    nb = s // MOBA_BLOCK
    assert nb <= MOBA_MAX_BLOCKS
    n_sel = max(min(MOBA_TOPK, nb - 1), 1)
    tq = min(tq, s)
    return pl.pallas_call(
        functools.partial(_moba_sel_kernel, tq=tq, n_sel=n_sel),
        grid=(nh, s // tq),
        in_specs=[pl.BlockSpec((1, tq, dh), lambda h, i: (h, i, 0)),
                  pl.BlockSpec((1, MOBA_MAX_BLOCKS, dh), lambda h, i: (h, 0, 0))],
        out_specs=pl.BlockSpec((1, MOBA_MAX_BLOCKS, tq), lambda h, i: (h, 0, i)),
        out_shape=jax.ShapeDtypeStruct((nh, MOBA_MAX_BLOCKS, s), F32),
        compiler_params=_cparams(("arbitrary", "arbitrary")),
    )(q_hsd, kmean)


def _attn_kernel(qi_ref, kg_ref, *refs, nh, n_near, group, has_mask, has_block_sel):
    q_ref, k_ref, vt_ref, bt_ref = refs[:4]
    extra_ref = refs[4] if (has_mask or has_block_sel) else None
    o_ref, m_scr, l_scr, acc_scr = refs[-4:]
    t = ATT_TILE
    step = pl.program_id(0)
    qi = qi_ref[step]
    kg = kg_ref[step]

    @pl.when(kg == 0)
    def _():
        m_scr[...] = jnp.full(m_scr.shape, NEG, F32)
        l_scr[...] = jnp.zeros(l_scr.shape, F32)
        acc_scr[...] = jnp.zeros(acc_scr.shape, F32)

    def step_update(with_bias):
        m = [m_scr[h] for h in range(nh)]
        l = [l_scr[h] for h in range(nh)]
        for g in range(group):
            keys = slice(g * t, (g + 1) * t)
            off = qi - (kg * group + g)
            bias_row = jnp.where(off < 0, n_near + 1, jnp.minimum(off, n_near))
            if has_mask:
                mb = extra_ref[0, g].astype(F32)
            scores = []
            for h in range(nh):
                s = lax.dot_general(k_ref[h, keys, :], q_ref[h], _NT, preferred_element_type=F32)
                if with_bias:
                    s = s + bt_ref[h, bias_row]
                if has_mask:
                    s = s + mb
                if has_block_sel:
                    s = s + extra_ref[h, pl.ds(kg * group + g, 1), :]
                scores.append(s)
            probs, alphas = [], []
            for h in range(nh):
                m_new = jnp.maximum(m[h], jnp.max(scores[h], axis=0, keepdims=True))
                alpha = jnp.exp2(m[h] - m_new)
                p = jnp.exp2(scores[h] - m_new)
                l[h] = alpha * l[h] + jnp.sum(p, axis=0, keepdims=True)
                m[h] = m_new
                probs.append(p.astype(BF16))
                alphas.append(alpha)
            for h in range(nh):
                acc_scr[h] = alphas[h] * acc_scr[h] + jnp.dot(vt_ref[h, :, keys], probs[h],
                                                              preferred_element_type=F32)
        for h in range(nh):
            m_scr[h] = m[h]
            l_scr[h] = l[h]

    all_far = qi - (kg * group + group - 1) >= n_near
    pl.when(all_far)(functools.partial(step_update, False))
    pl.when(jnp.logical_not(all_far))(functools.partial(step_update, True))

    @pl.when(kg == qi // group)
    def _():
        for h in range(nh):
            o_ref[h] = acc_scr[h] / l_scr[h]


def _attention(q, k, vt, bias_tiles, head_group, mask_bias=None, block_sel=None):
    assert mask_bias is None or block_sel is None
    nh, s, dk = q.shape
    dv = vt.shape[1]
    t = ATT_TILE
    group = ATT_KEY_TILES_PER_STEP
    nq = s // t
    n_near = bias_tiles.shape[1] - 2
    qi_list = [i for i in range(nq) for _ in range(i // group + 1)]
    kg_list = [j for i in range(nq) for j in range(i // group + 1)]
    qi_arr = jnp.asarray(qi_list, I32)
    kg_arr = jnp.asarray(kg_list, I32)
    in_specs = [pl.BlockSpec((nh, t, dk), lambda st, qi, kg: (0, qi[st], 0)),
                pl.BlockSpec((nh, group * t, dk), lambda st, qi, kg: (0, kg[st], 0)),
                pl.BlockSpec((nh, dv, group * t), lambda st, qi, kg: (0, 0, kg[st])),
                pl.BlockSpec((nh, n_near + 2, t, t), lambda st, qi, kg: (head_group, 0, 0, 0),
                             pipeline_mode=pl.Buffered(1))]
    args = [q, k, vt, bias_tiles]
    if mask_bias is not None:
        in_specs.append(pl.BlockSpec((1, group, t, t), lambda st, qi, kg: (qi[st], kg[st], 0, 0)))
        args.append(mask_bias)
    if block_sel is not None:
        in_specs.append(pl.BlockSpec((nh, MOBA_MAX_BLOCKS, t), lambda st, qi, kg: (0, 0, qi[st])))
        args.append(block_sel)
    grid_spec = pltpu.PrefetchScalarGridSpec(
        num_scalar_prefetch=2,
        grid=(len(qi_list),),
        in_specs=in_specs,
        out_specs=pl.BlockSpec((nh, dv, t), lambda st, qi, kg: (0, 0, qi[st])),
        scratch_shapes=[pltpu.VMEM((nh, 1, t), F32), pltpu.VMEM((nh, 1, t), F32),
                        pltpu.VMEM((nh, dv, t), F32)])
    return pl.pallas_call(
        functools.partial(_attn_kernel, nh=nh, n_near=n_near, group=group,
                          has_mask=mask_bias is not None, has_block_sel=block_sel is not None),
        grid_spec=grid_spec,
        out_shape=jax.ShapeDtypeStruct((nh, dv, s), F32),
        compiler_params=_cparams(("arbitrary",)),
    )(qi_arr, kg_arr, *args)


def _bitonic_stages(n, full_sort):
    levels = [2 ** p for p in range(1, n.bit_length())] if full_sort else [n]
    return [(k, k >> s) for k in levels for s in range(1, k.bit_length())]


def _plan_passes(stages, block):
    passes = []
    for k, j in stages:
        cross = j >= block
        if passes and passes[-1][2] == cross and (not cross or bin(passes[-1][0] | j).count("1") <= 3):
            passes[-1][0] |= j if cross else 0
            passes[-1][1].append((k, j))
        else:
            passes.append([j if cross else block - 1, [(k, j)], cross])
    return [(mask, sub) for mask, sub, _ in passes]


def _run_pass(load, store, n, mask, stages, regs=32):
    groups = {}
    for e in range(n):
        groups.setdefault(e & ~mask, []).append(e)
    groups = list(groups.values())
    per = max(1, regs // len(groups[0]))
    for g0 in range(0, len(groups), per):
        elems = sorted(e for grp in groups[g0:g0 + per] for e in grp)
        vals = {e: load(e) for e in elems}
        for k, j in stages:
            for e in elems:
                l = e ^ j
                if l > e:
                    hi = jnp.maximum(vals[e], vals[l])
                    lo = jnp.minimum(vals[e], vals[l])
                    vals[e], vals[l] = (hi, lo) if (e & k) == 0 else (lo, hi)
        for e in elems:
            store(e, vals[e])


def _sortable(x):
    return jnp.where(x == 0.0, 0.0, x)


def _dsa_sel_kernel(qi_ref, wt_ref, kn_ref, tri_ref, o_ref, keys_scr, seen_scr, work_scr, top_scr, *, tile, topk):
    n_chunks = kn_ref.shape[0]
    i = pl.program_id(0)
    n_live = i + 1
    groups_per_chunk = tile // 8
    causal = lax.broadcasted_iota(I32, (tile, tile), 0) <= lax.broadcasted_iota(I32, (tile, tile), 1)
    q_heads = [qi_ref[:, h * IDX_DIM:(h + 1) * IDX_DIM] for h in range(IDX_HEADS)]
    w_heads = [wt_ref[h:h + 1, :] for h in range(IDX_HEADS)]

    def score_chunk(c, diagonal):
        kc = kn_ref[c]
        dots = [lax.dot_general(kc, q_heads[h], _NT, preferred_element_type=F32) for h in range(IDX_HEADS)]
        sc = None
        for h in range(IDX_HEADS):
            term = w_heads[h] * jnp.maximum(dots[h], 0.0)
            sc = term if sc is None else sc + term
        if diagonal:
            sc = jnp.where(causal, sc, -jnp.inf)
        keys_scr[c] = _sortable(sc)

    def for_past_chunks(fn):
        def pair(p, carry):
            fn(2 * p)
            fn(2 * p + 1)
            return carry
        lax.fori_loop(0, i // 2, pair, 0)
        pl.when(i % 2 == 1)(lambda: fn(i - 1))

    for_past_chunks(lambda c: score_chunk(c, False))
    score_chunk(i, True)

    def count(pred_fn):
        def body(c, cnt):
            hits = [pred_fn(keys_scr[c, 8 * g:8 * g + 8, :]).astype(I32) for g in range(tile // 8)]
            while len(hits) > 1:
                hits = [hits[j] + hits[j + 1] for j in range(0, len(hits), 2)]
            return cnt + hits[0]
        cnt = lax.fori_loop(0, n_live, body, jnp.zeros((8, tile), I32))
        return jnp.sum(cnt, axis=0, keepdims=True)

    n_list = DSA_SORT_CHUNKS * groups_per_chunk
    n_batches = (n_live + DSA_SORT_CHUNKS - 1) // DSA_SORT_CHUNKS

    def pad_chunk(c, carry):
        keys_scr[c] = jnp.full((tile, tile), -jnp.inf, F32)
        return carry

    lax.fori_loop(n_live, n_batches * DSA_SORT_CHUNKS, pad_chunk, 0)
    top_scr[...] = jnp.full(top_scr.shape, -jnp.inf, F32)
    sort_passes = _plan_passes(_bitonic_stages(n_list, True), groups_per_chunk)
    merge_passes = _plan_passes(_bitonic_stages(n_list, False), groups_per_chunk)
    lane_halves = [slice(lo, lo + LANES) for lo in range(0, tile, LANES)]

    def run_network(passes, first_load, first_store, buf, lanes):
        def load(e):
            return buf[e, :, lanes]

        def store(e, v):
            buf[e, :, lanes] = v

        for n_pass, (mask, stages) in enumerate(passes):
            _run_pass(first_load if n_pass == 0 else load, first_store if n_pass == 0 else store,
                      n_list, mask, stages)

    def sort_batch(b, carry):
        for lanes in lane_halves:
            def load_keys(e, lanes=lanes):
                g = e % groups_per_chunk
                return keys_scr[b * DSA_SORT_CHUNKS + e // groups_per_chunk, 8 * g:8 * g + 8, lanes]

            def store_work(e, v, lanes=lanes):
                work_scr[e, :, lanes] = v

            run_network(sort_passes, load_keys, store_work, work_scr, lanes)

            def load_best(e, lanes=lanes):
                return jnp.maximum(top_scr[e, :, lanes], work_scr[n_list - 1 - e, :, lanes])

            def store_top(e, v, lanes=lanes):
                top_scr[e, :, lanes] = v

            run_network(merge_passes, load_best, store_top, top_scr, lanes)
        return carry

    lax.fori_loop(0, n_batches, sort_batch, 0)

    src, dst = top_scr, work_scr
    for shift in (4, 2, 1):
        for lanes in lane_halves:
            def load_pair(e, lanes=lanes, src=src, shift=shift):
                return jnp.maximum(src[e, :, lanes], pltpu.roll(src[n_list - 1 - e, :, lanes], shift, 0))

            def store_dst(e, v, lanes=lanes, dst=dst):
                dst[e, :, lanes] = v

            run_network(merge_passes, load_pair, store_dst, dst, lanes)
        src, dst = dst, src
    thr = src[topk - 1, 0:1, :]
    thr8 = jnp.broadcast_to(thr, (8, tile))
    n_gt = count(lambda kc: kc > thr8)
    need = (topk - n_gt).astype(F32)

    seen_scr[...] = jnp.zeros(seen_scr.shape, F32)

    def emit_chunk(c, diagonal):
        kc = keys_scr[c]
        eq = kc == thr
        eqf = eq.astype(F32)
        seen = seen_scr[...]
        seen_scr[...] = seen + jnp.sum(eqf, axis=0, keepdims=True)
        rank = seen + jnp.dot(tri_ref[...], eqf.astype(BF16), preferred_element_type=F32)
        sel = (kc > thr) | (eq & (rank < need))
        if diagonal:
            sel = sel & causal
        o_ref[0, c] = jnp.where(sel, 0.0, NEG).astype(BF16)

    for_past_chunks(lambda c: emit_chunk(c, False))
    emit_chunk(i, True)

    def fill_chunk(c, carry):
        o_ref[0, c] = jnp.full((tile, tile), NEG, BF16)
        return carry

    lax.fori_loop(n_live, n_chunks, fill_chunk, 0)


def _dsa_select(q_idx_bf16, w_idx, kn_bf16, topk):
    s = q_idx_bf16.shape[0]
    t = ATT_TILE
    n_chunks = s // t
    n_list = DSA_SORT_CHUNKS * (t // 8)
    assert topk <= n_list, "the sorting network keeps n_list best keys per query"
    kn3 = kn_bf16.reshape(n_chunks, t, IDX_DIM)
    wt = jnp.pad(w_idx.T, ((0, 8 - IDX_HEADS), (0, 0)))
    tri = (lax.broadcasted_iota(I32, (t, t), 1) < lax.broadcasted_iota(I32, (t, t), 0)).astype(BF16)
    return pl.pallas_call(
        functools.partial(_dsa_sel_kernel, tile=t, topk=topk),
        grid=(n_chunks,),
        in_specs=[pl.BlockSpec((t, IDX_HEADS * IDX_DIM), lambda i: (i, 0)),
                  pl.BlockSpec((8, t), lambda i: (0, i)),
                  pl.BlockSpec((n_chunks, t, IDX_DIM), lambda i: (0, 0, 0)),
                  pl.BlockSpec((t, t), lambda i: (0, 0))],
        out_specs=pl.BlockSpec((1, n_chunks, t, t), lambda i: (i, 0, 0, 0)),
        out_shape=jax.ShapeDtypeStruct((n_chunks, n_chunks, t, t), BF16),
        scratch_shapes=[pltpu.VMEM((max(n_chunks, DSA_SORT_CHUNKS), t, t), F32), pltpu.VMEM((1, t), F32),
                        pltpu.VMEM((n_list, 8, t), F32), pltpu.VMEM((n_list, 8, t), F32)],
        compiler_params=_cparams(("arbitrary",)),
    )(q_idx_bf16, wt, kn3, tri)


def _out_proj_kernel(y_ref, w_ref, x_ref, g_ref, o_ref):
    o_ref[...] = x_ref[...] + g_ref[...] * jnp.dot(y_ref[...], w_ref[...], preferred_element_type=F32)


def _out_proj(y_bf16, w_bf16, x, gate, tm=512):
    s, d = x.shape
    k = y_bf16.shape[1]
    return pl.pallas_call(
        _out_proj_kernel,
        grid=(s // tm,),
        in_specs=[pl.BlockSpec((tm, k), lambda i: (i, 0)),
                  pl.BlockSpec((k, d), lambda i: (0, 0)),
                  pl.BlockSpec((tm, d), lambda i: (i, 0)),
                  pl.BlockSpec((1, d), lambda i: (0, 0))],
        out_specs=pl.BlockSpec((tm, d), lambda i: (i, 0)),
        out_shape=jax.ShapeDtypeStruct((s, d), F32),
        compiler_params=_cparams(("arbitrary",)),
    )(y_bf16, w_bf16, x, gate.reshape(1, d))


def _cmp_exchange(v, i, j):
    hi = jnp.maximum(v[i], v[j])
    lo = jnp.minimum(v[i], v[j])
    v[i], v[j] = hi, lo


def _bitonic_sort_desc(v):
    n = len(v)
    k = 2
    while k <= n:
        j = k // 2
        while j >= 1:
            for i in range(n):
                l = i ^ j
                if l > i:
                    if (i & k) == 0:
                        _cmp_exchange(v, i, l)
                    else:
                        _cmp_exchange(v, l, i)
            j //= 2
        k *= 2


def _bitonic_merge_desc(v):
    n = len(v)
    j = n // 2
    while j >= 1:
        for i in range(n):
            l = i ^ j
            if l > i:
                _cmp_exchange(v, i, l)
        j //= 2


def _top16_rows(x):
    tt = x.shape[1]
    v = [x[8 * i:8 * i + 8, :] for i in range(PEER_NKEYS // 8)]
    _bitonic_sort_desc(v)
    for shift in (4, 2, 1):
        other = [pltpu.roll(a, shift, 0) for a in v]
        v = [jnp.maximum(v[i], other[len(v) - 1 - i]) for i in range(len(v))]
        _bitonic_merge_desc(v)
    return v


def _peer_route_kernel(q_ref, sk_ref, r1_ref, e1_ref, e0_ref, ns_ref, s0_scr, s1_scr, *, tt):
    sub = lax.broadcasted_iota(I32, (8, tt), 0)
    tops = [[], []]
    for h in range(PEER_HEADS):
        for p in range(2):
            lo = (2 * h + p) * (PEER_DKEY // 2)
            qhp = q_ref[:, lo:lo + PEER_DKEY // 2].astype(BF16)
            sc = lax.dot_general(sk_ref[2 * h + p], qhp, _NT, preferred_element_type=F32)
            (s0_scr if p == 0 else s1_scr)[h] = sc
            tops[p].append(_top16_rows(sc))
    a = [sum(jnp.where(sub == h, tops[0][h][k], 0.0) for h in range(PEER_HEADS)) for k in range(PEER_TOPK)]
    b = [sum(jnp.where(sub == h, tops[1][h][k], 0.0) for h in range(PEER_HEADS)) for k in range(PEER_TOPK)]
    cand = [a[i] + b[j] for i in range(PEER_TOPK) for j in range(PEER_TOPK) if (i + 1) * (j + 1) <= PEER_TOPK]
    cand = cand + [jnp.full((8, tt), -jnp.inf, F32)] * (64 - len(cand))
    _bitonic_sort_desc(cand)
    tau = cand[PEER_TOPK - 1]
    z = sum(jnp.exp(cand[k] - cand[0]) for k in range(PEER_TOPK))
    half_inv_z = 0.5 / z
    for h in range(PEER_HEADS):
        s0 = s0_scr[h]
        s1 = s1_scr[h]
        tau_h = tau[h:h + 1, :]
        nsel = jnp.zeros((PEER_NKEYS, tt), F32)
        rank1 = jnp.zeros((PEER_NKEYS, tt), F32)
        for k in range(PEER_TOPK):
            bk = b[k][h:h + 1, :]
            nsel = nsel + jnp.where(s0 + bk >= tau_h, 1.0, 0.0)
            rank1 = rank1 + jnp.where(bk > s1, 1.0, 0.0)
        ns_ref[h] = nsel
        r1_ref[h] = rank1.astype(BF16)
        e0_ref[h] = jnp.exp(s0 - a[0][h:h + 1, :]) * half_inv_z[h:h + 1, :]
        e1_ref[h] = jnp.exp(s1 - b[0][h:h + 1, :]).astype(BF16)


def _peer_route(q, subkeys_bf16, tt=256):
    s = q.shape[0]
    big = lambda dt: jax.ShapeDtypeStruct((PEER_HEADS, PEER_NKEYS, s), dt)
    big_spec = pl.BlockSpec((PEER_HEADS, PEER_NKEYS, tt), lambda i: (0, 0, i))
    return pl.pallas_call(
        functools.partial(_peer_route_kernel, tt=tt),
        grid=(s // tt,),
        in_specs=[pl.BlockSpec((tt, PEER_HEADS * PEER_DKEY), lambda i: (i, 0)),
                  pl.BlockSpec((2 * PEER_HEADS, PEER_NKEYS, PEER_DKEY // 2), lambda i: (0, 0, 0))],
        out_specs=[big_spec, big_spec, big_spec, big_spec],
        out_shape=[big(BF16), big(BF16), big(F32), big(F32)],
        scratch_shapes=[pltpu.VMEM((PEER_HEADS, PEER_NKEYS, tt), F32),
                        pltpu.VMEM((PEER_HEADS, PEER_NKEYS, tt), F32)],
        compiler_params=_cparams(("arbitrary",)),
    )(q, subkeys_bf16)


def _peer_dense_kernel(h_ref, u_ref, vt_ref, r1_ref, e1_ref, e0_ref, ns_ref,
                       x_ref, g_ref, gf_ref, o_ref, acc_scr, *, rows_per_step, final_norm):
    k = pl.program_id(1)

    @pl.when(k == 0)
    def _():
        acc_scr[...] = jnp.zeros(acc_scr.shape, F32)

    tt = h_ref.shape[0]
    pack_rows = 16
    zero = jnp.zeros((pack_rows, tt), BF16)

    def activations(c):
        return lax.dot_general(u_ref[c * MXU_DEPTH:(c + 1) * MXU_DEPTH, :], h_ref[...], _NT,
                               preferred_element_type=F32)

    def expert_rows(ii, a):
        i = k * rows_per_step + ii
        bcast = lambda ref, h: jnp.broadcast_to(ref[h, pl.ds(i, 1), :], (pack_rows, tt)).astype(BF16)
        e0_rows = [bcast(e0_ref, h) for h in range(PEER_HEADS)]
        ns_rows = [bcast(ns_ref, h) for h in range(PEER_HEADS)]
        gelu2 = (a * (1.0 + lax.erf(a * math.sqrt(0.5)))).astype(BF16)
        out = []
        for g in range(PEER_NKEYS // pack_rows):
            rows = slice(g * pack_rows, (g + 1) * pack_rows)
            wgt = None
            for h in range(PEER_HEADS):
                term = jnp.where(r1_ref[h, rows, :] < ns_rows[h], e1_ref[h, rows, :], zero) * e0_rows[h]
                wgt = term if wgt is None else wgt + term
            out.append(wgt * gelu2[rows, :])
        return out

    depth_rows = MXU_DEPTH // PEER_NKEYS
    n_chunks = rows_per_step // depth_rows
    mixed = None
    act = activations(0)
    for c in range(n_chunks):
        act_next = activations(c + 1) if c + 1 < n_chunks else None
        p = jnp.concatenate([r for j in range(depth_rows)
                             for r in expert_rows(c * depth_rows + j, act[j * PEER_NKEYS:(j + 1) * PEER_NKEYS, :])],
                            axis=0)
        d = jnp.dot(vt_ref[:, c * MXU_DEPTH:(c + 1) * MXU_DEPTH], p, preferred_element_type=F32)
        mixed = d if mixed is None else mixed + d
        act = act_next
    acc_scr[...] += mixed

    @pl.when(k == pl.num_programs(1) - 1)
    def _():
        y = x_ref[...] + g_ref[...] * acc_scr[...].T
        if final_norm:
            y = y * lax.rsqrt(jnp.mean(y * y, axis=-1, keepdims=True) + EPS) * gf_ref[...]
        o_ref[...] = y


def _peer_dense(h_bf16, u_bf16, vt_bf16, rank1, e1, e0, nsel, x, gate, g_final, final_norm,
                tt=512, rows_per_step=8):
    s, d = x.shape
    n_exp = u_bf16.shape[0]
    et = rows_per_step * PEER_NKEYS
    tok = lambda j, k: (0, 0, j)
    big_spec = pl.BlockSpec((PEER_HEADS, PEER_NKEYS, tt), tok)
    return pl.pallas_call(
        functools.partial(_peer_dense_kernel, rows_per_step=rows_per_step, final_norm=final_norm),
        grid=(s // tt, n_exp // et),
        in_specs=[pl.BlockSpec((tt, d), lambda j, k: (j, 0)),
                  pl.BlockSpec((et, d), lambda j, k: (k, 0)),
                  pl.BlockSpec((d, et), lambda j, k: (0, k)),
                  big_spec, big_spec, big_spec, big_spec,
                  pl.BlockSpec((tt, d), lambda j, k: (j, 0)),
                  pl.BlockSpec((1, d), lambda j, k: (0, 0)),
                  pl.BlockSpec((1, d), lambda j, k: (0, 0))],
        out_specs=pl.BlockSpec((tt, d), lambda j, k: (j, 0)),
        out_shape=jax.ShapeDtypeStruct((s, d), F32),
        scratch_shapes=[pltpu.VMEM((d, tt), F32)],
        compiler_params=_cparams(("arbitrary", "arbitrary")),
    )(h_bf16, u_bf16, vt_bf16, rank1, e1, e0, nsel, x, gate.reshape(1, d), g_final.reshape(1, d))


def _heads_first(t, nh):
    s = t.shape[0]
    return jnp.transpose(t.reshape(s, nh, HEAD_DIM), (1, 0, 2))


def _mixer(h_proj, w_out_bf16, bias_tiles, x, gate):
    s = h_proj.shape[0]
    parts = []
    off = 0
    for size in PROJ_SIZES:
        parts.append(h_proj[:, off:off + size])
        off += size
    mq, mk, mv, dq, dk, dv, iq, ik, iw = parts
    scale = HEAD_DIM ** -0.5 * LOG2E
    dims_first = lambda a, nh: a.T.reshape(nh, HEAD_DIM, s).astype(BF16)

    mq_h, mk_h = _heads_first(mq, MOBA_HEADS), _heads_first(mk, MOBA_HEADS)
    block_sel = _moba_select(mq_h, _moba_kmean(mk_h))
    blk = jnp.arange(s, dtype=I32) // MOBA_BLOCK
    onehot = (blk[:, None] == jnp.arange(MOBA_MAX_BLOCKS, dtype=I32)[None, :]).astype(BF16)
    q_aug = jnp.concatenate([(mq_h * scale).astype(BF16), jnp.transpose(block_sel, (0, 2, 1)).astype(BF16)], axis=-1)
    k_aug = jnp.concatenate([mk_h.astype(BF16), jnp.broadcast_to(onehot, (MOBA_HEADS, s, MOBA_MAX_BLOCKS))], axis=-1)
    y_moba = _attention(q_aug, k_aug, dims_first(mv, MOBA_HEADS), bias_tiles, 0)

    ikf = ik.astype(F32)
    ik_n = ikf * lax.rsqrt(jnp.mean(ikf * ikf, axis=-1, keepdims=True) + EPS)
    topk = min(DSA_TOPK_MAX, s // 4)
    mask_bias = _dsa_select(iq.astype(BF16), iw * IDX_HEADS ** -0.5, ik_n.astype(BF16), topk)
    dq_h, dk_h = _heads_first(dq, DSA_HEADS), _heads_first(dk, DSA_HEADS)
    y_dsa = _attention((dq_h * scale).astype(BF16), dk_h.astype(BF16), dims_first(dv, DSA_HEADS),
                       bias_tiles, 1, mask_bias=mask_bias)

    y = jnp.concatenate([y_moba.reshape(MOBA_WIDTH, s), y_dsa.reshape(DSA_WIDTH, s)], axis=0).T
    return _out_proj(y.astype(BF16), w_out_bf16, x, gate)


def kernel(x, c, w_ada, b_ada, norm_attn, norm_ffn, w_in, w_out, rel_bias,
           peer_wq, peer_subkeys, peer_u, peer_v, norm_final):
    batch, s, d = x.shape
    assert batch == 1 and s % (ATT_KEY_TILES_PER_STEP * ATT_TILE) == 0
    depth = w_ada.shape[0]
    x = x[0]
    mod = _ada_mod(c, w_ada, b_ada)
    bias_tiles = _bias_tiles(rel_bias, ATT_TILE)
    n_pad = -(-PROJ_WIDTH // LANES) * LANES
    for l in range(depth):
        sh1, sc1, g1, sh2, sc2, g2 = jnp.split(mod[l], 6)
        w_in_l = jnp.pad(w_in[l], ((0, 0), (0, n_pad - PROJ_WIDTH))).astype(BF16)
        _, proj = _norm_mod_matmul(x, norm_attn[l], sh1, sc1, w_in_l)
        x = _mixer(proj, w_out[l].astype(BF16), bias_tiles, x, g1)
        h2, q = _norm_mod_matmul(x, norm_ffn[l], sh2, sc2, peer_wq[l].astype(BF16))
        subkeys = peer_subkeys[l].reshape(2 * PEER_HEADS, PEER_NKEYS, PEER_DKEY // 2).astype(BF16)
        rank1, e1, e0, nsel = _peer_route(q, subkeys)
        x = _peer_dense(h2, peer_u[l].astype(BF16), peer_v[l].T.astype(BF16), rank1, e1, e0, nsel,
                        x, g2, norm_final, final_norm=(l == depth - 1))
    return x[None]
```

```python
import functools
import math

import jax
import jax.numpy as jnp
from jax import lax
from jax.experimental import pallas as pl
from jax.experimental.pallas import tpu as pltpu

F32 = jnp.float32
BF16 = jnp.bfloat16
I32 = jnp.int32

HEAD_DIM = 64
MOBA_HEADS = 8
DSA_HEADS = 8
N_HEADS = MOBA_HEADS + DSA_HEADS
MOBA_WIDTH = MOBA_HEADS * HEAD_DIM
DSA_WIDTH = DSA_HEADS * HEAD_DIM
MOBA_BLOCK = 256
MOBA_TOPK = 3
MOBA_MAX_BLOCKS = 64
IDX_HEADS = 4
IDX_DIM = 64
DSA_TOPK_MAX = 256
DSA_SORT_CHUNKS = 8
N_BUCKETS = 32
MAX_DISTANCE = 4096
PEER_HEADS = 8
PEER_NKEYS = 128
PEER_DKEY = 256
PEER_TOPK = 16
EPS = 1e-6
PROJ_SIZES = (MOBA_WIDTH, MOBA_WIDTH, MOBA_WIDTH, DSA_WIDTH, DSA_WIDTH, DSA_WIDTH,
              IDX_HEADS * IDX_DIM, IDX_DIM, IDX_HEADS)
PROJ_WIDTH = sum(PROJ_SIZES)

NEG = -1e30
LOG2E = math.log2(math.e)
ATT_TILE = MOBA_BLOCK
ATT_KEY_TILES_PER_STEP = 4
LANES = 128
MXU_DEPTH = 256
VMEM_LIMIT = 56 * 1024 * 1024

_NT = (((1,), (1,)), ((), ()))


def _far_bias_tiles(tile):
    max_exact = N_BUCKETS // 2
    sat = int(math.ceil(max_exact * (MAX_DISTANCE / max_exact) ** ((N_BUCKETS - max_exact - 1) / (N_BUCKETS - max_exact)))) + 4
    return -(-(sat + tile - 1) // tile)


def _cparams(sem):
    return pltpu.CompilerParams(dimension_semantics=sem, vmem_limit_bytes=VMEM_LIMIT)


def _ada_kernel(c_ref, w_ref, b_ref, o_ref):
    c = c_ref[...]
    sc = c * jax.nn.sigmoid(c)
    o_ref[0] = jnp.dot(sc, w_ref[0], preferred_element_type=F32,
                       precision=lax.Precision.HIGHEST) + b_ref[0]


def _ada_mod(c, w_ada, b_ada):
    depth, d, n = w_ada.shape
    tn = 1536
    c8 = jnp.broadcast_to(c, (8, d))
    out = pl.pallas_call(
        _ada_kernel,
        grid=(depth, n // tn),
        in_specs=[pl.BlockSpec((8, d), lambda l, j: (0, 0)),
                  pl.BlockSpec((1, d, tn), lambda l, j: (l, 0, j)),
                  pl.BlockSpec((1, 1, tn), lambda l, j: (l, 0, j))],
        out_specs=pl.BlockSpec((1, 8, tn), lambda l, j: (l, 0, j)),
        out_shape=jax.ShapeDtypeStruct((depth, 8, n), F32),
        compiler_params=_cparams(("arbitrary", "arbitrary")),
    )(c8, w_ada, b_ada.reshape(depth, 1, n))
    return out[:, 0, :]


def _nmm_kernel(x_ref, g_ref, sh_ref, sc_ref, w_ref, h_ref, o_ref):
    x = x_ref[...]
    y = x * lax.rsqrt(jnp.mean(x * x, axis=-1, keepdims=True) + EPS) * g_ref[...]
    hb = (y * (1.0 + sc_ref[...]) + sh_ref[...]).astype(BF16)
    h_ref[...] = hb
    o_ref[...] = jnp.dot(hb, w_ref[...], preferred_element_type=F32)


def _norm_mod_matmul(x, g, shift, scale, w_bf16, tm=256):
    s, d = x.shape
    n = w_bf16.shape[1]
    row = lambda i: (0, 0)
    return pl.pallas_call(
        _nmm_kernel,
        grid=(s // tm,),
        in_specs=[pl.BlockSpec((tm, d), lambda i: (i, 0)),
                  pl.BlockSpec((1, d), row), pl.BlockSpec((1, d), row), pl.BlockSpec((1, d), row),
                  pl.BlockSpec((d, n), row)],
        out_specs=[pl.BlockSpec((tm, d), lambda i: (i, 0)),
                   pl.BlockSpec((tm, n), lambda i: (i, 0))],
        out_shape=[jax.ShapeDtypeStruct((s, d), BF16), jax.ShapeDtypeStruct((s, n), F32)],
        compiler_params=_cparams(("arbitrary",)),
    )(x, g.reshape(1, d), shift.reshape(1, d), scale.reshape(1, d), w_bf16)


def _t5_bucket(d):
    max_exact = N_BUCKETS // 2
    d = jnp.maximum(d, 0)
    df = jnp.maximum(d, 1).astype(F32)
    large = max_exact + (jnp.log(df / max_exact) / math.log(MAX_DISTANCE / max_exact)
                         * (N_BUCKETS - max_exact)).astype(I32)
    return jnp.where(d < max_exact, d, jnp.minimum(large, N_BUCKETS - 1))


def _bias_tile_kernel(tab_ref, o_ref, *, tile, n_near):
    h = pl.program_id(0)
    off = pl.program_id(1)
    b = lax.broadcasted_iota(I32, (tile, tile), 0)
    a = lax.broadcasted_iota(I32, (tile, tile), 1)
    d = jnp.where(off > n_near, -1, off * tile + a - b)
    bucket = _t5_bucket(d)
    val = jnp.zeros((tile, tile), F32)
    for k in range(N_BUCKETS):
        val = jnp.where(bucket == k, tab_ref[h, k], val)
    val = (val - tab_ref[h, N_BUCKETS - 1]) * LOG2E
    o_ref[0, 0] = jnp.where(d < 0, NEG, val)


def _bias_tiles(rel_bias, tile):
    n_near = _far_bias_tiles(tile)
    nh = rel_bias.shape[0]
    return pl.pallas_call(
        functools.partial(_bias_tile_kernel, tile=tile, n_near=n_near),
        grid=(nh, n_near + 2),
        in_specs=[pl.BlockSpec(memory_space=pltpu.SMEM)],
        out_specs=pl.BlockSpec((1, 1, tile, tile), lambda h, o: (h, o, 0, 0)),
        out_shape=jax.ShapeDtypeStruct((nh, n_near + 2, tile, tile), F32),
        compiler_params=_cparams(("arbitrary", "arbitrary")),
    )(rel_bias)


def _kmean_kernel(k_ref, o_ref, *, nb):
    k = k_ref[0]
    km = jnp.mean(k.reshape(nb, MOBA_BLOCK, HEAD_DIM), axis=1)
    o_ref[0] = jnp.zeros(o_ref.shape[1:], F32)
    o_ref[0, :nb, :] = km


def _moba_kmean(k_hsd):
    nh, s, dh = k_hsd.shape
    nb = s // MOBA_BLOCK
    return pl.pallas_call(
        functools.partial(_kmean_kernel, nb=nb),
        grid=(nh,),
        in_specs=[pl.BlockSpec((1, s, dh), lambda h: (h, 0, 0))],
        out_specs=pl.BlockSpec((1, MOBA_MAX_BLOCKS, dh), lambda h: (h, 0, 0)),
        out_shape=jax.ShapeDtypeStruct((nh, MOBA_MAX_BLOCKS, dh), F32),
        compiler_params=_cparams(("arbitrary",)),
    )(k_hsd)


def _moba_sel_kernel(q_ref, km_ref, o_ref, *, tq, n_sel):
    gate = lax.dot_general(km_ref[0], q_ref[0], _NT, preferred_element_type=F32,
                           precision=lax.Precision.HIGHEST)
    t = pl.program_id(1) * tq + lax.broadcasted_iota(I32, (MOBA_MAX_BLOCKS, tq), 1)
    b0 = t // MOBA_BLOCK
    n = lax.broadcasted_iota(I32, (MOBA_MAX_BLOCKS, tq), 0)
    avail = n < b0
    taken = n == b0
    for _ in range(n_sel):
        g = jnp.where(avail, gate, -jnp.inf)
        m = jnp.max(g, axis=0, keepdims=True)
        idx = jnp.min(jnp.where((g == m) & avail, n, 1 << 20), axis=0, keepdims=True)
        pick = n == idx
        taken = taken | pick
        avail = avail & jnp.logical_not(pick)
    o_ref[0] = jnp.where(taken, 0.0, NEG)


def _moba_select(q_hsd, kmean, tq=512):
    nh, s, dh = q_hsd.shape
    nb = s // MOBA_BLOCK
    assert nb <= MOBA_MAX_BLOCKS
    n_sel = max(min(MOBA_TOPK, nb - 1), 1)
    tq = min(tq, s)
    return pl.pallas_call(
        functools.partial(_moba_sel_kernel, tq=tq, n_sel=n_sel),
        grid=(nh, s // tq),
        in_specs=[pl.BlockSpec((1, tq, dh), lambda h, i: (h, i, 0)),
                  pl.BlockSpec((1, MOBA_MAX_BLOCKS, dh), lambda h, i: (h, 0, 0))],
        out_specs=pl.BlockSpec((1, MOBA_MAX_BLOCKS, tq), lambda h, i: (h, 0, i)),
        out_shape=jax.ShapeDtypeStruct((nh, MOBA_MAX_BLOCKS, s), F32),
        compiler_params=_cparams(("arbitrary", "arbitrary")),
    )(q_hsd, kmean)


def _attn_kernel(qi_ref, kg_ref, *refs, nh, n_near, group, has_mask, has_block_sel):
    q_ref, k_ref, vt_ref, bt_ref = refs[:4]
    extra_ref = refs[4] if (has_mask or has_block_sel) else None
    o_ref, m_scr, acc_scr = refs[-3:]
    t = ATT_TILE
    step = pl.program_id(0)
    qi = qi_ref[step]
    kg = kg_ref[step]

    @pl.when(kg == 0)
    def _():
        m_scr[...] = jnp.full(m_scr.shape, NEG, F32)
        acc_scr[...] = jnp.zeros(acc_scr.shape, F32)

    def step_update(with_bias):
        m = [m_scr[h] for h in range(nh)]
        for g in range(group):
            keys = slice(g * t, (g + 1) * t)
            off = qi - (kg * group + g)
            bias_row = jnp.where(off < 0, n_near + 1, jnp.minimum(off, n_near))
            if has_mask:
                mb = extra_ref[0, g].astype(F32)
            scores = []
            for h in range(nh):
                s = lax.dot_general(k_ref[h, keys, :], q_ref[h], _NT, preferred_element_type=F32)
                if with_bias:
                    s = s + bt_ref[h, bias_row]
                if has_mask:
                    s = s + mb
                if has_block_sel:
                    s = s + extra_ref[h, pl.ds(kg * group + g, 1), :]
                scores.append(s)
            probs, alphas = [], []
            for h in range(nh):
                m_new = jnp.maximum(m[h], jnp.max(scores[h], axis=0, keepdims=True))
                alpha = jnp.exp2(m[h] - m_new)
                p = jnp.exp2(scores[h] - m_new)
                m[h] = m_new
                probs.append(p.astype(BF16))
                alphas.append(alpha)
            for h in range(nh):
                acc_scr[h] = alphas[h] * acc_scr[h] + jnp.dot(vt_ref[h, :, keys], probs[h],
                                                              preferred_element_type=F32)
        for h in range(nh):
            m_scr[h] = m[h]

    all_far = qi - (kg * group + group - 1) >= n_near
    pl.when(all_far)(functools.partial(step_update, False))
    pl.when(jnp.logical_not(all_far))(functools.partial(step_update, True))

    @pl.when(kg == qi // group)
    def _():
        for h in range(nh):
            o_ref[h] = acc_scr[h, :HEAD_DIM, :] / acc_scr[h, HEAD_DIM:HEAD_DIM + 1, :]


def _attention(q, k, vt, bias_tiles, head_group, mask_bias=None, block_sel=None):
    assert mask_bias is None or block_sel is None
    nh, s, dk = q.shape
    vt = jnp.concatenate([vt, jnp.ones((nh, 8, s), vt.dtype)], axis=1)
    dv = vt.shape[1]
    t = ATT_TILE
    group = ATT_KEY_TILES_PER_STEP
    nq = s // t
    n_near = bias_tiles.shape[1] - 2
    qi_list = [i for i in range(nq) for _ in range(i // group + 1)]
    kg_list = [j for i in range(nq) for j in range(i // group + 1)]
    qi_arr = jnp.asarray(qi_list, I32)
    kg_arr = jnp.asarray(kg_list, I32)
    in_specs = [pl.BlockSpec((nh, t, dk), lambda st, qi, kg: (0, qi[st], 0)),
                pl.BlockSpec((nh, group * t, dk), lambda st, qi, kg: (0, kg[st], 0)),
                pl.BlockSpec((nh, dv, group * t), lambda st, qi, kg: (0, 0, kg[st])),
                pl.BlockSpec((nh, n_near + 2, t, t), lambda st, qi, kg: (head_group, 0, 0, 0),
                             pipeline_mode=pl.Buffered(1))]
    args = [q, k, vt, bias_tiles]
    if mask_bias is not None:
        in_specs.append(pl.BlockSpec((1, group, t, t), lambda st, qi, kg: (qi[st], kg[st], 0, 0)))
        args.append(mask_bias)
    if block_sel is not None:
        in_specs.append(pl.BlockSpec((nh, MOBA_MAX_BLOCKS, t), lambda st, qi, kg: (0, 0, qi[st])))
        args.append(block_sel)
    grid_spec = pltpu.PrefetchScalarGridSpec(
        num_scalar_prefetch=2,
        grid=(len(qi_list),),
        in_specs=in_specs,
        out_specs=pl.BlockSpec((nh, HEAD_DIM, t), lambda st, qi, kg: (0, 0, qi[st])),
        scratch_shapes=[pltpu.VMEM((nh, 1, t), F32), pltpu.VMEM((nh, dv, t), F32)])
    return pl.pallas_call(
        functools.partial(_attn_kernel, nh=nh, n_near=n_near, group=group,
                          has_mask=mask_bias is not None, has_block_sel=block_sel is not None),
        grid_spec=grid_spec,
        out_shape=jax.ShapeDtypeStruct((nh, HEAD_DIM, s), F32),
        compiler_params=_cparams(("arbitrary",)),
    )(qi_arr, kg_arr, *args)


def _bitonic_stages(n, full_sort):
    levels = [2 ** p for p in range(1, n.bit_length())] if full_sort else [n]
    return [(k, k >> s) for k in levels for s in range(1, k.bit_length())]


def _plan_passes(stages, block):
    passes = []
    for k, j in stages:
        cross = j >= block
        if passes and passes[-1][2] == cross and (not cross or bin(passes[-1][0] | j).count("1") <= 3):
            passes[-1][0] |= j if cross else 0
            passes[-1][1].append((k, j))
        else:
            passes.append([j if cross else block - 1, [(k, j)], cross])
    return [(mask, sub) for mask, sub, _ in passes]


def _run_pass(load, store, n, mask, stages, regs=32):
    groups = {}
    for e in range(n):
        groups.setdefault(e & ~mask, []).append(e)
    groups = list(groups.values())
    per = max(1, regs // len(groups[0]))
    for g0 in range(0, len(groups), per):
        elems = sorted(e for grp in groups[g0:g0 + per] for e in grp)
        vals = {e: load(e) for e in elems}
        for k, j in stages:
            for e in elems:
                l = e ^ j
                if l > e:
                    hi = jnp.maximum(vals[e], vals[l])
                    lo = jnp.minimum(vals[e], vals[l])
                    vals[e], vals[l] = (hi, lo) if (e & k) == 0 else (lo, hi)
        for e in elems:
            store(e, vals[e])


def _sortable(x):
    return jnp.where(x == 0.0, 0.0, x)


def _dsa_sel_kernel(qi_ref, wt_ref, kn_ref, tri_ref, o_ref, keys_scr, seen_scr, work_scr, top_scr, *, tile, topk):
    n_chunks = kn_ref.shape[0]
    i = pl.program_id(0)
    n_live = i + 1
    groups_per_chunk = tile // 8
    causal = lax.broadcasted_iota(I32, (tile, tile), 0) <= lax.broadcasted_iota(I32, (tile, tile), 1)
    q_heads = [qi_ref[:, h * IDX_DIM:(h + 1) * IDX_DIM] for h in range(IDX_HEADS)]
    w_heads = [wt_ref[h:h + 1, :] for h in range(IDX_HEADS)]

    def score_chunk(c, diagonal):
        kc = kn_ref[c]
        dots = [lax.dot_general(kc, q_heads[h], _NT, preferred_element_type=F32) for h in range(IDX_HEADS)]
        sc = None
        for h in range(IDX_HEADS):
            term = w_heads[h] * jnp.maximum(dots[h], 0.0)
            sc = term if sc is None else sc + term
        if diagonal:
            sc = jnp.where(causal, sc, -jnp.inf)
        keys_scr[c] = _sortable(sc)

    def for_past_chunks(fn):
        def pair(p, carry):
            fn(2 * p)
            fn(2 * p + 1)
            return carry
        lax.fori_loop(0, i // 2, pair, 0)
        pl.when(i % 2 == 1)(lambda: fn(i - 1))

    for_past_chunks(lambda c: score_chunk(c, False))
    score_chunk(i, True)

    def count(pred_fn):
        def body(c, cnt):
            hits = [pred_fn(keys_scr[c, 8 * g:8 * g + 8, :]).astype(I32) for g in range(tile // 8)]
            while len(hits) > 1:
                hits = [hits[j] + hits[j + 1] for j in range(0, len(hits), 2)]
            return cnt + hits[0]
        cnt = lax.fori_loop(0, n_live, body, jnp.zeros((8, tile), I32))
        return jnp.sum(cnt, axis=0, keepdims=True)

    n_list = DSA_SORT_CHUNKS * groups_per_chunk
    n_batches = (n_live + DSA_SORT_CHUNKS - 1) // DSA_SORT_CHUNKS

    def pad_chunk(c, carry):
        keys_scr[c] = jnp.full((tile, tile), -jnp.inf, F32)
        return carry

    lax.fori_loop(n_live, n_batches * DSA_SORT_CHUNKS, pad_chunk, 0)
    top_scr[...] = jnp.full(top_scr.shape, -jnp.inf, F32)
    sort_passes = _plan_passes(_bitonic_stages(n_list, True), groups_per_chunk)
    merge_passes = _plan_passes(_bitonic_stages(n_list, False), groups_per_chunk)
    lane_halves = [slice(lo, lo + LANES) for lo in range(0, tile, LANES)]

    def run_network(passes, first_load, first_store, buf, lanes):
        def load(e):
            return buf[e, :, lanes]

        def store(e, v):
            buf[e, :, lanes] = v

        for n_pass, (mask, stages) in enumerate(passes):
            _run_pass(first_load if n_pass == 0 else load, first_store if n_pass == 0 else store,
                      n_list, mask, stages)

    def sort_batch(b, carry):
        for lanes in lane_halves:
            def load_keys(e, lanes=lanes):
                g = e % groups_per_chunk
                return keys_scr[b * DSA_SORT_CHUNKS + e // groups_per_chunk, 8 * g:8 * g + 8, lanes]

            def store_work(e, v, lanes=lanes):
                work_scr[e, :, lanes] = v

            run_network(sort_passes, load_keys, store_work, work_scr, lanes)

            def load_best(e, lanes=lanes):
                return jnp.maximum(top_scr[e, :, lanes], work_scr[n_list - 1 - e, :, lanes])

            def store_top(e, v, lanes=lanes):
                top_scr[e, :, lanes] = v

            run_network(merge_passes, load_best, store_top, top_scr, lanes)
        return carry

    lax.fori_loop(0, n_batches, sort_batch, 0)

    src, dst = top_scr, work_scr
    for shift in (4, 2, 1):
        for lanes in lane_halves:
            def load_pair(e, lanes=lanes, src=src, shift=shift):
                return jnp.maximum(src[e, :, lanes], pltpu.roll(src[n_list - 1 - e, :, lanes], shift, 0))

            def store_dst(e, v, lanes=lanes, dst=dst):
                dst[e, :, lanes] = v

            run_network(merge_passes, load_pair, store_dst, dst, lanes)
        src, dst = dst, src
    thr = src[topk - 1, 0:1, :]
    thr8 = jnp.broadcast_to(thr, (8, tile))
    n_gt = count(lambda kc: kc > thr8)
    need = (topk - n_gt).astype(F32)

    seen_scr[...] = jnp.zeros(seen_scr.shape, F32)

    def emit_chunk(c, diagonal):
        kc = keys_scr[c]
        eq = kc == thr
        eqf = eq.astype(F32)
        seen = seen_scr[...]
        seen_scr[...] = seen + jnp.sum(eqf, axis=0, keepdims=True)
        rank = seen + jnp.dot(tri_ref[...], eqf.astype(BF16), preferred_element_type=F32)
        sel = (kc > thr) | (eq & (rank < need))
        if diagonal:
            sel = sel & causal
        o_ref[0, c] = jnp.where(sel, 0.0, NEG).astype(BF16)

    for_past_chunks(lambda c: emit_chunk(c, False))
    emit_chunk(i, True)

    def fill_chunk(c, carry):
        o_ref[0, c] = jnp.full((tile, tile), NEG, BF16)
        return carry

    lax.fori_loop(n_live, n_chunks, fill_chunk, 0)


def _dsa_select(q_idx_bf16, w_idx, kn_bf16, topk):
    s = q_idx_bf16.shape[0]
    t = ATT_TILE
    n_chunks = s // t
    n_list = DSA_SORT_CHUNKS * (t // 8)
    assert topk <= n_list, "the sorting network keeps n_list best keys per query"
    kn3 = kn_bf16.reshape(n_chunks, t, IDX_DIM)
    wt = jnp.pad(w_idx.T, ((0, 8 - IDX_HEADS), (0, 0)))
    tri = (lax.broadcasted_iota(I32, (t, t), 1) < lax.broadcasted_iota(I32, (t, t), 0)).astype(BF16)
    return pl.pallas_call(
        functools.partial(_dsa_sel_kernel, tile=t, topk=topk),
        grid=(n_chunks,),
        in_specs=[pl.BlockSpec((t, IDX_HEADS * IDX_DIM), lambda i: (i, 0)),
                  pl.BlockSpec((8, t), lambda i: (0, i)),
                  pl.BlockSpec((n_chunks, t, IDX_DIM), lambda i: (0, 0, 0)),
                  pl.BlockSpec((t, t), lambda i: (0, 0))],
        out_specs=pl.BlockSpec((1, n_chunks, t, t), lambda i: (i, 0, 0, 0)),
        out_shape=jax.ShapeDtypeStruct((n_chunks, n_chunks, t, t), BF16),
        scratch_shapes=[pltpu.VMEM((max(n_chunks, DSA_SORT_CHUNKS), t, t), F32), pltpu.VMEM((1, t), F32),
                        pltpu.VMEM((n_list, 8, t), F32), pltpu.VMEM((n_list, 8, t), F32)],
        compiler_params=_cparams(("arbitrary",)),
    )(q_idx_bf16, wt, kn3, tri)


def _out_proj_kernel(y_ref, w_ref, x_ref, g_ref, o_ref):
    o_ref[...] = x_ref[...] + g_ref[...] * jnp.dot(y_ref[...], w_ref[...], preferred_element_type=F32)


def _out_proj(y_bf16, w_bf16, x, gate, tm=512):
    s, d = x.shape
    k = y_bf16.shape[1]
    return pl.pallas_call(
        _out_proj_kernel,
        grid=(s // tm,),
        in_specs=[pl.BlockSpec((tm, k), lambda i: (i, 0)),
                  pl.BlockSpec((k, d), lambda i: (0, 0)),
                  pl.BlockSpec((tm, d), lambda i: (i, 0)),
                  pl.BlockSpec((1, d), lambda i: (0, 0))],
        out_specs=pl.BlockSpec((tm, d), lambda i: (i, 0)),
        out_shape=jax.ShapeDtypeStruct((s, d), F32),
        compiler_params=_cparams(("arbitrary",)),
    )(y_bf16, w_bf16, x, gate.reshape(1, d))


def _cmp_exchange(v, i, j):
    hi = jnp.maximum(v[i], v[j])
    lo = jnp.minimum(v[i], v[j])
    v[i], v[j] = hi, lo


def _bitonic_sort_desc(v):
    n = len(v)
    k = 2
    while k <= n:
        j = k // 2
        while j >= 1:
            for i in range(n):
                l = i ^ j
                if l > i:
                    if (i & k) == 0:
                        _cmp_exchange(v, i, l)
                    else:
                        _cmp_exchange(v, l, i)
            j //= 2
        k *= 2


def _bitonic_merge_desc(v):
    n = len(v)
    j = n // 2
    while j >= 1:
        for i in range(n):
            l = i ^ j
            if l > i:
                _cmp_exchange(v, i, l)
        j //= 2


def _top16_rows(x):
    tt = x.shape[1]
    v = [x[8 * i:8 * i + 8, :] for i in range(PEER_NKEYS // 8)]
    _bitonic_sort_desc(v)
    for shift in (4, 2, 1):
        other = [pltpu.roll(a, shift, 0) for a in v]
        v = [jnp.maximum(v[i], other[len(v) - 1 - i]) for i in range(len(v))]
        _bitonic_merge_desc(v)
    return v


def _peer_route_kernel(q_ref, sk_ref, r1_ref, e1_ref, e0_ref, ns_ref, s0_scr, s1_scr, *, tt):
    sub = lax.broadcasted_iota(I32, (8, tt), 0)
    tops = [[], []]
    for h in range(PEER_HEADS):
        for p in range(2):
            lo = (2 * h + p) * (PEER_DKEY // 2)
            qhp = q_ref[:, lo:lo + PEER_DKEY // 2].astype(BF16)
            sc = lax.dot_general(sk_ref[2 * h + p], qhp, _NT, preferred_element_type=F32)
            (s0_scr if p == 0 else s1_scr)[h] = sc
            tops[p].append(_top16_rows(sc))
    a = [sum(jnp.where(sub == h, tops[0][h][k], 0.0) for h in range(PEER_HEADS)) for k in range(PEER_TOPK)]
    b = [sum(jnp.where(sub == h, tops[1][h][k], 0.0) for h in range(PEER_HEADS)) for k in range(PEER_TOPK)]
    cand = [a[i] + b[j] for i in range(PEER_TOPK) for j in range(PEER_TOPK) if (i + 1) * (j + 1) <= PEER_TOPK]
    cand = cand + [jnp.full((8, tt), -jnp.inf, F32)] * (64 - len(cand))
    _bitonic_sort_desc(cand)
    tau = cand[PEER_TOPK - 1]
    z = sum(jnp.exp(cand[k] - cand[0]) for k in range(PEER_TOPK))
    half_inv_z = 0.5 / z
    for h in range(PEER_HEADS):
        s0 = s0_scr[h]
        s1 = s1_scr[h]
        tau_h = tau[h:h + 1, :]
        nsel = jnp.zeros((PEER_NKEYS, tt), F32)
        rank1 = jnp.zeros((PEER_NKEYS, tt), F32)
        for k in range(PEER_TOPK):
            bk = b[k][h:h + 1, :]
            nsel = nsel + jnp.where(s0 + bk >= tau_h, 1.0, 0.0)
            rank1 = rank1 + jnp.where(bk > s1, 1.0, 0.0)
        ns_ref[h] = nsel
        r1_ref[h] = rank1.astype(BF16)
        e0_ref[h] = jnp.exp(s0 - a[0][h:h + 1, :]) * half_inv_z[h:h + 1, :]
        e1_ref[h] = jnp.exp(s1 - b[0][h:h + 1, :]).astype(BF16)


def _peer_route(q, subkeys_bf16, tt=256):
    s = q.shape[0]
    big = lambda dt: jax.ShapeDtypeStruct((PEER_HEADS, PEER_NKEYS, s), dt)
    big_spec = pl.BlockSpec((PEER_HEADS, PEER_NKEYS, tt), lambda i: (0, 0, i))
    return pl.pallas_call(
        functools.partial(_peer_route_kernel, tt=tt),
        grid=(s // tt,),
        in_specs=[pl.BlockSpec((tt, PEER_HEADS * PEER_DKEY), lambda i: (i, 0)),
                  pl.BlockSpec((2 * PEER_HEADS, PEER_NKEYS, PEER_DKEY // 2), lambda i: (0, 0, 0))],
        out_specs=[big_spec, big_spec, big_spec, big_spec],
        out_shape=[big(BF16), big(BF16), big(F32), big(F32)],
        scratch_shapes=[pltpu.VMEM((PEER_HEADS, PEER_NKEYS, tt), F32),
                        pltpu.VMEM((PEER_HEADS, PEER_NKEYS, tt), F32)],
        compiler_params=_cparams(("arbitrary",)),
    )(q, subkeys_bf16)


def _peer_dense_kernel(ht_ref, u_ref, vt_ref, r1_ref, e1_ref, e0_ref, ns_ref,
                       x_ref, g_ref, gf_ref, o_ref, acc_scr, p_scr, *, rows_per_step, final_norm):
    k = pl.program_id(1)

    @pl.when(k == 0)
    def _():
        acc_scr[...] = jnp.zeros(acc_scr.shape, F32)

    tt = ht_ref.shape[1]
    pack_rows = 16
    zero = jnp.zeros((pack_rows, tt), BF16)

    def activations(c):
        return jnp.dot(u_ref[c * MXU_DEPTH:(c + 1) * MXU_DEPTH, :], ht_ref[...], preferred_element_type=F32)

    def expert_rows(ii, a):
        i = k * rows_per_step + ii
        def bcast(ref, h):
            half = jnp.broadcast_to(ref[h, pl.ds(i, 1), :], (pack_rows // 2, tt))
            return jnp.concatenate([half, half], axis=0).astype(BF16)
        e0_rows = [bcast(e0_ref, h) for h in range(PEER_HEADS)]
        ns_rows = [bcast(ns_ref, h) for h in range(PEER_HEADS)]
        gelu2 = (a * (1.0 + lax.erf(a * math.sqrt(0.5)))).astype(BF16)
        for g in range(PEER_NKEYS // pack_rows):
            rows = slice(g * pack_rows, (g + 1) * pack_rows)
            wgt = None
            for h in range(PEER_HEADS):
                term = jnp.where(r1_ref[h, rows, :] < ns_rows[h], e1_ref[h, rows, :], zero) * e0_rows[h]
                wgt = term if wgt is None else wgt + term
            p_scr[ii * PEER_NKEYS + g * pack_rows:ii * PEER_NKEYS + (g + 1) * pack_rows, :] = wgt * gelu2[rows, :]

    depth_rows = MXU_DEPTH // PEER_NKEYS
    n_chunks = rows_per_step // depth_rows
    act = activations(0)
    for c in range(n_chunks):
        act_next = activations(c + 1) if c + 1 < n_chunks else None
        for j in range(depth_rows):
            expert_rows(c * depth_rows + j, act[j * PEER_NKEYS:(j + 1) * PEER_NKEYS, :])
        act = act_next
    acc_scr[...] += jnp.dot(vt_ref[...], p_scr[...], preferred_element_type=F32)

    @pl.when(k == pl.num_programs(1) - 1)
    def _():
        y = x_ref[...] + g_ref[...] * acc_scr[...].T
        if final_norm:
            y = y * lax.rsqrt(jnp.mean(y * y, axis=-1, keepdims=True) + EPS) * gf_ref[...]
        o_ref[...] = y


def _peer_dense(ht_bf16, u_bf16, vt_bf16, rank1, e1, e0, nsel, x, gate, g_final, final_norm,
                tt=512, rows_per_step=8):
    s, d = x.shape
    n_exp = u_bf16.shape[0]
    et = rows_per_step * PEER_NKEYS
    tok = lambda j, k: (0, 0, j)
    big_spec = pl.BlockSpec((PEER_HEADS, PEER_NKEYS, tt), tok)
    return pl.pallas_call(
        functools.partial(_peer_dense_kernel, rows_per_step=rows_per_step, final_norm=final_norm),
        grid=(s // tt, n_exp // et),
        in_specs=[pl.BlockSpec((d, tt), lambda j, k: (0, j)),
                  pl.BlockSpec((et, d), lambda j, k: (k, 0)),
                  pl.BlockSpec((d, et), lambda j, k: (0, k)),
                  big_spec, big_spec, big_spec, big_spec,
                  pl.BlockSpec((tt, d), lambda j, k: (j, 0)),
                  pl.BlockSpec((1, d), lambda j, k: (0, 0)),
                  pl.BlockSpec((1, d), lambda j, k: (0, 0))],
        out_specs=pl.BlockSpec((tt, d), lambda j, k: (j, 0)),
        out_shape=jax.ShapeDtypeStruct((s, d), F32),
        scratch_shapes=[pltpu.VMEM((d, tt), F32), pltpu.VMEM((et, tt), BF16)],
        compiler_params=_cparams(("arbitrary", "arbitrary")),
    )(ht_bf16, u_bf16, vt_bf16, rank1, e1, e0, nsel, x, gate.reshape(1, d), g_final.reshape(1, d))


def _heads_first(t, nh):
    s = t.shape[0]
    return jnp.transpose(t.reshape(s, nh, HEAD_DIM), (1, 0, 2))


def _mixer(h_proj, w_out_bf16, bias_tiles, x, gate):
    s = h_proj.shape[0]
    parts = []
    off = 0
    for size in PROJ_SIZES:
        parts.append(h_proj[:, off:off + size])
        off += size
    mq, mk, mv, dq, dk, dv, iq, ik, iw = parts
    scale = HEAD_DIM ** -0.5 * LOG2E
    dims_first = lambda a, nh: a.T.reshape(nh, HEAD_DIM, s).astype(BF16)

    mq_h, mk_h = _heads_first(mq, MOBA_HEADS), _heads_first(mk, MOBA_HEADS)
    block_sel = _moba_select(mq_h, _moba_kmean(mk_h))
    blk = jnp.arange(s, dtype=I32) // MOBA_BLOCK
    onehot = (blk[:, None] == jnp.arange(MOBA_MAX_BLOCKS, dtype=I32)[None, :]).astype(BF16)
    q_aug = jnp.concatenate([(mq_h * scale).astype(BF16), jnp.transpose(block_sel, (0, 2, 1)).astype(BF16)], axis=-1)
    k_aug = jnp.concatenate([mk_h.astype(BF16), jnp.broadcast_to(onehot, (MOBA_HEADS, s, MOBA_MAX_BLOCKS))], axis=-1)
    y_moba = _attention(q_aug, k_aug, dims_first(mv, MOBA_HEADS), bias_tiles, 0)

    ikf = ik.astype(F32)
    ik_n = ikf * lax.rsqrt(jnp.mean(ikf * ikf, axis=-1, keepdims=True) + EPS)
    topk = min(DSA_TOPK_MAX, s // 4)
    mask_bias = _dsa_select(iq.astype(BF16), iw * IDX_HEADS ** -0.5, ik_n.astype(BF16), topk)
    dq_h, dk_h = _heads_first(dq, DSA_HEADS), _heads_first(dk, DSA_HEADS)
    y_dsa = _attention((dq_h * scale).astype(BF16), dk_h.astype(BF16), dims_first(dv, DSA_HEADS),
                       bias_tiles, 1, mask_bias=mask_bias)

    y = jnp.concatenate([y_moba.reshape(MOBA_WIDTH, s), y_dsa.reshape(DSA_WIDTH, s)], axis=0).T
    return _out_proj(y.astype(BF16), w_out_bf16, x, gate)


def kernel(x, c, w_ada, b_ada, norm_attn, norm_ffn, w_in, w_out, rel_bias,
           peer_wq, peer_subkeys, peer_u, peer_v, norm_final):
    batch, s, d = x.shape
    assert batch == 1 and s % (ATT_KEY_TILES_PER_STEP * ATT_TILE) == 0
    depth = w_ada.shape[0]
    x = x[0]
    mod = _ada_mod(c, w_ada, b_ada)
    bias_tiles = _bias_tiles(rel_bias, ATT_TILE)
    n_pad = -(-PROJ_WIDTH // LANES) * LANES
    for l in range(depth):
        sh1, sc1, g1, sh2, sc2, g2 = jnp.split(mod[l], 6)
        w_in_l = jnp.pad(w_in[l], ((0, 0), (0, n_pad - PROJ_WIDTH))).astype(BF16)
        _, proj = _norm_mod_matmul(x, norm_attn[l], sh1, sc1, w_in_l)
        x = _mixer(proj, w_out[l].astype(BF16), bias_tiles, x, g1)
        h2, q = _norm_mod_matmul(x, norm_ffn[l], sh2, sc2, peer_wq[l].astype(BF16))
        subkeys = peer_subkeys[l].reshape(2 * PEER_HEADS, PEER_NKEYS, PEER_DKEY // 2).astype(BF16)
        rank1, e1, e0, nsel = _peer_route(q, subkeys)
        x = _peer_dense(h2.T, peer_u[l].astype(BF16), peer_v[l].T.astype(BF16), rank1, e1, e0, nsel,
                        x, g2, norm_final, final_norm=(l == depth - 1))
    return x[None]
```

```python
import functools
import math

import jax
import jax.numpy as jnp
from jax import lax
from jax.experimental import pallas as pl
from jax.experimental.pallas import tpu as pltpu

F32 = jnp.float32
BF16 = jnp.bfloat16
I32 = jnp.int32

HEAD_DIM = 64
MOBA_HEADS = 8
DSA_HEADS = 8
N_HEADS = MOBA_HEADS + DSA_HEADS
MOBA_WIDTH = MOBA_HEADS * HEAD_DIM
DSA_WIDTH = DSA_HEADS * HEAD_DIM
MOBA_BLOCK = 256
MOBA_TOPK = 3
MOBA_MAX_BLOCKS = 64
IDX_HEADS = 4
IDX_DIM = 64
DSA_TOPK_MAX = 256
DSA_SORT_CHUNKS = 8
N_BUCKETS = 32
MAX_DISTANCE = 4096
PEER_HEADS = 8
PEER_NKEYS = 128
PEER_DKEY = 256
PEER_TOPK = 16
EPS = 1e-6
PROJ_SIZES = (MOBA_WIDTH, MOBA_WIDTH, MOBA_WIDTH, DSA_WIDTH, DSA_WIDTH, DSA_WIDTH,
              IDX_HEADS * IDX_DIM, IDX_DIM, IDX_HEADS)
PROJ_WIDTH = sum(PROJ_SIZES)

NEG = -1e30
LOG2E = math.log2(math.e)
ATT_TILE = MOBA_BLOCK
ATT_KEY_TILES_PER_STEP = 4
LANES = 128
MXU_DEPTH = 256
VMEM_LIMIT = 56 * 1024 * 1024

_NT = (((1,), (1,)), ((), ()))


def _far_bias_tiles(tile):
    max_exact = N_BUCKETS // 2
    sat = int(math.ceil(max_exact * (MAX_DISTANCE / max_exact) ** ((N_BUCKETS - max_exact - 1) / (N_BUCKETS - max_exact)))) + 4
    return -(-(sat + tile - 1) // tile)


def _cparams(sem):
    return pltpu.CompilerParams(dimension_semantics=sem, vmem_limit_bytes=VMEM_LIMIT)


def _ada_kernel(c_ref, w_ref, b_ref, o_ref):
    c = c_ref[...]
    sc = c * jax.nn.sigmoid(c)
    o_ref[0] = jnp.dot(sc, w_ref[0], preferred_element_type=F32,
                       precision=lax.Precision.HIGHEST) + b_ref[0]


def _ada_mod(c, w_ada, b_ada):
    depth, d, n = w_ada.shape
    tn = 1536
    c8 = jnp.broadcast_to(c, (8, d))
    out = pl.pallas_call(
        _ada_kernel,
        grid=(depth, n // tn),
        in_specs=[pl.BlockSpec((8, d), lambda l, j: (0, 0)),
                  pl.BlockSpec((1, d, tn), lambda l, j: (l, 0, j)),
                  pl.BlockSpec((1, 1, tn), lambda l, j: (l, 0, j))],
        out_specs=pl.BlockSpec((1, 8, tn), lambda l, j: (l, 0, j)),
        out_shape=jax.ShapeDtypeStruct((depth, 8, n), F32),
        compiler_params=_cparams(("arbitrary", "arbitrary")),
    )(c8, w_ada, b_ada.reshape(depth, 1, n))
    return out[:, 0, :]


def _nmm_kernel(x_ref, g_ref, sh_ref, sc_ref, w_ref, h_ref, o_ref):
    x = x_ref[...]
    y = x * lax.rsqrt(jnp.mean(x * x, axis=-1, keepdims=True) + EPS) * g_ref[...]
    hb = (y * (1.0 + sc_ref[...]) + sh_ref[...]).astype(BF16)
    h_ref[...] = hb
    o_ref[...] = jnp.dot(hb, w_ref[...], preferred_element_type=F32)


def _norm_mod_matmul(x, g, shift, scale, w_bf16, tm=256):
    s, d = x.shape
    n = w_bf16.shape[1]
    row = lambda i: (0, 0)
    return pl.pallas_call(
        _nmm_kernel,
        grid=(s // tm,),
        in_specs=[pl.BlockSpec((tm, d), lambda i: (i, 0)),
                  pl.BlockSpec((1, d), row), pl.BlockSpec((1, d), row), pl.BlockSpec((1, d), row),
                  pl.BlockSpec((d, n), row)],
        out_specs=[pl.BlockSpec((tm, d), lambda i: (i, 0)),
                   pl.BlockSpec((tm, n), lambda i: (i, 0))],
        out_shape=[jax.ShapeDtypeStruct((s, d), BF16), jax.ShapeDtypeStruct((s, n), F32)],
        compiler_params=_cparams(("arbitrary",)),
    )(x, g.reshape(1, d), shift.reshape(1, d), scale.reshape(1, d), w_bf16)


def _t5_bucket(d):
    max_exact = N_BUCKETS // 2
    d = jnp.maximum(d, 0)
    df = jnp.maximum(d, 1).astype(F32)
    large = max_exact + (jnp.log(df / max_exact) / math.log(MAX_DISTANCE / max_exact)
                         * (N_BUCKETS - max_exact)).astype(I32)
    return jnp.where(d < max_exact, d, jnp.minimum(large, N_BUCKETS - 1))


def _bias_tile_kernel(tab_ref, o_ref, *, tile, n_near):
    h = pl.program_id(0)
    off = pl.program_id(1)
    b = lax.broadcasted_iota(I32, (tile, tile), 0)
    a = lax.broadcasted_iota(I32, (tile, tile), 1)
    d = jnp.where(off > n_near, -1, off * tile + a - b)
    bucket = _t5_bucket(d)
    val = jnp.zeros((tile, tile), F32)
    for k in range(N_BUCKETS):
        val = jnp.where(bucket == k, tab_ref[h, k], val)
    val = (val - tab_ref[h, N_BUCKETS - 1]) * LOG2E
    o_ref[0, 0] = jnp.where(d < 0, NEG, val)


def _bias_tiles(rel_bias, tile):
    n_near = _far_bias_tiles(tile)
    nh = rel_bias.shape[0]
    return pl.pallas_call(
        functools.partial(_bias_tile_kernel, tile=tile, n_near=n_near),
        grid=(nh, n_near + 2),
        in_specs=[pl.BlockSpec(memory_space=pltpu.SMEM)],
        out_specs=pl.BlockSpec((1, 1, tile, tile), lambda h, o: (h, o, 0, 0)),
        out_shape=jax.ShapeDtypeStruct((nh, n_near + 2, tile, tile), F32),
        compiler_params=_cparams(("arbitrary", "arbitrary")),
    )(rel_bias)


def _kmean_kernel(k_ref, o_ref, *, nb):
    k = k_ref[0]
    km = jnp.mean(k.reshape(nb, MOBA_BLOCK, HEAD_DIM), axis=1)
    o_ref[0] = jnp.zeros(o_ref.shape[1:], F32)
    o_ref[0, :nb, :] = km


def _moba_kmean(k_hsd):
    nh, s, dh = k_hsd.shape
    nb = s // MOBA_BLOCK
    return pl.pallas_call(
        functools.partial(_kmean_kernel, nb=nb),
        grid=(nh,),
        in_specs=[pl.BlockSpec((1, s, dh), lambda h: (h, 0, 0))],
        out_specs=pl.BlockSpec((1, MOBA_MAX_BLOCKS, dh), lambda h: (h, 0, 0)),
        out_shape=jax.ShapeDtypeStruct((nh, MOBA_MAX_BLOCKS, dh), F32),
        compiler_params=_cparams(("arbitrary",)),
    )(k_hsd)


def _moba_sel_kernel(q_ref, km_ref, o_ref, *, tq, n_sel):
    gate = lax.dot_general(km_ref[0], q_ref[0], _NT, preferred_element_type=F32,
                           precision=lax.Precision.HIGHEST)
    t = pl.program_id(1) * tq + lax.broadcasted_iota(I32, (MOBA_MAX_BLOCKS, tq), 1)
    b0 = t // MOBA_BLOCK
    n = lax.broadcasted_iota(I32, (MOBA_MAX_BLOCKS, tq), 0)
    avail = n < b0
    taken = n == b0
    for _ in range(n_sel):
        g = jnp.where(avail, gate, -jnp.inf)
        m = jnp.max(g, axis=0, keepdims=True)
        idx = jnp.min(jnp.where((g == m) & avail, n, 1 << 20), axis=0, keepdims=True)
        pick = n == idx
        taken = taken | pick
        avail = avail & jnp.logical_not(pick)
    o_ref[0] = jnp.where(taken, 0.0, NEG)


def _moba_select(q_hsd, kmean, tq=512):
    nh, s, dh = q_hsd.shape
    nb = s // MOBA_BLOCK
    assert nb <= MOBA_MAX_BLOCKS
    n_sel = max(min(MOBA_TOPK, nb - 1), 1)
    tq = min(tq, s)
    return pl.pallas_call(
        functools.partial(_moba_sel_kernel, tq=tq, n_sel=n_sel),
        grid=(nh, s // tq),
        in_specs=[pl.BlockSpec((1, tq, dh), lambda h, i: (h, i, 0)),
                  pl.BlockSpec((1, MOBA_MAX_BLOCKS, dh), lambda h, i: (h, 0, 0))],
        out_specs=pl.BlockSpec((1, MOBA_MAX_BLOCKS, tq), lambda h, i: (h, 0, i)),
        out_shape=jax.ShapeDtypeStruct((nh, MOBA_MAX_BLOCKS, s), F32),
        compiler_params=_cparams(("arbitrary", "arbitrary")),
    )(q_hsd, kmean)


def _attn_kernel(qi_ref, kg_ref, *refs, nh, n_near, group, has_mask, has_block_sel):
    q_ref, k_ref, vt_ref, bt_ref = refs[:4]
    extra_ref = refs[4] if (has_mask or has_block_sel) else None
    o_ref, m_scr, acc_scr = refs[-3:]
    t = ATT_TILE
    step = pl.program_id(0)
    qi = qi_ref[step]
    kg = kg_ref[step]

    @pl.when(kg == 0)
    def _():
        m_scr[...] = jnp.full(m_scr.shape, NEG, F32)
        acc_scr[...] = jnp.zeros(acc_scr.shape, F32)

    def step_update(with_bias):
        offs = [qi - (kg * group + g) for g in range(group)]
        bias_rows = [jnp.where(off < 0, n_near + 1, jnp.minimum(off, n_near)) for off in offs]
        scores = []
        for h in range(nh):
            s = lax.dot_general(k_ref[h], q_ref[h], _NT, preferred_element_type=F32)
            parts = []
            for g in range(group):
                sg = s[g * t:(g + 1) * t, :]
                if with_bias:
                    sg = sg + bt_ref[h, bias_rows[g]]
                if has_mask:
                    sg = sg + extra_ref[0, g].astype(F32)
                if has_block_sel:
                    sg = sg + extra_ref[h, pl.ds(kg * group + g, 1), :]
                parts.append(sg)
            scores.append(parts)
        probs, alphas = [], []
        for h in range(nh):
            m_old = m_scr[h]
            tile_max = [jnp.max(sg, axis=0, keepdims=True) for sg in scores[h]]
            m_new = functools.reduce(jnp.maximum, tile_max, m_old)
            alphas.append(jnp.exp2(m_old - m_new))
            probs.append(jnp.concatenate([jnp.exp2(sg - m_new).astype(BF16) for sg in scores[h]], axis=0))
            m_scr[h] = m_new
        for h in range(nh):
            acc_scr[h] = alphas[h] * acc_scr[h] + jnp.dot(vt_ref[h], probs[h], preferred_element_type=F32)

    all_far = qi - (kg * group + group - 1) >= n_near
    pl.when(all_far)(functools.partial(step_update, False))
    pl.when(jnp.logical_not(all_far))(functools.partial(step_update, True))

    @pl.when(kg == qi // group)
    def _():
        for h in range(nh):
            o_ref[h] = acc_scr[h, :HEAD_DIM, :] / acc_scr[h, HEAD_DIM:HEAD_DIM + 1, :]


def _attention(q, k, vt, bias_tiles, head_group, mask_bias=None, block_sel=None):
    assert mask_bias is None or block_sel is None
    nh, s, dk = q.shape
    vt = jnp.concatenate([vt, jnp.ones((nh, 8, s), vt.dtype)], axis=1)
    dv = vt.shape[1]
    t = ATT_TILE
    group = ATT_KEY_TILES_PER_STEP
    nq = s // t
    n_near = bias_tiles.shape[1] - 2
    qi_list = [i for i in range(nq) for _ in range(i // group + 1)]
    kg_list = [j for i in range(nq) for j in range(i // group + 1)]
    qi_arr = jnp.asarray(qi_list, I32)
    kg_arr = jnp.asarray(kg_list, I32)
    in_specs = [pl.BlockSpec((nh, t, dk), lambda st, qi, kg: (0, qi[st], 0)),
                pl.BlockSpec((nh, group * t, dk), lambda st, qi, kg: (0, kg[st], 0)),
                pl.BlockSpec((nh, dv, group * t), lambda st, qi, kg: (0, 0, kg[st])),
                pl.BlockSpec((nh, n_near + 2, t, t), lambda st, qi, kg: (head_group, 0, 0, 0),
                             pipeline_mode=pl.Buffered(1))]
    args = [q, k, vt, bias_tiles]
    if mask_bias is not None:
        in_specs.append(pl.BlockSpec((1, group, t, t), lambda st, qi, kg: (qi[st], kg[st], 0, 0)))
        args.append(mask_bias)
    if block_sel is not None:
        in_specs.append(pl.BlockSpec((nh, MOBA_MAX_BLOCKS, t), lambda st, qi, kg: (0, 0, qi[st])))
        args.append(block_sel)
    grid_spec = pltpu.PrefetchScalarGridSpec(
        num_scalar_prefetch=2,
        grid=(len(qi_list),),
        in_specs=in_specs,
        out_specs=pl.BlockSpec((nh, HEAD_DIM, t), lambda st, qi, kg: (0, 0, qi[st])),
        scratch_shapes=[pltpu.VMEM((nh, 1, t), F32), pltpu.VMEM((nh, dv, t), F32)])
    return pl.pallas_call(
        functools.partial(_attn_kernel, nh=nh, n_near=n_near, group=group,
                          has_mask=mask_bias is not None, has_block_sel=block_sel is not None),
        grid_spec=grid_spec,
        out_shape=jax.ShapeDtypeStruct((nh, HEAD_DIM, s), F32),
        compiler_params=_cparams(("arbitrary",)),
    )(qi_arr, kg_arr, *args)


def _bitonic_stages(n, full_sort):
    levels = [2 ** p for p in range(1, n.bit_length())] if full_sort else [n]
    return [(k, k >> s) for k in levels for s in range(1, k.bit_length())]


def _plan_passes(stages, block):
    passes = []
    for k, j in stages:
        cross = j >= block
        if passes and passes[-1][2] == cross and (not cross or bin(passes[-1][0] | j).count("1") <= 3):
            passes[-1][0] |= j if cross else 0
            passes[-1][1].append((k, j))
        else:
            passes.append([j if cross else block - 1, [(k, j)], cross])
    return [(mask, sub) for mask, sub, _ in passes]


def _run_pass(load, store, n, mask, stages, regs=32):
    groups = {}
    for e in range(n):
        groups.setdefault(e & ~mask, []).append(e)
    groups = list(groups.values())
    per = max(1, regs // len(groups[0]))
    for g0 in range(0, len(groups), per):
        elems = sorted(e for grp in groups[g0:g0 + per] for e in grp)
        vals = {e: load(e) for e in elems}
        for k, j in stages:
            for e in elems:
                l = e ^ j
                if l > e:
                    hi = jnp.maximum(vals[e], vals[l])
                    lo = jnp.minimum(vals[e], vals[l])
                    vals[e], vals[l] = (hi, lo) if (e & k) == 0 else (lo, hi)
        for e in elems:
            store(e, vals[e])


def _sortable(x):
    return jnp.where(x == 0.0, 0.0, x)


def _dsa_sel_kernel(qi_ref, wt_ref, kn_ref, tri_ref, o_ref, keys_scr, seen_scr, work_scr, top_scr, *, tile, topk):
    n_chunks = kn_ref.shape[0]
    i = pl.program_id(0)
    n_live = i + 1
    groups_per_chunk = tile // 8
    causal = lax.broadcasted_iota(I32, (tile, tile), 0) <= lax.broadcasted_iota(I32, (tile, tile), 1)
    q_heads = [qi_ref[:, h * IDX_DIM:(h + 1) * IDX_DIM] for h in range(IDX_HEADS)]
    w_heads = [wt_ref[h:h + 1, :] for h in range(IDX_HEADS)]

    def score_chunk(c, diagonal):
        kc = kn_ref[c]
        dots = [lax.dot_general(kc, q_heads[h], _NT, preferred_element_type=F32) for h in range(IDX_HEADS)]
        sc = None
        for h in range(IDX_HEADS):
            term = w_heads[h] * jnp.maximum(dots[h], 0.0)
            sc = term if sc is None else sc + term
        if diagonal:
            sc = jnp.where(causal, sc, -jnp.inf)
        keys_scr[c] = _sortable(sc)

    def for_past_chunks(fn):
        def pair(p, carry):
            fn(2 * p)
            fn(2 * p + 1)
            return carry
        lax.fori_loop(0, i // 2, pair, 0)
        pl.when(i % 2 == 1)(lambda: fn(i - 1))

    for_past_chunks(lambda c: score_chunk(c, False))
    score_chunk(i, True)

    def count(pred_fn):
        def body(c, cnt):
            hits = [pred_fn(keys_scr[c, 8 * g:8 * g + 8, :]).astype(I32) for g in range(tile // 8)]
            while len(hits) > 1:
                hits = [hits[j] + hits[j + 1] for j in range(0, len(hits), 2)]
            return cnt + hits[0]
        cnt = lax.fori_loop(0, n_live, body, jnp.zeros((8, tile), I32))
        return jnp.sum(cnt, axis=0, keepdims=True)

    n_list = DSA_SORT_CHUNKS * groups_per_chunk
    n_batches = (n_live + DSA_SORT_CHUNKS - 1) // DSA_SORT_CHUNKS

    def pad_chunk(c, carry):
        keys_scr[c] = jnp.full((tile, tile), -jnp.inf, F32)
        return carry

    lax.fori_loop(n_live, n_batches * DSA_SORT_CHUNKS, pad_chunk, 0)
    top_scr[...] = jnp.full(top_scr.shape, -jnp.inf, F32)
    sort_passes = _plan_passes(_bitonic_stages(n_list, True), groups_per_chunk)
    merge_passes = _plan_passes(_bitonic_stages(n_list, False), groups_per_chunk)
    lane_halves = [slice(lo, lo + LANES) for lo in range(0, tile, LANES)]

    def run_network(passes, first_load, first_store, buf, lanes):
        def load(e):
            return buf[e, :, lanes]

        def store(e, v):
            buf[e, :, lanes] = v

        for n_pass, (mask, stages) in enumerate(passes):
            _run_pass(first_load if n_pass == 0 else load, first_store if n_pass == 0 else store,
                      n_list, mask, stages)

    def sort_batch(b, carry):
        for lanes in lane_halves:
            def load_keys(e, lanes=lanes):
                g = e % groups_per_chunk
                return keys_scr[b * DSA_SORT_CHUNKS + e // groups_per_chunk, 8 * g:8 * g + 8, lanes]

            def store_work(e, v, lanes=lanes):
                work_scr[e, :, lanes] = v

            run_network(sort_passes, load_keys, store_work, work_scr, lanes)

            def load_best(e, lanes=lanes):
                return jnp.maximum(top_scr[e, :, lanes], work_scr[n_list - 1 - e, :, lanes])

            def store_top(e, v, lanes=lanes):
                top_scr[e, :, lanes] = v

            run_network(merge_passes, load_best, store_top, top_scr, lanes)
        return carry

    lax.fori_loop(0, n_batches, sort_batch, 0)

    src, dst = top_scr, work_scr
    for shift in (4, 2, 1):
        for lanes in lane_halves:
            def load_pair(e, lanes=lanes, src=src, shift=shift):
                return jnp.maximum(src[e, :, lanes], pltpu.roll(src[n_list - 1 - e, :, lanes], shift, 0))

            def store_dst(e, v, lanes=lanes, dst=dst):
                dst[e, :, lanes] = v

            run_network(merge_passes, load_pair, store_dst, dst, lanes)
        src, dst = dst, src
    thr = src[topk - 1, 0:1, :]
    thr8 = jnp.broadcast_to(thr, (8, tile))
    n_gt = count(lambda kc: kc > thr8)
    need = (topk - n_gt).astype(F32)

    seen_scr[...] = jnp.zeros(seen_scr.shape, F32)

    def emit_chunk(c, diagonal):
        kc = keys_scr[c]
        eq = kc == thr
        eqf = eq.astype(F32)
        seen = seen_scr[...]
        seen_scr[...] = seen + jnp.sum(eqf, axis=0, keepdims=True)
        rank = seen + jnp.dot(tri_ref[...], eqf.astype(BF16), preferred_element_type=F32)
        sel = (kc > thr) | (eq & (rank < need))
        if diagonal:
            sel = sel & causal
        o_ref[0, c] = jnp.where(sel, 0.0, NEG).astype(BF16)

    for_past_chunks(lambda c: emit_chunk(c, False))
    emit_chunk(i, True)

    def fill_chunk(c, carry):
        o_ref[0, c] = jnp.full((tile, tile), NEG, BF16)
        return carry

    lax.fori_loop(n_live, n_chunks, fill_chunk, 0)


def _dsa_select(q_idx_bf16, w_idx, kn_bf16, topk):
    s = q_idx_bf16.shape[0]
    t = ATT_TILE
    n_chunks = s // t
    n_list = DSA_SORT_CHUNKS * (t // 8)
    assert topk <= n_list, "the sorting network keeps n_list best keys per query"
    kn3 = kn_bf16.reshape(n_chunks, t, IDX_DIM)
    wt = jnp.pad(w_idx.T, ((0, 8 - IDX_HEADS), (0, 0)))
    tri = (lax.broadcasted_iota(I32, (t, t), 1) < lax.broadcasted_iota(I32, (t, t), 0)).astype(BF16)
    return pl.pallas_call(
        functools.partial(_dsa_sel_kernel, tile=t, topk=topk),
        grid=(n_chunks,),
        in_specs=[pl.BlockSpec((t, IDX_HEADS * IDX_DIM), lambda i: (i, 0)),
                  pl.BlockSpec((8, t), lambda i: (0, i)),
                  pl.BlockSpec((n_chunks, t, IDX_DIM), lambda i: (0, 0, 0)),
                  pl.BlockSpec((t, t), lambda i: (0, 0))],
        out_specs=pl.BlockSpec((1, n_chunks, t, t), lambda i: (i, 0, 0, 0)),
        out_shape=jax.ShapeDtypeStruct((n_chunks, n_chunks, t, t), BF16),
        scratch_shapes=[pltpu.VMEM((max(n_chunks, DSA_SORT_CHUNKS), t, t), F32), pltpu.VMEM((1, t), F32),
                        pltpu.VMEM((n_list, 8, t), F32), pltpu.VMEM((n_list, 8, t), F32)],
        compiler_params=_cparams(("arbitrary",)),
    )(q_idx_bf16, wt, kn3, tri)


def _out_proj_kernel(y_ref, w_ref, x_ref, g_ref, o_ref):
    o_ref[...] = x_ref[...] + g_ref[...] * jnp.dot(y_ref[...], w_ref[...], preferred_element_type=F32)


def _out_proj(y_bf16, w_bf16, x, gate, tm=512):
    s, d = x.shape
    k = y_bf16.shape[1]
    return pl.pallas_call(
        _out_proj_kernel,
        grid=(s // tm,),
        in_specs=[pl.BlockSpec((tm, k), lambda i: (i, 0)),
                  pl.BlockSpec((k, d), lambda i: (0, 0)),
                  pl.BlockSpec((tm, d), lambda i: (i, 0)),
                  pl.BlockSpec((1, d), lambda i: (0, 0))],
        out_specs=pl.BlockSpec((tm, d), lambda i: (i, 0)),
        out_shape=jax.ShapeDtypeStruct((s, d), F32),
        compiler_params=_cparams(("arbitrary",)),
    )(y_bf16, w_bf16, x, gate.reshape(1, d))


def _cmp_exchange(v, i, j):
    hi = jnp.maximum(v[i], v[j])
    lo = jnp.minimum(v[i], v[j])
    v[i], v[j] = hi, lo


def _bitonic_sort_desc(v):
    n = len(v)
    k = 2
    while k <= n:
        j = k // 2
        while j >= 1:
            for i in range(n):
                l = i ^ j
                if l > i:
                    if (i & k) == 0:
                        _cmp_exchange(v, i, l)
                    else:
                        _cmp_exchange(v, l, i)
            j //= 2
        k *= 2


def _bitonic_merge_desc(v):
    n = len(v)
    j = n // 2
    while j >= 1:
        for i in range(n):
            l = i ^ j
            if l > i:
                _cmp_exchange(v, i, l)
        j //= 2


def _top16_rows(x):
    tt = x.shape[1]
    v = [x[8 * i:8 * i + 8, :] for i in range(PEER_NKEYS // 8)]
    _bitonic_sort_desc(v)
    for shift in (4, 2, 1):
        other = [pltpu.roll(a, shift, 0) for a in v]
        v = [jnp.maximum(v[i], other[len(v) - 1 - i]) for i in range(len(v))]
        _bitonic_merge_desc(v)
    return v


def _peer_route_kernel(q_ref, sk_ref, r1_ref, e1_ref, e0_ref, ns_ref, s0_scr, s1_scr, *, tt):
    sub = lax.broadcasted_iota(I32, (8, tt), 0)
    tops = [[], []]
    for h in range(PEER_HEADS):
        for p in range(2):
            lo = (2 * h + p) * (PEER_DKEY // 2)
            qhp = q_ref[:, lo:lo + PEER_DKEY // 2].astype(BF16)
            sc = lax.dot_general(sk_ref[2 * h + p], qhp, _NT, preferred_element_type=F32)
            (s0_scr if p == 0 else s1_scr)[h] = sc
            tops[p].append(_top16_rows(sc))
    a = [sum(jnp.where(sub == h, tops[0][h][k], 0.0) for h in range(PEER_HEADS)) for k in range(PEER_TOPK)]
    b = [sum(jnp.where(sub == h, tops[1][h][k], 0.0) for h in range(PEER_HEADS)) for k in range(PEER_TOPK)]
    cand = [a[i] + b[j] for i in range(PEER_TOPK) for j in range(PEER_TOPK) if (i + 1) * (j + 1) <= PEER_TOPK]
    cand = cand + [jnp.full((8, tt), -jnp.inf, F32)] * (64 - len(cand))
    _bitonic_sort_desc(cand)
    tau = cand[PEER_TOPK - 1]
    z = sum(jnp.exp(cand[k] - cand[0]) for k in range(PEER_TOPK))
    half_inv_z = 0.5 / z
    for h in range(PEER_HEADS):
        s0 = s0_scr[h]
        s1 = s1_scr[h]
        tau_h = tau[h:h + 1, :]
        nsel = jnp.zeros((PEER_NKEYS, tt), F32)
        rank1 = jnp.zeros((PEER_NKEYS, tt), F32)
        for k in range(PEER_TOPK):
            bk = b[k][h:h + 1, :]
            nsel = nsel + jnp.where(s0 + bk >= tau_h, 1.0, 0.0)
            rank1 = rank1 + jnp.where(bk > s1, 1.0, 0.0)
        ns_ref[h] = nsel
        r1_ref[h] = rank1.astype(BF16)
        e0_ref[h] = jnp.exp(s0 - a[0][h:h + 1, :]) * half_inv_z[h:h + 1, :]
        e1_ref[h] = jnp.exp(s1 - b[0][h:h + 1, :]).astype(BF16)


def _peer_route(q, subkeys_bf16, tt=256):
    s = q.shape[0]
    big = lambda dt: jax.ShapeDtypeStruct((PEER_HEADS, PEER_NKEYS, s), dt)
    big_spec = pl.BlockSpec((PEER_HEADS, PEER_NKEYS, tt), lambda i: (0, 0, i))
    return pl.pallas_call(
        functools.partial(_peer_route_kernel, tt=tt),
        grid=(s // tt,),
        in_specs=[pl.BlockSpec((tt, PEER_HEADS * PEER_DKEY), lambda i: (i, 0)),
                  pl.BlockSpec((2 * PEER_HEADS, PEER_NKEYS, PEER_DKEY // 2), lambda i: (0, 0, 0))],
        out_specs=[big_spec, big_spec, big_spec, big_spec],
        out_shape=[big(BF16), big(BF16), big(F32), big(F32)],
        scratch_shapes=[pltpu.VMEM((PEER_HEADS, PEER_NKEYS, tt), F32),
                        pltpu.VMEM((PEER_HEADS, PEER_NKEYS, tt), F32)],
        compiler_params=_cparams(("arbitrary",)),
    )(q, subkeys_bf16)


def _peer_dense_kernel(ht_ref, u_ref, vt_ref, r1_ref, e1_ref, e0_ref, ns_ref,
                       x_ref, g_ref, gf_ref, o_ref, acc_scr, p_scr, *, rows_per_step, final_norm):
    k = pl.program_id(1)

    @pl.when(k == 0)
    def _():
        acc_scr[...] = jnp.zeros(acc_scr.shape, F32)

    tt = ht_ref.shape[1]
    pack_rows = 16
    zero = jnp.zeros((pack_rows, tt), BF16)

    def activations(c):
        return jnp.dot(u_ref[c * MXU_DEPTH:(c + 1) * MXU_DEPTH, :], ht_ref[...], preferred_element_type=F32)

    def expert_rows(ii, a):
        i = k * rows_per_step + ii
        def bcast(ref, h):
            half = jnp.broadcast_to(ref[h, pl.ds(i, 1), :], (pack_rows // 2, tt))
            return jnp.concatenate([half, half], axis=0).astype(BF16)
        e0_rows = [bcast(e0_ref, h) for h in range(PEER_HEADS)]
        ns_rows = [bcast(ns_ref, h) for h in range(PEER_HEADS)]
        gelu2 = (a * (1.0 + lax.erf(a * math.sqrt(0.5)))).astype(BF16)
        for g in range(PEER_NKEYS // pack_rows):
            rows = slice(g * pack_rows, (g + 1) * pack_rows)
            wgt = None
            for h in range(PEER_HEADS):
                term = jnp.where(r1_ref[h, rows, :] < ns_rows[h], e1_ref[h, rows, :], zero) * e0_rows[h]
                wgt = term if wgt is None else wgt + term
            p_scr[ii * PEER_NKEYS + g * pack_rows:ii * PEER_NKEYS + (g + 1) * pack_rows, :] = wgt * gelu2[rows, :]

    depth_rows = MXU_DEPTH // PEER_NKEYS
    n_chunks = rows_per_step // depth_rows
    act = activations(0)
    for c in range(n_chunks):
        act_next = activations(c + 1) if c + 1 < n_chunks else None
        for j in range(depth_rows):
            expert_rows(c * depth_rows + j, act[j * PEER_NKEYS:(j + 1) * PEER_NKEYS, :])
        act = act_next
    acc_scr[...] += jnp.dot(vt_ref[...], p_scr[...], preferred_element_type=F32)

    @pl.when(k == pl.num_programs(1) - 1)
    def _():
        y = x_ref[...] + g_ref[...] * acc_scr[...].T
        if final_norm:
            y = y * lax.rsqrt(jnp.mean(y * y, axis=-1, keepdims=True) + EPS) * gf_ref[...]
        o_ref[...] = y


def _peer_dense(ht_bf16, u_bf16, vt_bf16, rank1, e1, e0, nsel, x, gate, g_final, final_norm,
                tt=512, rows_per_step=8):
    s, d = x.shape
    n_exp = u_bf16.shape[0]
    et = rows_per_step * PEER_NKEYS
    tok = lambda j, k: (0, 0, j)
    big_spec = pl.BlockSpec((PEER_HEADS, PEER_NKEYS, tt), tok)
    return pl.pallas_call(
        functools.partial(_peer_dense_kernel, rows_per_step=rows_per_step, final_norm=final_norm),
        grid=(s // tt, n_exp // et),
        in_specs=[pl.BlockSpec((d, tt), lambda j, k: (0, j)),
                  pl.BlockSpec((et, d), lambda j, k: (k, 0)),
                  pl.BlockSpec((d, et), lambda j, k: (0, k)),
                  big_spec, big_spec, big_spec, big_spec,
                  pl.BlockSpec((tt, d), lambda j, k: (j, 0)),
                  pl.BlockSpec((1, d), lambda j, k: (0, 0)),
                  pl.BlockSpec((1, d), lambda j, k: (0, 0))],
        out_specs=pl.BlockSpec((tt, d), lambda j, k: (j, 0)),
        out_shape=jax.ShapeDtypeStruct((s, d), F32),
        scratch_shapes=[pltpu.VMEM((d, tt), F32), pltpu.VMEM((et, tt), BF16)],
        compiler_params=_cparams(("arbitrary", "arbitrary")),
    )(ht_bf16, u_bf16, vt_bf16, rank1, e1, e0, nsel, x, gate.reshape(1, d), g_final.reshape(1, d))


def _heads_first(t, nh):
    s = t.shape[0]
    return jnp.transpose(t.reshape(s, nh, HEAD_DIM), (1, 0, 2))


def _mixer(h_proj, w_out_bf16, bias_tiles, x, gate):
    s = h_proj.shape[0]
    parts = []
    off = 0
    for size in PROJ_SIZES:
        parts.append(h_proj[:, off:off + size])
        off += size
    mq, mk, mv, dq, dk, dv, iq, ik, iw = parts
    scale = HEAD_DIM ** -0.5 * LOG2E
    dims_first = lambda a, nh: a.T.reshape(nh, HEAD_DIM, s).astype(BF16)

    mq_h, mk_h = _heads_first(mq, MOBA_HEADS), _heads_first(mk, MOBA_HEADS)
    block_sel = _moba_select(mq_h, _moba_kmean(mk_h))
    blk = jnp.arange(s, dtype=I32) // MOBA_BLOCK
    onehot = (blk[:, None] == jnp.arange(MOBA_MAX_BLOCKS, dtype=I32)[None, :]).astype(BF16)
    q_aug = jnp.concatenate([(mq_h * scale).astype(BF16), jnp.transpose(block_sel, (0, 2, 1)).astype(BF16)], axis=-1)
    k_aug = jnp.concatenate([mk_h.astype(BF16), jnp.broadcast_to(onehot, (MOBA_HEADS, s, MOBA_MAX_BLOCKS))], axis=-1)
    y_moba = _attention(q_aug, k_aug, dims_first(mv, MOBA_HEADS), bias_tiles, 0)

    ikf = ik.astype(F32)
    ik_n = ikf * lax.rsqrt(jnp.mean(ikf * ikf, axis=-1, keepdims=True) + EPS)
    topk = min(DSA_TOPK_MAX, s // 4)
    mask_bias = _dsa_select(iq.astype(BF16), iw * IDX_HEADS ** -0.5, ik_n.astype(BF16), topk)
    dq_h, dk_h = _heads_first(dq, DSA_HEADS), _heads_first(dk, DSA_HEADS)
    y_dsa = _attention((dq_h * scale).astype(BF16), dk_h.astype(BF16), dims_first(dv, DSA_HEADS),
                       bias_tiles, 1, mask_bias=mask_bias)

    y = jnp.concatenate([y_moba.reshape(MOBA_WIDTH, s), y_dsa.reshape(DSA_WIDTH, s)], axis=0).T
    return _out_proj(y.astype(BF16), w_out_bf16, x, gate)


def kernel(x, c, w_ada, b_ada, norm_attn, norm_ffn, w_in, w_out, rel_bias,
           peer_wq, peer_subkeys, peer_u, peer_v, norm_final):
    batch, s, d = x.shape
    assert batch == 1 and s % (ATT_KEY_TILES_PER_STEP * ATT_TILE) == 0
    depth = w_ada.shape[0]
    x = x[0]
    mod = _ada_mod(c, w_ada, b_ada)
    bias_tiles = _bias_tiles(rel_bias, ATT_TILE)
    n_pad = -(-PROJ_WIDTH // LANES) * LANES
    for l in range(depth):
        sh1, sc1, g1, sh2, sc2, g2 = jnp.split(mod[l], 6)
        w_in_l = jnp.pad(w_in[l], ((0, 0), (0, n_pad - PROJ_WIDTH))).astype(BF16)
        _, proj = _norm_mod_matmul(x, norm_attn[l], sh1, sc1, w_in_l)
        x = _mixer(proj, w_out[l].astype(BF16), bias_tiles, x, g1)
        h2, q = _norm_mod_matmul(x, norm_ffn[l], sh2, sc2, peer_wq[l].astype(BF16))
        subkeys = peer_subkeys[l].reshape(2 * PEER_HEADS, PEER_NKEYS, PEER_DKEY // 2).astype(BF16)
        rank1, e1, e0, nsel = _peer_route(q, subkeys)
        x = _peer_dense(h2.T, peer_u[l].astype(BF16), peer_v[l].T.astype(BF16), rank1, e1, e0, nsel,
                        x, g2, norm_final, final_norm=(l == depth - 1))
    return x[None]
```

```python
import functools
import math

import jax
import jax.numpy as jnp
from jax import lax
from jax.experimental import pallas as pl
from jax.experimental.pallas import tpu as pltpu

F32 = jnp.float32
BF16 = jnp.bfloat16
I32 = jnp.int32

HEAD_DIM = 64
MOBA_HEADS = 8
DSA_HEADS = 8
N_HEADS = MOBA_HEADS + DSA_HEADS
MOBA_WIDTH = MOBA_HEADS * HEAD_DIM
DSA_WIDTH = DSA_HEADS * HEAD_DIM
MOBA_BLOCK = 256
MOBA_TOPK = 3
MOBA_MAX_BLOCKS = 64
IDX_HEADS = 4
IDX_DIM = 64
DSA_TOPK_MAX = 256
DSA_SORT_CHUNKS = 8
N_BUCKETS = 32
MAX_DISTANCE = 4096
PEER_HEADS = 8
PEER_NKEYS = 128
PEER_DKEY = 256
PEER_TOPK = 16
EPS = 1e-6
PROJ_SIZES = (MOBA_WIDTH, MOBA_WIDTH, MOBA_WIDTH, DSA_WIDTH, DSA_WIDTH, DSA_WIDTH,
              IDX_HEADS * IDX_DIM, IDX_DIM, IDX_HEADS)
PROJ_WIDTH = sum(PROJ_SIZES)

NEG = -1e30
LOG2E = math.log2(math.e)
ATT_TILE = MOBA_BLOCK
ATT_KEY_TILES_PER_STEP = 4
LANES = 128
MXU_DEPTH = 256
VMEM_LIMIT = 56 * 1024 * 1024

_NT = (((1,), (1,)), ((), ()))


def _far_bias_tiles(tile):
    max_exact = N_BUCKETS // 2
    sat = int(math.ceil(max_exact * (MAX_DISTANCE / max_exact) ** ((N_BUCKETS - max_exact - 1) / (N_BUCKETS - max_exact)))) + 4
    return -(-(sat + tile - 1) // tile)


def _cparams(sem):
    return pltpu.CompilerParams(dimension_semantics=sem, vmem_limit_bytes=VMEM_LIMIT)


def _ada_kernel(c_ref, w_ref, b_ref, o_ref):
    c = c_ref[...]
    sc = c * jax.nn.sigmoid(c)
    o_ref[0] = jnp.dot(sc, w_ref[0], preferred_element_type=F32,
                       precision=lax.Precision.HIGHEST) + b_ref[0]


def _ada_mod(c, w_ada, b_ada):
    depth, d, n = w_ada.shape
    tn = 1536
    c8 = jnp.broadcast_to(c, (8, d))
    out = pl.pallas_call(
        _ada_kernel,
        grid=(depth, n // tn),
        in_specs=[pl.BlockSpec((8, d), lambda l, j: (0, 0)),
                  pl.BlockSpec((1, d, tn), lambda l, j: (l, 0, j)),
                  pl.BlockSpec((1, 1, tn), lambda l, j: (l, 0, j))],
        out_specs=pl.BlockSpec((1, 8, tn), lambda l, j: (l, 0, j)),
        out_shape=jax.ShapeDtypeStruct((depth, 8, n), F32),
        compiler_params=_cparams(("arbitrary", "arbitrary")),
    )(c8, w_ada, b_ada.reshape(depth, 1, n))
    return out[:, 0, :]


def _nmm_kernel(x_ref, g_ref, sh_ref, sc_ref, w_ref, h_ref, o_ref):
    x = x_ref[...]
    y = x * lax.rsqrt(jnp.mean(x * x, axis=-1, keepdims=True) + EPS) * g_ref[...]
    hb = (y * (1.0 + sc_ref[...]) + sh_ref[...]).astype(BF16)
    h_ref[...] = hb
    o_ref[...] = jnp.dot(hb, w_ref[...], preferred_element_type=F32)


def _norm_mod_matmul(x, g, shift, scale, w_bf16, tm=256):
    s, d = x.shape
    n = w_bf16.shape[1]
    row = lambda i: (0, 0)
    return pl.pallas_call(
        _nmm_kernel,
        grid=(s // tm,),
        in_specs=[pl.BlockSpec((tm, d), lambda i: (i, 0)),
                  pl.BlockSpec((1, d), row), pl.BlockSpec((1, d), row), pl.BlockSpec((1, d), row),
                  pl.BlockSpec((d, n), row)],
        out_specs=[pl.BlockSpec((tm, d), lambda i: (i, 0)),
                   pl.BlockSpec((tm, n), lambda i: (i, 0))],
        out_shape=[jax.ShapeDtypeStruct((s, d), BF16), jax.ShapeDtypeStruct((s, n), F32)],
        compiler_params=_cparams(("arbitrary",)),
    )(x, g.reshape(1, d), shift.reshape(1, d), scale.reshape(1, d), w_bf16)


def _t5_bucket(d):
    max_exact = N_BUCKETS // 2
    d = jnp.maximum(d, 0)
    df = jnp.maximum(d, 1).astype(F32)
    large = max_exact + (jnp.log(df / max_exact) / math.log(MAX_DISTANCE / max_exact)
                         * (N_BUCKETS - max_exact)).astype(I32)
    return jnp.where(d < max_exact, d, jnp.minimum(large, N_BUCKETS - 1))


def _bias_tile_kernel(tab_ref, o_ref, *, tile, n_near):
    h = pl.program_id(0)
    off = pl.program_id(1)
    b = lax.broadcasted_iota(I32, (tile, tile), 0)
    a = lax.broadcasted_iota(I32, (tile, tile), 1)
    d = jnp.where(off > n_near, -1, off * tile + a - b)
    bucket = _t5_bucket(d)
    val = jnp.zeros((tile, tile), F32)
    for k in range(N_BUCKETS):
        val = jnp.where(bucket == k, tab_ref[h, k], val)
    val = (val - tab_ref[h, N_BUCKETS - 1]) * LOG2E
    o_ref[0, 0] = jnp.where(d < 0, NEG, val)


def _bias_tiles(rel_bias, tile):
    n_near = _far_bias_tiles(tile)
    nh = rel_bias.shape[0]
    return pl.pallas_call(
        functools.partial(_bias_tile_kernel, tile=tile, n_near=n_near),
        grid=(nh, n_near + 2),
        in_specs=[pl.BlockSpec(memory_space=pltpu.SMEM)],
        out_specs=pl.BlockSpec((1, 1, tile, tile), lambda h, o: (h, o, 0, 0)),
        out_shape=jax.ShapeDtypeStruct((nh, n_near + 2, tile, tile), F32),
        compiler_params=_cparams(("arbitrary", "arbitrary")),
    )(rel_bias)


def _kmean_kernel(k_ref, o_ref, *, nb):
    k = k_ref[0]
    km = jnp.mean(k.reshape(nb, MOBA_BLOCK, HEAD_DIM), axis=1)
    o_ref[0] = jnp.zeros(o_ref.shape[1:], F32)
    o_ref[0, :nb, :] = km


def _moba_kmean(k_hsd):
    nh, s, dh = k_hsd.shape
    nb = s // MOBA_BLOCK
    return pl.pallas_call(
        functools.partial(_kmean_kernel, nb=nb),
        grid=(nh,),
        in_specs=[pl.BlockSpec((1, s, dh), lambda h: (h, 0, 0))],
        out_specs=pl.BlockSpec((1, MOBA_MAX_BLOCKS, dh), lambda h: (h, 0, 0)),
        out_shape=jax.ShapeDtypeStruct((nh, MOBA_MAX_BLOCKS, dh), F32),
        compiler_params=_cparams(("arbitrary",)),
    )(k_hsd)


def _moba_sel_kernel(q_ref, km_ref, o_ref, *, tq, n_sel):
    gate = lax.dot_general(km_ref[0], q_ref[0], _NT, preferred_element_type=F32,
                           precision=lax.Precision.HIGHEST)
    t = pl.program_id(1) * tq + lax.broadcasted_iota(I32, (MOBA_MAX_BLOCKS, tq), 1)
    b0 = t // MOBA_BLOCK
    n = lax.broadcasted_iota(I32, (MOBA_MAX_BLOCKS, tq), 0)
    avail = n < b0
    taken = n == b0
    for _ in range(n_sel):
        g = jnp.where(avail, gate, -jnp.inf)
        m = jnp.max(g, axis=0, keepdims=True)
        idx = jnp.min(jnp.where((g == m) & avail, n, 1 << 20), axis=0, keepdims=True)
        pick = n == idx
        taken = taken | pick
        avail = avail & jnp.logical_not(pick)
    o_ref[0] = jnp.where(taken, 0.0, NEG)


def _moba_select(q_hsd, kmean, tq=2048):
    nh, s, dh = q_hsd.shape
    nb = s // MOBA_BLOCK
    assert nb <= MOBA_MAX_BLOCKS
    n_sel = max(min(MOBA_TOPK, nb - 1), 1)
    tq = min(tq, s)
    return pl.pallas_call(
        functools.partial(_moba_sel_kernel, tq=tq, n_sel=n_sel),
        grid=(nh, s // tq),
        in_specs=[pl.BlockSpec((1, tq, dh), lambda h, i: (h, i, 0)),
                  pl.BlockSpec((1, MOBA_MAX_BLOCKS, dh), lambda h, i: (h, 0, 0))],
        out_specs=pl.BlockSpec((1, MOBA_MAX_BLOCKS, tq), lambda h, i: (h, 0, i)),
        out_shape=jax.ShapeDtypeStruct((nh, MOBA_MAX_BLOCKS, s), F32),
        compiler_params=_cparams(("arbitrary", "arbitrary")),
    )(q_hsd, kmean)


def _attn_kernel(qi_ref, kg_ref, *refs, nh, n_near, group, has_mask, has_block_sel):
    q_ref, k_ref, vt_ref, bt_ref = refs[:4]
    extra_ref = refs[4] if (has_mask or has_block_sel) else None
    o_ref, m_scr, acc_scr = refs[-3:]
    t = ATT_TILE
    step = pl.program_id(0)
    qi = qi_ref[step]
    kg = kg_ref[step]

    @pl.when(kg == 0)
    def _():
        m_scr[...] = jnp.full(m_scr.shape, NEG, F32)
        acc_scr[...] = jnp.zeros(acc_scr.shape, F32)

    def step_update(with_bias):
        m = [m_scr[h] for h in range(nh)]
        for g in range(group):
            keys = slice(g * t, (g + 1) * t)
            off = qi - (kg * group + g)
            bias_row = jnp.where(off < 0, n_near + 1, jnp.minimum(off, n_near))
            if has_mask:
                mb = extra_ref[0, g].astype(F32)
            scores = []
            for h in range(nh):
                s = lax.dot_general(k_ref[h, keys, :], q_ref[h], _NT, preferred_element_type=F32)
                if with_bias:
                    s = s + bt_ref[h, bias_row]
                if has_mask:
                    s = s + mb
                if has_block_sel:
                    s = s + extra_ref[h, pl.ds(kg * group + g, 1), :]
                scores.append(s)
            probs, alphas = [], []
            for h in range(nh):
                m_new = jnp.maximum(m[h], jnp.max(scores[h], axis=0, keepdims=True))
                alpha = jnp.exp2(m[h] - m_new)
                p = jnp.exp2(scores[h] - m_new)
                m[h] = m_new
                probs.append(p.astype(BF16))
                alphas.append(alpha)
            for h in range(nh):
                acc_scr[h] = alphas[h] * acc_scr[h] + jnp.dot(vt_ref[h, :, keys], probs[h],
                                                              preferred_element_type=F32)
        for h in range(nh):
            m_scr[h] = m[h]

    all_far = qi - (kg * group + group - 1) >= n_near
    pl.when(all_far)(functools.partial(step_update, False))
    pl.when(jnp.logical_not(all_far))(functools.partial(step_update, True))

    @pl.when(kg == qi // group)
    def _():
        for h in range(nh):
            o_ref[h] = acc_scr[h, :HEAD_DIM, :] / acc_scr[h, HEAD_DIM:HEAD_DIM + 1, :]


def _attention(q, k, vt, bias_tiles, head_group, mask_bias=None, block_sel=None):
    assert mask_bias is None or block_sel is None
    nh, s, dk = q.shape
    vt = jnp.concatenate([vt, jnp.ones((nh, 8, s), vt.dtype)], axis=1)
    dv = vt.shape[1]
    t = ATT_TILE
    group = ATT_KEY_TILES_PER_STEP
    nq = s // t
    n_near = bias_tiles.shape[1] - 2
    qi_list = [i for i in range(nq) for _ in range(i // group + 1)]
    kg_list = [j for i in range(nq) for j in range(i // group + 1)]
    qi_arr = jnp.asarray(qi_list, I32)
    kg_arr = jnp.asarray(kg_list, I32)
    in_specs = [pl.BlockSpec((nh, t, dk), lambda st, qi, kg: (0, qi[st], 0)),
                pl.BlockSpec((nh, group * t, dk), lambda st, qi, kg: (0, kg[st], 0)),
                pl.BlockSpec((nh, dv, group * t), lambda st, qi, kg: (0, 0, kg[st])),
                pl.BlockSpec((nh, n_near + 2, t, t), lambda st, qi, kg: (head_group, 0, 0, 0),
                             pipeline_mode=pl.Buffered(1))]
    args = [q, k, vt, bias_tiles]
    if mask_bias is not None:
        in_specs.append(pl.BlockSpec((1, group, t, t), lambda st, qi, kg: (qi[st], kg[st], 0, 0)))
        args.append(mask_bias)
    if block_sel is not None:
        in_specs.append(pl.BlockSpec((nh, MOBA_MAX_BLOCKS, t), lambda st, qi, kg: (0, 0, qi[st])))
        args.append(block_sel)
    grid_spec = pltpu.PrefetchScalarGridSpec(
        num_scalar_prefetch=2,
        grid=(len(qi_list),),
        in_specs=in_specs,
        out_specs=pl.BlockSpec((nh, HEAD_DIM, t), lambda st, qi, kg: (0, 0, qi[st])),
        scratch_shapes=[pltpu.VMEM((nh, 1, t), F32), pltpu.VMEM((nh, dv, t), F32)])
    return pl.pallas_call(
        functools.partial(_attn_kernel, nh=nh, n_near=n_near, group=group,
                          has_mask=mask_bias is not None, has_block_sel=block_sel is not None),
        grid_spec=grid_spec,
        out_shape=jax.ShapeDtypeStruct((nh, HEAD_DIM, s), F32),
        compiler_params=_cparams(("arbitrary",)),
    )(qi_arr, kg_arr, *args)


def _bitonic_stages(n, full_sort):
    levels = [2 ** p for p in range(1, n.bit_length())] if full_sort else [n]
    return [(k, k >> s) for k in levels for s in range(1, k.bit_length())]


def _plan_passes(stages, block):
    passes = []
    for k, j in stages:
        cross = j >= block
        if passes and passes[-1][2] == cross and (not cross or bin(passes[-1][0] | j).count("1") <= 3):
            passes[-1][0] |= j if cross else 0
            passes[-1][1].append((k, j))
        else:
            passes.append([j if cross else block - 1, [(k, j)], cross])
    return [(mask, sub) for mask, sub, _ in passes]


def _run_pass(load, store, n, mask, stages, regs=32):
    groups = {}
    for e in range(n):
        groups.setdefault(e & ~mask, []).append(e)
    groups = list(groups.values())
    per = max(1, regs // len(groups[0]))
    for g0 in range(0, len(groups), per):
        elems = sorted(e for grp in groups[g0:g0 + per] for e in grp)
        vals = {e: load(e) for e in elems}
        for k, j in stages:
            for e in elems:
                l = e ^ j
                if l > e:
                    hi = jnp.maximum(vals[e], vals[l])
                    lo = jnp.minimum(vals[e], vals[l])
                    vals[e], vals[l] = (hi, lo) if (e & k) == 0 else (lo, hi)
        for e in elems:
            store(e, vals[e])


def _sortable(x):
    return jnp.where(x == 0.0, 0.0, x)


def _dsa_sel_kernel(qi_ref, wt_ref, kn_ref, tri_ref, o_ref, keys_scr, seen_scr, work_scr, top_scr, *, tile, topk):
    n_chunks = kn_ref.shape[0]
    i = pl.program_id(0)
    n_live = i + 1
    groups_per_chunk = tile // 8
    causal = lax.broadcasted_iota(I32, (tile, tile), 0) <= lax.broadcasted_iota(I32, (tile, tile), 1)
    q_heads = [qi_ref[:, h * IDX_DIM:(h + 1) * IDX_DIM] for h in range(IDX_HEADS)]
    w_heads = [wt_ref[h:h + 1, :] for h in range(IDX_HEADS)]

    def score_chunk(c, diagonal):
        kc = kn_ref[c]
        dots = [lax.dot_general(kc, q_heads[h], _NT, preferred_element_type=F32) for h in range(IDX_HEADS)]
        sc = None
        for h in range(IDX_HEADS):
            term = w_heads[h] * jnp.maximum(dots[h], 0.0)
            sc = term if sc is None else sc + term
        if diagonal:
            sc = jnp.where(causal, sc, -jnp.inf)
        keys_scr[c] = _sortable(sc)

    def for_past_chunks(fn):
        def pair(p, carry):
            fn(2 * p)
            fn(2 * p + 1)
            return carry
        lax.fori_loop(0, i // 2, pair, 0)
        pl.when(i % 2 == 1)(lambda: fn(i - 1))

    for_past_chunks(lambda c: score_chunk(c, False))
    score_chunk(i, True)

    def count(pred_fn):
        def body(c, cnt):
            hits = [pred_fn(keys_scr[c, 8 * g:8 * g + 8, :]).astype(I32) for g in range(tile // 8)]
            while len(hits) > 1:
                hits = [hits[j] + hits[j + 1] for j in range(0, len(hits), 2)]
            return cnt + hits[0]
        cnt = lax.fori_loop(0, n_live, body, jnp.zeros((8, tile), I32))
        return jnp.sum(cnt, axis=0, keepdims=True)

    n_list = DSA_SORT_CHUNKS * groups_per_chunk
    n_batches = (n_live + DSA_SORT_CHUNKS - 1) // DSA_SORT_CHUNKS

    def pad_chunk(c, carry):
        keys_scr[c] = jnp.full((tile, tile), -jnp.inf, F32)
        return carry

    lax.fori_loop(n_live, n_batches * DSA_SORT_CHUNKS, pad_chunk, 0)
    top_scr[...] = jnp.full(top_scr.shape, -jnp.inf, F32)
    sort_passes = _plan_passes(_bitonic_stages(n_list, True), groups_per_chunk)
    merge_passes = _plan_passes(_bitonic_stages(n_list, False), groups_per_chunk)
    lane_halves = [slice(lo, lo + LANES) for lo in range(0, tile, LANES)]

    def run_network(passes, first_load, first_store, buf, lanes):
        def load(e):
            return buf[e, :, lanes]

        def store(e, v):
            buf[e, :, lanes] = v

        for n_pass, (mask, stages) in enumerate(passes):
            _run_pass(first_load if n_pass == 0 else load, first_store if n_pass == 0 else store,
                      n_list, mask, stages)

    def sort_batch(b, carry):
        for lanes in lane_halves:
            def load_keys(e, lanes=lanes):
                g = e % groups_per_chunk
                return keys_scr[b * DSA_SORT_CHUNKS + e // groups_per_chunk, 8 * g:8 * g + 8, lanes]

            def store_work(e, v, lanes=lanes):
                work_scr[e, :, lanes] = v

            run_network(sort_passes, load_keys, store_work, work_scr, lanes)

            def load_best(e, lanes=lanes):
                return jnp.maximum(top_scr[e, :, lanes], work_scr[n_list - 1 - e, :, lanes])

            def store_top(e, v, lanes=lanes):
                top_scr[e, :, lanes] = v

            run_network(merge_passes, load_best, store_top, top_scr, lanes)
        return carry

    lax.fori_loop(0, n_batches, sort_batch, 0)

    src, dst = top_scr, work_scr
    for shift in (4, 2, 1):
        for lanes in lane_halves:
            def load_pair(e, lanes=lanes, src=src, shift=shift):
                return jnp.maximum(src[e, :, lanes], pltpu.roll(src[n_list - 1 - e, :, lanes], shift, 0))

            def store_dst(e, v, lanes=lanes, dst=dst):
                dst[e, :, lanes] = v

            run_network(merge_passes, load_pair, store_dst, dst, lanes)
        src, dst = dst, src
    thr = src[topk - 1, 0:1, :]
    thr8 = jnp.broadcast_to(thr, (8, tile))
    n_gt = count(lambda kc: kc > thr8)
    need = (topk - n_gt).astype(F32)

    seen_scr[...] = jnp.zeros(seen_scr.shape, F32)

    def emit_chunk(c, diagonal):
        kc = keys_scr[c]
        eq = kc == thr
        eqf = eq.astype(F32)
        seen = seen_scr[...]
        seen_scr[...] = seen + jnp.sum(eqf, axis=0, keepdims=True)
        rank = seen + jnp.dot(tri_ref[...], eqf.astype(BF16), preferred_element_type=F32)
        sel = (kc > thr) | (eq & (rank < need))
        if diagonal:
            sel = sel & causal
        o_ref[0, c] = jnp.where(sel, 0.0, NEG).astype(BF16)

    for_past_chunks(lambda c: emit_chunk(c, False))
    emit_chunk(i, True)

    def fill_chunk(c, carry):
        o_ref[0, c] = jnp.full((tile, tile), NEG, BF16)
        return carry

    lax.fori_loop(n_live, n_chunks, fill_chunk, 0)


def _dsa_select(q_idx_bf16, w_idx, kn_bf16, topk):
    s = q_idx_bf16.shape[0]
    t = ATT_TILE
    n_chunks = s // t
    n_list = DSA_SORT_CHUNKS * (t // 8)
    assert topk <= n_list, "the sorting network keeps n_list best keys per query"
    kn3 = kn_bf16.reshape(n_chunks, t, IDX_DIM)
    wt = jnp.pad(w_idx.T, ((0, 8 - IDX_HEADS), (0, 0)))
    tri = (lax.broadcasted_iota(I32, (t, t), 1) < lax.broadcasted_iota(I32, (t, t), 0)).astype(BF16)
    return pl.pallas_call(
        functools.partial(_dsa_sel_kernel, tile=t, topk=topk),
        grid=(n_chunks,),
        in_specs=[pl.BlockSpec((t, IDX_HEADS * IDX_DIM), lambda i: (i, 0)),
                  pl.BlockSpec((8, t), lambda i: (0, i)),
                  pl.BlockSpec((n_chunks, t, IDX_DIM), lambda i: (0, 0, 0)),
                  pl.BlockSpec((t, t), lambda i: (0, 0))],
        out_specs=pl.BlockSpec((1, n_chunks, t, t), lambda i: (i, 0, 0, 0)),
        out_shape=jax.ShapeDtypeStruct((n_chunks, n_chunks, t, t), BF16),
        scratch_shapes=[pltpu.VMEM((max(n_chunks, DSA_SORT_CHUNKS), t, t), F32), pltpu.VMEM((1, t), F32),
                        pltpu.VMEM((n_list, 8, t), F32), pltpu.VMEM((n_list, 8, t), F32)],
        compiler_params=_cparams(("arbitrary",)),
    )(q_idx_bf16, wt, kn3, tri)


def _out_proj_kernel(y_ref, w_ref, x_ref, g_ref, o_ref):
    o_ref[...] = x_ref[...] + g_ref[...] * jnp.dot(y_ref[...], w_ref[...], preferred_element_type=F32)


def _out_proj(y_bf16, w_bf16, x, gate, tm=512):
    s, d = x.shape
    k = y_bf16.shape[1]
    return pl.pallas_call(
        _out_proj_kernel,
        grid=(s // tm,),
        in_specs=[pl.BlockSpec((tm, k), lambda i: (i, 0)),
                  pl.BlockSpec((k, d), lambda i: (0, 0)),
                  pl.BlockSpec((tm, d), lambda i: (i, 0)),
                  pl.BlockSpec((1, d), lambda i: (0, 0))],
        out_specs=pl.BlockSpec((tm, d), lambda i: (i, 0)),
        out_shape=jax.ShapeDtypeStruct((s, d), F32),
        compiler_params=_cparams(("arbitrary",)),
    )(y_bf16, w_bf16, x, gate.reshape(1, d))


def _cmp_exchange(v, i, j):
    hi = jnp.maximum(v[i], v[j])
    lo = jnp.minimum(v[i], v[j])
    v[i], v[j] = hi, lo


def _bitonic_sort_desc(v):
    n = len(v)
    k = 2
    while k <= n:
        j = k // 2
        while j >= 1:
            for i in range(n):
                l = i ^ j
                if l > i:
                    if (i & k) == 0:
                        _cmp_exchange(v, i, l)
                    else:
                        _cmp_exchange(v, l, i)
            j //= 2
        k *= 2


def _bitonic_merge_desc(v):
    n = len(v)
    j = n // 2
    while j >= 1:
        for i in range(n):
            l = i ^ j
            if l > i:
                _cmp_exchange(v, i, l)
        j //= 2


def _top16_rows(x):
    tt = x.shape[1]
    v = [x[8 * i:8 * i + 8, :] for i in range(PEER_NKEYS // 8)]
    _bitonic_sort_desc(v)
    for shift in (4, 2, 1):
        other = [pltpu.roll(a, shift, 0) for a in v]
        v = [jnp.maximum(v[i], other[len(v) - 1 - i]) for i in range(len(v))]
        _bitonic_merge_desc(v)
    return v


def _peer_route_kernel(q_ref, sk_ref, r1_ref, e1_ref, e0_ref, ns_ref, s0_scr, s1_scr, *, tt):
    sub = lax.broadcasted_iota(I32, (8, tt), 0)
    tops = [[], []]
    for h in range(PEER_HEADS):
        for p in range(2):
            lo = (2 * h + p) * (PEER_DKEY // 2)
            qhp = q_ref[:, lo:lo + PEER_DKEY // 2].astype(BF16)
            sc = lax.dot_general(sk_ref[2 * h + p], qhp, _NT, preferred_element_type=F32)
            (s0_scr if p == 0 else s1_scr)[h] = sc
            tops[p].append(_top16_rows(sc))
    a = [sum(jnp.where(sub == h, tops[0][h][k], 0.0) for h in range(PEER_HEADS)) for k in range(PEER_TOPK)]
    b = [sum(jnp.where(sub == h, tops[1][h][k], 0.0) for h in range(PEER_HEADS)) for k in range(PEER_TOPK)]
    cand = [a[i] + b[j] for i in range(PEER_TOPK) for j in range(PEER_TOPK) if (i + 1) * (j + 1) <= PEER_TOPK]
    cand = cand + [jnp.full((8, tt), -jnp.inf, F32)] * (64 - len(cand))
    _bitonic_sort_desc(cand)
    tau = cand[PEER_TOPK - 1]
    z = sum(jnp.exp(cand[k] - cand[0]) for k in range(PEER_TOPK))
    half_inv_z = 0.5 / z
    nsel_of_rank = [sum(jnp.where(a[r] + b[k] >= tau, 1.0, 0.0) for k in range(PEER_TOPK)) for r in range(PEER_TOPK)]
    for h in range(PEER_HEADS):
        s0 = s0_scr[h]
        s1 = s1_scr[h]
        nsel = jnp.zeros((PEER_NKEYS, tt), F32)
        rank1 = jnp.full((PEER_NKEYS, tt), float(PEER_TOPK), F32)
        for r in reversed(range(PEER_TOPK)):
            nsel = jnp.where(s0 == a[r][h:h + 1, :], nsel_of_rank[r][h:h + 1, :], nsel)
            rank1 = jnp.where(s1 == b[r][h:h + 1, :], float(r), rank1)
        ns_ref[h] = nsel
        r1_ref[h] = rank1.astype(BF16)
        e0_ref[h] = jnp.exp(s0 - a[0][h:h + 1, :]) * half_inv_z[h:h + 1, :]
        e1_ref[h] = jnp.exp(s1 - b[0][h:h + 1, :]).astype(BF16)


def _peer_route(q, subkeys_bf16, tt=256):
    s = q.shape[0]
    big = lambda dt: jax.ShapeDtypeStruct((PEER_HEADS, PEER_NKEYS, s), dt)
    big_spec = pl.BlockSpec((PEER_HEADS, PEER_NKEYS, tt), lambda i: (0, 0, i))
    return pl.pallas_call(
        functools.partial(_peer_route_kernel, tt=tt),
        grid=(s // tt,),
        in_specs=[pl.BlockSpec((tt, PEER_HEADS * PEER_DKEY), lambda i: (i, 0)),
                  pl.BlockSpec((2 * PEER_HEADS, PEER_NKEYS, PEER_DKEY // 2), lambda i: (0, 0, 0))],
        out_specs=[big_spec, big_spec, big_spec, big_spec],
        out_shape=[big(BF16), big(BF16), big(F32), big(F32)],
        scratch_shapes=[pltpu.VMEM((PEER_HEADS, PEER_NKEYS, tt), F32),
                        pltpu.VMEM((PEER_HEADS, PEER_NKEYS, tt), F32)],
        compiler_params=_cparams(("arbitrary",)),
    )(q, subkeys_bf16)


def _peer_dense_kernel(ht_ref, u_ref, vt_ref, r1_ref, e1_ref, e0_ref, ns_ref,
                       x_ref, g_ref, gf_ref, o_ref, acc_scr, p_scr, *, rows_per_step, final_norm):
    k = pl.program_id(1)

    @pl.when(k == 0)
    def _():
        acc_scr[...] = jnp.zeros(acc_scr.shape, F32)

    tt = ht_ref.shape[1]
    pack_rows = 16
    zero = jnp.zeros((pack_rows, tt), BF16)

    def activations(c):
        return jnp.dot(u_ref[c * MXU_DEPTH:(c + 1) * MXU_DEPTH, :], ht_ref[...], preferred_element_type=F32)

    def expert_rows(ii, a):
        i = k * rows_per_step + ii
        def bcast(ref, h):
            half = jnp.broadcast_to(ref[h, pl.ds(i, 1), :], (pack_rows // 2, tt))
            return jnp.concatenate([half, half], axis=0).astype(BF16)
        e0_rows = [bcast(e0_ref, h) for h in range(PEER_HEADS)]
        ns_rows = [bcast(ns_ref, h) for h in range(PEER_HEADS)]
        gelu2 = (a * (1.0 + lax.erf(a * math.sqrt(0.5)))).astype(BF16)
        for g in range(PEER_NKEYS // pack_rows):
            rows = slice(g * pack_rows, (g + 1) * pack_rows)
            wgt = None
            for h in range(PEER_HEADS):
                term = jnp.where(r1_ref[h, rows, :] < ns_rows[h], e1_ref[h, rows, :], zero) * e0_rows[h]
                wgt = term if wgt is None else wgt + term
            p_scr[ii * PEER_NKEYS + g * pack_rows:ii * PEER_NKEYS + (g + 1) * pack_rows, :] = wgt * gelu2[rows, :]

    depth_rows = MXU_DEPTH // PEER_NKEYS
    n_chunks = rows_per_step // depth_rows
    act = activations(0)
    for c in range(n_chunks):
        act_next = activations(c + 1) if c + 1 < n_chunks else None
        for j in range(depth_rows):
            expert_rows(c * depth_rows + j, act[j * PEER_NKEYS:(j + 1) * PEER_NKEYS, :])
        act = act_next
    acc_scr[...] += jnp.dot(vt_ref[...], p_scr[...], preferred_element_type=F32)

    @pl.when(k == pl.num_programs(1) - 1)
    def _():
        y = x_ref[...] + g_ref[...] * acc_scr[...].T
        if final_norm:
            y = y * lax.rsqrt(jnp.mean(y * y, axis=-1, keepdims=True) + EPS) * gf_ref[...]
        o_ref[...] = y


def _peer_dense(ht_bf16, u_bf16, vt_bf16, rank1, e1, e0, nsel, x, gate, g_final, final_norm,
                tt=512, rows_per_step=16):
    s, d = x.shape
    n_exp = u_bf16.shape[0]
    et = rows_per_step * PEER_NKEYS
    tok = lambda j, k: (0, 0, j)
    big_spec = pl.BlockSpec((PEER_HEADS, PEER_NKEYS, tt), tok)
    return pl.pallas_call(
        functools.partial(_peer_dense_kernel, rows_per_step=rows_per_step, final_norm=final_norm),
        grid=(s // tt, n_exp // et),
        in_specs=[pl.BlockSpec((d, tt), lambda j, k: (0, j)),
                  pl.BlockSpec((et, d), lambda j, k: (k, 0)),
                  pl.BlockSpec((d, et), lambda j, k: (0, k)),
                  big_spec, big_spec, big_spec, big_spec,
                  pl.BlockSpec((tt, d), lambda j, k: (j, 0)),
                  pl.BlockSpec((1, d), lambda j, k: (0, 0)),
                  pl.BlockSpec((1, d), lambda j, k: (0, 0))],
        out_specs=pl.BlockSpec((tt, d), lambda j, k: (j, 0)),
        out_shape=jax.ShapeDtypeStruct((s, d), F32),
        scratch_shapes=[pltpu.VMEM((d, tt), F32), pltpu.VMEM((et, tt), BF16)],
        compiler_params=_cparams(("arbitrary", "arbitrary")),
    )(ht_bf16, u_bf16, vt_bf16, rank1, e1, e0, nsel, x, gate.reshape(1, d), g_final.reshape(1, d))


def _heads_first(t, nh):
    s = t.shape[0]
    return jnp.transpose(t.reshape(s, nh, HEAD_DIM), (1, 0, 2))


def _mixer(h_proj, w_out_bf16, bias_tiles, x, gate):
    s = h_proj.shape[0]
    parts = []
    off = 0
    for size in PROJ_SIZES:
        parts.append(h_proj[:, off:off + size])
        off += size
    mq, mk, mv, dq, dk, dv, iq, ik, iw = parts
    scale = HEAD_DIM ** -0.5 * LOG2E
    dims_first = lambda a, nh: a.T.reshape(nh, HEAD_DIM, s).astype(BF16)

    mq_h, mk_h = _heads_first(mq, MOBA_HEADS), _heads_first(mk, MOBA_HEADS)
    block_sel = _moba_select(mq_h, _moba_kmean(mk_h))
    blk = jnp.arange(s, dtype=I32) // MOBA_BLOCK
    onehot = (blk[:, None] == jnp.arange(MOBA_MAX_BLOCKS, dtype=I32)[None, :]).astype(BF16)
    q_aug = jnp.concatenate([(mq_h * scale).astype(BF16), jnp.transpose(block_sel, (0, 2, 1)).astype(BF16)], axis=-1)
    k_aug = jnp.concatenate([mk_h.astype(BF16), jnp.broadcast_to(onehot, (MOBA_HEADS, s, MOBA_MAX_BLOCKS))], axis=-1)
    y_moba = _attention(q_aug, k_aug, dims_first(mv, MOBA_HEADS), bias_tiles, 0)

    ikf = ik.astype(F32)
    ik_n = ikf * lax.rsqrt(jnp.mean(ikf * ikf, axis=-1, keepdims=True) + EPS)
    topk = min(DSA_TOPK_MAX, s // 4)
    mask_bias = _dsa_select(iq.astype(BF16), iw * IDX_HEADS ** -0.5, ik_n.astype(BF16), topk)
    dq_h, dk_h = _heads_first(dq, DSA_HEADS), _heads_first(dk, DSA_HEADS)
    y_dsa = _attention((dq_h * scale).astype(BF16), dk_h.astype(BF16), dims_first(dv, DSA_HEADS),
                       bias_tiles, 1, mask_bias=mask_bias)

    y = jnp.concatenate([y_moba.reshape(MOBA_WIDTH, s), y_dsa.reshape(DSA_WIDTH, s)], axis=0).T
    return _out_proj(y.astype(BF16), w_out_bf16, x, gate)


def kernel(x, c, w_ada, b_ada, norm_attn, norm_ffn, w_in, w_out, rel_bias,
           peer_wq, peer_subkeys, peer_u, peer_v, norm_final):
    batch, s, d = x.shape
    assert batch == 1 and s % (ATT_KEY_TILES_PER_STEP * ATT_TILE) == 0
    depth = w_ada.shape[0]
    x = x[0]
    mod = _ada_mod(c, w_ada, b_ada)
    bias_tiles = _bias_tiles(rel_bias, ATT_TILE)
    n_pad = -(-PROJ_WIDTH // LANES) * LANES
    for l in range(depth):
        sh1, sc1, g1, sh2, sc2, g2 = jnp.split(mod[l], 6)
        w_in_l = jnp.pad(w_in[l], ((0, 0), (0, n_pad - PROJ_WIDTH))).astype(BF16)
        _, proj = _norm_mod_matmul(x, norm_attn[l], sh1, sc1, w_in_l)
        x = _mixer(proj, w_out[l].astype(BF16), bias_tiles, x, g1)
        h2, q = _norm_mod_matmul(x, norm_ffn[l], sh2, sc2, peer_wq[l].astype(BF16))
        subkeys = peer_subkeys[l].reshape(2 * PEER_HEADS, PEER_NKEYS, PEER_DKEY // 2).astype(BF16)
        rank1, e1, e0, nsel = _peer_route(q, subkeys)
        x = _peer_dense(h2.T, peer_u[l].astype(BF16), peer_v[l].T.astype(BF16), rank1, e1, e0, nsel,
                        x, g2, norm_final, final_norm=(l == depth - 1))
    return x[None]
```

```python
import functools
import math

import jax
import jax.numpy as jnp
from jax import lax
from jax.experimental import pallas as pl
from jax.experimental.pallas import tpu as pltpu

F32 = jnp.float32
BF16 = jnp.bfloat16
I32 = jnp.int32

HEAD_DIM = 64
MOBA_HEADS = 8
DSA_HEADS = 8
N_HEADS = MOBA_HEADS + DSA_HEADS
MOBA_WIDTH = MOBA_HEADS * HEAD_DIM
DSA_WIDTH = DSA_HEADS * HEAD_DIM
MOBA_BLOCK = 256
MOBA_TOPK = 3
MOBA_MAX_BLOCKS = 64
IDX_HEADS = 4
IDX_DIM = 64
DSA_TOPK_MAX = 256
DSA_SORT_CHUNKS = 8
N_BUCKETS = 32
MAX_DISTANCE = 4096
PEER_HEADS = 8
PEER_NKEYS = 128
PEER_DKEY = 256
PEER_TOPK = 16
EPS = 1e-6
PROJ_SIZES = (MOBA_WIDTH, MOBA_WIDTH, MOBA_WIDTH, DSA_WIDTH, DSA_WIDTH, DSA_WIDTH,
              IDX_HEADS * IDX_DIM, IDX_DIM, IDX_HEADS)
PROJ_WIDTH = sum(PROJ_SIZES)

NEG = -1e30
LOG2E = math.log2(math.e)
ATT_TILE = MOBA_BLOCK
ATT_KEY_TILES_PER_STEP = 8
LANES = 128
MXU_DEPTH = 256
VMEM_LIMIT = 56 * 1024 * 1024

_NT = (((1,), (1,)), ((), ()))


def _far_bias_tiles(tile):
    max_exact = N_BUCKETS // 2
    sat = int(math.ceil(max_exact * (MAX_DISTANCE / max_exact) ** ((N_BUCKETS - max_exact - 1) / (N_BUCKETS - max_exact)))) + 4
    return -(-(sat + tile - 1) // tile)


def _cparams(sem):
    return pltpu.CompilerParams(dimension_semantics=sem, vmem_limit_bytes=VMEM_LIMIT)


def _ada_kernel(c_ref, w_ref, b_ref, o_ref):
    c = c_ref[...]
    sc = c * jax.nn.sigmoid(c)
    o_ref[0] = jnp.dot(sc, w_ref[0], preferred_element_type=F32,
                       precision=lax.Precision.HIGHEST) + b_ref[0]


def _ada_mod(c, w_ada, b_ada):
    depth, d, n = w_ada.shape
    tn = 1536
    c8 = jnp.broadcast_to(c, (8, d))
    out = pl.pallas_call(
        _ada_kernel,
        grid=(depth, n // tn),
        in_specs=[pl.BlockSpec((8, d), lambda l, j: (0, 0)),
                  pl.BlockSpec((1, d, tn), lambda l, j: (l, 0, j)),
                  pl.BlockSpec((1, 1, tn), lambda l, j: (l, 0, j))],
        out_specs=pl.BlockSpec((1, 8, tn), lambda l, j: (l, 0, j)),
        out_shape=jax.ShapeDtypeStruct((depth, 8, n), F32),
        compiler_params=_cparams(("arbitrary", "arbitrary")),
    )(c8, w_ada, b_ada.reshape(depth, 1, n))
    return out[:, 0, :]


def _nmm_kernel(x_ref, g_ref, sh_ref, sc_ref, w_ref, h_ref, o_ref):
    x = x_ref[...]
    y = x * lax.rsqrt(jnp.mean(x * x, axis=-1, keepdims=True) + EPS) * g_ref[...]
    hb = (y * (1.0 + sc_ref[...]) + sh_ref[...]).astype(BF16)
    h_ref[...] = hb
    o_ref[...] = jnp.dot(hb, w_ref[...], preferred_element_type=F32)


def _norm_mod_matmul(x, g, shift, scale, w_bf16, tm=256):
    s, d = x.shape
    n = w_bf16.shape[1]
    row = lambda i: (0, 0)
    return pl.pallas_call(
        _nmm_kernel,
        grid=(s // tm,),
        in_specs=[pl.BlockSpec((tm, d), lambda i: (i, 0)),
                  pl.BlockSpec((1, d), row), pl.BlockSpec((1, d), row), pl.BlockSpec((1, d), row),
                  pl.BlockSpec((d, n), row)],
        out_specs=[pl.BlockSpec((tm, d), lambda i: (i, 0)),
                   pl.BlockSpec((tm, n), lambda i: (i, 0))],
        out_shape=[jax.ShapeDtypeStruct((s, d), BF16), jax.ShapeDtypeStruct((s, n), F32)],
        compiler_params=_cparams(("arbitrary",)),
    )(x, g.reshape(1, d), shift.reshape(1, d), scale.reshape(1, d), w_bf16)


def _t5_bucket(d):
    max_exact = N_BUCKETS // 2
    d = jnp.maximum(d, 0)
    df = jnp.maximum(d, 1).astype(F32)
    large = max_exact + (jnp.log(df / max_exact) / math.log(MAX_DISTANCE / max_exact)
                         * (N_BUCKETS - max_exact)).astype(I32)
    return jnp.where(d < max_exact, d, jnp.minimum(large, N_BUCKETS - 1))


def _bias_tile_kernel(tab_ref, o_ref, *, tile, n_near):
    h = pl.program_id(0)
    off = pl.program_id(1)
    b = lax.broadcasted_iota(I32, (tile, tile), 0)
    a = lax.broadcasted_iota(I32, (tile, tile), 1)
    d = jnp.where(off > n_near, -1, off * tile + a - b)
    bucket = _t5_bucket(d)
    val = jnp.zeros((tile, tile), F32)
    for k in range(N_BUCKETS):
        val = jnp.where(bucket == k, tab_ref[h, k], val)
    val = (val - tab_ref[h, N_BUCKETS - 1]) * LOG2E
    o_ref[0, 0] = jnp.where(d < 0, NEG, val)


def _bias_tiles(rel_bias, tile):
    n_near = _far_bias_tiles(tile)
    nh = rel_bias.shape[0]
    return pl.pallas_call(
        functools.partial(_bias_tile_kernel, tile=tile, n_near=n_near),
        grid=(nh, n_near + 2),
        in_specs=[pl.BlockSpec(memory_space=pltpu.SMEM)],
        out_specs=pl.BlockSpec((1, 1, tile, tile), lambda h, o: (h, o, 0, 0)),
        out_shape=jax.ShapeDtypeStruct((nh, n_near + 2, tile, tile), F32),
        compiler_params=_cparams(("arbitrary", "arbitrary")),
    )(rel_bias)


def _kmean_kernel(k_ref, o_ref, *, nb):
    k = k_ref[0]
    km = jnp.mean(k.reshape(nb, MOBA_BLOCK, HEAD_DIM), axis=1)
    o_ref[0] = jnp.zeros(o_ref.shape[1:], F32)
    o_ref[0, :nb, :] = km


def _moba_kmean(k_hsd):
    nh, s, dh = k_hsd.shape
    nb = s // MOBA_BLOCK
    return pl.pallas_call(
        functools.partial(_kmean_kernel, nb=nb),
        grid=(nh,),
        in_specs=[pl.BlockSpec((1, s, dh), lambda h: (h, 0, 0))],
        out_specs=pl.BlockSpec((1, MOBA_MAX_BLOCKS, dh), lambda h: (h, 0, 0)),
        out_shape=jax.ShapeDtypeStruct((nh, MOBA_MAX_BLOCKS, dh), F32),
        compiler_params=_cparams(("arbitrary",)),
    )(k_hsd)


def _moba_sel_kernel(q_ref, km_ref, o_ref, *, tq, n_sel):
    gate = lax.dot_general(km_ref[0], q_ref[0], _NT, preferred_element_type=F32,
                           precision=lax.Precision.HIGHEST)
    t = pl.program_id(1) * tq + lax.broadcasted_iota(I32, (MOBA_MAX_BLOCKS, tq), 1)
    b0 = t // MOBA_BLOCK
    n = lax.broadcasted_iota(I32, (MOBA_MAX_BLOCKS, tq), 0)
    avail = n < b0
    taken = n == b0
    for _ in range(n_sel):
        g = jnp.where(avail, gate, -jnp.inf)
        m = jnp.max(g, axis=0, keepdims=True)
        idx = jnp.min(jnp.where((g == m) & avail, n, 1 << 20), axis=0, keepdims=True)
        pick = n == idx
        taken = taken | pick
        avail = avail & jnp.logical_not(pick)
    o_ref[0] = jnp.where(taken, 0.0, NEG)


def _moba_select(q_hsd, kmean, tq=2048):
    nh, s, dh = q_hsd.shape
    nb = s // MOBA_BLOCK
    assert nb <= MOBA_MAX_BLOCKS
    n_sel = max(min(MOBA_TOPK, nb - 1), 1)
    tq = min(tq, s)
    return pl.pallas_call(
        functools.partial(_moba_sel_kernel, tq=tq, n_sel=n_sel),
        grid=(nh, s // tq),
        in_specs=[pl.BlockSpec((1, tq, dh), lambda h, i: (h, i, 0)),
                  pl.BlockSpec((1, MOBA_MAX_BLOCKS, dh), lambda h, i: (h, 0, 0))],
        out_specs=pl.BlockSpec((1, MOBA_MAX_BLOCKS, tq), lambda h, i: (h, 0, i)),
        out_shape=jax.ShapeDtypeStruct((nh, MOBA_MAX_BLOCKS, s), F32),
        compiler_params=_cparams(("arbitrary", "arbitrary")),
    )(q_hsd, kmean)


def _attn_kernel(qi_ref, kg_ref, *refs, nh, n_near, group, has_mask, has_block_sel):
    q_ref, k_ref, vt_ref, bt_ref = refs[:4]
    extra_ref = refs[4] if (has_mask or has_block_sel) else None
    o_ref, m_scr, acc_scr = refs[-3:]
    t = ATT_TILE
    step = pl.program_id(0)
    qi = qi_ref[step]
    kg = kg_ref[step]

    @pl.when(kg == 0)
    def _():
        m_scr[...] = jnp.full(m_scr.shape, NEG, F32)
        acc_scr[...] = jnp.zeros(acc_scr.shape, F32)

    def step_update(with_bias):
        m = [m_scr[h] for h in range(nh)]
        for g in range(group):
            keys = slice(g * t, (g + 1) * t)
            off = qi - (kg * group + g)
            bias_row = jnp.where(off < 0, n_near + 1, jnp.minimum(off, n_near))
            if has_mask:
                mb = extra_ref[0, g].astype(F32)
            scores = []
            for h in range(nh):
                s = lax.dot_general(k_ref[h, keys, :], q_ref[h], _NT, preferred_element_type=F32)
                if with_bias:
                    s = s + bt_ref[h, bias_row]
                if has_mask:
                    s = s + mb
                if has_block_sel:
                    s = s + extra_ref[h, pl.ds(kg * group + g, 1), :]
                scores.append(s)
            probs, alphas = [], []
            for h in range(nh):
                m_new = jnp.maximum(m[h], jnp.max(scores[h], axis=0, keepdims=True))
                alpha = jnp.exp2(m[h] - m_new)
                p = jnp.exp2(scores[h] - m_new)
                m[h] = m_new
                probs.append(p.astype(BF16))
                alphas.append(alpha)
            for h in range(nh):
                acc_scr[h] = alphas[h] * acc_scr[h] + jnp.dot(vt_ref[h, :, keys], probs[h],
                                                              preferred_element_type=F32)
        for h in range(nh):
            m_scr[h] = m[h]

    all_far = qi - (kg * group + group - 1) >= n_near
    pl.when(all_far)(functools.partial(step_update, False))
    pl.when(jnp.logical_not(all_far))(functools.partial(step_update, True))

    @pl.when(kg == qi // group)
    def _():
        for h in range(nh):
            o_ref[h] = acc_scr[h, :HEAD_DIM, :] / acc_scr[h, HEAD_DIM:HEAD_DIM + 1, :]


def _attention(q, k, vt, bias_tiles, head_group, mask_bias=None, block_sel=None):
    assert mask_bias is None or block_sel is None
    nh, s, dk = q.shape
    vt = jnp.concatenate([vt, jnp.ones((nh, 8, s), vt.dtype)], axis=1)
    dv = vt.shape[1]
    t = ATT_TILE
    group = ATT_KEY_TILES_PER_STEP
    nq = s // t
    n_near = bias_tiles.shape[1] - 2
    qi_list = [i for i in range(nq) for _ in range(i // group + 1)]
    kg_list = [j for i in range(nq) for j in range(i // group + 1)]
    qi_arr = jnp.asarray(qi_list, I32)
    kg_arr = jnp.asarray(kg_list, I32)
    in_specs = [pl.BlockSpec((nh, t, dk), lambda st, qi, kg: (0, qi[st], 0)),
                pl.BlockSpec((nh, group * t, dk), lambda st, qi, kg: (0, kg[st], 0)),
                pl.BlockSpec((nh, dv, group * t), lambda st, qi, kg: (0, 0, kg[st])),
                pl.BlockSpec((nh, n_near + 2, t, t), lambda st, qi, kg: (head_group, 0, 0, 0),
                             pipeline_mode=pl.Buffered(1))]
    args = [q, k, vt, bias_tiles]
    if mask_bias is not None:
        in_specs.append(pl.BlockSpec((1, group, t, t), lambda st, qi, kg: (qi[st], kg[st], 0, 0)))
        args.append(mask_bias)
    if block_sel is not None:
        in_specs.append(pl.BlockSpec((nh, MOBA_MAX_BLOCKS, t), lambda st, qi, kg: (0, 0, qi[st])))
        args.append(block_sel)
    grid_spec = pltpu.PrefetchScalarGridSpec(
        num_scalar_prefetch=2,
        grid=(len(qi_list),),
        in_specs=in_specs,
        out_specs=pl.BlockSpec((nh, HEAD_DIM, t), lambda st, qi, kg: (0, 0, qi[st])),
        scratch_shapes=[pltpu.VMEM((nh, 1, t), F32), pltpu.VMEM((nh, dv, t), F32)])
    return pl.pallas_call(
        functools.partial(_attn_kernel, nh=nh, n_near=n_near, group=group,
                          has_mask=mask_bias is not None, has_block_sel=block_sel is not None),
        grid_spec=grid_spec,
        out_shape=jax.ShapeDtypeStruct((nh, HEAD_DIM, s), F32),
        compiler_params=_cparams(("arbitrary",)),
    )(qi_arr, kg_arr, *args)


def _bitonic_stages(n, full_sort):
    levels = [2 ** p for p in range(1, n.bit_length())] if full_sort else [n]
    return [(k, k >> s) for k in levels for s in range(1, k.bit_length())]


def _plan_passes(stages, block):
    passes = []
    for k, j in stages:
        cross = j >= block
        if passes and passes[-1][2] == cross and (not cross or bin(passes[-1][0] | j).count("1") <= 3):
            passes[-1][0] |= j if cross else 0
            passes[-1][1].append((k, j))
        else:
            passes.append([j if cross else block - 1, [(k, j)], cross])
    return [(mask, sub) for mask, sub, _ in passes]


def _run_pass(load, store, n, mask, stages, regs=32):
    groups = {}
    for e in range(n):
        groups.setdefault(e & ~mask, []).append(e)
    groups = list(groups.values())
    per = max(1, regs // len(groups[0]))
    for g0 in range(0, len(groups), per):
        elems = sorted(e for grp in groups[g0:g0 + per] for e in grp)
        vals = {e: load(e) for e in elems}
        for k, j in stages:
            for e in elems:
                l = e ^ j
                if l > e:
                    hi = jnp.maximum(vals[e], vals[l])
                    lo = jnp.minimum(vals[e], vals[l])
                    vals[e], vals[l] = (hi, lo) if (e & k) == 0 else (lo, hi)
        for e in elems:
            store(e, vals[e])


def _sortable(x):
    return jnp.where(x == 0.0, 0.0, x)


def _dsa_sel_kernel(qi_ref, wt_ref, kn_ref, tri_ref, o_ref, keys_scr, seen_scr, work_scr, top_scr, *, tile, topk):
    n_chunks = kn_ref.shape[0]
    i = pl.program_id(0)
    n_live = i + 1
    groups_per_chunk = tile // 8
    causal = lax.broadcasted_iota(I32, (tile, tile), 0) <= lax.broadcasted_iota(I32, (tile, tile), 1)
    q_heads = [qi_ref[:, h * IDX_DIM:(h + 1) * IDX_DIM] for h in range(IDX_HEADS)]
    w_heads = [wt_ref[h:h + 1, :] for h in range(IDX_HEADS)]

    def score_chunk(c, diagonal):
        kc = kn_ref[c]
        dots = [lax.dot_general(kc, q_heads[h], _NT, preferred_element_type=F32) for h in range(IDX_HEADS)]
        sc = None
        for h in range(IDX_HEADS):
            term = w_heads[h] * jnp.maximum(dots[h], 0.0)
            sc = term if sc is None else sc + term
        if diagonal:
            sc = jnp.where(causal, sc, -jnp.inf)
        keys_scr[c] = _sortable(sc)

    def for_past_chunks(fn):
        def pair(p, carry):
            fn(2 * p)
            fn(2 * p + 1)
            return carry
        lax.fori_loop(0, i // 2, pair, 0)
        pl.when(i % 2 == 1)(lambda: fn(i - 1))

    for_past_chunks(lambda c: score_chunk(c, False))
    score_chunk(i, True)

    def count(pred_fn):
        def body(c, cnt):
            hits = [pred_fn(keys_scr[c, 8 * g:8 * g + 8, :]).astype(I32) for g in range(tile // 8)]
            while len(hits) > 1:
                hits = [hits[j] + hits[j + 1] for j in range(0, len(hits), 2)]
            return cnt + hits[0]
        cnt = lax.fori_loop(0, n_live, body, jnp.zeros((8, tile), I32))
        return jnp.sum(cnt, axis=0, keepdims=True)

    n_list = DSA_SORT_CHUNKS * groups_per_chunk
    n_batches = (n_live + DSA_SORT_CHUNKS - 1) // DSA_SORT_CHUNKS

    def pad_chunk(c, carry):
        keys_scr[c] = jnp.full((tile, tile), -jnp.inf, F32)
        return carry

    lax.fori_loop(n_live, n_batches * DSA_SORT_CHUNKS, pad_chunk, 0)
    top_scr[...] = jnp.full(top_scr.shape, -jnp.inf, F32)
    sort_passes = _plan_passes(_bitonic_stages(n_list, True), groups_per_chunk)
    merge_passes = _plan_passes(_bitonic_stages(n_list, False), groups_per_chunk)
    lane_halves = [slice(lo, lo + LANES) for lo in range(0, tile, LANES)]

    def run_network(passes, first_load, first_store, buf, lanes):
        def load(e):
            return buf[e, :, lanes]

        def store(e, v):
            buf[e, :, lanes] = v

        for n_pass, (mask, stages) in enumerate(passes):
            _run_pass(first_load if n_pass == 0 else load, first_store if n_pass == 0 else store,
                      n_list, mask, stages)

    def sort_batch(b, carry):
        for lanes in lane_halves:
            def load_keys(e, lanes=lanes):
                g = e % groups_per_chunk
                return keys_scr[b * DSA_SORT_CHUNKS + e // groups_per_chunk, 8 * g:8 * g + 8, lanes]

            def store_work(e, v, lanes=lanes):
                work_scr[e, :, lanes] = v

            run_network(sort_passes, load_keys, store_work, work_scr, lanes)

            def load_best(e, lanes=lanes):
                return jnp.maximum(top_scr[e, :, lanes], work_scr[n_list - 1 - e, :, lanes])

            def store_top(e, v, lanes=lanes):
                top_scr[e, :, lanes] = v

            run_network(merge_passes, load_best, store_top, top_scr, lanes)
        return carry

    lax.fori_loop(0, n_batches, sort_batch, 0)

    src, dst = top_scr, work_scr
    for shift in (4, 2, 1):
        for lanes in lane_halves:
            def load_pair(e, lanes=lanes, src=src, shift=shift):
                return jnp.maximum(src[e, :, lanes], pltpu.roll(src[n_list - 1 - e, :, lanes], shift, 0))

            def store_dst(e, v, lanes=lanes, dst=dst):
                dst[e, :, lanes] = v

            run_network(merge_passes, load_pair, store_dst, dst, lanes)
        src, dst = dst, src
    thr = src[topk - 1, 0:1, :]
    thr8 = jnp.broadcast_to(thr, (8, tile))
    n_gt = count(lambda kc: kc > thr8)
    need = (topk - n_gt).astype(F32)

    seen_scr[...] = jnp.zeros(seen_scr.shape, F32)

    def emit_chunk(c, diagonal):
        kc = keys_scr[c]
        eq = kc == thr
        eqf = eq.astype(F32)
        seen = seen_scr[...]
        seen_scr[...] = seen + jnp.sum(eqf, axis=0, keepdims=True)
        rank = seen + jnp.dot(tri_ref[...], eqf.astype(BF16), preferred_element_type=F32)
        sel = (kc > thr) | (eq & (rank < need))
        if diagonal:
            sel = sel & causal
        o_ref[0, c] = jnp.where(sel, 0.0, NEG).astype(BF16)

    for_past_chunks(lambda c: emit_chunk(c, False))
    emit_chunk(i, True)

    def fill_chunk(c, carry):
        o_ref[0, c] = jnp.full((tile, tile), NEG, BF16)
        return carry

    lax.fori_loop(n_live, n_chunks, fill_chunk, 0)


def _dsa_select(q_idx_bf16, w_idx, kn_bf16, topk):
    s = q_idx_bf16.shape[0]
    t = ATT_TILE
    n_chunks = s // t
    n_list = DSA_SORT_CHUNKS * (t // 8)
    assert topk <= n_list, "the sorting network keeps n_list best keys per query"
    kn3 = kn_bf16.reshape(n_chunks, t, IDX_DIM)
    wt = jnp.pad(w_idx.T, ((0, 8 - IDX_HEADS), (0, 0)))
    tri = (lax.broadcasted_iota(I32, (t, t), 1) < lax.broadcasted_iota(I32, (t, t), 0)).astype(BF16)
    return pl.pallas_call(
        functools.partial(_dsa_sel_kernel, tile=t, topk=topk),
        grid=(n_chunks,),
        in_specs=[pl.BlockSpec((t, IDX_HEADS * IDX_DIM), lambda i: (i, 0)),
                  pl.BlockSpec((8, t), lambda i: (0, i)),
                  pl.BlockSpec((n_chunks, t, IDX_DIM), lambda i: (0, 0, 0)),
                  pl.BlockSpec((t, t), lambda i: (0, 0))],
        out_specs=pl.BlockSpec((1, n_chunks, t, t), lambda i: (i, 0, 0, 0)),
        out_shape=jax.ShapeDtypeStruct((n_chunks, n_chunks, t, t), BF16),
        scratch_shapes=[pltpu.VMEM((max(n_chunks, DSA_SORT_CHUNKS), t, t), F32), pltpu.VMEM((1, t), F32),
                        pltpu.VMEM((n_list, 8, t), F32), pltpu.VMEM((n_list, 8, t), F32)],
        compiler_params=_cparams(("arbitrary",)),
    )(q_idx_bf16, wt, kn3, tri)


def _out_proj_kernel(y_ref, w_ref, x_ref, g_ref, o_ref):
    o_ref[...] = x_ref[...] + g_ref[...] * jnp.dot(y_ref[...], w_ref[...], preferred_element_type=F32)


def _out_proj(y_bf16, w_bf16, x, gate, tm=512):
    s, d = x.shape
    k = y_bf16.shape[1]
    return pl.pallas_call(
        _out_proj_kernel,
        grid=(s // tm,),
        in_specs=[pl.BlockSpec((tm, k), lambda i: (i, 0)),
                  pl.BlockSpec((k, d), lambda i: (0, 0)),
                  pl.BlockSpec((tm, d), lambda i: (i, 0)),
                  pl.BlockSpec((1, d), lambda i: (0, 0))],
        out_specs=pl.BlockSpec((tm, d), lambda i: (i, 0)),
        out_shape=jax.ShapeDtypeStruct((s, d), F32),
        compiler_params=_cparams(("arbitrary",)),
    )(y_bf16, w_bf16, x, gate.reshape(1, d))


def _cmp_exchange(v, i, j):
    hi = jnp.maximum(v[i], v[j])
    lo = jnp.minimum(v[i], v[j])
    v[i], v[j] = hi, lo


def _bitonic_sort_desc(v):
    n = len(v)
    k = 2
    while k <= n:
        j = k // 2
        while j >= 1:
            for i in range(n):
                l = i ^ j
                if l > i:
                    if (i & k) == 0:
                        _cmp_exchange(v, i, l)
                    else:
                        _cmp_exchange(v, l, i)
            j //= 2
        k *= 2


def _bitonic_merge_desc(v):
    n = len(v)
    j = n // 2
    while j >= 1:
        for i in range(n):
            l = i ^ j
            if l > i:
                _cmp_exchange(v, i, l)
        j //= 2


def _top16_rows(x):
    tt = x.shape[1]
    v = [x[8 * i:8 * i + 8, :] for i in range(PEER_NKEYS // 8)]
    _bitonic_sort_desc(v)
    for shift in (4, 2, 1):
        other = [pltpu.roll(a, shift, 0) for a in v]
        v = [jnp.maximum(v[i], other[len(v) - 1 - i]) for i in range(len(v))]
        _bitonic_merge_desc(v)
    return v


def _peer_route_kernel(q_ref, sk_ref, r1_ref, e1_ref, e0_ref, ns_ref, s0_scr, s1_scr, *, tt):
    sub = lax.broadcasted_iota(I32, (8, tt), 0)
    tops = [[], []]
    for h in range(PEER_HEADS):
        for p in range(2):
            lo = (2 * h + p) * (PEER_DKEY // 2)
            qhp = q_ref[:, lo:lo + PEER_DKEY // 2].astype(BF16)
            sc = lax.dot_general(sk_ref[2 * h + p], qhp, _NT, preferred_element_type=F32)
            (s0_scr if p == 0 else s1_scr)[h] = sc
            tops[p].append(_top16_rows(sc))
    a = [sum(jnp.where(sub == h, tops[0][h][k], 0.0) for h in range(PEER_HEADS)) for k in range(PEER_TOPK)]
    b = [sum(jnp.where(sub == h, tops[1][h][k], 0.0) for h in range(PEER_HEADS)) for k in range(PEER_TOPK)]
    cand = [a[i] + b[j] for i in range(PEER_TOPK) for j in range(PEER_TOPK) if (i + 1) * (j + 1) <= PEER_TOPK]
    cand = cand + [jnp.full((8, tt), -jnp.inf, F32)] * (64 - len(cand))
    _bitonic_sort_desc(cand)
    tau = cand[PEER_TOPK - 1]
    z = sum(jnp.exp(cand[k] - cand[0]) for k in range(PEER_TOPK))
    half_inv_z = 0.5 / z
    nsel_of_rank = [sum(jnp.where(a[r] + b[k] >= tau, 1.0, 0.0) for k in range(PEER_TOPK)) for r in range(PEER_TOPK)]
    for h in range(PEER_HEADS):
        s0 = s0_scr[h]
        s1 = s1_scr[h]
        nsel = jnp.zeros((PEER_NKEYS, tt), F32)
        rank1 = jnp.full((PEER_NKEYS, tt), float(PEER_TOPK), F32)
        for r in reversed(range(PEER_TOPK)):
            nsel = jnp.where(s0 == a[r][h:h + 1, :], nsel_of_rank[r][h:h + 1, :], nsel)
            rank1 = jnp.where(s1 == b[r][h:h + 1, :], float(r), rank1)
        ns_ref[h] = nsel
        r1_ref[h] = rank1.astype(BF16)
        e0_ref[h] = jnp.exp(s0 - a[0][h:h + 1, :]) * half_inv_z[h:h + 1, :]
        e1_ref[h] = jnp.exp(s1 - b[0][h:h + 1, :]).astype(BF16)


def _peer_route(q, subkeys_bf16, tt=256):
    s = q.shape[0]
    big = lambda dt: jax.ShapeDtypeStruct((PEER_HEADS, PEER_NKEYS, s), dt)
    big_spec = pl.BlockSpec((PEER_HEADS, PEER_NKEYS, tt), lambda i: (0, 0, i))
    return pl.pallas_call(
        functools.partial(_peer_route_kernel, tt=tt),
        grid=(s // tt,),
        in_specs=[pl.BlockSpec((tt, PEER_HEADS * PEER_DKEY), lambda i: (i, 0)),
                  pl.BlockSpec((2 * PEER_HEADS, PEER_NKEYS, PEER_DKEY // 2), lambda i: (0, 0, 0))],
        out_specs=[big_spec, big_spec, big_spec, big_spec],
        out_shape=[big(BF16), big(BF16), big(F32), big(F32)],
        scratch_shapes=[pltpu.VMEM((PEER_HEADS, PEER_NKEYS, tt), F32),
                        pltpu.VMEM((PEER_HEADS, PEER_NKEYS, tt), F32)],
        compiler_params=_cparams(("arbitrary",)),
    )(q, subkeys_bf16)


def _peer_dense_kernel(ht_ref, u_ref, vt_ref, r1_ref, e1_ref, e0_ref, ns_ref,
                       x_ref, g_ref, gf_ref, o_ref, acc_scr, p_scr, *, rows_per_step, final_norm):
    k = pl.program_id(1)

    @pl.when(k == 0)
    def _():
        acc_scr[...] = jnp.zeros(acc_scr.shape, F32)

    tt = ht_ref.shape[1]
    pack_rows = 16
    zero = jnp.zeros((pack_rows, tt), BF16)

    def activations(c):
        return jnp.dot(u_ref[c * MXU_DEPTH:(c + 1) * MXU_DEPTH, :], ht_ref[...], preferred_element_type=F32)

    def expert_rows(ii, a):
        i = k * rows_per_step + ii
        def bcast(ref, h):
            half = jnp.broadcast_to(ref[h, pl.ds(i, 1), :], (pack_rows // 2, tt))
            return jnp.concatenate([half, half], axis=0).astype(BF16)
        e0_rows = [bcast(e0_ref, h) for h in range(PEER_HEADS)]
        ns_rows = [bcast(ns_ref, h) for h in range(PEER_HEADS)]
        gelu2 = (a * (1.0 + lax.erf(a * math.sqrt(0.5)))).astype(BF16)
        for g in range(PEER_NKEYS // pack_rows):
            rows = slice(g * pack_rows, (g + 1) * pack_rows)
            wgt = None
            for h in range(PEER_HEADS):
                term = jnp.where(r1_ref[h, rows, :] < ns_rows[h], e1_ref[h, rows, :], zero) * e0_rows[h]
                wgt = term if wgt is None else wgt + term
            p_scr[ii * PEER_NKEYS + g * pack_rows:ii * PEER_NKEYS + (g + 1) * pack_rows, :] = wgt * gelu2[rows, :]

    depth_rows = MXU_DEPTH // PEER_NKEYS
    n_chunks = rows_per_step // depth_rows
    act = activations(0)
    for c in range(n_chunks):
        act_next = activations(c + 1) if c + 1 < n_chunks else None
        for j in range(depth_rows):
            expert_rows(c * depth_rows + j, act[j * PEER_NKEYS:(j + 1) * PEER_NKEYS, :])
        act = act_next
    acc_scr[...] += jnp.dot(vt_ref[...], p_scr[...], preferred_element_type=F32)

    @pl.when(k == pl.num_programs(1) - 1)
    def _():
        y = x_ref[...] + g_ref[...] * acc_scr[...].T
        if final_norm:
            y = y * lax.rsqrt(jnp.mean(y * y, axis=-1, keepdims=True) + EPS) * gf_ref[...]
        o_ref[...] = y


def _peer_dense(ht_bf16, u_bf16, vt_bf16, rank1, e1, e0, nsel, x, gate, g_final, final_norm,
                tt=512, rows_per_step=16):
    s, d = x.shape
    n_exp = u_bf16.shape[0]
    et = rows_per_step * PEER_NKEYS
    tok = lambda j, k: (0, 0, j)
    big_spec = pl.BlockSpec((PEER_HEADS, PEER_NKEYS, tt), tok)
    return pl.pallas_call(
        functools.partial(_peer_dense_kernel, rows_per_step=rows_per_step, final_norm=final_norm),
        grid=(s // tt, n_exp // et),
        in_specs=[pl.BlockSpec((d, tt), lambda j, k: (0, j)),
                  pl.BlockSpec((et, d), lambda j, k: (k, 0)),
                  pl.BlockSpec((d, et), lambda j, k: (0, k)),
                  big_spec, big_spec, big_spec, big_spec,
                  pl.BlockSpec((tt, d), lambda j, k: (j, 0)),
                  pl.BlockSpec((1, d), lambda j, k: (0, 0)),
                  pl.BlockSpec((1, d), lambda j, k: (0, 0))],
        out_specs=pl.BlockSpec((tt, d), lambda j, k: (j, 0)),
        out_shape=jax.ShapeDtypeStruct((s, d), F32),
        scratch_shapes=[pltpu.VMEM((d, tt), F32), pltpu.VMEM((et, tt), BF16)],
        compiler_params=_cparams(("arbitrary", "arbitrary")),
    )(ht_bf16, u_bf16, vt_bf16, rank1, e1, e0, nsel, x, gate.reshape(1, d), g_final.reshape(1, d))


def _heads_first(t, nh):
    s = t.shape[0]
    return jnp.transpose(t.reshape(s, nh, HEAD_DIM), (1, 0, 2))


def _mixer(h_proj, w_out_bf16, bias_tiles, x, gate):
    s = h_proj.shape[0]
    parts = []
    off = 0
    for size in PROJ_SIZES:
        parts.append(h_proj[:, off:off + size])
        off += size
    mq, mk, mv, dq, dk, dv, iq, ik, iw = parts
    scale = HEAD_DIM ** -0.5 * LOG2E
    dims_first = lambda a, nh: a.T.reshape(nh, HEAD_DIM, s).astype(BF16)

    mq_h, mk_h = _heads_first(mq, MOBA_HEADS), _heads_first(mk, MOBA_HEADS)
    block_sel = _moba_select(mq_h, _moba_kmean(mk_h))
    blk = jnp.arange(s, dtype=I32) // MOBA_BLOCK
    onehot = (blk[:, None] == jnp.arange(MOBA_MAX_BLOCKS, dtype=I32)[None, :]).astype(BF16)
    q_aug = jnp.concatenate([(mq_h * scale).astype(BF16), jnp.transpose(block_sel, (0, 2, 1)).astype(BF16)], axis=-1)
    k_aug = jnp.concatenate([mk_h.astype(BF16), jnp.broadcast_to(onehot, (MOBA_HEADS, s, MOBA_MAX_BLOCKS))], axis=-1)
    y_moba = _attention(q_aug, k_aug, dims_first(mv, MOBA_HEADS), bias_tiles, 0)

    ikf = ik.astype(F32)
    ik_n = ikf * lax.rsqrt(jnp.mean(ikf * ikf, axis=-1, keepdims=True) + EPS)
    topk = min(DSA_TOPK_MAX, s // 4)
    mask_bias = _dsa_select(iq.astype(BF16), iw * IDX_HEADS ** -0.5, ik_n.astype(BF16), topk)
    dq_h, dk_h = _heads_first(dq, DSA_HEADS), _heads_first(dk, DSA_HEADS)
    y_dsa = _attention((dq_h * scale).astype(BF16), dk_h.astype(BF16), dims_first(dv, DSA_HEADS),
                       bias_tiles, 1, mask_bias=mask_bias)

    y = jnp.concatenate([y_moba.reshape(MOBA_WIDTH, s), y_dsa.reshape(DSA_WIDTH, s)], axis=0).T
    return _out_proj(y.astype(BF16), w_out_bf16, x, gate)


def kernel(x, c, w_ada, b_ada, norm_attn, norm_ffn, w_in, w_out, rel_bias,
           peer_wq, peer_subkeys, peer_u, peer_v, norm_final):
    batch, s, d = x.shape
    assert batch == 1 and s % (ATT_KEY_TILES_PER_STEP * ATT_TILE) == 0
    depth = w_ada.shape[0]
    x = x[0]
    mod = _ada_mod(c, w_ada, b_ada)
    bias_tiles = _bias_tiles(rel_bias, ATT_TILE)
    n_pad = -(-PROJ_WIDTH // LANES) * LANES
    for l in range(depth):
        sh1, sc1, g1, sh2, sc2, g2 = jnp.split(mod[l], 6)
        w_in_l = jnp.pad(w_in[l], ((0, 0), (0, n_pad - PROJ_WIDTH))).astype(BF16)
        _, proj = _norm_mod_matmul(x, norm_attn[l], sh1, sc1, w_in_l)
        x = _mixer(proj, w_out[l].astype(BF16), bias_tiles, x, g1)
        h2, q = _norm_mod_matmul(x, norm_ffn[l], sh2, sc2, peer_wq[l].astype(BF16))
        subkeys = peer_subkeys[l].reshape(2 * PEER_HEADS, PEER_NKEYS, PEER_DKEY // 2).astype(BF16)
        rank1, e1, e0, nsel = _peer_route(q, subkeys)
        x = _peer_dense(h2.T, peer_u[l].astype(BF16), peer_v[l].T.astype(BF16), rank1, e1, e0, nsel,
                        x, g2, norm_final, final_norm=(l == depth - 1))
    return x[None]
```

```python
import functools
import math

import jax
import jax.numpy as jnp
from jax import lax
from jax.experimental import pallas as pl
from jax.experimental.pallas import tpu as pltpu

F32 = jnp.float32
BF16 = jnp.bfloat16
I32 = jnp.int32

HEAD_DIM = 64
MOBA_HEADS = 8
DSA_HEADS = 8
N_HEADS = MOBA_HEADS + DSA_HEADS
MOBA_WIDTH = MOBA_HEADS * HEAD_DIM
DSA_WIDTH = DSA_HEADS * HEAD_DIM
MOBA_BLOCK = 256
MOBA_TOPK = 3
MOBA_MAX_BLOCKS = 64
IDX_HEADS = 4
IDX_DIM = 64
DSA_TOPK_MAX = 256
DSA_SORT_CHUNKS = 8
N_BUCKETS = 32
MAX_DISTANCE = 4096
PEER_HEADS = 8
PEER_NKEYS = 128
PEER_DKEY = 256
PEER_TOPK = 16
EPS = 1e-6
PROJ_SIZES = (MOBA_WIDTH, MOBA_WIDTH, MOBA_WIDTH, DSA_WIDTH, DSA_WIDTH, DSA_WIDTH,
              IDX_HEADS * IDX_DIM, IDX_DIM, IDX_HEADS)
PROJ_WIDTH = sum(PROJ_SIZES)

NEG = -1e30
LOG2E = math.log2(math.e)
ATT_TILE = MOBA_BLOCK
ATT_KEY_TILES_PER_STEP = 4
LANES = 128
MXU_DEPTH = 256
VMEM_LIMIT = 56 * 1024 * 1024

_NT = (((1,), (1,)), ((), ()))


def _far_bias_tiles(tile):
    max_exact = N_BUCKETS // 2
    sat = int(math.ceil(max_exact * (MAX_DISTANCE / max_exact) ** ((N_BUCKETS - max_exact - 1) / (N_BUCKETS - max_exact)))) + 4
    return -(-(sat + tile - 1) // tile)


def _cparams(sem):
    return pltpu.CompilerParams(dimension_semantics=sem, vmem_limit_bytes=VMEM_LIMIT)


def _ada_kernel(c_ref, w_ref, b_ref, o_ref):
    c = c_ref[...]
    sc = c * jax.nn.sigmoid(c)
    o_ref[0] = jnp.dot(sc, w_ref[0], preferred_element_type=F32,
                       precision=lax.Precision.HIGHEST) + b_ref[0]


def _ada_mod(c, w_ada, b_ada):
    depth, d, n = w_ada.shape
    tn = 1536
    c8 = jnp.broadcast_to(c, (8, d))
    out = pl.pallas_call(
        _ada_kernel,
        grid=(depth, n // tn),
        in_specs=[pl.BlockSpec((8, d), lambda l, j: (0, 0)),
                  pl.BlockSpec((1, d, tn), lambda l, j: (l, 0, j)),
                  pl.BlockSpec((1, 1, tn), lambda l, j: (l, 0, j))],
        out_specs=pl.BlockSpec((1, 8, tn), lambda l, j: (l, 0, j)),
        out_shape=jax.ShapeDtypeStruct((depth, 8, n), F32),
        compiler_params=_cparams(("arbitrary", "arbitrary")),
    )(c8, w_ada, b_ada.reshape(depth, 1, n))
    return out[:, 0, :]


def _nmm_kernel(x_ref, g_ref, sh_ref, sc_ref, w_ref, h_ref, o_ref):
    x = x_ref[...]
    y = x * lax.rsqrt(jnp.mean(x * x, axis=-1, keepdims=True) + EPS) * g_ref[...]
    hb = (y * (1.0 + sc_ref[...]) + sh_ref[...]).astype(BF16)
    h_ref[...] = hb
    o_ref[...] = jnp.dot(hb, w_ref[...], preferred_element_type=F32)


def _norm_mod_matmul(x, g, shift, scale, w_bf16, tm=256):
    s, d = x.shape
    n = w_bf16.shape[1]
    row = lambda i: (0, 0)
    return pl.pallas_call(
        _nmm_kernel,
        grid=(s // tm,),
        in_specs=[pl.BlockSpec((tm, d), lambda i: (i, 0)),
                  pl.BlockSpec((1, d), row), pl.BlockSpec((1, d), row), pl.BlockSpec((1, d), row),
                  pl.BlockSpec((d, n), row)],
        out_specs=[pl.BlockSpec((tm, d), lambda i: (i, 0)),
                   pl.BlockSpec((tm, n), lambda i: (i, 0))],
        out_shape=[jax.ShapeDtypeStruct((s, d), BF16), jax.ShapeDtypeStruct((s, n), F32)],
        compiler_params=_cparams(("arbitrary",)),
    )(x, g.reshape(1, d), shift.reshape(1, d), scale.reshape(1, d), w_bf16)


def _t5_bucket(d):
    max_exact = N_BUCKETS // 2
    d = jnp.maximum(d, 0)
    df = jnp.maximum(d, 1).astype(F32)
    large = max_exact + (jnp.log(df / max_exact) / math.log(MAX_DISTANCE / max_exact)
                         * (N_BUCKETS - max_exact)).astype(I32)
    return jnp.where(d < max_exact, d, jnp.minimum(large, N_BUCKETS - 1))


def _bias_tile_kernel(tab_ref, o_ref, *, tile, n_near):
    h = pl.program_id(0)
    off = pl.program_id(1)
    b = lax.broadcasted_iota(I32, (tile, tile), 0)
    a = lax.broadcasted_iota(I32, (tile, tile), 1)
    d = jnp.where(off > n_near, -1, off * tile + a - b)
    bucket = _t5_bucket(d)
    val = lax.fori_loop(jnp.min(bucket), jnp.max(bucket) + 1,
                        lambda k, val: jnp.where(bucket == k, tab_ref[h, k], val),
                        jnp.zeros((tile, tile), F32))
    val = (val - tab_ref[h, N_BUCKETS - 1]) * LOG2E
    o_ref[0, 0] = jnp.where(d < 0, NEG, val)


def _bias_tiles(rel_bias, tile):
    n_near = _far_bias_tiles(tile)
    nh = rel_bias.shape[0]
    return pl.pallas_call(
        functools.partial(_bias_tile_kernel, tile=tile, n_near=n_near),
        grid=(nh, n_near + 2),
        in_specs=[pl.BlockSpec(memory_space=pltpu.SMEM)],
        out_specs=pl.BlockSpec((1, 1, tile, tile), lambda h, o: (h, o, 0, 0)),
        out_shape=jax.ShapeDtypeStruct((nh, n_near + 2, tile, tile), F32),
        compiler_params=_cparams(("arbitrary", "arbitrary")),
    )(rel_bias)


def _kmean_kernel(k_ref, o_ref, *, nb):
    k = k_ref[0]
    km = jnp.mean(k.reshape(nb, MOBA_BLOCK, HEAD_DIM), axis=1)
    o_ref[0] = jnp.zeros(o_ref.shape[1:], F32)
    o_ref[0, :nb, :] = km


def _moba_kmean(k_hsd):
    nh, s, dh = k_hsd.shape
    nb = s // MOBA_BLOCK
    return pl.pallas_call(
        functools.partial(_kmean_kernel, nb=nb),
        grid=(nh,),
        in_specs=[pl.BlockSpec((1, s, dh), lambda h: (h, 0, 0))],
        out_specs=pl.BlockSpec((1, MOBA_MAX_BLOCKS, dh), lambda h: (h, 0, 0)),
        out_shape=jax.ShapeDtypeStruct((nh, MOBA_MAX_BLOCKS, dh), F32),
        compiler_params=_cparams(("arbitrary",)),
    )(k_hsd)


def _moba_sel_kernel(q_ref, km_ref, o_ref, *, tq, n_sel):
    gate = lax.dot_general(km_ref[0], q_ref[0], _NT, preferred_element_type=F32,
                           precision=lax.Precision.HIGHEST)
    t = pl.program_id(1) * tq + lax.broadcasted_iota(I32, (MOBA_MAX_BLOCKS, tq), 1)
    b0 = t // MOBA_BLOCK
    n = lax.broadcasted_iota(I32, (MOBA_MAX_BLOCKS, tq), 0)
    avail = n < b0
    taken = n == b0
    for _ in range(n_sel):
        g = jnp.where(avail, gate, -jnp.inf)
        m = jnp.max(g, axis=0, keepdims=True)
        idx = jnp.min(jnp.where((g == m) & avail, n, 1 << 20), axis=0, keepdims=True)
        pick = n == idx
        taken = taken | pick
        avail = avail & jnp.logical_not(pick)
    o_ref[0] = jnp.where(taken, 0.0, NEG)


def _moba_select(q_hsd, kmean, tq=2048):
    nh, s, dh = q_hsd.shape
    nb = s // MOBA_BLOCK
    assert nb <= MOBA_MAX_BLOCKS
    n_sel = max(min(MOBA_TOPK, nb - 1), 1)
    tq = min(tq, s)
    return pl.pallas_call(
        functools.partial(_moba_sel_kernel, tq=tq, n_sel=n_sel),
        grid=(nh, s // tq),
        in_specs=[pl.BlockSpec((1, tq, dh), lambda h, i: (h, i, 0)),
                  pl.BlockSpec((1, MOBA_MAX_BLOCKS, dh), lambda h, i: (h, 0, 0))],
        out_specs=pl.BlockSpec((1, MOBA_MAX_BLOCKS, tq), lambda h, i: (h, 0, i)),
        out_shape=jax.ShapeDtypeStruct((nh, MOBA_MAX_BLOCKS, s), F32),
        compiler_params=_cparams(("arbitrary", "arbitrary")),
    )(q_hsd, kmean)


def _attn_kernel(qi_ref, kg_ref, *refs, nh, n_near, group, has_mask):
    q_ref, k_ref, vt_ref, bt_ref = refs[:4]
    mask_ref = refs[4] if has_mask else None
    o_ref, m_scr, acc_scr = refs[-3:]
    t = ATT_TILE
    step = pl.program_id(0)
    qi = qi_ref[step]
    kg = kg_ref[step]

    @pl.when(kg == 0)
    def _():
        m_scr[...] = jnp.full(m_scr.shape, NEG, F32)
        acc_scr[...] = jnp.zeros(acc_scr.shape, F32)

    def step_update(with_bias):
        m = [m_scr[h] for h in range(nh)]
        for g in range(group):
            keys = slice(g * t, (g + 1) * t)
            off = qi - (kg * group + g)
            bias_row = jnp.where(off < 0, n_near + 1, jnp.minimum(off, n_near))
            if has_mask:
                mb = mask_ref[0, g].astype(F32)
            scores = []
            for h in range(nh):
                s = lax.dot_general(k_ref[h, keys, :], q_ref[h], _NT, preferred_element_type=F32)
                if with_bias:
                    s = s + bt_ref[h, bias_row]
                if has_mask:
                    s = s + mb
                scores.append(s)
            probs, alphas = [], []
            for h in range(nh):
                m_new = jnp.maximum(m[h], jnp.max(scores[h], axis=0, keepdims=True))
                alpha = jnp.exp2(m[h] - m_new)
                p = jnp.exp2(scores[h] - m_new)
                m[h] = m_new
                probs.append(p.astype(BF16))
                alphas.append(alpha)
            for h in range(nh):
                acc_scr[h] = alphas[h] * acc_scr[h] + jnp.dot(vt_ref[h, :, keys], probs[h],
                                                              preferred_element_type=F32)
        for h in range(nh):
            m_scr[h] = m[h]

    all_far = qi - (kg * group + group - 1) >= n_near
    pl.when(all_far)(functools.partial(step_update, False))
    pl.when(jnp.logical_not(all_far))(functools.partial(step_update, True))

    @pl.when(kg == qi // group)
    def _():
        for h in range(nh):
            o_ref[h] = acc_scr[h, :HEAD_DIM, :] / acc_scr[h, HEAD_DIM:HEAD_DIM + 1, :]


def _attention(q, k, vt, bias_tiles, head_group, mask_bias=None):
    nh, s, dk = q.shape
    vt = jnp.concatenate([vt, jnp.ones((nh, 8, s), vt.dtype)], axis=1)
    dv = vt.shape[1]
    t = ATT_TILE
    group = ATT_KEY_TILES_PER_STEP
    nq = s // t
    n_near = bias_tiles.shape[1] - 2
    qi_list = [i for i in range(nq) for _ in range(i // group + 1)]
    kg_list = [j for i in range(nq) for j in range(i // group + 1)]
    qi_arr = jnp.asarray(qi_list, I32)
    kg_arr = jnp.asarray(kg_list, I32)
    in_specs = [pl.BlockSpec((nh, t, dk), lambda st, qi, kg: (0, qi[st], 0)),
                pl.BlockSpec((nh, group * t, dk), lambda st, qi, kg: (0, kg[st], 0)),
                pl.BlockSpec((nh, dv, group * t), lambda st, qi, kg: (0, 0, kg[st])),
                pl.BlockSpec((nh, n_near + 2, t, t), lambda st, qi, kg: (head_group, 0, 0, 0),
                             pipeline_mode=pl.Buffered(1))]
    args = [q, k, vt, bias_tiles]
    if mask_bias is not None:
        in_specs.append(pl.BlockSpec((1, group, t, t), lambda st, qi, kg: (qi[st], kg[st], 0, 0)))
        args.append(mask_bias)
    grid_spec = pltpu.PrefetchScalarGridSpec(
        num_scalar_prefetch=2,
        grid=(len(qi_list),),
        in_specs=in_specs,
        out_specs=pl.BlockSpec((nh, HEAD_DIM, t), lambda st, qi, kg: (0, 0, qi[st])),
        scratch_shapes=[pltpu.VMEM((nh, 1, t), F32), pltpu.VMEM((nh, dv, t), F32)])
    return pl.pallas_call(
        functools.partial(_attn_kernel, nh=nh, n_near=n_near, group=group, has_mask=mask_bias is not None),
        grid_spec=grid_spec,
        out_shape=jax.ShapeDtypeStruct((nh, HEAD_DIM, s), F32),
        compiler_params=_cparams(("arbitrary",)),
    )(qi_arr, kg_arr, *args)


def _bitonic_stages(n, full_sort):
    levels = [2 ** p for p in range(1, n.bit_length())] if full_sort else [n]
    return [(k, k >> s) for k in levels for s in range(1, k.bit_length())]


def _plan_passes(stages, block):
    passes = []
    for k, j in stages:
        cross = j >= block
        if passes and passes[-1][2] == cross and (not cross or bin(passes[-1][0] | j).count("1") <= 3):
            passes[-1][0] |= j if cross else 0
            passes[-1][1].append((k, j))
        else:
            passes.append([j if cross else block - 1, [(k, j)], cross])
    return [(mask, sub) for mask, sub, _ in passes]


def _run_pass(load, store, n, mask, stages, regs=32):
    groups = {}
    for e in range(n):
        groups.setdefault(e & ~mask, []).append(e)
    groups = list(groups.values())
    per = max(1, regs // len(groups[0]))
    for g0 in range(0, len(groups), per):
        elems = sorted(e for grp in groups[g0:g0 + per] for e in grp)
        vals = {e: load(e) for e in elems}
        for k, j in stages:
            for e in elems:
                l = e ^ j
                if l > e:
                    hi = jnp.maximum(vals[e], vals[l])
                    lo = jnp.minimum(vals[e], vals[l])
                    vals[e], vals[l] = (hi, lo) if (e & k) == 0 else (lo, hi)
        for e in elems:
            store(e, vals[e])


def _sortable(x):
    return jnp.where(x == 0.0, 0.0, x)


def _dsa_sel_kernel(qi_ref, wt_ref, kn_ref, tri_ref, o_ref, keys_scr, seen_scr, work_scr, top_scr, *, tile, topk):
    n_chunks = kn_ref.shape[0]
    i = pl.program_id(0)
    n_live = i + 1
    groups_per_chunk = tile // 8
    causal = lax.broadcasted_iota(I32, (tile, tile), 0) <= lax.broadcasted_iota(I32, (tile, tile), 1)
    q_heads = [qi_ref[:, h * IDX_DIM:(h + 1) * IDX_DIM] for h in range(IDX_HEADS)]
    w_heads = [wt_ref[h:h + 1, :] for h in range(IDX_HEADS)]

    def score_chunk(c, diagonal):
        kc = kn_ref[c]
        dots = [lax.dot_general(kc, q_heads[h], _NT, preferred_element_type=F32) for h in range(IDX_HEADS)]
        sc = None
        for h in range(IDX_HEADS):
            term = w_heads[h] * jnp.maximum(dots[h], 0.0)
            sc = term if sc is None else sc + term
        if diagonal:
            sc = jnp.where(causal, sc, -jnp.inf)
        keys_scr[c] = _sortable(sc)

    def for_past_chunks(fn):
        def pair(p, carry):
            fn(2 * p)
            fn(2 * p + 1)
            return carry
        lax.fori_loop(0, i // 2, pair, 0)
        pl.when(i % 2 == 1)(lambda: fn(i - 1))

    for_past_chunks(lambda c: score_chunk(c, False))
    score_chunk(i, True)

    n_list = DSA_SORT_CHUNKS * groups_per_chunk
    n_batches = (n_live + DSA_SORT_CHUNKS - 1) // DSA_SORT_CHUNKS

    def pad_chunk(c, carry):
        keys_scr[c] = jnp.full((tile, tile), -jnp.inf, F32)
        return carry

    lax.fori_loop(n_live, n_batches * DSA_SORT_CHUNKS, pad_chunk, 0)
    top_scr[...] = jnp.full(top_scr.shape, -jnp.inf, F32)
    sort_passes = _plan_passes(_bitonic_stages(n_list, True), groups_per_chunk)
    merge_passes = _plan_passes(_bitonic_stages(n_list, False), groups_per_chunk)
    lane_halves = [slice(lo, lo + LANES) for lo in range(0, tile, LANES)]

    def run_network(passes, first_load, first_store, buf, lanes):
        def load(e):
            return buf[e, :, lanes]

        def store(e, v):
            buf[e, :, lanes] = v

        for n_pass, (mask, stages) in enumerate(passes):
            _run_pass(first_load if n_pass == 0 else load, first_store if n_pass == 0 else store,
                      n_list, mask, stages)

    def sort_batch(b, carry):
        for lanes in lane_halves:
            def load_keys(e, lanes=lanes):
                g = e % groups_per_chunk
                return keys_scr[b * DSA_SORT_CHUNKS + e // groups_per_chunk, 8 * g:8 * g + 8, lanes]

            def store_work(e, v, lanes=lanes):
                work_scr[e, :, lanes] = v

            run_network(sort_passes, load_keys, store_work, work_scr, lanes)

            def load_best(e, lanes=lanes):
                return jnp.maximum(top_scr[e, :, lanes], work_scr[n_list - 1 - e, :, lanes])

            def store_top(e, v, lanes=lanes):
                top_scr[e, :, lanes] = v

            run_network(merge_passes, load_best, store_top, top_scr, lanes)
        return carry

    lax.fori_loop(0, n_batches, sort_batch, 0)

    src, dst = top_scr, work_scr
    for shift in (4, 2, 1):
        for lanes in lane_halves:
            def load_pair(e, lanes=lanes, src=src, shift=shift):
                return jnp.maximum(src[e, :, lanes], pltpu.roll(src[n_list - 1 - e, :, lanes], shift, 0))

            def store_dst(e, v, lanes=lanes, dst=dst):
                dst[e, :, lanes] = v

            run_network(merge_passes, load_pair, store_dst, dst, lanes)
        src, dst = dst, src
    thr = src[topk - 1, 0:1, :]
    hits = [(src[e] > thr).astype(I32) for e in range(topk - 1)]
    while len(hits) > 1:
        hits = [hits[j] + hits[j + 1] if j + 1 < len(hits) else hits[j] for j in range(0, len(hits), 2)]
    n_gt = hits[0][0:1, :]
    need = (topk - n_gt).astype(F32)

    seen_scr[...] = jnp.zeros(seen_scr.shape, F32)

    def emit_chunk(c, diagonal):
        kc = keys_scr[c]
        eq = kc == thr
        eqf = eq.astype(F32)
        seen = seen_scr[...]
        seen_scr[...] = seen + jnp.sum(eqf, axis=0, keepdims=True)
        rank = seen + jnp.dot(tri_ref[...], eqf.astype(BF16), preferred_element_type=F32)
        sel = (kc > thr) | (eq & (rank < need))
        if diagonal:
            sel = sel & causal
        o_ref[0, c] = jnp.where(sel, 0.0, NEG).astype(BF16)

    for_past_chunks(lambda c: emit_chunk(c, False))
    emit_chunk(i, True)

    def fill_chunk(c, carry):
        o_ref[0, c] = jnp.full((tile, tile), NEG, BF16)
        return carry

    lax.fori_loop(n_live, n_chunks, fill_chunk, 0)


def _dsa_select(q_idx_bf16, w_idx, kn_bf16, topk):
    s = q_idx_bf16.shape[0]
    t = ATT_TILE
    n_chunks = s // t
    n_list = DSA_SORT_CHUNKS * (t // 8)
    assert topk <= n_list, "the sorting network keeps n_list best keys per query"
    kn3 = kn_bf16.reshape(n_chunks, t, IDX_DIM)
    wt = jnp.pad(w_idx.T, ((0, 8 - IDX_HEADS), (0, 0)))
    tri = (lax.broadcasted_iota(I32, (t, t), 1) < lax.broadcasted_iota(I32, (t, t), 0)).astype(BF16)
    return pl.pallas_call(
        functools.partial(_dsa_sel_kernel, tile=t, topk=topk),
        grid=(n_chunks,),
        in_specs=[pl.BlockSpec((t, IDX_HEADS * IDX_DIM), lambda i: (i, 0)),
                  pl.BlockSpec((8, t), lambda i: (0, i)),
                  pl.BlockSpec((n_chunks, t, IDX_DIM), lambda i: (0, 0, 0)),
                  pl.BlockSpec((t, t), lambda i: (0, 0))],
        out_specs=pl.BlockSpec((1, n_chunks, t, t), lambda i: (i, 0, 0, 0)),
        out_shape=jax.ShapeDtypeStruct((n_chunks, n_chunks, t, t), BF16),
        scratch_shapes=[pltpu.VMEM((max(n_chunks, DSA_SORT_CHUNKS), t, t), F32), pltpu.VMEM((1, t), F32),
                        pltpu.VMEM((n_list, 8, t), F32), pltpu.VMEM((n_list, 8, t), F32)],
        compiler_params=_cparams(("arbitrary",)),
    )(q_idx_bf16, wt, kn3, tri)


def _out_proj_kernel(y_ref, w_ref, x_ref, g_ref, o_ref):
    o_ref[...] = x_ref[...] + g_ref[...] * jnp.dot(y_ref[...], w_ref[...], preferred_element_type=F32)


def _out_proj(y_bf16, w_bf16, x, gate, tm=512):
    s, d = x.shape
    k = y_bf16.shape[1]
    return pl.pallas_call(
        _out_proj_kernel,
        grid=(s // tm,),
        in_specs=[pl.BlockSpec((tm, k), lambda i: (i, 0)),
                  pl.BlockSpec((k, d), lambda i: (0, 0)),
                  pl.BlockSpec((tm, d), lambda i: (i, 0)),
                  pl.BlockSpec((1, d), lambda i: (0, 0))],
        out_specs=pl.BlockSpec((tm, d), lambda i: (i, 0)),
        out_shape=jax.ShapeDtypeStruct((s, d), F32),
        compiler_params=_cparams(("arbitrary",)),
    )(y_bf16, w_bf16, x, gate.reshape(1, d))


def _cmp_exchange(v, i, j):
    hi = jnp.maximum(v[i], v[j])
    lo = jnp.minimum(v[i], v[j])
    v[i], v[j] = hi, lo


def _bitonic_sort_desc(v):
    n = len(v)
    k = 2
    while k <= n:
        j = k // 2
        while j >= 1:
            for i in range(n):
                l = i ^ j
                if l > i:
                    if (i & k) == 0:
                        _cmp_exchange(v, i, l)
                    else:
                        _cmp_exchange(v, l, i)
            j //= 2
        k *= 2


def _bitonic_merge_desc(v):
    n = len(v)
    j = n // 2
    while j >= 1:
        for i in range(n):
            l = i ^ j
            if l > i:
                _cmp_exchange(v, i, l)
        j //= 2


def _top16_rows(x):
    tt = x.shape[1]
    v = [x[8 * i:8 * i + 8, :] for i in range(PEER_NKEYS // 8)]
    _bitonic_sort_desc(v)
    for shift in (4, 2, 1):
        other = [pltpu.roll(a, shift, 0) for a in v]
        v = [jnp.maximum(v[i], other[len(v) - 1 - i]) for i in range(len(v))]
        _bitonic_merge_desc(v)
    return v


def _peer_route_kernel(q_ref, sk_ref, r1_ref, e1_ref, e0_ref, ns_ref, s0_scr, s1_scr, *, tt):
    sub = lax.broadcasted_iota(I32, (8, tt), 0)
    tops = [[], []]
    for h in range(PEER_HEADS):
        for p in range(2):
            lo = (2 * h + p) * (PEER_DKEY // 2)
            qhp = q_ref[:, lo:lo + PEER_DKEY // 2].astype(BF16)
            sc = lax.dot_general(sk_ref[2 * h + p], qhp, _NT, preferred_element_type=F32)
            (s0_scr if p == 0 else s1_scr)[h] = sc
            tops[p].append(_top16_rows(sc))
    a = [sum(jnp.where(sub == h, tops[0][h][k], 0.0) for h in range(PEER_HEADS)) for k in range(PEER_TOPK)]
    b = [sum(jnp.where(sub == h, tops[1][h][k], 0.0) for h in range(PEER_HEADS)) for k in range(PEER_TOPK)]
    cand = [a[i] + b[j] for i in range(PEER_TOPK) for j in range(PEER_TOPK) if (i + 1) * (j + 1) <= PEER_TOPK]
    cand = cand + [jnp.full((8, tt), -jnp.inf, F32)] * (64 - len(cand))
    _bitonic_sort_desc(cand)
    tau = cand[PEER_TOPK - 1]
    z = sum(jnp.exp(cand[k] - cand[0]) for k in range(PEER_TOPK))
    half_inv_z = 0.5 / z
    nsel_of_rank = [sum(jnp.where(a[r] + b[k] >= tau, 1.0, 0.0) for k in range(PEER_TOPK)) for r in range(PEER_TOPK)]
    for h in range(PEER_HEADS):
        s0 = s0_scr[h]
        s1 = s1_scr[h]
        nsel = jnp.zeros((PEER_NKEYS, tt), F32)
        rank1 = jnp.full((PEER_NKEYS, tt), float(PEER_TOPK), F32)
        for r in reversed(range(PEER_TOPK)):
            nsel = jnp.where(s0 == a[r][h:h + 1, :], nsel_of_rank[r][h:h + 1, :], nsel)
            rank1 = jnp.where(s1 == b[r][h:h + 1, :], float(r), rank1)
        ns_ref[h] = nsel
        r1_ref[h] = rank1.astype(BF16)
        e0_ref[h] = jnp.exp(s0 - a[0][h:h + 1, :]) * half_inv_z[h:h + 1, :]
        e1_ref[h] = jnp.exp(s1 - b[0][h:h + 1, :]).astype(BF16)


def _peer_route(q, subkeys_bf16, tt=256):
    s = q.shape[0]
    big = lambda dt: jax.ShapeDtypeStruct((PEER_HEADS, PEER_NKEYS, s), dt)
    big_spec = pl.BlockSpec((PEER_HEADS, PEER_NKEYS, tt), lambda i: (0, 0, i))
    return pl.pallas_call(
        functools.partial(_peer_route_kernel, tt=tt),
        grid=(s // tt,),
        in_specs=[pl.BlockSpec((tt, PEER_HEADS * PEER_DKEY), lambda i: (i, 0)),
                  pl.BlockSpec((2 * PEER_HEADS, PEER_NKEYS, PEER_DKEY // 2), lambda i: (0, 0, 0))],
        out_specs=[big_spec, big_spec, big_spec, big_spec],
        out_shape=[big(BF16), big(BF16), big(F32), big(F32)],
        scratch_shapes=[pltpu.VMEM((PEER_HEADS, PEER_NKEYS, tt), F32),
                        pltpu.VMEM((PEER_HEADS, PEER_NKEYS, tt), F32)],
        compiler_params=_cparams(("arbitrary",)),
    )(q, subkeys_bf16)


def _peer_dense_kernel(ht_ref, u_ref, vt_ref, r1_ref, e1_ref, e0_ref, ns_ref,
                       x_ref, g_ref, gf_ref, o_ref, acc_scr, p_scr, *, rows_per_step, final_norm):
    k = pl.program_id(1)

    @pl.when(k == 0)
    def _():
        acc_scr[...] = jnp.zeros(acc_scr.shape, F32)

    tt = ht_ref.shape[1]
    pack_rows = 16
    zero = jnp.zeros((pack_rows, tt), BF16)

    def activations(c):
        return jnp.dot(u_ref[c * MXU_DEPTH:(c + 1) * MXU_DEPTH, :], ht_ref[...], preferred_element_type=F32)

    def expert_rows(ii, a):
        i = k * rows_per_step + ii
        def bcast(ref, h):
            half = jnp.broadcast_to(ref[h, pl.ds(i, 1), :], (pack_rows // 2, tt))
            return jnp.concatenate([half, half], axis=0).astype(BF16)
        e0_rows = [bcast(e0_ref, h) for h in range(PEER_HEADS)]
        ns_rows = [bcast(ns_ref, h) for h in range(PEER_HEADS)]
        gelu2 = (a * (1.0 + lax.erf(a * math.sqrt(0.5)))).astype(BF16)
        for g in range(PEER_NKEYS // pack_rows):
            rows = slice(g * pack_rows, (g + 1) * pack_rows)
            wgt = None
            for h in range(PEER_HEADS):
                term = jnp.where(r1_ref[h, rows, :] < ns_rows[h], e1_ref[h, rows, :], zero) * e0_rows[h]
                wgt = term if wgt is None else wgt + term
            p_scr[ii * PEER_NKEYS + g * pack_rows:ii * PEER_NKEYS + (g + 1) * pack_rows, :] = wgt * gelu2[rows, :]

    depth_rows = MXU_DEPTH // PEER_NKEYS
    n_chunks = rows_per_step // depth_rows
    act = activations(0)
    for c in range(n_chunks):
        act_next = activations(c + 1) if c + 1 < n_chunks else None
        for j in range(depth_rows):
            expert_rows(c * depth_rows + j, act[j * PEER_NKEYS:(j + 1) * PEER_NKEYS, :])
        act = act_next
    acc_scr[...] += jnp.dot(vt_ref[...], p_scr[...], preferred_element_type=F32)

    @pl.when(k == pl.num_programs(1) - 1)
    def _():
        y = x_ref[...] + g_ref[...] * acc_scr[...].T
        if final_norm:
            y = y * lax.rsqrt(jnp.mean(y * y, axis=-1, keepdims=True) + EPS) * gf_ref[...]
        o_ref[...] = y


def _peer_dense(ht_bf16, u_bf16, vt_bf16, rank1, e1, e0, nsel, x, gate, g_final, final_norm,
                tt=512, rows_per_step=16):
    s, d = x.shape
    n_exp = u_bf16.shape[0]
    et = rows_per_step * PEER_NKEYS
    tok = lambda j, k: (0, 0, j)
    big_spec = pl.BlockSpec((PEER_HEADS, PEER_NKEYS, tt), tok)
    return pl.pallas_call(
        functools.partial(_peer_dense_kernel, rows_per_step=rows_per_step, final_norm=final_norm),
        grid=(s // tt, n_exp // et),
        in_specs=[pl.BlockSpec((d, tt), lambda j, k: (0, j)),
                  pl.BlockSpec((et, d), lambda j, k: (k, 0)),
                  pl.BlockSpec((d, et), lambda j, k: (0, k)),
                  big_spec, big_spec, big_spec, big_spec,
                  pl.BlockSpec((tt, d), lambda j, k: (j, 0)),
                  pl.BlockSpec((1, d), lambda j, k: (0, 0)),
                  pl.BlockSpec((1, d), lambda j, k: (0, 0))],
        out_specs=pl.BlockSpec((tt, d), lambda j, k: (j, 0)),
        out_shape=jax.ShapeDtypeStruct((s, d), F32),
        scratch_shapes=[pltpu.VMEM((d, tt), F32), pltpu.VMEM((et, tt), BF16)],
        compiler_params=_cparams(("arbitrary", "arbitrary")),
    )(ht_bf16, u_bf16, vt_bf16, rank1, e1, e0, nsel, x, gate.reshape(1, d), g_final.reshape(1, d))


def _heads_first(t, nh):
    s = t.shape[0]
    return jnp.transpose(t.reshape(s, nh, HEAD_DIM), (1, 0, 2))


def _mixer(h_proj, w_out_bf16, bias_tiles, x, gate):
    s = h_proj.shape[0]
    parts = []
    off = 0
    for size in PROJ_SIZES:
        parts.append(h_proj[:, off:off + size])
        off += size
    mq, mk, mv, dq, dk, dv, iq, ik, iw = parts
    scale = HEAD_DIM ** -0.5 * LOG2E
    dims_first = lambda a, nh: a.T.reshape(nh, HEAD_DIM, s).astype(BF16)

    mq_h, mk_h = _heads_first(mq, MOBA_HEADS), _heads_first(mk, MOBA_HEADS)
    block_sel = _moba_select(mq_h, _moba_kmean(mk_h))
    blk = jnp.arange(s, dtype=I32) // MOBA_BLOCK
    onehot = (blk[:, None] == jnp.arange(MOBA_MAX_BLOCKS, dtype=I32)[None, :]).astype(BF16)
    q_aug = jnp.concatenate([(mq_h * scale).astype(BF16), jnp.transpose(block_sel, (0, 2, 1)).astype(BF16)], axis=-1)
    k_aug = jnp.concatenate([mk_h.astype(BF16), jnp.broadcast_to(onehot, (MOBA_HEADS, s, MOBA_MAX_BLOCKS))], axis=-1)
    y_moba = _attention(q_aug, k_aug, dims_first(mv, MOBA_HEADS), bias_tiles, 0)

    ikf = ik.astype(F32)
    ik_n = ikf * lax.rsqrt(jnp.mean(ikf * ikf, axis=-1, keepdims=True) + EPS)
    topk = min(DSA_TOPK_MAX, s // 4)
    mask_bias = _dsa_select(iq.astype(BF16), iw * IDX_HEADS ** -0.5, ik_n.astype(BF16), topk)
    dq_h, dk_h = _heads_first(dq, DSA_HEADS), _heads_first(dk, DSA_HEADS)
    y_dsa = _attention((dq_h * scale).astype(BF16), dk_h.astype(BF16), dims_first(dv, DSA_HEADS),
                       bias_tiles, 1, mask_bias=mask_bias)

    y = jnp.concatenate([y_moba.reshape(MOBA_WIDTH, s), y_dsa.reshape(DSA_WIDTH, s)], axis=0).T
    return _out_proj(y.astype(BF16), w_out_bf16, x, gate)


def kernel(x, c, w_ada, b_ada, norm_attn, norm_ffn, w_in, w_out, rel_bias,
           peer_wq, peer_subkeys, peer_u, peer_v, norm_final):
    batch, s, d = x.shape
    assert batch == 1 and s % (ATT_KEY_TILES_PER_STEP * ATT_TILE) == 0
    depth = w_ada.shape[0]
    x = x[0]
    mod = _ada_mod(c, w_ada, b_ada)
    bias_tiles = _bias_tiles(rel_bias, ATT_TILE)
    n_pad = -(-PROJ_WIDTH // LANES) * LANES
    for l in range(depth):
        sh1, sc1, g1, sh2, sc2, g2 = jnp.split(mod[l], 6)
        w_in_l = jnp.pad(w_in[l], ((0, 0), (0, n_pad - PROJ_WIDTH))).astype(BF16)
        _, proj = _norm_mod_matmul(x, norm_attn[l], sh1, sc1, w_in_l)
        x = _mixer(proj, w_out[l].astype(BF16), bias_tiles, x, g1)
        h2, q = _norm_mod_matmul(x, norm_ffn[l], sh2, sc2, peer_wq[l].astype(BF16))
        subkeys = peer_subkeys[l].reshape(2 * PEER_HEADS, PEER_NKEYS, PEER_DKEY // 2).astype(BF16)
        rank1, e1, e0, nsel = _peer_route(q, subkeys)
        x = _peer_dense(h2.T, peer_u[l].astype(BF16), peer_v[l].T.astype(BF16), rank1, e1, e0, nsel,
                        x, g2, norm_final, final_norm=(l == depth - 1))
    return x[None]
```

```python
import functools
import math

import jax
import jax.numpy as jnp
from jax import lax
from jax.experimental import pallas as pl
from jax.experimental.pallas import tpu as pltpu

F32 = jnp.float32
BF16 = jnp.bfloat16
I32 = jnp.int32

HEAD_DIM = 64
MOBA_HEADS = 8
DSA_HEADS = 8
N_HEADS = MOBA_HEADS + DSA_HEADS
MOBA_WIDTH = MOBA_HEADS * HEAD_DIM
DSA_WIDTH = DSA_HEADS * HEAD_DIM
MOBA_BLOCK = 256
MOBA_TOPK = 3
MOBA_MAX_BLOCKS = 64
IDX_HEADS = 4
IDX_DIM = 64
DSA_TOPK_MAX = 256
DSA_SORT_CHUNKS = 8
N_BUCKETS = 32
MAX_DISTANCE = 4096
PEER_HEADS = 8
PEER_NKEYS = 128
PEER_DKEY = 256
PEER_TOPK = 16
EPS = 1e-6
PROJ_SIZES = (MOBA_WIDTH, MOBA_WIDTH, MOBA_WIDTH, DSA_WIDTH, DSA_WIDTH, DSA_WIDTH,
              IDX_HEADS * IDX_DIM, IDX_DIM, IDX_HEADS)
PROJ_WIDTH = sum(PROJ_SIZES)

NEG = -1e30
LOG2E = math.log2(math.e)
ATT_TILE = MOBA_BLOCK
ATT_KEY_TILES_PER_STEP = 4
LANES = 128
MXU_DEPTH = 256
VMEM_LIMIT = 56 * 1024 * 1024

_NT = (((1,), (1,)), ((), ()))


def _far_bias_tiles(tile):
    max_exact = N_BUCKETS // 2
    sat = int(math.ceil(max_exact * (MAX_DISTANCE / max_exact) ** ((N_BUCKETS - max_exact - 1) / (N_BUCKETS - max_exact)))) + 4
    return -(-(sat + tile - 1) // tile)


def _cparams(sem):
    return pltpu.CompilerParams(dimension_semantics=sem, vmem_limit_bytes=VMEM_LIMIT)


def _ada_kernel(c_ref, w_ref, b_ref, o_ref):
    c = c_ref[...]
    sc = c * jax.nn.sigmoid(c)
    o_ref[0] = jnp.dot(sc, w_ref[0], preferred_element_type=F32,
                       precision=lax.Precision.HIGHEST) + b_ref[0]


def _ada_mod(c, w_ada, b_ada):
    depth, d, n = w_ada.shape
    tn = 1536
    c8 = jnp.broadcast_to(c, (8, d))
    out = pl.pallas_call(
        _ada_kernel,
        grid=(depth, n // tn),
        in_specs=[pl.BlockSpec((8, d), lambda l, j: (0, 0)),
                  pl.BlockSpec((1, d, tn), lambda l, j: (l, 0, j)),
                  pl.BlockSpec((1, 1, tn), lambda l, j: (l, 0, j))],
        out_specs=pl.BlockSpec((1, 8, tn), lambda l, j: (l, 0, j)),
        out_shape=jax.ShapeDtypeStruct((depth, 8, n), F32),
        compiler_params=_cparams(("arbitrary", "arbitrary")),
    )(c8, w_ada, b_ada.reshape(depth, 1, n))
    return out[:, 0, :]


def _nmm_kernel(x_ref, g_ref, sh_ref, sc_ref, w_ref, h_ref, o_ref):
    x = x_ref[...]
    y = x * lax.rsqrt(jnp.mean(x * x, axis=-1, keepdims=True) + EPS) * g_ref[...]
    hb = (y * (1.0 + sc_ref[...]) + sh_ref[...]).astype(BF16)
    h_ref[...] = hb
    o_ref[...] = jnp.dot(hb, w_ref[...], preferred_element_type=F32)


def _norm_mod_matmul(x, g, shift, scale, w_bf16, tm=512):
    s, d = x.shape
    n = w_bf16.shape[1]
    row = lambda i: (0, 0)
    return pl.pallas_call(
        _nmm_kernel,
        grid=(s // tm,),
        in_specs=[pl.BlockSpec((tm, d), lambda i: (i, 0)),
                  pl.BlockSpec((1, d), row), pl.BlockSpec((1, d), row), pl.BlockSpec((1, d), row),
                  pl.BlockSpec((d, n), row)],
        out_specs=[pl.BlockSpec((tm, d), lambda i: (i, 0)),
                   pl.BlockSpec((tm, n), lambda i: (i, 0))],
        out_shape=[jax.ShapeDtypeStruct((s, d), BF16), jax.ShapeDtypeStruct((s, n), F32)],
        compiler_params=_cparams(("arbitrary",)),
    )(x, g.reshape(1, d), shift.reshape(1, d), scale.reshape(1, d), w_bf16)


def _t5_bucket(d):
    max_exact = N_BUCKETS // 2
    d = jnp.maximum(d, 0)
    df = jnp.maximum(d, 1).astype(F32)
    large = max_exact + (jnp.log(df / max_exact) / math.log(MAX_DISTANCE / max_exact)
                         * (N_BUCKETS - max_exact)).astype(I32)
    return jnp.where(d < max_exact, d, jnp.minimum(large, N_BUCKETS - 1))


def _bias_tile_kernel(tab_ref, o_ref, *, tile):
    h = pl.program_id(0)
    off = pl.program_id(1)
    b = lax.broadcasted_iota(I32, (tile, tile), 0)
    a = lax.broadcasted_iota(I32, (tile, tile), 1)
    d = off * tile + a - b
    bucket = _t5_bucket(d)
    val = lax.fori_loop(jnp.min(bucket), jnp.max(bucket) + 1,
                        lambda k, val: jnp.where(bucket == k, tab_ref[h, k], val),
                        jnp.zeros((tile, tile), F32))
    val = (val - tab_ref[h, N_BUCKETS - 1]) * LOG2E
    o_ref[0, 0] = jnp.where(d < 0, NEG, val)


def _bias_tiles(rel_bias, tile):
    n_near = _far_bias_tiles(tile)
    nh = rel_bias.shape[0]
    return pl.pallas_call(
        functools.partial(_bias_tile_kernel, tile=tile),
        grid=(nh, n_near + 1),
        in_specs=[pl.BlockSpec(memory_space=pltpu.SMEM)],
        out_specs=pl.BlockSpec((1, 1, tile, tile), lambda h, o: (h, o, 0, 0)),
        out_shape=jax.ShapeDtypeStruct((nh, n_near + 1, tile, tile), F32),
        compiler_params=_cparams(("arbitrary", "arbitrary")),
    )(rel_bias)


def _kmean_kernel(k_ref, o_ref, *, nb):
    k = k_ref[0]
    km = jnp.mean(k.reshape(nb, MOBA_BLOCK, HEAD_DIM), axis=1)
    o_ref[0] = jnp.zeros(o_ref.shape[1:], F32)
    o_ref[0, :nb, :] = km


def _moba_kmean(k_hsd):
    nh, s, dh = k_hsd.shape
    nb = s // MOBA_BLOCK
    return pl.pallas_call(
        functools.partial(_kmean_kernel, nb=nb),
        grid=(nh,),
        in_specs=[pl.BlockSpec((1, s, dh), lambda h: (h, 0, 0))],
        out_specs=pl.BlockSpec((1, MOBA_MAX_BLOCKS, dh), lambda h: (h, 0, 0)),
        out_shape=jax.ShapeDtypeStruct((nh, MOBA_MAX_BLOCKS, dh), F32),
        compiler_params=_cparams(("arbitrary",)),
    )(k_hsd)


def _moba_sel_kernel(q_ref, km_ref, o_ref, *, tq, n_sel):
    gate = lax.dot_general(km_ref[0], q_ref[0], _NT, preferred_element_type=F32,
                           precision=lax.Precision.HIGHEST)
    t = pl.program_id(1) * tq + lax.broadcasted_iota(I32, (MOBA_MAX_BLOCKS, tq), 1)
    b0 = t // MOBA_BLOCK
    n = lax.broadcasted_iota(I32, (MOBA_MAX_BLOCKS, tq), 0)
    avail = n < b0
    taken = n == b0
    for _ in range(n_sel):
        g = jnp.where(avail, gate, -jnp.inf)
        m = jnp.max(g, axis=0, keepdims=True)
        idx = jnp.min(jnp.where((g == m) & avail, n, 1 << 20), axis=0, keepdims=True)
        pick = n == idx
        taken = taken | pick
        avail = avail & jnp.logical_not(pick)
    o_ref[0] = jnp.where(taken, 0.0, NEG)


def _moba_select(q_hsd, kmean, tq=2048):
    nh, s, dh = q_hsd.shape
    nb = s // MOBA_BLOCK
    assert nb <= MOBA_MAX_BLOCKS
    n_sel = max(min(MOBA_TOPK, nb - 1), 1)
    tq = min(tq, s)
    return pl.pallas_call(
        functools.partial(_moba_sel_kernel, tq=tq, n_sel=n_sel),
        grid=(nh, s // tq),
        in_specs=[pl.BlockSpec((1, tq, dh), lambda h, i: (h, i, 0)),
                  pl.BlockSpec((1, MOBA_MAX_BLOCKS, dh), lambda h, i: (h, 0, 0))],
        out_specs=pl.BlockSpec((1, MOBA_MAX_BLOCKS, tq), lambda h, i: (h, 0, i)),
        out_shape=jax.ShapeDtypeStruct((nh, MOBA_MAX_BLOCKS, s), F32),
        compiler_params=_cparams(("arbitrary", "arbitrary")),
    )(q_hsd, kmean)


def _attn_kernel(qi_ref, kg_ref, *refs, nh, n_near, group, has_mask):
    q_ref, k_ref, vt_ref, bt_ref = refs[:4]
    mask_ref = refs[4] if has_mask else None
    o_ref, m_scr, acc_scr = refs[-3:]
    t = ATT_TILE
    step = pl.program_id(0)
    qi = qi_ref[step]
    kg = kg_ref[step]

    @pl.when(kg == 0)
    def _():
        m_scr[...] = jnp.full(m_scr.shape, NEG, F32)
        acc_scr[...] = jnp.zeros(acc_scr.shape, F32)

    def step_update(with_bias, n_tiles):
        m = [m_scr[h] for h in range(nh)]
        for g in range(n_tiles):
            keys = slice(g * t, (g + 1) * t)
            off = qi - (kg * group + g)
            bias_row = jnp.minimum(off, n_near)
            if has_mask:
                mb = mask_ref[0, g].astype(F32)
            scores = []
            for h in range(nh):
                s = lax.dot_general(k_ref[h, keys, :], q_ref[h], _NT, preferred_element_type=F32)
                if with_bias:
                    s = s + bt_ref[h, bias_row]
                if has_mask:
                    s = s + mb
                scores.append(s)
            probs, alphas = [], []
            for h in range(nh):
                m_new = jnp.maximum(m[h], jnp.max(scores[h], axis=0, keepdims=True))
                alpha = jnp.exp2(m[h] - m_new)
                p = jnp.exp2(scores[h] - m_new)
                m[h] = m_new
                probs.append(p.astype(BF16))
                alphas.append(alpha)
            for h in range(nh):
                acc_scr[h] = alphas[h] * acc_scr[h] + jnp.dot(vt_ref[h, :, keys], probs[h],
                                                              preferred_element_type=F32)
        for h in range(nh):
            m_scr[h] = m[h]

    all_far = qi - (kg * group + group - 1) >= n_near
    pl.when(all_far)(functools.partial(step_update, False, group))
    n_causal = jnp.minimum(qi - kg * group + 1, group)
    for n_tiles in range(1, group + 1):
        pl.when(jnp.logical_not(all_far) & (n_causal == n_tiles))(functools.partial(step_update, True, n_tiles))

    @pl.when(kg == qi // group)
    def _():
        for h in range(nh):
            o_ref[h] = acc_scr[h, :HEAD_DIM, :] / acc_scr[h, HEAD_DIM:HEAD_DIM + 1, :]


def _attention(q, k, vt, bias_tiles, head_group, mask_bias=None):
    nh, s, dk = q.shape
    vt = jnp.concatenate([vt, jnp.ones((nh, 8, s), vt.dtype)], axis=1)
    dv = vt.shape[1]
    t = ATT_TILE
    group = ATT_KEY_TILES_PER_STEP
    nq = s // t
    n_near = bias_tiles.shape[1] - 1
    qi_list = [i for i in range(nq) for _ in range(i // group + 1)]
    kg_list = [j for i in range(nq) for j in range(i // group + 1)]
    qi_arr = jnp.asarray(qi_list, I32)
    kg_arr = jnp.asarray(kg_list, I32)
    in_specs = [pl.BlockSpec((nh, t, dk), lambda st, qi, kg: (0, qi[st], 0)),
                pl.BlockSpec((nh, group * t, dk), lambda st, qi, kg: (0, kg[st], 0)),
                pl.BlockSpec((nh, dv, group * t), lambda st, qi, kg: (0, 0, kg[st])),
                pl.BlockSpec((nh, n_near + 1, t, t), lambda st, qi, kg: (head_group, 0, 0, 0),
                             pipeline_mode=pl.Buffered(1))]
    args = [q, k, vt, bias_tiles]
    if mask_bias is not None:
        in_specs.append(pl.BlockSpec((1, group, t, t), lambda st, qi, kg: (qi[st], kg[st], 0, 0)))
        args.append(mask_bias)
    grid_spec = pltpu.PrefetchScalarGridSpec(
        num_scalar_prefetch=2,
        grid=(len(qi_list),),
        in_specs=in_specs,
        out_specs=pl.BlockSpec((nh, HEAD_DIM, t), lambda st, qi, kg: (0, 0, qi[st])),
        scratch_shapes=[pltpu.VMEM((nh, 1, t), F32), pltpu.VMEM((nh, dv, t), F32)])
    return pl.pallas_call(
        functools.partial(_attn_kernel, nh=nh, n_near=n_near, group=group, has_mask=mask_bias is not None),
        grid_spec=grid_spec,
        out_shape=jax.ShapeDtypeStruct((nh, HEAD_DIM, s), F32),
        compiler_params=_cparams(("arbitrary",)),
    )(qi_arr, kg_arr, *args)


def _bitonic_stages(n, full_sort):
    levels = [2 ** p for p in range(1, n.bit_length())] if full_sort else [n]
    return [(k, k >> s) for k in levels for s in range(1, k.bit_length())]


def _plan_passes(stages, block):
    passes = []
    for k, j in stages:
        cross = j >= block
        if passes and passes[-1][2] == cross and (not cross or bin(passes[-1][0] | j).count("1") <= 3):
            passes[-1][0] |= j if cross else 0
            passes[-1][1].append((k, j))
        else:
            passes.append([j if cross else block - 1, [(k, j)], cross])
    return [(mask, sub) for mask, sub, _ in passes]


def _run_pass(load, store, n, mask, stages, regs=32):
    groups = {}
    for e in range(n):
        groups.setdefault(e & ~mask, []).append(e)
    groups = list(groups.values())
    per = max(1, regs // len(groups[0]))
    for g0 in range(0, len(groups), per):
        elems = sorted(e for grp in groups[g0:g0 + per] for e in grp)
        vals = {e: load(e) for e in elems}
        for k, j in stages:
            for e in elems:
                l = e ^ j
                if l > e:
                    hi = jnp.maximum(vals[e], vals[l])
                    lo = jnp.minimum(vals[e], vals[l])
                    vals[e], vals[l] = (hi, lo) if (e & k) == 0 else (lo, hi)
        for e in elems:
            store(e, vals[e])


def _sortable(x):
    return jnp.where(x == 0.0, 0.0, x)


def _dsa_sel_kernel(qi_ref, wt_ref, kn_ref, tri_ref, o_ref, keys_scr, seen_scr, work_scr, top_scr, *, tile, topk):
    n_chunks = kn_ref.shape[0]
    i = pl.program_id(0)
    n_live = i + 1
    groups_per_chunk = tile // 8
    causal = lax.broadcasted_iota(I32, (tile, tile), 0) <= lax.broadcasted_iota(I32, (tile, tile), 1)
    q_heads = [qi_ref[:, h * IDX_DIM:(h + 1) * IDX_DIM] for h in range(IDX_HEADS)]
    w_heads = [wt_ref[h:h + 1, :] for h in range(IDX_HEADS)]

    def score_chunk(c, diagonal):
        kc = kn_ref[c]
        dots = [lax.dot_general(kc, q_heads[h], _NT, preferred_element_type=F32) for h in range(IDX_HEADS)]
        sc = None
        for h in range(IDX_HEADS):
            term = w_heads[h] * jnp.maximum(dots[h], 0.0)
            sc = term if sc is None else sc + term
        if diagonal:
            sc = jnp.where(causal, sc, -jnp.inf)
        keys_scr[c] = _sortable(sc)

    def for_past_chunks(fn):
        def pair(p, carry):
            fn(2 * p)
            fn(2 * p + 1)
            return carry
        lax.fori_loop(0, i // 2, pair, 0)
        pl.when(i % 2 == 1)(lambda: fn(i - 1))

    for_past_chunks(lambda c: score_chunk(c, False))
    score_chunk(i, True)

    n_list = DSA_SORT_CHUNKS * groups_per_chunk
    n_batches = (n_live + DSA_SORT_CHUNKS - 1) // DSA_SORT_CHUNKS

    def pad_chunk(c, carry):
        keys_scr[c] = jnp.full((tile, tile), -jnp.inf, F32)
        return carry

    lax.fori_loop(n_live, n_batches * DSA_SORT_CHUNKS, pad_chunk, 0)
    top_scr[...] = jnp.full(top_scr.shape, -jnp.inf, F32)
    sort_passes = _plan_passes(_bitonic_stages(n_list, True), groups_per_chunk)
    merge_passes = _plan_passes(_bitonic_stages(n_list, False), groups_per_chunk)
    lane_halves = [slice(lo, lo + LANES) for lo in range(0, tile, LANES)]

    def run_network(passes, first_load, first_store, buf, lanes):
        def load(e):
            return buf[e, :, lanes]

        def store(e, v):
            buf[e, :, lanes] = v

        for n_pass, (mask, stages) in enumerate(passes):
            _run_pass(first_load if n_pass == 0 else load, first_store if n_pass == 0 else store,
                      n_list, mask, stages)

    def sort_batch(b, carry):
        for lanes in lane_halves:
            def load_keys(e, lanes=lanes):
                g = e % groups_per_chunk
                return keys_scr[b * DSA_SORT_CHUNKS + e // groups_per_chunk, 8 * g:8 * g + 8, lanes]

            def store_work(e, v, lanes=lanes):
                work_scr[e, :, lanes] = v

            run_network(sort_passes, load_keys, store_work, work_scr, lanes)

            def load_best(e, lanes=lanes):
                return jnp.maximum(top_scr[e, :, lanes], work_scr[n_list - 1 - e, :, lanes])

            def store_top(e, v, lanes=lanes):
                top_scr[e, :, lanes] = v

            run_network(merge_passes, load_best, store_top, top_scr, lanes)
        return carry

    lax.fori_loop(0, n_batches, sort_batch, 0)

    src, dst = top_scr, work_scr
    for shift in (4, 2, 1):
        for lanes in lane_halves:
            def load_pair(e, lanes=lanes, src=src, shift=shift):
                return jnp.maximum(src[e, :, lanes], pltpu.roll(src[n_list - 1 - e, :, lanes], shift, 0))

            def store_dst(e, v, lanes=lanes, dst=dst):
                dst[e, :, lanes] = v

            run_network(merge_passes, load_pair, store_dst, dst, lanes)
        src, dst = dst, src
    thr = src[topk - 1, 0:1, :]
    hits = [(src[e] > thr).astype(I32) for e in range(topk - 1)]
    while len(hits) > 1:
        hits = [hits[j] + hits[j + 1] if j + 1 < len(hits) else hits[j] for j in range(0, len(hits), 2)]
    n_gt = hits[0][0:1, :]
    need = (topk - n_gt).astype(F32)

    seen_scr[...] = jnp.zeros(seen_scr.shape, F32)

    def emit_chunk(c, diagonal):
        kc = keys_scr[c]
        eq = kc == thr
        eqf = eq.astype(F32)
        seen = seen_scr[...]
        seen_scr[...] = seen + jnp.sum(eqf, axis=0, keepdims=True)
        rank = seen + jnp.dot(tri_ref[...], eqf.astype(BF16), preferred_element_type=F32)
        sel = (kc > thr) | (eq & (rank < need))
        if diagonal:
            sel = sel & causal
        o_ref[0, c] = jnp.where(sel, 0.0, NEG).astype(BF16)

    for_past_chunks(lambda c: emit_chunk(c, False))
    emit_chunk(i, True)

    def fill_chunk(c, carry):
        o_ref[0, c] = jnp.full((tile, tile), NEG, BF16)
        return carry

    lax.fori_loop(n_live, n_chunks, fill_chunk, 0)


def _dsa_select(q_idx_bf16, w_idx, kn_bf16, topk):
    s = q_idx_bf16.shape[0]
    t = ATT_TILE
    n_chunks = s // t
    n_list = DSA_SORT_CHUNKS * (t // 8)
    assert topk <= n_list, "the sorting network keeps n_list best keys per query"
    kn3 = kn_bf16.reshape(n_chunks, t, IDX_DIM)
    wt = jnp.pad(w_idx.T, ((0, 8 - IDX_HEADS), (0, 0)))
    tri = (lax.broadcasted_iota(I32, (t, t), 1) < lax.broadcasted_iota(I32, (t, t), 0)).astype(BF16)
    return pl.pallas_call(
        functools.partial(_dsa_sel_kernel, tile=t, topk=topk),
        grid=(n_chunks,),
        in_specs=[pl.BlockSpec((t, IDX_HEADS * IDX_DIM), lambda i: (i, 0)),
                  pl.BlockSpec((8, t), lambda i: (0, i)),
                  pl.BlockSpec((n_chunks, t, IDX_DIM), lambda i: (0, 0, 0)),
                  pl.BlockSpec((t, t), lambda i: (0, 0))],
        out_specs=pl.BlockSpec((1, n_chunks, t, t), lambda i: (i, 0, 0, 0)),
        out_shape=jax.ShapeDtypeStruct((n_chunks, n_chunks, t, t), BF16),
        scratch_shapes=[pltpu.VMEM((max(n_chunks, DSA_SORT_CHUNKS), t, t), F32), pltpu.VMEM((1, t), F32),
                        pltpu.VMEM((n_list, 8, t), F32), pltpu.VMEM((n_list, 8, t), F32)],
        compiler_params=_cparams(("arbitrary",)),
    )(q_idx_bf16, wt, kn3, tri)


def _out_proj_kernel(y_ref, w_ref, x_ref, g_ref, o_ref):
    o_ref[...] = x_ref[...] + g_ref[...] * jnp.dot(y_ref[...], w_ref[...], preferred_element_type=F32)


def _out_proj(y_bf16, w_bf16, x, gate, tm=512):
    s, d = x.shape
    k = y_bf16.shape[1]
    return pl.pallas_call(
        _out_proj_kernel,
        grid=(s // tm,),
        in_specs=[pl.BlockSpec((tm, k), lambda i: (i, 0)),
                  pl.BlockSpec((k, d), lambda i: (0, 0)),
                  pl.BlockSpec((tm, d), lambda i: (i, 0)),
                  pl.BlockSpec((1, d), lambda i: (0, 0))],
        out_specs=pl.BlockSpec((tm, d), lambda i: (i, 0)),
        out_shape=jax.ShapeDtypeStruct((s, d), F32),
        compiler_params=_cparams(("arbitrary",)),
    )(y_bf16, w_bf16, x, gate.reshape(1, d))


def _cmp_exchange(v, i, j):
    hi = jnp.maximum(v[i], v[j])
    lo = jnp.minimum(v[i], v[j])
    v[i], v[j] = hi, lo


def _bitonic_sort_desc(v):
    n = len(v)
    k = 2
    while k <= n:
        j = k // 2
        while j >= 1:
            for i in range(n):
                l = i ^ j
                if l > i:
                    if (i & k) == 0:
                        _cmp_exchange(v, i, l)
                    else:
                        _cmp_exchange(v, l, i)
            j //= 2
        k *= 2


def _bitonic_merge_desc(v):
    n = len(v)
    j = n // 2
    while j >= 1:
        for i in range(n):
            l = i ^ j
            if l > i:
                _cmp_exchange(v, i, l)
        j //= 2


def _top16_rows(x):
    tt = x.shape[1]
    v = [x[8 * i:8 * i + 8, :] for i in range(PEER_NKEYS // 8)]
    _bitonic_sort_desc(v)
    for shift in (4, 2, 1):
        other = [pltpu.roll(a, shift, 0) for a in v]
        v = [jnp.maximum(v[i], other[len(v) - 1 - i]) for i in range(len(v))]
        _bitonic_merge_desc(v)
    return v


def _peer_route_kernel(q_ref, sk_ref, r1_ref, e1_ref, e0_ref, ns_ref, s0_scr, s1_scr, *, tt):
    sub = lax.broadcasted_iota(I32, (8, tt), 0)
    tops = [[], []]
    for h in range(PEER_HEADS):
        for p in range(2):
            lo = (2 * h + p) * (PEER_DKEY // 2)
            qhp = q_ref[:, lo:lo + PEER_DKEY // 2].astype(BF16)
            sc = lax.dot_general(sk_ref[2 * h + p], qhp, _NT, preferred_element_type=F32)
            (s0_scr if p == 0 else s1_scr)[h] = sc
            tops[p].append(_top16_rows(sc))
    a = [sum(jnp.where(sub == h, tops[0][h][k], 0.0) for h in range(PEER_HEADS)) for k in range(PEER_TOPK)]
    b = [sum(jnp.where(sub == h, tops[1][h][k], 0.0) for h in range(PEER_HEADS)) for k in range(PEER_TOPK)]
    cand = [a[i] + b[j] for i in range(PEER_TOPK) for j in range(PEER_TOPK) if (i + 1) * (j + 1) <= PEER_TOPK]
    cand = cand + [jnp.full((8, tt), -jnp.inf, F32)] * (64 - len(cand))
    _bitonic_sort_desc(cand)
    tau = cand[PEER_TOPK - 1]
    z = sum(jnp.exp(cand[k] - cand[0]) for k in range(PEER_TOPK))
    half_inv_z = 0.5 / z
    nsel_of_rank = [sum(jnp.where(a[r] + b[k] >= tau, 1.0, 0.0) for k in range(PEER_TOPK)) for r in range(PEER_TOPK)]
    for h in range(PEER_HEADS):
        s0 = s0_scr[h]
        s1 = s1_scr[h]
        nsel = jnp.zeros((PEER_NKEYS, tt), F32)
        rank1 = jnp.full((PEER_NKEYS, tt), float(PEER_TOPK), F32)
        for r in reversed(range(PEER_TOPK)):
            nsel = jnp.where(s0 == a[r][h:h + 1, :], nsel_of_rank[r][h:h + 1, :], nsel)
            rank1 = jnp.where(s1 == b[r][h:h + 1, :], float(r), rank1)
        ns_ref[h] = nsel
        r1_ref[h] = rank1.astype(BF16)
        e0_ref[h] = jnp.exp(s0 - a[0][h:h + 1, :]) * half_inv_z[h:h + 1, :]
        e1_ref[h] = jnp.exp(s1 - b[0][h:h + 1, :]).astype(BF16)


def _peer_route(q, subkeys_bf16, tt=256):
    s = q.shape[0]
    big = lambda dt: jax.ShapeDtypeStruct((PEER_HEADS, PEER_NKEYS, s), dt)
    big_spec = pl.BlockSpec((PEER_HEADS, PEER_NKEYS, tt), lambda i: (0, 0, i))
    return pl.pallas_call(
        functools.partial(_peer_route_kernel, tt=tt),
        grid=(s // tt,),
        in_specs=[pl.BlockSpec((tt, PEER_HEADS * PEER_DKEY), lambda i: (i, 0)),
                  pl.BlockSpec((2 * PEER_HEADS, PEER_NKEYS, PEER_DKEY // 2), lambda i: (0, 0, 0))],
        out_specs=[big_spec, big_spec, big_spec, big_spec],
        out_shape=[big(BF16), big(BF16), big(F32), big(F32)],
        scratch_shapes=[pltpu.VMEM((PEER_HEADS, PEER_NKEYS, tt), F32),
                        pltpu.VMEM((PEER_HEADS, PEER_NKEYS, tt), F32)],
        compiler_params=_cparams(("arbitrary",)),
    )(q, subkeys_bf16)


def _peer_dense_kernel(ht_ref, u_ref, vt_ref, r1_ref, e1_ref, e0_ref, ns_ref,
                       x_ref, g_ref, gf_ref, o_ref, acc_scr, p_scr, *, rows_per_step, final_norm):
    k = pl.program_id(1)

    @pl.when(k == 0)
    def _():
        acc_scr[...] = jnp.zeros(acc_scr.shape, F32)

    tt = ht_ref.shape[1]
    pack_rows = 16
    zero = jnp.zeros((pack_rows, tt), BF16)

    def activations(c):
        return jnp.dot(u_ref[c * MXU_DEPTH:(c + 1) * MXU_DEPTH, :], ht_ref[...], preferred_element_type=F32)

    def expert_rows(ii, a):
        i = k * rows_per_step + ii
        def bcast(ref, h):
            half = jnp.broadcast_to(ref[h, pl.ds(i, 1), :], (pack_rows // 2, tt))
            return jnp.concatenate([half, half], axis=0).astype(BF16)
        e0_rows = [bcast(e0_ref, h) for h in range(PEER_HEADS)]
        ns_rows = [bcast(ns_ref, h) for h in range(PEER_HEADS)]
        gelu2 = (a * (1.0 + lax.erf(a * math.sqrt(0.5)))).astype(BF16)
        for g in range(PEER_NKEYS // pack_rows):
            rows = slice(g * pack_rows, (g + 1) * pack_rows)
            wgt = None
            for h in range(PEER_HEADS):
                term = jnp.where(r1_ref[h, rows, :] < ns_rows[h], e1_ref[h, rows, :], zero) * e0_rows[h]
                wgt = term if wgt is None else wgt + term
            p_scr[ii * PEER_NKEYS + g * pack_rows:ii * PEER_NKEYS + (g + 1) * pack_rows, :] = wgt * gelu2[rows, :]

    depth_rows = MXU_DEPTH // PEER_NKEYS
    n_chunks = rows_per_step // depth_rows
    act = activations(0)
    for c in range(n_chunks):
        act_next = activations(c + 1) if c + 1 < n_chunks else None
        for j in range(depth_rows):
            expert_rows(c * depth_rows + j, act[j * PEER_NKEYS:(j + 1) * PEER_NKEYS, :])
        act = act_next
    acc_scr[...] += jnp.dot(vt_ref[...], p_scr[...], preferred_element_type=F32)

    @pl.when(k == pl.num_programs(1) - 1)
    def _():
        y = x_ref[...] + g_ref[...] * acc_scr[...].T
        if final_norm:
            y = y * lax.rsqrt(jnp.mean(y * y, axis=-1, keepdims=True) + EPS) * gf_ref[...]
        o_ref[...] = y


def _peer_dense(ht_bf16, u_bf16, vt_bf16, rank1, e1, e0, nsel, x, gate, g_final, final_norm,
                tt=512, rows_per_step=16):
    s, d = x.shape
    n_exp = u_bf16.shape[0]
    et = rows_per_step * PEER_NKEYS
    tok = lambda j, k: (0, 0, j)
    big_spec = pl.BlockSpec((PEER_HEADS, PEER_NKEYS, tt), tok)
    return pl.pallas_call(
        functools.partial(_peer_dense_kernel, rows_per_step=rows_per_step, final_norm=final_norm),
        grid=(s // tt, n_exp // et),
        in_specs=[pl.BlockSpec((d, tt), lambda j, k: (0, j)),
                  pl.BlockSpec((et, d), lambda j, k: (k, 0)),
                  pl.BlockSpec((d, et), lambda j, k: (0, k)),
                  big_spec, big_spec, big_spec, big_spec,
                  pl.BlockSpec((tt, d), lambda j, k: (j, 0)),
                  pl.BlockSpec((1, d), lambda j, k: (0, 0)),
                  pl.BlockSpec((1, d), lambda j, k: (0, 0))],
        out_specs=pl.BlockSpec((tt, d), lambda j, k: (j, 0)),
        out_shape=jax.ShapeDtypeStruct((s, d), F32),
        scratch_shapes=[pltpu.VMEM((d, tt), F32), pltpu.VMEM((et, tt), BF16)],
        compiler_params=_cparams(("arbitrary", "arbitrary")),
    )(ht_bf16, u_bf16, vt_bf16, rank1, e1, e0, nsel, x, gate.reshape(1, d), g_final.reshape(1, d))


def _heads_first(t, nh):
    s = t.shape[0]
    return jnp.transpose(t.reshape(s, nh, HEAD_DIM), (1, 0, 2))


def _mixer(h_proj, w_out_bf16, bias_tiles, x, gate):
    s = h_proj.shape[0]
    parts = []
    off = 0
    for size in PROJ_SIZES:
        parts.append(h_proj[:, off:off + size])
        off += size
    mq, mk, mv, dq, dk, dv, iq, ik, iw = parts
    scale = HEAD_DIM ** -0.5 * LOG2E
    dims_first = lambda a, nh: a.T.reshape(nh, HEAD_DIM, s).astype(BF16)

    mq_h, mk_h = _heads_first(mq, MOBA_HEADS), _heads_first(mk, MOBA_HEADS)
    block_sel = _moba_select(mq_h, _moba_kmean(mk_h))
    blk = jnp.arange(s, dtype=I32) // MOBA_BLOCK
    onehot = (blk[:, None] == jnp.arange(MOBA_MAX_BLOCKS, dtype=I32)[None, :]).astype(BF16)
    q_aug = jnp.concatenate([(mq_h * scale).astype(BF16), jnp.transpose(block_sel, (0, 2, 1)).astype(BF16)], axis=-1)
    k_aug = jnp.concatenate([mk_h.astype(BF16), jnp.broadcast_to(onehot, (MOBA_HEADS, s, MOBA_MAX_BLOCKS))], axis=-1)
    y_moba = _attention(q_aug, k_aug, dims_first(mv, MOBA_HEADS), bias_tiles, 0)

    ikf = ik.astype(F32)
    ik_n = ikf * lax.rsqrt(jnp.mean(ikf * ikf, axis=-1, keepdims=True) + EPS)
    topk = min(DSA_TOPK_MAX, s // 4)
    mask_bias = _dsa_select(iq.astype(BF16), iw * IDX_HEADS ** -0.5, ik_n.astype(BF16), topk)
    dq_h, dk_h = _heads_first(dq, DSA_HEADS), _heads_first(dk, DSA_HEADS)
    y_dsa = _attention((dq_h * scale).astype(BF16), dk_h.astype(BF16), dims_first(dv, DSA_HEADS),
                       bias_tiles, 1, mask_bias=mask_bias)

    y = jnp.concatenate([y_moba.reshape(MOBA_WIDTH, s), y_dsa.reshape(DSA_WIDTH, s)], axis=0).T
    return _out_proj(y.astype(BF16), w_out_bf16, x, gate)


def kernel(x, c, w_ada, b_ada, norm_attn, norm_ffn, w_in, w_out, rel_bias,
           peer_wq, peer_subkeys, peer_u, peer_v, norm_final):
    batch, s, d = x.shape
    assert batch == 1 and s % (ATT_KEY_TILES_PER_STEP * ATT_TILE) == 0
    depth = w_ada.shape[0]
    x = x[0]
    mod = _ada_mod(c, w_ada, b_ada)
    bias_tiles = _bias_tiles(rel_bias, ATT_TILE)
    n_pad = -(-PROJ_WIDTH // LANES) * LANES
    for l in range(depth):
        sh1, sc1, g1, sh2, sc2, g2 = jnp.split(mod[l], 6)
        w_in_l = jnp.pad(w_in[l], ((0, 0), (0, n_pad - PROJ_WIDTH))).astype(BF16)
        _, proj = _norm_mod_matmul(x, norm_attn[l], sh1, sc1, w_in_l)
        x = _mixer(proj, w_out[l].astype(BF16), bias_tiles, x, g1)
        h2, q = _norm_mod_matmul(x, norm_ffn[l], sh2, sc2, peer_wq[l].astype(BF16))
        subkeys = peer_subkeys[l].reshape(2 * PEER_HEADS, PEER_NKEYS, PEER_DKEY // 2).astype(BF16)
        rank1, e1, e0, nsel = _peer_route(q, subkeys)
        x = _peer_dense(h2.T, peer_u[l].astype(BF16), peer_v[l].T.astype(BF16), rank1, e1, e0, nsel,
                        x, g2, norm_final, final_norm=(l == depth - 1))
    return x[None]
```

```python
import functools
import math

import jax
import jax.numpy as jnp
from jax import lax
from jax.experimental import pallas as pl
from jax.experimental.pallas import tpu as pltpu

F32 = jnp.float32
BF16 = jnp.bfloat16
I32 = jnp.int32

HEAD_DIM = 64
MOBA_HEADS = 8
DSA_HEADS = 8
N_HEADS = MOBA_HEADS + DSA_HEADS
MOBA_WIDTH = MOBA_HEADS * HEAD_DIM
DSA_WIDTH = DSA_HEADS * HEAD_DIM
MOBA_BLOCK = 256
MOBA_TOPK = 3
MOBA_MAX_BLOCKS = 64
IDX_HEADS = 4
IDX_DIM = 64
DSA_TOPK_MAX = 256
DSA_SORT_CHUNKS = 8
N_BUCKETS = 32
MAX_DISTANCE = 4096
PEER_HEADS = 8
PEER_NKEYS = 128
PEER_DKEY = 256
PEER_TOPK = 16
EPS = 1e-6
PROJ_SIZES = (MOBA_WIDTH, MOBA_WIDTH, MOBA_WIDTH, DSA_WIDTH, DSA_WIDTH, DSA_WIDTH,
              IDX_HEADS * IDX_DIM, IDX_DIM, IDX_HEADS)
PROJ_WIDTH = sum(PROJ_SIZES)

NEG = -1e30
LOG2E = math.log2(math.e)
ATT_TILE = MOBA_BLOCK
ATT_KEY_TILES_PER_STEP = 4
LANES = 128
SUBLANES = 8
BF16_PACK_ROWS = 2 * SUBLANES
MXU_DEPTH = 256
VMEM_LIMIT = 56 * 1024 * 1024
NETWORK_REGS = 32

ADA_COLS = 1536
PROJ_ROWS = 512
MOBA_SEL_QUERIES = 2048
PEER_ROUTE_TOKENS = 256
PEER_DENSE_TOKENS = 512
PEER_DENSE_KEY_ROWS = 16

_NT = (((1,), (1,)), ((), ()))


def _far_bias_tiles(tile):
    max_exact = N_BUCKETS // 2
    sat = int(math.ceil(max_exact * (MAX_DISTANCE / max_exact) ** ((N_BUCKETS - max_exact - 1) / (N_BUCKETS - max_exact)))) + 4
    return -(-(sat + tile - 1) // tile)


def _cparams(sem):
    return pltpu.CompilerParams(dimension_semantics=sem, vmem_limit_bytes=VMEM_LIMIT)


def _ada_kernel(c_ref, w_ref, b_ref, o_ref):
    c = c_ref[...]
    sc = c * jax.nn.sigmoid(c)
    o_ref[0] = jnp.dot(sc, w_ref[0], preferred_element_type=F32,
                       precision=lax.Precision.HIGHEST) + b_ref[0]


def _ada_mod(c, w_ada, b_ada):
    depth, d, n = w_ada.shape
    tn = ADA_COLS
    c8 = jnp.broadcast_to(c, (SUBLANES, d))
    out = pl.pallas_call(
        _ada_kernel,
        grid=(depth, n // tn),
        in_specs=[pl.BlockSpec((SUBLANES, d), lambda l, j: (0, 0)),
                  pl.BlockSpec((1, d, tn), lambda l, j: (l, 0, j)),
                  pl.BlockSpec((1, 1, tn), lambda l, j: (l, 0, j))],
        out_specs=pl.BlockSpec((1, SUBLANES, tn), lambda l, j: (l, 0, j)),
        out_shape=jax.ShapeDtypeStruct((depth, SUBLANES, n), F32),
        compiler_params=_cparams(("arbitrary", "arbitrary")),
    )(c8, w_ada, b_ada.reshape(depth, 1, n))
    return out[:, 0, :]


def _nmm_kernel(x_ref, g_ref, sh_ref, sc_ref, w_ref, h_ref, o_ref):
    x = x_ref[...]
    y = x * lax.rsqrt(jnp.mean(x * x, axis=-1, keepdims=True) + EPS) * g_ref[...]
    hb = (y * (1.0 + sc_ref[...]) + sh_ref[...]).astype(BF16)
    h_ref[...] = hb
    o_ref[...] = jnp.dot(hb, w_ref[...], preferred_element_type=F32)


def _norm_mod_matmul(x, g, shift, scale, w_bf16, tm=PROJ_ROWS):
    s, d = x.shape
    n = w_bf16.shape[1]
    row = lambda i: (0, 0)
    return pl.pallas_call(
        _nmm_kernel,
        grid=(s // tm,),
        in_specs=[pl.BlockSpec((tm, d), lambda i: (i, 0)),
                  pl.BlockSpec((1, d), row), pl.BlockSpec((1, d), row), pl.BlockSpec((1, d), row),
                  pl.BlockSpec((d, n), row)],
        out_specs=[pl.BlockSpec((tm, d), lambda i: (i, 0)),
                   pl.BlockSpec((tm, n), lambda i: (i, 0))],
        out_shape=[jax.ShapeDtypeStruct((s, d), BF16), jax.ShapeDtypeStruct((s, n), F32)],
        compiler_params=_cparams(("arbitrary",)),
    )(x, g.reshape(1, d), shift.reshape(1, d), scale.reshape(1, d), w_bf16)


def _t5_bucket(d):
    max_exact = N_BUCKETS // 2
    d = jnp.maximum(d, 0)
    df = jnp.maximum(d, 1).astype(F32)
    large = max_exact + (jnp.log(df / max_exact) / math.log(MAX_DISTANCE / max_exact)
                         * (N_BUCKETS - max_exact)).astype(I32)
    return jnp.where(d < max_exact, d, jnp.minimum(large, N_BUCKETS - 1))


def _bias_tile_kernel(tab_ref, o_ref, *, tile):
    h = pl.program_id(0)
    off = pl.program_id(1)
    b = lax.broadcasted_iota(I32, (tile, tile), 0)
    a = lax.broadcasted_iota(I32, (tile, tile), 1)
    d = off * tile + a - b
    bucket = _t5_bucket(d)
    val = lax.fori_loop(jnp.min(bucket), jnp.max(bucket) + 1,
                        lambda k, val: jnp.where(bucket == k, tab_ref[h, k], val),
                        jnp.zeros((tile, tile), F32))
    val = (val - tab_ref[h, N_BUCKETS - 1]) * LOG2E
    o_ref[0, 0] = jnp.where(d < 0, NEG, val)


def _bias_tiles(rel_bias, tile):
    n_near = _far_bias_tiles(tile)
    nh = rel_bias.shape[0]
    return pl.pallas_call(
        functools.partial(_bias_tile_kernel, tile=tile),
        grid=(nh, n_near + 1),
        in_specs=[pl.BlockSpec(memory_space=pltpu.SMEM)],
        out_specs=pl.BlockSpec((1, 1, tile, tile), lambda h, o: (h, o, 0, 0)),
        out_shape=jax.ShapeDtypeStruct((nh, n_near + 1, tile, tile), F32),
        compiler_params=_cparams(("arbitrary", "arbitrary")),
    )(rel_bias)


def _kmean_kernel(k_ref, o_ref, *, nb):
    k = k_ref[0]
    km = jnp.mean(k.reshape(nb, MOBA_BLOCK, HEAD_DIM), axis=1)
    o_ref[0] = jnp.zeros(o_ref.shape[1:], F32)
    o_ref[0, :nb, :] = km


def _moba_kmean(k_hsd):
    nh, s, dh = k_hsd.shape
    nb = s // MOBA_BLOCK
    return pl.pallas_call(
        functools.partial(_kmean_kernel, nb=nb),
        grid=(nh,),
        in_specs=[pl.BlockSpec((1, s, dh), lambda h: (h, 0, 0))],
        out_specs=pl.BlockSpec((1, MOBA_MAX_BLOCKS, dh), lambda h: (h, 0, 0)),
        out_shape=jax.ShapeDtypeStruct((nh, MOBA_MAX_BLOCKS, dh), F32),
        compiler_params=_cparams(("arbitrary",)),
    )(k_hsd)


def _moba_sel_kernel(q_ref, km_ref, o_ref, *, tq, n_sel):
    gate = lax.dot_general(km_ref[0], q_ref[0], _NT, preferred_element_type=F32,
                           precision=lax.Precision.HIGHEST)
    t = pl.program_id(1) * tq + lax.broadcasted_iota(I32, (MOBA_MAX_BLOCKS, tq), 1)
    b0 = t // MOBA_BLOCK
    n = lax.broadcasted_iota(I32, (MOBA_MAX_BLOCKS, tq), 0)
    avail = n < b0
    taken = n == b0
    for _ in range(n_sel):
        g = jnp.where(avail, gate, -jnp.inf)
        m = jnp.max(g, axis=0, keepdims=True)
        idx = jnp.min(jnp.where((g == m) & avail, n, 1 << 20), axis=0, keepdims=True)
        pick = n == idx
        taken = taken | pick
        avail = avail & jnp.logical_not(pick)
    o_ref[0] = jnp.where(taken, 0.0, NEG)


def _moba_select(q_hsd, kmean, tq=MOBA_SEL_QUERIES):
    nh, s, dh = q_hsd.shape
    nb = s // MOBA_BLOCK
    assert nb <= MOBA_MAX_BLOCKS
    n_sel = max(min(MOBA_TOPK, nb - 1), 1)
    tq = min(tq, s)
    return pl.pallas_call(
        functools.partial(_moba_sel_kernel, tq=tq, n_sel=n_sel),
        grid=(nh, s // tq),
        in_specs=[pl.BlockSpec((1, tq, dh), lambda h, i: (h, i, 0)),
                  pl.BlockSpec((1, MOBA_MAX_BLOCKS, dh), lambda h, i: (h, 0, 0))],
        out_specs=pl.BlockSpec((1, MOBA_MAX_BLOCKS, tq), lambda h, i: (h, 0, i)),
        out_shape=jax.ShapeDtypeStruct((nh, MOBA_MAX_BLOCKS, s), F32),
        compiler_params=_cparams(("arbitrary", "arbitrary")),
    )(q_hsd, kmean)


def _attn_kernel(qi_ref, kg_ref, *refs, nh, n_near, group, has_mask):
    q_ref, k_ref, vt_ref, bt_ref = refs[:4]
    mask_ref = refs[4] if has_mask else None
    o_ref, m_scr, acc_scr = refs[-3:]
    t = ATT_TILE
    step = pl.program_id(0)
    qi = qi_ref[step]
    kg = kg_ref[step]

    @pl.when(kg == 0)
    def _():
        m_scr[...] = jnp.full(m_scr.shape, NEG, F32)
        acc_scr[...] = jnp.zeros(acc_scr.shape, F32)

    def step_update(with_bias, n_tiles):
        m = [m_scr[h] for h in range(nh)]
        for g in range(n_tiles):
            keys = slice(g * t, (g + 1) * t)
            off = qi - (kg * group + g)
            bias_row = jnp.minimum(off, n_near)
            if has_mask:
                mb = mask_ref[0, g].astype(F32)
            scores = []
            for h in range(nh):
                s = lax.dot_general(k_ref[h, keys, :], q_ref[h], _NT, preferred_element_type=F32)
                if with_bias:
                    s = s + bt_ref[h, bias_row]
                if has_mask:
                    s = s + mb
                scores.append(s)
            probs, alphas = [], []
            for h in range(nh):
                m_new = jnp.maximum(m[h], jnp.max(scores[h], axis=0, keepdims=True))
                alpha = jnp.exp2(m[h] - m_new)
                p = jnp.exp2(scores[h] - m_new)
                m[h] = m_new
                probs.append(p.astype(BF16))
                alphas.append(alpha)
            for h in range(nh):
                acc_scr[h] = alphas[h] * acc_scr[h] + jnp.dot(vt_ref[h, :, keys], probs[h],
                                                              preferred_element_type=F32)
        for h in range(nh):
            m_scr[h] = m[h]

    all_far = qi - (kg * group + group - 1) >= n_near
    pl.when(all_far)(functools.partial(step_update, False, group))
    n_causal = jnp.minimum(qi - kg * group + 1, group)
    for n_tiles in range(1, group + 1):
        pl.when(jnp.logical_not(all_far) & (n_causal == n_tiles))(functools.partial(step_update, True, n_tiles))

    @pl.when(kg == qi // group)
    def _():
        for h in range(nh):
            o_ref[h] = acc_scr[h, :HEAD_DIM, :] / acc_scr[h, HEAD_DIM:HEAD_DIM + 1, :]


def _attention(q, k, vt, bias_tiles, head_group, mask_bias=None):
    nh, s, dk = q.shape
    vt = jnp.concatenate([vt, jnp.ones((nh, SUBLANES, s), vt.dtype)], axis=1)
    dv = vt.shape[1]
    t = ATT_TILE
    group = ATT_KEY_TILES_PER_STEP
    nq = s // t
    n_near = bias_tiles.shape[1] - 1
    qi_list = [i for i in range(nq) for _ in range(i // group + 1)]
    kg_list = [j for i in range(nq) for j in range(i // group + 1)]
    qi_arr = jnp.asarray(qi_list, I32)
    kg_arr = jnp.asarray(kg_list, I32)
    in_specs = [pl.BlockSpec((nh, t, dk), lambda st, qi, kg: (0, qi[st], 0)),
                pl.BlockSpec((nh, group * t, dk), lambda st, qi, kg: (0, kg[st], 0)),
                pl.BlockSpec((nh, dv, group * t), lambda st, qi, kg: (0, 0, kg[st])),
                pl.BlockSpec((nh, n_near + 1, t, t), lambda st, qi, kg: (head_group, 0, 0, 0),
                             pipeline_mode=pl.Buffered(1))]
    args = [q, k, vt, bias_tiles]
    if mask_bias is not None:
        in_specs.append(pl.BlockSpec((1, group, t, t), lambda st, qi, kg: (qi[st], kg[st], 0, 0)))
        args.append(mask_bias)
    grid_spec = pltpu.PrefetchScalarGridSpec(
        num_scalar_prefetch=2,
        grid=(len(qi_list),),
        in_specs=in_specs,
        out_specs=pl.BlockSpec((nh, HEAD_DIM, t), lambda st, qi, kg: (0, 0, qi[st])),
        scratch_shapes=[pltpu.VMEM((nh, 1, t), F32), pltpu.VMEM((nh, dv, t), F32)])
    return pl.pallas_call(
        functools.partial(_attn_kernel, nh=nh, n_near=n_near, group=group, has_mask=mask_bias is not None),
        grid_spec=grid_spec,
        out_shape=jax.ShapeDtypeStruct((nh, HEAD_DIM, s), F32),
        compiler_params=_cparams(("arbitrary",)),
    )(qi_arr, kg_arr, *args)


def _bitonic_stages(n, full_sort):
    levels = [2 ** p for p in range(1, n.bit_length())] if full_sort else [n]
    return [(k, k >> s) for k in levels for s in range(1, k.bit_length())]


def _plan_passes(stages, block):
    passes = []
    for k, j in stages:
        cross = j >= block
        if passes and passes[-1][2] == cross and (not cross or bin(passes[-1][0] | j).count("1") <= 3):
            passes[-1][0] |= j if cross else 0
            passes[-1][1].append((k, j))
        else:
            passes.append([j if cross else block - 1, [(k, j)], cross])
    return [(mask, sub) for mask, sub, _ in passes]


def _run_pass(load, store, n, mask, stages, regs=NETWORK_REGS):
    groups = {}
    for e in range(n):
        groups.setdefault(e & ~mask, []).append(e)
    groups = list(groups.values())
    per = max(1, regs // len(groups[0]))
    for g0 in range(0, len(groups), per):
        elems = sorted(e for grp in groups[g0:g0 + per] for e in grp)
        vals = {e: load(e) for e in elems}
        for k, j in stages:
            for e in elems:
                l = e ^ j
                if l > e:
                    hi = jnp.maximum(vals[e], vals[l])
                    lo = jnp.minimum(vals[e], vals[l])
                    vals[e], vals[l] = (hi, lo) if (e & k) == 0 else (lo, hi)
        for e in elems:
            store(e, vals[e])


def _sortable(x):
    return jnp.where(x == 0.0, 0.0, x)


def _dsa_sel_kernel(qi_ref, wt_ref, kn_ref, tri_ref, o_ref, keys_scr, seen_scr, work_scr, top_scr, *, tile, topk):
    n_chunks = kn_ref.shape[0]
    i = pl.program_id(0)
    n_live = i + 1
    groups_per_chunk = tile // SUBLANES
    causal = lax.broadcasted_iota(I32, (tile, tile), 0) <= lax.broadcasted_iota(I32, (tile, tile), 1)
    q_heads = [qi_ref[:, h * IDX_DIM:(h + 1) * IDX_DIM] for h in range(IDX_HEADS)]
    w_heads = [wt_ref[h:h + 1, :] for h in range(IDX_HEADS)]

    def score_chunk(c, diagonal):
        kc = kn_ref[c]
        dots = [lax.dot_general(kc, q_heads[h], _NT, preferred_element_type=F32) for h in range(IDX_HEADS)]
        sc = None
        for h in range(IDX_HEADS):
            term = w_heads[h] * jnp.maximum(dots[h], 0.0)
            sc = term if sc is None else sc + term
        if diagonal:
            sc = jnp.where(causal, sc, -jnp.inf)
        keys_scr[c] = _sortable(sc)

    def for_past_chunks(fn):
        def pair(p, carry):
            fn(2 * p)
            fn(2 * p + 1)
            return carry
        lax.fori_loop(0, i // 2, pair, 0)
        pl.when(i % 2 == 1)(lambda: fn(i - 1))

    for_past_chunks(lambda c: score_chunk(c, False))
    score_chunk(i, True)

    n_list = DSA_SORT_CHUNKS * groups_per_chunk
    n_batches = (n_live + DSA_SORT_CHUNKS - 1) // DSA_SORT_CHUNKS

    def pad_chunk(c, carry):
        keys_scr[c] = jnp.full((tile, tile), -jnp.inf, F32)
        return carry

    lax.fori_loop(n_live, n_batches * DSA_SORT_CHUNKS, pad_chunk, 0)
    top_scr[...] = jnp.full(top_scr.shape, -jnp.inf, F32)
    sort_passes = _plan_passes(_bitonic_stages(n_list, True), groups_per_chunk)
    merge_passes = _plan_passes(_bitonic_stages(n_list, False), groups_per_chunk)
    lane_halves = [slice(lo, lo + LANES) for lo in range(0, tile, LANES)]

    def run_network(passes, first_load, first_store, buf, lanes):
        def load(e):
            return buf[e, :, lanes]

        def store(e, v):
            buf[e, :, lanes] = v

        for n_pass, (mask, stages) in enumerate(passes):
            _run_pass(first_load if n_pass == 0 else load, first_store if n_pass == 0 else store,
                      n_list, mask, stages)

    def sort_batch(b, carry):
        for lanes in lane_halves:
            def load_keys(e, lanes=lanes):
                g = e % groups_per_chunk
                return keys_scr[b * DSA_SORT_CHUNKS + e // groups_per_chunk, SUBLANES * g:SUBLANES * (g + 1), lanes]

            def store_work(e, v, lanes=lanes):
                work_scr[e, :, lanes] = v

            run_network(sort_passes, load_keys, store_work, work_scr, lanes)

            def load_best(e, lanes=lanes):
                return jnp.maximum(top_scr[e, :, lanes], work_scr[n_list - 1 - e, :, lanes])

            def store_top(e, v, lanes=lanes):
                top_scr[e, :, lanes] = v

            run_network(merge_passes, load_best, store_top, top_scr, lanes)
        return carry

    lax.fori_loop(0, n_batches, sort_batch, 0)

    src, dst = top_scr, work_scr
    for shift in (4, 2, 1):
        for lanes in lane_halves:
            def load_pair(e, lanes=lanes, src=src, shift=shift):
                return jnp.maximum(src[e, :, lanes], pltpu.roll(src[n_list - 1 - e, :, lanes], shift, 0))

            def store_dst(e, v, lanes=lanes, dst=dst):
                dst[e, :, lanes] = v

            run_network(merge_passes, load_pair, store_dst, dst, lanes)
        src, dst = dst, src
    thr = src[topk - 1, 0:1, :]
    hits = [(src[e] > thr).astype(I32) for e in range(topk - 1)]
    while len(hits) > 1:
        hits = [hits[j] + hits[j + 1] if j + 1 < len(hits) else hits[j] for j in range(0, len(hits), 2)]
    n_gt = hits[0][0:1, :]
    need = (topk - n_gt).astype(F32)

    seen_scr[...] = jnp.zeros(seen_scr.shape, F32)

    def emit_chunk(c, diagonal):
        kc = keys_scr[c]
        eq = kc == thr
        eqf = eq.astype(F32)
        seen = seen_scr[...]
        seen_scr[...] = seen + jnp.sum(eqf, axis=0, keepdims=True)
        rank = seen + jnp.dot(tri_ref[...], eqf.astype(BF16), preferred_element_type=F32)
        sel = (kc > thr) | (eq & (rank < need))
        if diagonal:
            sel = sel & causal
        o_ref[0, c] = jnp.where(sel, 0.0, NEG).astype(BF16)

    for_past_chunks(lambda c: emit_chunk(c, False))
    emit_chunk(i, True)

    def fill_chunk(c, carry):
        o_ref[0, c] = jnp.full((tile, tile), NEG, BF16)
        return carry

    lax.fori_loop(n_live, n_chunks, fill_chunk, 0)


def _dsa_select(q_idx_bf16, w_idx, kn_bf16, topk):
    s = q_idx_bf16.shape[0]
    t = ATT_TILE
    n_chunks = s // t
    n_list = DSA_SORT_CHUNKS * (t // SUBLANES)
    assert topk <= n_list, "the sorting network keeps n_list best keys per query"
    kn3 = kn_bf16.reshape(n_chunks, t, IDX_DIM)
    wt = jnp.pad(w_idx.T, ((0, SUBLANES - IDX_HEADS), (0, 0)))
    tri = (lax.broadcasted_iota(I32, (t, t), 1) < lax.broadcasted_iota(I32, (t, t), 0)).astype(BF16)
    return pl.pallas_call(
        functools.partial(_dsa_sel_kernel, tile=t, topk=topk),
        grid=(n_chunks,),
        in_specs=[pl.BlockSpec((t, IDX_HEADS * IDX_DIM), lambda i: (i, 0)),
                  pl.BlockSpec((SUBLANES, t), lambda i: (0, i)),
                  pl.BlockSpec((n_chunks, t, IDX_DIM), lambda i: (0, 0, 0)),
                  pl.BlockSpec((t, t), lambda i: (0, 0))],
        out_specs=pl.BlockSpec((1, n_chunks, t, t), lambda i: (i, 0, 0, 0)),
        out_shape=jax.ShapeDtypeStruct((n_chunks, n_chunks, t, t), BF16),
        scratch_shapes=[pltpu.VMEM((max(n_chunks, DSA_SORT_CHUNKS), t, t), F32), pltpu.VMEM((1, t), F32),
                        pltpu.VMEM((n_list, SUBLANES, t), F32), pltpu.VMEM((n_list, SUBLANES, t), F32)],
        compiler_params=_cparams(("arbitrary",)),
    )(q_idx_bf16, wt, kn3, tri)


def _out_proj_kernel(y_ref, w_ref, x_ref, g_ref, o_ref):
    o_ref[...] = x_ref[...] + g_ref[...] * jnp.dot(y_ref[...], w_ref[...], preferred_element_type=F32)


def _out_proj(y_bf16, w_bf16, x, gate, tm=PROJ_ROWS):
    s, d = x.shape
    k = y_bf16.shape[1]
    return pl.pallas_call(
        _out_proj_kernel,
        grid=(s // tm,),
        in_specs=[pl.BlockSpec((tm, k), lambda i: (i, 0)),
                  pl.BlockSpec((k, d), lambda i: (0, 0)),
                  pl.BlockSpec((tm, d), lambda i: (i, 0)),
                  pl.BlockSpec((1, d), lambda i: (0, 0))],
        out_specs=pl.BlockSpec((tm, d), lambda i: (i, 0)),
        out_shape=jax.ShapeDtypeStruct((s, d), F32),
        compiler_params=_cparams(("arbitrary",)),
    )(y_bf16, w_bf16, x, gate.reshape(1, d))


def _cmp_exchange(v, i, j):
    hi = jnp.maximum(v[i], v[j])
    lo = jnp.minimum(v[i], v[j])
    v[i], v[j] = hi, lo


def _bitonic_sort_desc(v):
    n = len(v)
    k = 2
    while k <= n:
        j = k // 2
        while j >= 1:
            for i in range(n):
                l = i ^ j
                if l > i:
                    if (i & k) == 0:
                        _cmp_exchange(v, i, l)
                    else:
                        _cmp_exchange(v, l, i)
            j //= 2
        k *= 2


def _bitonic_merge_desc(v):
    n = len(v)
    j = n // 2
    while j >= 1:
        for i in range(n):
            l = i ^ j
            if l > i:
                _cmp_exchange(v, i, l)
        j //= 2


def _top16_rows(x):
    tt = x.shape[1]
    v = [x[SUBLANES * i:SUBLANES * (i + 1), :] for i in range(PEER_NKEYS // SUBLANES)]
    _bitonic_sort_desc(v)
    for shift in (4, 2, 1):
        other = [pltpu.roll(a, shift, 0) for a in v]
        v = [jnp.maximum(v[i], other[len(v) - 1 - i]) for i in range(len(v))]
        _bitonic_merge_desc(v)
    return v


def _peer_route_kernel(q_ref, sk_ref, r1_ref, e1_ref, e0_ref, ns_ref, s0_scr, s1_scr, *, tt):
    assert PEER_HEADS == SUBLANES
    sub = lax.broadcasted_iota(I32, (SUBLANES, tt), 0)
    tops = [[], []]
    for h in range(PEER_HEADS):
        for p in range(2):
            lo = (2 * h + p) * (PEER_DKEY // 2)
            qhp = q_ref[:, lo:lo + PEER_DKEY // 2].astype(BF16)
            sc = lax.dot_general(sk_ref[2 * h + p], qhp, _NT, preferred_element_type=F32)
            (s0_scr if p == 0 else s1_scr)[h] = sc
            tops[p].append(_top16_rows(sc))
    a = [sum(jnp.where(sub == h, tops[0][h][k], 0.0) for h in range(PEER_HEADS)) for k in range(PEER_TOPK)]
    b = [sum(jnp.where(sub == h, tops[1][h][k], 0.0) for h in range(PEER_HEADS)) for k in range(PEER_TOPK)]
    cand = [a[i] + b[j] for i in range(PEER_TOPK) for j in range(PEER_TOPK) if (i + 1) * (j + 1) <= PEER_TOPK]
    cand = cand + [jnp.full((SUBLANES, tt), -jnp.inf, F32)] * (64 - len(cand))
    _bitonic_sort_desc(cand)
    tau = cand[PEER_TOPK - 1]
    z = sum(jnp.exp(cand[k] - cand[0]) for k in range(PEER_TOPK))
    half_inv_z = 0.5 / z
    nsel_of_rank = [sum(jnp.where(a[r] + b[k] >= tau, 1.0, 0.0) for k in range(PEER_TOPK)) for r in range(PEER_TOPK)]
    for h in range(PEER_HEADS):
        s0 = s0_scr[h]
        s1 = s1_scr[h]
        nsel = jnp.zeros((PEER_NKEYS, tt), F32)
        rank1 = jnp.full((PEER_NKEYS, tt), float(PEER_TOPK), F32)
        for r in reversed(range(PEER_TOPK)):
            nsel = jnp.where(s0 == a[r][h:h + 1, :], nsel_of_rank[r][h:h + 1, :], nsel)
            rank1 = jnp.where(s1 == b[r][h:h + 1, :], float(r), rank1)
        ns_ref[h] = nsel
        r1_ref[h] = rank1.astype(BF16)
        e0_ref[h] = jnp.exp(s0 - a[0][h:h + 1, :]) * half_inv_z[h:h + 1, :]
        e1_ref[h] = jnp.exp(s1 - b[0][h:h + 1, :]).astype(BF16)


def _peer_route(q, subkeys_bf16, tt=PEER_ROUTE_TOKENS):
    s = q.shape[0]
    big = lambda dt: jax.ShapeDtypeStruct((PEER_HEADS, PEER_NKEYS, s), dt)
    big_spec = pl.BlockSpec((PEER_HEADS, PEER_NKEYS, tt), lambda i: (0, 0, i))
    return pl.pallas_call(
        functools.partial(_peer_route_kernel, tt=tt),
        grid=(s // tt,),
        in_specs=[pl.BlockSpec((tt, PEER_HEADS * PEER_DKEY), lambda i: (i, 0)),
                  pl.BlockSpec((2 * PEER_HEADS, PEER_NKEYS, PEER_DKEY // 2), lambda i: (0, 0, 0))],
        out_specs=[big_spec, big_spec, big_spec, big_spec],
        out_shape=[big(BF16), big(BF16), big(F32), big(F32)],
        scratch_shapes=[pltpu.VMEM((PEER_HEADS, PEER_NKEYS, tt), F32),
                        pltpu.VMEM((PEER_HEADS, PEER_NKEYS, tt), F32)],
        compiler_params=_cparams(("arbitrary",)),
    )(q, subkeys_bf16)


def _peer_dense_kernel(ht_ref, u_ref, vt_ref, r1_ref, e1_ref, e0_ref, ns_ref,
                       x_ref, g_ref, gf_ref, o_ref, acc_scr, p_scr, *, rows_per_step, final_norm):
    k = pl.program_id(1)

    @pl.when(k == 0)
    def _():
        acc_scr[...] = jnp.zeros(acc_scr.shape, F32)

    tt = ht_ref.shape[1]
    pack_rows = BF16_PACK_ROWS
    zero = jnp.zeros((pack_rows, tt), BF16)

    def activations(c):
        return jnp.dot(u_ref[c * MXU_DEPTH:(c + 1) * MXU_DEPTH, :], ht_ref[...], preferred_element_type=F32)

    def expert_rows(ii, a):
        i = k * rows_per_step + ii
        def bcast(ref, h):
            half = jnp.broadcast_to(ref[h, pl.ds(i, 1), :], (pack_rows // 2, tt))
            return jnp.concatenate([half, half], axis=0).astype(BF16)
        e0_rows = [bcast(e0_ref, h) for h in range(PEER_HEADS)]
        ns_rows = [bcast(ns_ref, h) for h in range(PEER_HEADS)]
        gelu2 = (a * (1.0 + lax.erf(a * math.sqrt(0.5)))).astype(BF16)
        for g in range(PEER_NKEYS // pack_rows):
            rows = slice(g * pack_rows, (g + 1) * pack_rows)
            wgt = None
            for h in range(PEER_HEADS):
                term = jnp.where(r1_ref[h, rows, :] < ns_rows[h], e1_ref[h, rows, :], zero) * e0_rows[h]
                wgt = term if wgt is None else wgt + term
            p_scr[ii * PEER_NKEYS + g * pack_rows:ii * PEER_NKEYS + (g + 1) * pack_rows, :] = wgt * gelu2[rows, :]

    depth_rows = MXU_DEPTH // PEER_NKEYS
    n_chunks = rows_per_step // depth_rows
    act = activations(0)
    for c in range(n_chunks):
        act_next = activations(c + 1) if c + 1 < n_chunks else None
        for j in range(depth_rows):
            expert_rows(c * depth_rows + j, act[j * PEER_NKEYS:(j + 1) * PEER_NKEYS, :])
        act = act_next
    acc_scr[...] += jnp.dot(vt_ref[...], p_scr[...], preferred_element_type=F32)

    @pl.when(k == pl.num_programs(1) - 1)
    def _():
        y = x_ref[...] + g_ref[...] * acc_scr[...].T
        if final_norm:
            y = y * lax.rsqrt(jnp.mean(y * y, axis=-1, keepdims=True) + EPS) * gf_ref[...]
        o_ref[...] = y


def _peer_dense(ht_bf16, u_bf16, vt_bf16, rank1, e1, e0, nsel, x, gate, g_final, final_norm,
                tt=PEER_DENSE_TOKENS, rows_per_step=PEER_DENSE_KEY_ROWS):
    s, d = x.shape
    n_exp = u_bf16.shape[0]
    et = rows_per_step * PEER_NKEYS
    tok = lambda j, k: (0, 0, j)
    big_spec = pl.BlockSpec((PEER_HEADS, PEER_NKEYS, tt), tok)
    return pl.pallas_call(
        functools.partial(_peer_dense_kernel, rows_per_step=rows_per_step, final_norm=final_norm),
        grid=(s // tt, n_exp // et),
        in_specs=[pl.BlockSpec((d, tt), lambda j, k: (0, j)),
                  pl.BlockSpec((et, d), lambda j, k: (k, 0)),
                  pl.BlockSpec((d, et), lambda j, k: (0, k)),
                  big_spec, big_spec, big_spec, big_spec,
                  pl.BlockSpec((tt, d), lambda j, k: (j, 0)),
                  pl.BlockSpec((1, d), lambda j, k: (0, 0)),
                  pl.BlockSpec((1, d), lambda j, k: (0, 0))],
        out_specs=pl.BlockSpec((tt, d), lambda j, k: (j, 0)),
        out_shape=jax.ShapeDtypeStruct((s, d), F32),
        scratch_shapes=[pltpu.VMEM((d, tt), F32), pltpu.VMEM((et, tt), BF16)],
        compiler_params=_cparams(("arbitrary", "arbitrary")),
    )(ht_bf16, u_bf16, vt_bf16, rank1, e1, e0, nsel, x, gate.reshape(1, d), g_final.reshape(1, d))


def _heads_first(t, nh):
    s = t.shape[0]
    return jnp.transpose(t.reshape(s, nh, HEAD_DIM), (1, 0, 2))


def _mixer(h_proj, w_out_bf16, bias_tiles, x, gate):
    s = h_proj.shape[0]
    parts = []
    off = 0
    for size in PROJ_SIZES:
        parts.append(h_proj[:, off:off + size])
        off += size
    mq, mk, mv, dq, dk, dv, iq, ik, iw = parts
    scale = HEAD_DIM ** -0.5 * LOG2E
    dims_first = lambda a, nh: a.T.reshape(nh, HEAD_DIM, s).astype(BF16)

    mq_h, mk_h = _heads_first(mq, MOBA_HEADS), _heads_first(mk, MOBA_HEADS)
    block_sel = _moba_select(mq_h, _moba_kmean(mk_h))
    blk = jnp.arange(s, dtype=I32) // MOBA_BLOCK
    onehot = (blk[:, None] == jnp.arange(MOBA_MAX_BLOCKS, dtype=I32)[None, :]).astype(BF16)
    q_aug = jnp.concatenate([(mq_h * scale).astype(BF16), jnp.transpose(block_sel, (0, 2, 1)).astype(BF16)], axis=-1)
    k_aug = jnp.concatenate([mk_h.astype(BF16), jnp.broadcast_to(onehot, (MOBA_HEADS, s, MOBA_MAX_BLOCKS))], axis=-1)
    y_moba = _attention(q_aug, k_aug, dims_first(mv, MOBA_HEADS), bias_tiles, 0)

    ikf = ik.astype(F32)
    ik_n = ikf * lax.rsqrt(jnp.mean(ikf * ikf, axis=-1, keepdims=True) + EPS)
    topk = min(DSA_TOPK_MAX, s // 4)
    mask_bias = _dsa_select(iq.astype(BF16), iw * IDX_HEADS ** -0.5, ik_n.astype(BF16), topk)
    dq_h, dk_h = _heads_first(dq, DSA_HEADS), _heads_first(dk, DSA_HEADS)
    y_dsa = _attention((dq_h * scale).astype(BF16), dk_h.astype(BF16), dims_first(dv, DSA_HEADS),
                       bias_tiles, 1, mask_bias=mask_bias)

    y = jnp.concatenate([y_moba.reshape(MOBA_WIDTH, s), y_dsa.reshape(DSA_WIDTH, s)], axis=0).T
    return _out_proj(y.astype(BF16), w_out_bf16, x, gate)


def kernel(x, c, w_ada, b_ada, norm_attn, norm_ffn, w_in, w_out, rel_bias,
           peer_wq, peer_subkeys, peer_u, peer_v, norm_final):
    batch, s, d = x.shape
    assert batch == 1 and s % (ATT_KEY_TILES_PER_STEP * ATT_TILE) == 0
    depth = w_ada.shape[0]
    x = x[0]
    mod = _ada_mod(c, w_ada, b_ada)
    bias_tiles = _bias_tiles(rel_bias, ATT_TILE)
    n_pad = -(-PROJ_WIDTH // LANES) * LANES
    for l in range(depth):
        sh1, sc1, g1, sh2, sc2, g2 = jnp.split(mod[l], 6)
        w_in_l = jnp.pad(w_in[l], ((0, 0), (0, n_pad - PROJ_WIDTH))).astype(BF16)
        _, proj = _norm_mod_matmul(x, norm_attn[l], sh1, sc1, w_in_l)
        x = _mixer(proj, w_out[l].astype(BF16), bias_tiles, x, g1)
        h2, q = _norm_mod_matmul(x, norm_ffn[l], sh2, sc2, peer_wq[l].astype(BF16))
        subkeys = peer_subkeys[l].reshape(2 * PEER_HEADS, PEER_NKEYS, PEER_DKEY // 2).astype(BF16)
        rank1, e1, e0, nsel = _peer_route(q, subkeys)
        x = _peer_dense(h2.T, peer_u[l].astype(BF16), peer_v[l].T.astype(BF16), rank1, e1, e0, nsel,
                        x, g2, norm_final, final_norm=(l == depth - 1))
    return x[None]
```

```python
import functools
import math

import jax
import jax.numpy as jnp
from jax import lax
from jax.experimental import pallas as pl
from jax.experimental.pallas import tpu as pltpu

F32 = jnp.float32
BF16 = jnp.bfloat16
I32 = jnp.int32

HEAD_DIM = 64
MOBA_HEADS = 8
DSA_HEADS = 8
N_HEADS = MOBA_HEADS + DSA_HEADS
MOBA_WIDTH = MOBA_HEADS * HEAD_DIM
DSA_WIDTH = DSA_HEADS * HEAD_DIM
MOBA_BLOCK = 256
MOBA_TOPK = 3
MOBA_MAX_BLOCKS = 64
IDX_HEADS = 4
IDX_DIM = 64
DSA_TOPK_MAX = 256
DSA_SORT_CHUNKS = 8
N_BUCKETS = 32
MAX_DISTANCE = 4096
PEER_HEADS = 8
PEER_NKEYS = 128
PEER_DKEY = 256
PEER_TOPK = 16
EPS = 1e-6
PROJ_SIZES = (MOBA_WIDTH, MOBA_WIDTH, MOBA_WIDTH, DSA_WIDTH, DSA_WIDTH, DSA_WIDTH,
              IDX_HEADS * IDX_DIM, IDX_DIM, IDX_HEADS)
PROJ_WIDTH = sum(PROJ_SIZES)

NEG = -1e30
LOG2E = math.log2(math.e)
ATT_TILE = MOBA_BLOCK
ATT_KEY_TILES_PER_STEP = 4
LANES = 128
SUBLANES = 8
BF16_PACK_ROWS = 2 * SUBLANES
MXU_DEPTH = 256
VMEM_LIMIT = 56 * 1024 * 1024
NETWORK_REGS = 32

ADA_COLS = 1536
PROJ_ROWS = 512
MOBA_SEL_QUERIES = 2048
PEER_ROUTE_TOKENS = 256
PEER_DENSE_TOKENS = 512
PEER_DENSE_KEY_ROWS = 16
TRANSPOSE_ROWS = 1024

_NT = (((1,), (1,)), ((), ()))


def _far_bias_tiles(tile):
    max_exact = N_BUCKETS // 2
    sat = int(math.ceil(max_exact * (MAX_DISTANCE / max_exact) ** ((N_BUCKETS - max_exact - 1) / (N_BUCKETS - max_exact)))) + 4
    return -(-(sat + tile - 1) // tile)


def _cparams(sem):
    return pltpu.CompilerParams(dimension_semantics=sem, vmem_limit_bytes=VMEM_LIMIT)


def _ada_kernel(c_ref, w_ref, b_ref, o_ref):
    c = c_ref[...]
    sc = c * jax.nn.sigmoid(c)
    o_ref[0] = jnp.dot(sc, w_ref[0], preferred_element_type=F32,
                       precision=lax.Precision.HIGHEST) + b_ref[0]


def _ada_mod(c, w_ada, b_ada):
    depth, d, n = w_ada.shape
    tn = ADA_COLS
    c8 = jnp.broadcast_to(c, (SUBLANES, d))
    out = pl.pallas_call(
        _ada_kernel,
        grid=(depth, n // tn),
        in_specs=[pl.BlockSpec((SUBLANES, d), lambda l, j: (0, 0)),
                  pl.BlockSpec((1, d, tn), lambda l, j: (l, 0, j)),
                  pl.BlockSpec((1, 1, tn), lambda l, j: (l, 0, j))],
        out_specs=pl.BlockSpec((1, SUBLANES, tn), lambda l, j: (l, 0, j)),
        out_shape=jax.ShapeDtypeStruct((depth, SUBLANES, n), F32),
        compiler_params=_cparams(("arbitrary", "arbitrary")),
    )(c8, w_ada, b_ada.reshape(depth, 1, n))
    return out[:, 0, :]


def _nmm_kernel(x_ref, g_ref, sh_ref, sc_ref, w_ref, h_ref, o_ref):
    x = x_ref[...]
    y = x * lax.rsqrt(jnp.mean(x * x, axis=-1, keepdims=True) + EPS) * g_ref[...]
    hb = (y * (1.0 + sc_ref[...]) + sh_ref[...]).astype(BF16)
    h_ref[...] = hb
    o_ref[...] = jnp.dot(hb, w_ref[...], preferred_element_type=F32)


def _norm_mod_matmul(x, g, shift, scale, w_bf16, tm=PROJ_ROWS):
    s, d = x.shape
    n = w_bf16.shape[1]
    row = lambda i: (0, 0)
    return pl.pallas_call(
        _nmm_kernel,
        grid=(s // tm,),
        in_specs=[pl.BlockSpec((tm, d), lambda i: (i, 0)),
                  pl.BlockSpec((1, d), row), pl.BlockSpec((1, d), row), pl.BlockSpec((1, d), row),
                  pl.BlockSpec((d, n), row)],
        out_specs=[pl.BlockSpec((tm, d), lambda i: (i, 0)),
                   pl.BlockSpec((tm, n), lambda i: (i, 0))],
        out_shape=[jax.ShapeDtypeStruct((s, d), BF16), jax.ShapeDtypeStruct((s, n), F32)],
        compiler_params=_cparams(("arbitrary",)),
    )(x, g.reshape(1, d), shift.reshape(1, d), scale.reshape(1, d), w_bf16)


def _t5_bucket(d):
    max_exact = N_BUCKETS // 2
    d = jnp.maximum(d, 0)
    df = jnp.maximum(d, 1).astype(F32)
    large = max_exact + (jnp.log(df / max_exact) / math.log(MAX_DISTANCE / max_exact)
                         * (N_BUCKETS - max_exact)).astype(I32)
    return jnp.where(d < max_exact, d, jnp.minimum(large, N_BUCKETS - 1))


def _bias_tile_kernel(tab_ref, o_ref, *, tile):
    h = pl.program_id(0)
    off = pl.program_id(1)
    b = lax.broadcasted_iota(I32, (tile, tile), 0)
    a = lax.broadcasted_iota(I32, (tile, tile), 1)
    d = off * tile + a - b
    bucket = _t5_bucket(d)
    val = lax.fori_loop(jnp.min(bucket), jnp.max(bucket) + 1,
                        lambda k, val: jnp.where(bucket == k, tab_ref[h, k], val),
                        jnp.zeros((tile, tile), F32))
    val = (val - tab_ref[h, N_BUCKETS - 1]) * LOG2E
    o_ref[0, 0] = jnp.where(d < 0, NEG, val)


def _bias_tiles(rel_bias, tile):
    n_near = _far_bias_tiles(tile)
    nh = rel_bias.shape[0]
    return pl.pallas_call(
        functools.partial(_bias_tile_kernel, tile=tile),
        grid=(nh, n_near + 1),
        in_specs=[pl.BlockSpec(memory_space=pltpu.SMEM)],
        out_specs=pl.BlockSpec((1, 1, tile, tile), lambda h, o: (h, o, 0, 0)),
        out_shape=jax.ShapeDtypeStruct((nh, n_near + 1, tile, tile), F32),
        compiler_params=_cparams(("arbitrary", "arbitrary")),
    )(rel_bias)


def _kmean_kernel(k_ref, o_ref, *, nb):
    k = k_ref[0]
    km = jnp.mean(k.reshape(nb, MOBA_BLOCK, HEAD_DIM), axis=1)
    o_ref[0] = jnp.zeros(o_ref.shape[1:], F32)
    o_ref[0, :nb, :] = km


def _moba_kmean(k_hsd):
    nh, s, dh = k_hsd.shape
    nb = s // MOBA_BLOCK
    return pl.pallas_call(
        functools.partial(_kmean_kernel, nb=nb),
        grid=(nh,),
        in_specs=[pl.BlockSpec((1, s, dh), lambda h: (h, 0, 0))],
        out_specs=pl.BlockSpec((1, MOBA_MAX_BLOCKS, dh), lambda h: (h, 0, 0)),
        out_shape=jax.ShapeDtypeStruct((nh, MOBA_MAX_BLOCKS, dh), F32),
        compiler_params=_cparams(("arbitrary",)),
    )(k_hsd)


def _moba_sel_kernel(q_ref, km_ref, o_ref, *, tq, n_sel):
    gate = lax.dot_general(km_ref[0], q_ref[0], _NT, preferred_element_type=F32,
                           precision=lax.Precision.HIGHEST)
    t = pl.program_id(1) * tq + lax.broadcasted_iota(I32, (MOBA_MAX_BLOCKS, tq), 1)
    b0 = t // MOBA_BLOCK
    n = lax.broadcasted_iota(I32, (MOBA_MAX_BLOCKS, tq), 0)
    avail = n < b0
    taken = n == b0
    for _ in range(n_sel):
        g = jnp.where(avail, gate, -jnp.inf)
        m = jnp.max(g, axis=0, keepdims=True)
        idx = jnp.min(jnp.where((g == m) & avail, n, 1 << 20), axis=0, keepdims=True)
        pick = n == idx
        taken = taken | pick
        avail = avail & jnp.logical_not(pick)
    o_ref[0] = jnp.where(taken, 0.0, NEG)


def _moba_select(q_hsd, kmean, tq=MOBA_SEL_QUERIES):
    nh, s, dh = q_hsd.shape
    nb = s // MOBA_BLOCK
    assert nb <= MOBA_MAX_BLOCKS
    n_sel = max(min(MOBA_TOPK, nb - 1), 1)
    tq = min(tq, s)
    return pl.pallas_call(
        functools.partial(_moba_sel_kernel, tq=tq, n_sel=n_sel),
        grid=(nh, s // tq),
        in_specs=[pl.BlockSpec((1, tq, dh), lambda h, i: (h, i, 0)),
                  pl.BlockSpec((1, MOBA_MAX_BLOCKS, dh), lambda h, i: (h, 0, 0))],
        out_specs=pl.BlockSpec((1, MOBA_MAX_BLOCKS, tq), lambda h, i: (h, 0, i)),
        out_shape=jax.ShapeDtypeStruct((nh, MOBA_MAX_BLOCKS, s), F32),
        compiler_params=_cparams(("arbitrary", "arbitrary")),
    )(q_hsd, kmean)


def _attn_kernel(qi_ref, kg_ref, *refs, nh, n_near, group, has_mask):
    q_ref, k_ref, vt_ref, bt_ref = refs[:4]
    mask_ref = refs[4] if has_mask else None
    o_ref, m_scr, acc_scr = refs[-3:]
    t = ATT_TILE
    step = pl.program_id(0)
    qi = qi_ref[step]
    kg = kg_ref[step]

    @pl.when(kg == 0)
    def _():
        m_scr[...] = jnp.full(m_scr.shape, NEG, F32)
        acc_scr[...] = jnp.zeros(acc_scr.shape, F32)

    def step_update(with_bias, n_tiles):
        m = [m_scr[h] for h in range(nh)]
        for g in range(n_tiles):
            keys = slice(g * t, (g + 1) * t)
            off = qi - (kg * group + g)
            bias_row = jnp.minimum(off, n_near)
            if has_mask:
                mb = mask_ref[0, g].astype(F32)
            scores = []
            for h in range(nh):
                s = lax.dot_general(k_ref[h, keys, :], q_ref[h], _NT, preferred_element_type=F32)
                if with_bias:
                    s = s + bt_ref[h, bias_row]
                if has_mask:
                    s = s + mb
                scores.append(s)
            probs, alphas = [], []
            for h in range(nh):
                m_new = jnp.maximum(m[h], jnp.max(scores[h], axis=0, keepdims=True))
                alpha = jnp.exp2(m[h] - m_new)
                p = jnp.exp2(scores[h] - m_new)
                m[h] = m_new
                probs.append(p.astype(BF16))
                alphas.append(alpha)
            for h in range(nh):
                acc_scr[h] = alphas[h] * acc_scr[h] + jnp.dot(vt_ref[h, :, keys], probs[h],
                                                              preferred_element_type=F32)
        for h in range(nh):
            m_scr[h] = m[h]

    all_far = qi - (kg * group + group - 1) >= n_near
    pl.when(all_far)(functools.partial(step_update, False, group))
    n_causal = jnp.minimum(qi - kg * group + 1, group)
    for n_tiles in range(1, group + 1):
        pl.when(jnp.logical_not(all_far) & (n_causal == n_tiles))(functools.partial(step_update, True, n_tiles))

    @pl.when(kg == qi // group)
    def _():
        for h in range(nh):
            o_ref[h] = acc_scr[h, :HEAD_DIM, :] / acc_scr[h, HEAD_DIM:HEAD_DIM + 1, :]


def _attention(q, k, vt, bias_tiles, head_group, mask_bias=None):
    nh, s, dk = q.shape
    vt = jnp.concatenate([vt, jnp.ones((nh, SUBLANES, s), vt.dtype)], axis=1)
    dv = vt.shape[1]
    t = ATT_TILE
    group = ATT_KEY_TILES_PER_STEP
    nq = s // t
    n_near = bias_tiles.shape[1] - 1
    qi_list = [i for i in range(nq) for _ in range(i // group + 1)]
    kg_list = [j for i in range(nq) for j in range(i // group + 1)]
    qi_arr = jnp.asarray(qi_list, I32)
    kg_arr = jnp.asarray(kg_list, I32)
    in_specs = [pl.BlockSpec((nh, t, dk), lambda st, qi, kg: (0, qi[st], 0)),
                pl.BlockSpec((nh, group * t, dk), lambda st, qi, kg: (0, kg[st], 0)),
                pl.BlockSpec((nh, dv, group * t), lambda st, qi, kg: (0, 0, kg[st])),
                pl.BlockSpec((nh, n_near + 1, t, t), lambda st, qi, kg: (head_group, 0, 0, 0),
                             pipeline_mode=pl.Buffered(1))]
    args = [q, k, vt, bias_tiles]
    if mask_bias is not None:
        in_specs.append(pl.BlockSpec((1, group, t, t), lambda st, qi, kg: (qi[st], kg[st], 0, 0)))
        args.append(mask_bias)
    grid_spec = pltpu.PrefetchScalarGridSpec(
        num_scalar_prefetch=2,
        grid=(len(qi_list),),
        in_specs=in_specs,
        out_specs=pl.BlockSpec((nh, HEAD_DIM, t), lambda st, qi, kg: (0, 0, qi[st])),
        scratch_shapes=[pltpu.VMEM((nh, 1, t), F32), pltpu.VMEM((nh, dv, t), F32)])
    return pl.pallas_call(
        functools.partial(_attn_kernel, nh=nh, n_near=n_near, group=group, has_mask=mask_bias is not None),
        grid_spec=grid_spec,
        out_shape=jax.ShapeDtypeStruct((nh, HEAD_DIM, s), F32),
        compiler_params=_cparams(("arbitrary",)),
    )(qi_arr, kg_arr, *args)


def _bitonic_stages(n, full_sort):
    levels = [2 ** p for p in range(1, n.bit_length())] if full_sort else [n]
    return [(k, k >> s) for k in levels for s in range(1, k.bit_length())]


def _plan_passes(stages, block):
    passes = []
    for k, j in stages:
        cross = j >= block
        if passes and passes[-1][2] == cross and (not cross or bin(passes[-1][0] | j).count("1") <= 3):
            passes[-1][0] |= j if cross else 0
            passes[-1][1].append((k, j))
        else:
            passes.append([j if cross else block - 1, [(k, j)], cross])
    return [(mask, sub) for mask, sub, _ in passes]


def _run_pass(load, store, n, mask, stages, regs=NETWORK_REGS):
    groups = {}
    for e in range(n):
        groups.setdefault(e & ~mask, []).append(e)
    groups = list(groups.values())
    per = max(1, regs // len(groups[0]))
    for g0 in range(0, len(groups), per):
        elems = sorted(e for grp in groups[g0:g0 + per] for e in grp)
        vals = {e: load(e) for e in elems}
        for k, j in stages:
            for e in elems:
                l = e ^ j
                if l > e:
                    hi = jnp.maximum(vals[e], vals[l])
                    lo = jnp.minimum(vals[e], vals[l])
                    vals[e], vals[l] = (hi, lo) if (e & k) == 0 else (lo, hi)
        for e in elems:
            store(e, vals[e])


def _sortable(x):
    return jnp.where(x == 0.0, 0.0, x)


def _dsa_sel_kernel(qi_ref, wt_ref, kn_ref, tri_ref, o_ref, keys_scr, seen_scr, work_scr, top_scr, *, tile, topk):
    n_chunks = kn_ref.shape[0]
    i = pl.program_id(0)
    n_live = i + 1
    groups_per_chunk = tile // SUBLANES
    causal = lax.broadcasted_iota(I32, (tile, tile), 0) <= lax.broadcasted_iota(I32, (tile, tile), 1)
    q_heads = [qi_ref[:, h * IDX_DIM:(h + 1) * IDX_DIM] for h in range(IDX_HEADS)]
    w_heads = [wt_ref[h:h + 1, :] for h in range(IDX_HEADS)]

    def score_chunk(c, diagonal):
        kc = kn_ref[c]
        dots = [lax.dot_general(kc, q_heads[h], _NT, preferred_element_type=F32) for h in range(IDX_HEADS)]
        sc = None
        for h in range(IDX_HEADS):
            term = w_heads[h] * jnp.maximum(dots[h], 0.0)
            sc = term if sc is None else sc + term
        if diagonal:
            sc = jnp.where(causal, sc, -jnp.inf)
        keys_scr[c] = _sortable(sc)

    def for_past_chunks(fn):
        def pair(p, carry):
            fn(2 * p)
            fn(2 * p + 1)
            return carry
        lax.fori_loop(0, i // 2, pair, 0)
        pl.when(i % 2 == 1)(lambda: fn(i - 1))

    for_past_chunks(lambda c: score_chunk(c, False))
    score_chunk(i, True)

    n_list = DSA_SORT_CHUNKS * groups_per_chunk
    n_batches = (n_live + DSA_SORT_CHUNKS - 1) // DSA_SORT_CHUNKS

    def pad_chunk(c, carry):
        keys_scr[c] = jnp.full((tile, tile), -jnp.inf, F32)
        return carry

    lax.fori_loop(n_live, n_batches * DSA_SORT_CHUNKS, pad_chunk, 0)
    top_scr[...] = jnp.full(top_scr.shape, -jnp.inf, F32)
    sort_passes = _plan_passes(_bitonic_stages(n_list, True), groups_per_chunk)
    merge_passes = _plan_passes(_bitonic_stages(n_list, False), groups_per_chunk)
    lane_halves = [slice(lo, lo + LANES) for lo in range(0, tile, LANES)]

    def run_network(passes, first_load, first_store, buf, lanes):
        def load(e):
            return buf[e, :, lanes]

        def store(e, v):
            buf[e, :, lanes] = v

        for n_pass, (mask, stages) in enumerate(passes):
            _run_pass(first_load if n_pass == 0 else load, first_store if n_pass == 0 else store,
                      n_list, mask, stages)

    def sort_batch(b, carry):
        for lanes in lane_halves:
            def load_keys(e, lanes=lanes):
                g = e % groups_per_chunk
                return keys_scr[b * DSA_SORT_CHUNKS + e // groups_per_chunk, SUBLANES * g:SUBLANES * (g + 1), lanes]

            def store_work(e, v, lanes=lanes):
                work_scr[e, :, lanes] = v

            run_network(sort_passes, load_keys, store_work, work_scr, lanes)

            def load_best(e, lanes=lanes):
                return jnp.maximum(top_scr[e, :, lanes], work_scr[n_list - 1 - e, :, lanes])

            def store_top(e, v, lanes=lanes):
                top_scr[e, :, lanes] = v

            run_network(merge_passes, load_best, store_top, top_scr, lanes)
        return carry

    lax.fori_loop(0, n_batches, sort_batch, 0)

    src, dst = top_scr, work_scr
    for shift in (4, 2, 1):
        for lanes in lane_halves:
            def load_pair(e, lanes=lanes, src=src, shift=shift):
                return jnp.maximum(src[e, :, lanes], pltpu.roll(src[n_list - 1 - e, :, lanes], shift, 0))

            def store_dst(e, v, lanes=lanes, dst=dst):
                dst[e, :, lanes] = v

            run_network(merge_passes, load_pair, store_dst, dst, lanes)
        src, dst = dst, src
    thr = src[topk - 1, 0:1, :]
    hits = [(src[e] > thr).astype(I32) for e in range(topk - 1)]
    while len(hits) > 1:
        hits = [hits[j] + hits[j + 1] if j + 1 < len(hits) else hits[j] for j in range(0, len(hits), 2)]
    n_gt = hits[0][0:1, :]
    need = (topk - n_gt).astype(F32)

    seen_scr[...] = jnp.zeros(seen_scr.shape, F32)

    def emit_chunk(c, diagonal):
        kc = keys_scr[c]
        eq = kc == thr
        eqf = eq.astype(F32)
        seen = seen_scr[...]
        seen_scr[...] = seen + jnp.sum(eqf, axis=0, keepdims=True)
        rank = seen + jnp.dot(tri_ref[...], eqf.astype(BF16), preferred_element_type=F32)
        sel = (kc > thr) | (eq & (rank < need))
        if diagonal:
            sel = sel & causal
        o_ref[0, c] = jnp.where(sel, 0.0, NEG).astype(BF16)

    for_past_chunks(lambda c: emit_chunk(c, False))
    emit_chunk(i, True)

    def fill_chunk(c, carry):
        o_ref[0, c] = jnp.full((tile, tile), NEG, BF16)
        return carry

    lax.fori_loop(n_live, n_chunks, fill_chunk, 0)


def _dsa_select(q_idx_bf16, w_idx, kn_bf16, topk):
    s = q_idx_bf16.shape[0]
    t = ATT_TILE
    n_chunks = s // t
    n_list = DSA_SORT_CHUNKS * (t // SUBLANES)
    assert topk <= n_list, "the sorting network keeps n_list best keys per query"
    kn3 = kn_bf16.reshape(n_chunks, t, IDX_DIM)
    wt = jnp.pad(w_idx.T, ((0, SUBLANES - IDX_HEADS), (0, 0)))
    tri = (lax.broadcasted_iota(I32, (t, t), 1) < lax.broadcasted_iota(I32, (t, t), 0)).astype(BF16)
    return pl.pallas_call(
        functools.partial(_dsa_sel_kernel, tile=t, topk=topk),
        grid=(n_chunks,),
        in_specs=[pl.BlockSpec((t, IDX_HEADS * IDX_DIM), lambda i: (i, 0)),
                  pl.BlockSpec((SUBLANES, t), lambda i: (0, i)),
                  pl.BlockSpec((n_chunks, t, IDX_DIM), lambda i: (0, 0, 0)),
                  pl.BlockSpec((t, t), lambda i: (0, 0))],
        out_specs=pl.BlockSpec((1, n_chunks, t, t), lambda i: (i, 0, 0, 0)),
        out_shape=jax.ShapeDtypeStruct((n_chunks, n_chunks, t, t), BF16),
        scratch_shapes=[pltpu.VMEM((max(n_chunks, DSA_SORT_CHUNKS), t, t), F32), pltpu.VMEM((1, t), F32),
                        pltpu.VMEM((n_list, SUBLANES, t), F32), pltpu.VMEM((n_list, SUBLANES, t), F32)],
        compiler_params=_cparams(("arbitrary",)),
    )(q_idx_bf16, wt, kn3, tri)


def _transpose_bf16(a_bf16):
    n = a_bf16.shape[1]
    eye = (lax.broadcasted_iota(I32, (n, n), 0) == lax.broadcasted_iota(I32, (n, n), 1)).astype(BF16)
    return lax.dot_general(eye, a_bf16, _NT, preferred_element_type=F32)


def _out_proj_kernel(ya_ref, yb_ref, w_ref, x_ref, g_ref, o_ref):
    ka = ya_ref.shape[0]
    ya = _transpose_bf16(ya_ref[...].astype(BF16)).astype(BF16)
    yb = _transpose_bf16(yb_ref[...].astype(BF16)).astype(BF16)
    mixed = (jnp.dot(ya, w_ref[:ka, :], preferred_element_type=F32)
             + jnp.dot(yb, w_ref[ka:, :], preferred_element_type=F32))
    o_ref[...] = x_ref[...] + g_ref[...] * mixed


def _out_proj(yt_a, yt_b, w_bf16, x, gate, tm=PROJ_ROWS):
    s, d = x.shape
    ka, kb = yt_a.shape[0], yt_b.shape[0]
    return pl.pallas_call(
        _out_proj_kernel,
        grid=(s // tm,),
        in_specs=[pl.BlockSpec((ka, tm), lambda i: (0, i)),
                  pl.BlockSpec((kb, tm), lambda i: (0, i)),
                  pl.BlockSpec((ka + kb, d), lambda i: (0, 0)),
                  pl.BlockSpec((tm, d), lambda i: (i, 0)),
                  pl.BlockSpec((1, d), lambda i: (0, 0))],
        out_specs=pl.BlockSpec((tm, d), lambda i: (i, 0)),
        out_shape=jax.ShapeDtypeStruct((s, d), F32),
        compiler_params=_cparams(("arbitrary",)),
    )(yt_a, yt_b, w_bf16, x, gate.reshape(1, d))


def _transpose_cast_kernel(v_ref, o_ref):
    o_ref[...] = _transpose_bf16(v_ref[...].astype(BF16)).astype(BF16)


def _transpose_cast(v, tr=TRANSPOSE_ROWS):
    rows, d = v.shape
    return pl.pallas_call(
        _transpose_cast_kernel,
        grid=(rows // tr,),
        in_specs=[pl.BlockSpec((tr, d), lambda i: (i, 0))],
        out_specs=pl.BlockSpec((d, tr), lambda i: (0, i)),
        out_shape=jax.ShapeDtypeStruct((d, rows), BF16),
        compiler_params=_cparams(("arbitrary",)),
    )(v)


def _cmp_exchange(v, i, j):
    hi = jnp.maximum(v[i], v[j])
    lo = jnp.minimum(v[i], v[j])
    v[i], v[j] = hi, lo


def _bitonic_sort_desc(v):
    n = len(v)
    k = 2
    while k <= n:
        j = k // 2
        while j >= 1:
            for i in range(n):
                l = i ^ j
                if l > i:
                    if (i & k) == 0:
                        _cmp_exchange(v, i, l)
                    else:
                        _cmp_exchange(v, l, i)
            j //= 2
        k *= 2


def _bitonic_merge_desc(v):
    n = len(v)
    j = n // 2
    while j >= 1:
        for i in range(n):
            l = i ^ j
            if l > i:
                _cmp_exchange(v, i, l)
        j //= 2


def _top16_rows(x):
    tt = x.shape[1]
    v = [x[SUBLANES * i:SUBLANES * (i + 1), :] for i in range(PEER_NKEYS // SUBLANES)]
    _bitonic_sort_desc(v)
    for shift in (4, 2, 1):
        other = [pltpu.roll(a, shift, 0) for a in v]
        v = [jnp.maximum(v[i], other[len(v) - 1 - i]) for i in range(len(v))]
        _bitonic_merge_desc(v)
    return v


def _peer_route_kernel(q_ref, sk_ref, r1_ref, e1_ref, e0_ref, ns_ref, s0_scr, s1_scr, *, tt):
    assert PEER_HEADS == SUBLANES
    sub = lax.broadcasted_iota(I32, (SUBLANES, tt), 0)
    tops = [[], []]
    for h in range(PEER_HEADS):
        for p in range(2):
            lo = (2 * h + p) * (PEER_DKEY // 2)
            qhp = q_ref[:, lo:lo + PEER_DKEY // 2].astype(BF16)
            sc = lax.dot_general(sk_ref[2 * h + p], qhp, _NT, preferred_element_type=F32)
            (s0_scr if p == 0 else s1_scr)[h] = sc
            tops[p].append(_top16_rows(sc))
    a = [sum(jnp.where(sub == h, tops[0][h][k], 0.0) for h in range(PEER_HEADS)) for k in range(PEER_TOPK)]
    b = [sum(jnp.where(sub == h, tops[1][h][k], 0.0) for h in range(PEER_HEADS)) for k in range(PEER_TOPK)]
    cand = [a[i] + b[j] for i in range(PEER_TOPK) for j in range(PEER_TOPK) if (i + 1) * (j + 1) <= PEER_TOPK]
    cand = cand + [jnp.full((SUBLANES, tt), -jnp.inf, F32)] * (64 - len(cand))
    _bitonic_sort_desc(cand)
    tau = cand[PEER_TOPK - 1]
    z = sum(jnp.exp(cand[k] - cand[0]) for k in range(PEER_TOPK))
    half_inv_z = 0.5 / z
    nsel_of_rank = [sum(jnp.where(a[r] + b[k] >= tau, 1.0, 0.0) for k in range(PEER_TOPK)) for r in range(PEER_TOPK)]
    for h in range(PEER_HEADS):
        s0 = s0_scr[h]
        s1 = s1_scr[h]
        nsel = jnp.zeros((PEER_NKEYS, tt), F32)
        rank1 = jnp.full((PEER_NKEYS, tt), float(PEER_TOPK), F32)
        for r in reversed(range(PEER_TOPK)):
            nsel = jnp.where(s0 == a[r][h:h + 1, :], nsel_of_rank[r][h:h + 1, :], nsel)
            rank1 = jnp.where(s1 == b[r][h:h + 1, :], float(r), rank1)
        ns_ref[h] = nsel
        r1_ref[h] = rank1.astype(BF16)
        e0_ref[h] = jnp.exp(s0 - a[0][h:h + 1, :]) * half_inv_z[h:h + 1, :]
        e1_ref[h] = jnp.exp(s1 - b[0][h:h + 1, :]).astype(BF16)


def _peer_route(q, subkeys_bf16, tt=PEER_ROUTE_TOKENS):
    s = q.shape[0]
    big = lambda dt: jax.ShapeDtypeStruct((PEER_HEADS, PEER_NKEYS, s), dt)
    big_spec = pl.BlockSpec((PEER_HEADS, PEER_NKEYS, tt), lambda i: (0, 0, i))
    return pl.pallas_call(
        functools.partial(_peer_route_kernel, tt=tt),
        grid=(s // tt,),
        in_specs=[pl.BlockSpec((tt, PEER_HEADS * PEER_DKEY), lambda i: (i, 0)),
                  pl.BlockSpec((2 * PEER_HEADS, PEER_NKEYS, PEER_DKEY // 2), lambda i: (0, 0, 0))],
        out_specs=[big_spec, big_spec, big_spec, big_spec],
        out_shape=[big(BF16), big(BF16), big(F32), big(F32)],
        scratch_shapes=[pltpu.VMEM((PEER_HEADS, PEER_NKEYS, tt), F32),
                        pltpu.VMEM((PEER_HEADS, PEER_NKEYS, tt), F32)],
        compiler_params=_cparams(("arbitrary",)),
    )(q, subkeys_bf16)


def _peer_dense_kernel(ht_ref, u_ref, vt_ref, r1_ref, e1_ref, e0_ref, ns_ref,
                       x_ref, g_ref, gf_ref, o_ref, acc_scr, p_scr, *, rows_per_step, final_norm):
    k = pl.program_id(1)

    @pl.when(k == 0)
    def _():
        acc_scr[...] = jnp.zeros(acc_scr.shape, F32)

    tt = ht_ref.shape[1]
    pack_rows = BF16_PACK_ROWS
    zero = jnp.zeros((pack_rows, tt), BF16)

    def activations(c):
        return jnp.dot(u_ref[c * MXU_DEPTH:(c + 1) * MXU_DEPTH, :], ht_ref[...], preferred_element_type=F32)

    def expert_rows(ii, a):
        i = k * rows_per_step + ii
        def bcast(ref, h):
            half = jnp.broadcast_to(ref[h, pl.ds(i, 1), :], (pack_rows // 2, tt))
            return jnp.concatenate([half, half], axis=0).astype(BF16)
        e0_rows = [bcast(e0_ref, h) for h in range(PEER_HEADS)]
        ns_rows = [bcast(ns_ref, h) for h in range(PEER_HEADS)]
        gelu2 = (a * (1.0 + lax.erf(a * math.sqrt(0.5)))).astype(BF16)
        for g in range(PEER_NKEYS // pack_rows):
            rows = slice(g * pack_rows, (g + 1) * pack_rows)
            wgt = None
            for h in range(PEER_HEADS):
                term = jnp.where(r1_ref[h, rows, :] < ns_rows[h], e1_ref[h, rows, :], zero) * e0_rows[h]
                wgt = term if wgt is None else wgt + term
            p_scr[ii * PEER_NKEYS + g * pack_rows:ii * PEER_NKEYS + (g + 1) * pack_rows, :] = wgt * gelu2[rows, :]

    depth_rows = MXU_DEPTH // PEER_NKEYS
    n_chunks = rows_per_step // depth_rows
    act = activations(0)
    for c in range(n_chunks):
        act_next = activations(c + 1) if c + 1 < n_chunks else None
        for j in range(depth_rows):
            expert_rows(c * depth_rows + j, act[j * PEER_NKEYS:(j + 1) * PEER_NKEYS, :])
        act = act_next
    acc_scr[...] += jnp.dot(vt_ref[...], p_scr[...], preferred_element_type=F32)

    @pl.when(k == pl.num_programs(1) - 1)
    def _():
        y = x_ref[...] + g_ref[...] * acc_scr[...].T
        if final_norm:
            y = y * lax.rsqrt(jnp.mean(y * y, axis=-1, keepdims=True) + EPS) * gf_ref[...]
        o_ref[...] = y


def _peer_dense(ht_bf16, u_bf16, vt_bf16, rank1, e1, e0, nsel, x, gate, g_final, final_norm,
                tt=PEER_DENSE_TOKENS, rows_per_step=PEER_DENSE_KEY_ROWS):
    s, d = x.shape
    n_exp = u_bf16.shape[0]
    et = rows_per_step * PEER_NKEYS
    tok = lambda j, k: (0, 0, j)
    big_spec = pl.BlockSpec((PEER_HEADS, PEER_NKEYS, tt), tok)
    return pl.pallas_call(
        functools.partial(_peer_dense_kernel, rows_per_step=rows_per_step, final_norm=final_norm),
        grid=(s // tt, n_exp // et),
        in_specs=[pl.BlockSpec((d, tt), lambda j, k: (0, j)),
                  pl.BlockSpec((et, d), lambda j, k: (k, 0)),
                  pl.BlockSpec((d, et), lambda j, k: (0, k)),
                  big_spec, big_spec, big_spec, big_spec,
                  pl.BlockSpec((tt, d), lambda j, k: (j, 0)),
                  pl.BlockSpec((1, d), lambda j, k: (0, 0)),
                  pl.BlockSpec((1, d), lambda j, k: (0, 0))],
        out_specs=pl.BlockSpec((tt, d), lambda j, k: (j, 0)),
        out_shape=jax.ShapeDtypeStruct((s, d), F32),
        scratch_shapes=[pltpu.VMEM((d, tt), F32), pltpu.VMEM((et, tt), BF16)],
        compiler_params=_cparams(("arbitrary", "arbitrary")),
    )(ht_bf16, u_bf16, vt_bf16, rank1, e1, e0, nsel, x, gate.reshape(1, d), g_final.reshape(1, d))


def _heads_first(t, nh):
    s = t.shape[0]
    return jnp.transpose(t.reshape(s, nh, HEAD_DIM), (1, 0, 2))


def _mixer(h_proj, w_out_bf16, bias_tiles, x, gate):
    s = h_proj.shape[0]
    parts = []
    off = 0
    for size in PROJ_SIZES:
        parts.append(h_proj[:, off:off + size])
        off += size
    mq, mk, mv, dq, dk, dv, iq, ik, iw = parts
    scale = HEAD_DIM ** -0.5 * LOG2E
    dims_first = lambda a, nh: a.T.reshape(nh, HEAD_DIM, s).astype(BF16)

    mq_h, mk_h = _heads_first(mq, MOBA_HEADS), _heads_first(mk, MOBA_HEADS)
    block_sel = _moba_select(mq_h, _moba_kmean(mk_h))
    blk = jnp.arange(s, dtype=I32) // MOBA_BLOCK
    onehot = (blk[:, None] == jnp.arange(MOBA_MAX_BLOCKS, dtype=I32)[None, :]).astype(BF16)
    q_aug = jnp.concatenate([(mq_h * scale).astype(BF16), jnp.transpose(block_sel, (0, 2, 1)).astype(BF16)], axis=-1)
    k_aug = jnp.concatenate([mk_h.astype(BF16), jnp.broadcast_to(onehot, (MOBA_HEADS, s, MOBA_MAX_BLOCKS))], axis=-1)
    y_moba = _attention(q_aug, k_aug, dims_first(mv, MOBA_HEADS), bias_tiles, 0)

    ikf = ik.astype(F32)
    ik_n = ikf * lax.rsqrt(jnp.mean(ikf * ikf, axis=-1, keepdims=True) + EPS)
    topk = min(DSA_TOPK_MAX, s // 4)
    mask_bias = _dsa_select(iq.astype(BF16), iw * IDX_HEADS ** -0.5, ik_n.astype(BF16), topk)
    dq_h, dk_h = _heads_first(dq, DSA_HEADS), _heads_first(dk, DSA_HEADS)
    y_dsa = _attention((dq_h * scale).astype(BF16), dk_h.astype(BF16), dims_first(dv, DSA_HEADS),
                       bias_tiles, 1, mask_bias=mask_bias)

    return _out_proj(y_moba.reshape(MOBA_WIDTH, s), y_dsa.reshape(DSA_WIDTH, s), w_out_bf16, x, gate)


def kernel(x, c, w_ada, b_ada, norm_attn, norm_ffn, w_in, w_out, rel_bias,
           peer_wq, peer_subkeys, peer_u, peer_v, norm_final):
    batch, s, d = x.shape
    assert batch == 1 and s % (ATT_KEY_TILES_PER_STEP * ATT_TILE) == 0
    depth = w_ada.shape[0]
    x = x[0]
    mod = _ada_mod(c, w_ada, b_ada)
    bias_tiles = _bias_tiles(rel_bias, ATT_TILE)
    n_pad = -(-PROJ_WIDTH // LANES) * LANES
    for l in range(depth):
        sh1, sc1, g1, sh2, sc2, g2 = jnp.split(mod[l], 6)
        w_in_l = jnp.pad(w_in[l], ((0, 0), (0, n_pad - PROJ_WIDTH))).astype(BF16)
        _, proj = _norm_mod_matmul(x, norm_attn[l], sh1, sc1, w_in_l)
        x = _mixer(proj, w_out[l].astype(BF16), bias_tiles, x, g1)
        h2, q = _norm_mod_matmul(x, norm_ffn[l], sh2, sc2, peer_wq[l].astype(BF16))
        subkeys = peer_subkeys[l].reshape(2 * PEER_HEADS, PEER_NKEYS, PEER_DKEY // 2).astype(BF16)
        rank1, e1, e0, nsel = _peer_route(q, subkeys)
        x = _peer_dense(_transpose_cast(h2), peer_u[l].astype(BF16), _transpose_cast(peer_v[l]), rank1, e1, e0, nsel,
                        x, g2, norm_final, final_norm=(l == depth - 1))
    return x[None]
```

```python
import functools
import math

import jax
import jax.numpy as jnp
from jax import lax
from jax.experimental import pallas as pl
from jax.experimental.pallas import tpu as pltpu

F32 = jnp.float32
BF16 = jnp.bfloat16
I32 = jnp.int32

HEAD_DIM = 64
MOBA_HEADS = 8
DSA_HEADS = 8
N_HEADS = MOBA_HEADS + DSA_HEADS
MOBA_WIDTH = MOBA_HEADS * HEAD_DIM
DSA_WIDTH = DSA_HEADS * HEAD_DIM
MOBA_BLOCK = 256
MOBA_TOPK = 3
MOBA_MAX_BLOCKS = 64
IDX_HEADS = 4
IDX_DIM = 64
DSA_TOPK_MAX = 256
DSA_SORT_CHUNKS = 8
N_BUCKETS = 32
MAX_DISTANCE = 4096
PEER_HEADS = 8
PEER_NKEYS = 128
PEER_DKEY = 256
PEER_TOPK = 16
EPS = 1e-6
PROJ_SIZES = (MOBA_WIDTH, MOBA_WIDTH, MOBA_WIDTH, DSA_WIDTH, DSA_WIDTH, DSA_WIDTH,
              IDX_HEADS * IDX_DIM, IDX_DIM, IDX_HEADS)
PROJ_WIDTH = sum(PROJ_SIZES)

NEG = -1e30
LOG2E = math.log2(math.e)
ATT_TILE = MOBA_BLOCK
ATT_KEY_TILES_PER_STEP = 4
LANES = 128
SUBLANES = 8
BF16_PACK_ROWS = 2 * SUBLANES
MXU_DEPTH = 256
VMEM_LIMIT = 56 * 1024 * 1024
NETWORK_REGS = 32

ADA_COLS = 1536
PROJ_ROWS = 512
MOBA_SEL_QUERIES = 2048
PEER_ROUTE_TOKENS = 256
PEER_DENSE_TOKENS = 512
PEER_DENSE_KEY_ROWS = 16
TRANSPOSE_ROWS = 1024

_NT = (((1,), (1,)), ((), ()))


def _far_bias_tiles(tile):
    max_exact = N_BUCKETS // 2
    sat = int(math.ceil(max_exact * (MAX_DISTANCE / max_exact) ** ((N_BUCKETS - max_exact - 1) / (N_BUCKETS - max_exact)))) + 4
    return -(-(sat + tile - 1) // tile)


def _cparams(sem):
    return pltpu.CompilerParams(dimension_semantics=sem, vmem_limit_bytes=VMEM_LIMIT)


def _ada_kernel(c_ref, w_ref, b_ref, o_ref):
    c = c_ref[...]
    sc = c * jax.nn.sigmoid(c)
    o_ref[0] = jnp.dot(sc, w_ref[0], preferred_element_type=F32,
                       precision=lax.Precision.HIGHEST) + b_ref[0]


def _ada_mod(c, w_ada, b_ada):
    depth, d, n = w_ada.shape
    tn = ADA_COLS
    c8 = jnp.broadcast_to(c, (SUBLANES, d))
    out = pl.pallas_call(
        _ada_kernel,
        grid=(depth, n // tn),
        in_specs=[pl.BlockSpec((SUBLANES, d), lambda l, j: (0, 0)),
                  pl.BlockSpec((1, d, tn), lambda l, j: (l, 0, j)),
                  pl.BlockSpec((1, 1, tn), lambda l, j: (l, 0, j))],
        out_specs=pl.BlockSpec((1, SUBLANES, tn), lambda l, j: (l, 0, j)),
        out_shape=jax.ShapeDtypeStruct((depth, SUBLANES, n), F32),
        compiler_params=_cparams(("arbitrary", "arbitrary")),
    )(c8, w_ada, b_ada.reshape(depth, 1, n))
    return out[:, 0, :]


def _nmm_kernel(x_ref, g_ref, sh_ref, sc_ref, w_ref, h_ref, o_ref):
    x = x_ref[...]
    y = x * lax.rsqrt(jnp.mean(x * x, axis=-1, keepdims=True) + EPS) * g_ref[...]
    hb = (y * (1.0 + sc_ref[...]) + sh_ref[...]).astype(BF16)
    h_ref[...] = hb
    o_ref[...] = jnp.dot(hb, w_ref[...], preferred_element_type=F32)


def _norm_mod_matmul(x, g, shift, scale, w_bf16, tm=PROJ_ROWS):
    s, d = x.shape
    n = w_bf16.shape[1]
    row = lambda i: (0, 0)
    return pl.pallas_call(
        _nmm_kernel,
        grid=(s // tm,),
        in_specs=[pl.BlockSpec((tm, d), lambda i: (i, 0)),
                  pl.BlockSpec((1, d), row), pl.BlockSpec((1, d), row), pl.BlockSpec((1, d), row),
                  pl.BlockSpec((d, n), row)],
        out_specs=[pl.BlockSpec((tm, d), lambda i: (i, 0)),
                   pl.BlockSpec((tm, n), lambda i: (i, 0))],
        out_shape=[jax.ShapeDtypeStruct((s, d), BF16), jax.ShapeDtypeStruct((s, n), F32)],
        compiler_params=_cparams(("arbitrary",)),
    )(x, g.reshape(1, d), shift.reshape(1, d), scale.reshape(1, d), w_bf16)


def _t5_bucket(d):
    max_exact = N_BUCKETS // 2
    d = jnp.maximum(d, 0)
    df = jnp.maximum(d, 1).astype(F32)
    large = max_exact + (jnp.log(df / max_exact) / math.log(MAX_DISTANCE / max_exact)
                         * (N_BUCKETS - max_exact)).astype(I32)
    return jnp.where(d < max_exact, d, jnp.minimum(large, N_BUCKETS - 1))


def _bias_tile_kernel(tab_ref, o_ref, *, tile):
    h = pl.program_id(0)
    off = pl.program_id(1)
    b = lax.broadcasted_iota(I32, (tile, tile), 0)
    a = lax.broadcasted_iota(I32, (tile, tile), 1)
    d = off * tile + a - b
    bucket = _t5_bucket(d)
    val = lax.fori_loop(jnp.min(bucket), jnp.max(bucket) + 1,
                        lambda k, val: jnp.where(bucket == k, tab_ref[h, k], val),
                        jnp.zeros((tile, tile), F32))
    val = (val - tab_ref[h, N_BUCKETS - 1]) * LOG2E
    o_ref[0, 0] = jnp.where(d < 0, NEG, val)


def _bias_tiles(rel_bias, tile):
    n_near = _far_bias_tiles(tile)
    nh = rel_bias.shape[0]
    return pl.pallas_call(
        functools.partial(_bias_tile_kernel, tile=tile),
        grid=(nh, n_near + 1),
        in_specs=[pl.BlockSpec(memory_space=pltpu.SMEM)],
        out_specs=pl.BlockSpec((1, 1, tile, tile), lambda h, o: (h, o, 0, 0)),
        out_shape=jax.ShapeDtypeStruct((nh, n_near + 1, tile, tile), F32),
        compiler_params=_cparams(("arbitrary", "arbitrary")),
    )(rel_bias)


def _kmean_kernel(k_ref, o_ref, *, nb):
    k = k_ref[0]
    km = jnp.mean(k.reshape(nb, MOBA_BLOCK, HEAD_DIM), axis=1)
    o_ref[0] = jnp.zeros(o_ref.shape[1:], F32)
    o_ref[0, :nb, :] = km


def _moba_kmean(k_hsd):
    nh, s, dh = k_hsd.shape
    nb = s // MOBA_BLOCK
    return pl.pallas_call(
        functools.partial(_kmean_kernel, nb=nb),
        grid=(nh,),
        in_specs=[pl.BlockSpec((1, s, dh), lambda h: (h, 0, 0))],
        out_specs=pl.BlockSpec((1, MOBA_MAX_BLOCKS, dh), lambda h: (h, 0, 0)),
        out_shape=jax.ShapeDtypeStruct((nh, MOBA_MAX_BLOCKS, dh), F32),
        compiler_params=_cparams(("arbitrary",)),
    )(k_hsd)


def _moba_sel_kernel(q_ref, km_ref, o_ref, *, tq, n_sel):
    gate = lax.dot_general(km_ref[0], q_ref[0], _NT, preferred_element_type=F32,
                           precision=lax.Precision.HIGHEST)
    t = pl.program_id(1) * tq + lax.broadcasted_iota(I32, (MOBA_MAX_BLOCKS, tq), 1)
    b0 = t // MOBA_BLOCK
    n = lax.broadcasted_iota(I32, (MOBA_MAX_BLOCKS, tq), 0)
    avail = n < b0
    taken = n == b0
    for _ in range(n_sel):
        g = jnp.where(avail, gate, -jnp.inf)
        m = jnp.max(g, axis=0, keepdims=True)
        idx = jnp.min(jnp.where((g == m) & avail, n, 1 << 20), axis=0, keepdims=True)
        pick = n == idx
        taken = taken | pick
        avail = avail & jnp.logical_not(pick)
    o_ref[0] = jnp.where(taken, 0.0, NEG)


def _moba_select(q_hsd, kmean, tq=MOBA_SEL_QUERIES):
    nh, s, dh = q_hsd.shape
    nb = s // MOBA_BLOCK
    assert nb <= MOBA_MAX_BLOCKS
    n_sel = max(min(MOBA_TOPK, nb - 1), 1)
    tq = min(tq, s)
    return pl.pallas_call(
        functools.partial(_moba_sel_kernel, tq=tq, n_sel=n_sel),
        grid=(nh, s // tq),
        in_specs=[pl.BlockSpec((1, tq, dh), lambda h, i: (h, i, 0)),
                  pl.BlockSpec((1, MOBA_MAX_BLOCKS, dh), lambda h, i: (h, 0, 0))],
        out_specs=pl.BlockSpec((1, MOBA_MAX_BLOCKS, tq), lambda h, i: (h, 0, i)),
        out_shape=jax.ShapeDtypeStruct((nh, MOBA_MAX_BLOCKS, s), F32),
        compiler_params=_cparams(("arbitrary", "arbitrary")),
    )(q_hsd, kmean)


def _attn_kernel(qi_ref, kg_ref, *refs, nh, n_near, group, has_mask):
    q_ref, k_ref, vt_ref, bt_ref = refs[:4]
    mask_ref = refs[4] if has_mask else None
    o_ref, m_scr, acc_scr = refs[-3:]
    t = ATT_TILE
    step = pl.program_id(0)
    qi = qi_ref[step]
    kg = kg_ref[step]

    @pl.when(kg == 0)
    def _():
        m_scr[...] = jnp.full(m_scr.shape, NEG, F32)
        acc_scr[...] = jnp.zeros(acc_scr.shape, F32)

    def step_update(with_bias, n_tiles):
        m = [m_scr[h] for h in range(nh)]
        for g in range(n_tiles):
            keys = slice(g * t, (g + 1) * t)
            off = qi - (kg * group + g)
            bias_row = jnp.minimum(off, n_near)
            if has_mask:
                mb = mask_ref[0, g].astype(F32)
            scores = []
            for h in range(nh):
                s = lax.dot_general(k_ref[h, keys, :], q_ref[h], _NT, preferred_element_type=F32)
                if with_bias:
                    s = s + bt_ref[h, bias_row]
                if has_mask:
                    s = s + mb
                scores.append(s)
            probs, alphas = [], []
            for h in range(nh):
                m_new = jnp.maximum(m[h], jnp.max(scores[h], axis=0, keepdims=True))
                alpha = jnp.exp2(m[h] - m_new)
                p = jnp.exp2(scores[h] - m_new)
                m[h] = m_new
                probs.append(p.astype(BF16))
                alphas.append(alpha)
            for h in range(nh):
                acc_scr[h] = alphas[h] * acc_scr[h] + jnp.dot(vt_ref[h, :, keys], probs[h],
                                                              preferred_element_type=F32)
        for h in range(nh):
            m_scr[h] = m[h]

    all_far = qi - (kg * group + group - 1) >= n_near
    pl.when(all_far)(functools.partial(step_update, False, group))
    n_causal = jnp.minimum(qi - kg * group + 1, group)
    for n_tiles in range(1, group + 1):
        pl.when(jnp.logical_not(all_far) & (n_causal == n_tiles))(functools.partial(step_update, True, n_tiles))

    @pl.when(kg == qi // group)
    def _():
        for h in range(nh):
            o_ref[h] = acc_scr[h, :HEAD_DIM, :] / acc_scr[h, HEAD_DIM:HEAD_DIM + 1, :]


def _attention(q, k, vt, bias_tiles, head_group, mask_bias=None):
    nh, s, dk = q.shape
    vt = jnp.concatenate([vt, jnp.ones((nh, SUBLANES, s), vt.dtype)], axis=1)
    dv = vt.shape[1]
    t = ATT_TILE
    group = ATT_KEY_TILES_PER_STEP
    nq = s // t
    n_near = bias_tiles.shape[1] - 1
    qi_list = [i for i in range(nq) for _ in range(i // group + 1)]
    kg_list = [j for i in range(nq) for j in range(i // group + 1)]
    qi_arr = jnp.asarray(qi_list, I32)
    kg_arr = jnp.asarray(kg_list, I32)
    in_specs = [pl.BlockSpec((nh, t, dk), lambda st, qi, kg: (0, qi[st], 0)),
                pl.BlockSpec((nh, group * t, dk), lambda st, qi, kg: (0, kg[st], 0)),
                pl.BlockSpec((nh, dv, group * t), lambda st, qi, kg: (0, 0, kg[st])),
                pl.BlockSpec((nh, n_near + 1, t, t), lambda st, qi, kg: (head_group, 0, 0, 0),
                             pipeline_mode=pl.Buffered(1))]
    args = [q, k, vt, bias_tiles]
    if mask_bias is not None:
        in_specs.append(pl.BlockSpec((1, group, t, t), lambda st, qi, kg: (qi[st], kg[st], 0, 0)))
        args.append(mask_bias)
    grid_spec = pltpu.PrefetchScalarGridSpec(
        num_scalar_prefetch=2,
        grid=(len(qi_list),),
        in_specs=in_specs,
        out_specs=pl.BlockSpec((nh, HEAD_DIM, t), lambda st, qi, kg: (0, 0, qi[st])),
        scratch_shapes=[pltpu.VMEM((nh, 1, t), F32), pltpu.VMEM((nh, dv, t), F32)])
    return pl.pallas_call(
        functools.partial(_attn_kernel, nh=nh, n_near=n_near, group=group, has_mask=mask_bias is not None),
        grid_spec=grid_spec,
        out_shape=jax.ShapeDtypeStruct((nh, HEAD_DIM, s), F32),
        compiler_params=_cparams(("arbitrary",)),
    )(qi_arr, kg_arr, *args)


def _bitonic_stages(n, full_sort):
    levels = [2 ** p for p in range(1, n.bit_length())] if full_sort else [n]
    return [(k, k >> s) for k in levels for s in range(1, k.bit_length())]


def _plan_passes(stages, block):
    passes = []
    for k, j in stages:
        cross = j >= block
        if passes and passes[-1][2] == cross and (not cross or bin(passes[-1][0] | j).count("1") <= 3):
            passes[-1][0] |= j if cross else 0
            passes[-1][1].append((k, j))
        else:
            passes.append([j if cross else block - 1, [(k, j)], cross])
    return [(mask, sub) for mask, sub, _ in passes]


def _run_pass(load, store, n, mask, stages, regs=NETWORK_REGS):
    groups = {}
    for e in range(n):
        groups.setdefault(e & ~mask, []).append(e)
    groups = list(groups.values())
    per = max(1, regs // len(groups[0]))
    for g0 in range(0, len(groups), per):
        elems = sorted(e for grp in groups[g0:g0 + per] for e in grp)
        vals = {e: load(e) for e in elems}
        for k, j in stages:
            for e in elems:
                l = e ^ j
                if l > e:
                    hi = jnp.maximum(vals[e], vals[l])
                    lo = jnp.minimum(vals[e], vals[l])
                    vals[e], vals[l] = (hi, lo) if (e & k) == 0 else (lo, hi)
        for e in elems:
            store(e, vals[e])


def _sortable(x):
    return jnp.where(x == 0.0, 0.0, x)


def _dsa_sel_kernel(qi_ref, wt_ref, kn_ref, tri_ref, o_ref, keys_scr, seen_scr, work_scr, top_scr, *, tile, topk):
    n_chunks = kn_ref.shape[0]
    i = pl.program_id(0)
    n_live = i + 1
    groups_per_chunk = tile // SUBLANES
    causal = lax.broadcasted_iota(I32, (tile, tile), 0) <= lax.broadcasted_iota(I32, (tile, tile), 1)
    q_heads = [qi_ref[:, h * IDX_DIM:(h + 1) * IDX_DIM] for h in range(IDX_HEADS)]
    w_heads = [wt_ref[h:h + 1, :] for h in range(IDX_HEADS)]

    def score_chunk(c, diagonal):
        kc = kn_ref[c]
        dots = [lax.dot_general(kc, q_heads[h], _NT, preferred_element_type=F32) for h in range(IDX_HEADS)]
        sc = None
        for h in range(IDX_HEADS):
            term = w_heads[h] * jnp.maximum(dots[h], 0.0)
            sc = term if sc is None else sc + term
        if diagonal:
            sc = jnp.where(causal, sc, -jnp.inf)
        keys_scr[c] = _sortable(sc)

    def for_past_chunks(fn):
        def pair(p, carry):
            fn(2 * p)
            fn(2 * p + 1)
            return carry
        lax.fori_loop(0, i // 2, pair, 0)
        pl.when(i % 2 == 1)(lambda: fn(i - 1))

    for_past_chunks(lambda c: score_chunk(c, False))
    score_chunk(i, True)

    n_list = DSA_SORT_CHUNKS * groups_per_chunk
    n_batches = (n_live + DSA_SORT_CHUNKS - 1) // DSA_SORT_CHUNKS

    def pad_chunk(c, carry):
        keys_scr[c] = jnp.full((tile, tile), -jnp.inf, F32)
        return carry

    lax.fori_loop(n_live, n_batches * DSA_SORT_CHUNKS, pad_chunk, 0)
    top_scr[...] = jnp.full(top_scr.shape, -jnp.inf, F32)
    sort_passes = _plan_passes(_bitonic_stages(n_list, True), groups_per_chunk)
    merge_passes = _plan_passes(_bitonic_stages(n_list, False), groups_per_chunk)
    lane_halves = [slice(lo, lo + LANES) for lo in range(0, tile, LANES)]

    def run_network(passes, first_load, first_store, buf, lanes):
        def load(e):
            return buf[e, :, lanes]

        def store(e, v):
            buf[e, :, lanes] = v

        for n_pass, (mask, stages) in enumerate(passes):
            _run_pass(first_load if n_pass == 0 else load, first_store if n_pass == 0 else store,
                      n_list, mask, stages)

    def sort_batch(b, carry):
        for lanes in lane_halves:
            def load_keys(e, lanes=lanes):
                g = e % groups_per_chunk
                return keys_scr[b * DSA_SORT_CHUNKS + e // groups_per_chunk, SUBLANES * g:SUBLANES * (g + 1), lanes]

            def store_work(e, v, lanes=lanes):
                work_scr[e, :, lanes] = v

            run_network(sort_passes, load_keys, store_work, work_scr, lanes)

            def load_best(e, lanes=lanes):
                return jnp.maximum(top_scr[e, :, lanes], work_scr[n_list - 1 - e, :, lanes])

            def store_top(e, v, lanes=lanes):
                top_scr[e, :, lanes] = v

            run_network(merge_passes, load_best, store_top, top_scr, lanes)
        return carry

    lax.fori_loop(0, n_batches, sort_batch, 0)

    src, dst = top_scr, work_scr
    for shift in (4, 2, 1):
        for lanes in lane_halves:
            def load_pair(e, lanes=lanes, src=src, shift=shift):
                return jnp.maximum(src[e, :, lanes], pltpu.roll(src[n_list - 1 - e, :, lanes], shift, 0))

            def store_dst(e, v, lanes=lanes, dst=dst):
                dst[e, :, lanes] = v

            run_network(merge_passes, load_pair, store_dst, dst, lanes)
        src, dst = dst, src
    thr = src[topk - 1, 0:1, :]
    hits = [(src[e] > thr).astype(I32) for e in range(topk - 1)]
    while len(hits) > 1:
        hits = [hits[j] + hits[j + 1] if j + 1 < len(hits) else hits[j] for j in range(0, len(hits), 2)]
    n_gt = hits[0][0:1, :]
    need = (topk - n_gt).astype(F32)

    seen_scr[...] = jnp.zeros(seen_scr.shape, F32)

    def emit_chunk(c, diagonal):
        kc = keys_scr[c]
        eq = kc == thr
        eqf = eq.astype(F32)
        seen = seen_scr[...]
        seen_scr[...] = seen + jnp.sum(eqf, axis=0, keepdims=True)
        rank = seen + jnp.dot(tri_ref[...], eqf.astype(BF16), preferred_element_type=F32)
        sel = (kc > thr) | (eq & (rank < need))
        if diagonal:
            sel = sel & causal
        o_ref[0, c] = jnp.where(sel, 0.0, NEG).astype(BF16)

    for_past_chunks(lambda c: emit_chunk(c, False))
    emit_chunk(i, True)

    def fill_chunk(c, carry):
        o_ref[0, c] = jnp.full((tile, tile), NEG, BF16)
        return carry

    lax.fori_loop(n_live, n_chunks, fill_chunk, 0)


def _dsa_select(q_idx_bf16, w_idx, kn_bf16, topk):
    s = q_idx_bf16.shape[0]
    t = ATT_TILE
    n_chunks = s // t
    n_list = DSA_SORT_CHUNKS * (t // SUBLANES)
    assert topk <= n_list, "the sorting network keeps n_list best keys per query"
    kn3 = kn_bf16.reshape(n_chunks, t, IDX_DIM)
    wt = jnp.pad(w_idx.T, ((0, SUBLANES - IDX_HEADS), (0, 0)))
    tri = (lax.broadcasted_iota(I32, (t, t), 1) < lax.broadcasted_iota(I32, (t, t), 0)).astype(BF16)
    return pl.pallas_call(
        functools.partial(_dsa_sel_kernel, tile=t, topk=topk),
        grid=(n_chunks,),
        in_specs=[pl.BlockSpec((t, IDX_HEADS * IDX_DIM), lambda i: (i, 0)),
                  pl.BlockSpec((SUBLANES, t), lambda i: (0, i)),
                  pl.BlockSpec((n_chunks, t, IDX_DIM), lambda i: (0, 0, 0)),
                  pl.BlockSpec((t, t), lambda i: (0, 0))],
        out_specs=pl.BlockSpec((1, n_chunks, t, t), lambda i: (i, 0, 0, 0)),
        out_shape=jax.ShapeDtypeStruct((n_chunks, n_chunks, t, t), BF16),
        scratch_shapes=[pltpu.VMEM((max(n_chunks, DSA_SORT_CHUNKS), t, t), F32), pltpu.VMEM((1, t), F32),
                        pltpu.VMEM((n_list, SUBLANES, t), F32), pltpu.VMEM((n_list, SUBLANES, t), F32)],
        compiler_params=_cparams(("arbitrary",)),
    )(q_idx_bf16, wt, kn3, tri)


def _transpose_bf16(a_bf16):
    n = a_bf16.shape[1]
    eye = (lax.broadcasted_iota(I32, (n, n), 0) == lax.broadcasted_iota(I32, (n, n), 1)).astype(BF16)
    return lax.dot_general(eye, a_bf16, _NT, preferred_element_type=F32)


def _out_proj_kernel(ya_ref, yb_ref, w_ref, x_ref, g_ref, o_ref):
    ka = ya_ref.shape[0]
    ya = _transpose_bf16(ya_ref[...].astype(BF16)).astype(BF16)
    yb = _transpose_bf16(yb_ref[...].astype(BF16)).astype(BF16)
    mixed = (jnp.dot(ya, w_ref[:ka, :], preferred_element_type=F32)
             + jnp.dot(yb, w_ref[ka:, :], preferred_element_type=F32))
    o_ref[...] = x_ref[...] + g_ref[...] * mixed


def _out_proj(yt_a, yt_b, w_bf16, x, gate, tm=PROJ_ROWS):
    s, d = x.shape
    ka, kb = yt_a.shape[0], yt_b.shape[0]
    return pl.pallas_call(
        _out_proj_kernel,
        grid=(s // tm,),
        in_specs=[pl.BlockSpec((ka, tm), lambda i: (0, i)),
                  pl.BlockSpec((kb, tm), lambda i: (0, i)),
                  pl.BlockSpec((ka + kb, d), lambda i: (0, 0)),
                  pl.BlockSpec((tm, d), lambda i: (i, 0)),
                  pl.BlockSpec((1, d), lambda i: (0, 0))],
        out_specs=pl.BlockSpec((tm, d), lambda i: (i, 0)),
        out_shape=jax.ShapeDtypeStruct((s, d), F32),
        compiler_params=_cparams(("arbitrary",)),
    )(yt_a, yt_b, w_bf16, x, gate.reshape(1, d))


def _transpose_cast_kernel(v_ref, o_ref):
    o_ref[...] = _transpose_bf16(v_ref[0].astype(BF16)).astype(BF16)


def _transpose_cast(v, layer=0, tr=TRANSPOSE_ROWS):
    v = v if v.ndim == 3 else v[None]
    _, rows, d = v.shape
    return pl.pallas_call(
        _transpose_cast_kernel,
        grid=(rows // tr,),
        in_specs=[pl.BlockSpec((1, tr, d), lambda i: (layer, i, 0))],
        out_specs=pl.BlockSpec((d, tr), lambda i: (0, i)),
        out_shape=jax.ShapeDtypeStruct((d, rows), BF16),
        compiler_params=_cparams(("arbitrary",)),
    )(v)


def _cast_kernel(v_ref, o_ref):
    o_ref[...] = v_ref[0].astype(BF16)


def _cast_bf16(v, layer, tr=TRANSPOSE_ROWS):
    _, rows, d = v.shape
    return pl.pallas_call(
        _cast_kernel,
        grid=(rows // tr,),
        in_specs=[pl.BlockSpec((1, tr, d), lambda i: (layer, i, 0))],
        out_specs=pl.BlockSpec((tr, d), lambda i: (i, 0)),
        out_shape=jax.ShapeDtypeStruct((rows, d), BF16),
        compiler_params=_cparams(("arbitrary",)),
    )(v)


def _cmp_exchange(v, i, j):
    hi = jnp.maximum(v[i], v[j])
    lo = jnp.minimum(v[i], v[j])
    v[i], v[j] = hi, lo


def _bitonic_sort_desc(v):
    n = len(v)
    k = 2
    while k <= n:
        j = k // 2
        while j >= 1:
            for i in range(n):
                l = i ^ j
                if l > i:
                    if (i & k) == 0:
                        _cmp_exchange(v, i, l)
                    else:
                        _cmp_exchange(v, l, i)
            j //= 2
        k *= 2


def _bitonic_merge_desc(v):
    n = len(v)
    j = n // 2
    while j >= 1:
        for i in range(n):
            l = i ^ j
            if l > i:
                _cmp_exchange(v, i, l)
        j //= 2


def _top16_rows(x):
    tt = x.shape[1]
    v = [x[SUBLANES * i:SUBLANES * (i + 1), :] for i in range(PEER_NKEYS // SUBLANES)]
    _bitonic_sort_desc(v)
    for shift in (4, 2, 1):
        other = [pltpu.roll(a, shift, 0) for a in v]
        v = [jnp.maximum(v[i], other[len(v) - 1 - i]) for i in range(len(v))]
        _bitonic_merge_desc(v)
    return v


def _peer_route_kernel(q_ref, sk_ref, r1_ref, e1_ref, e0_ref, ns_ref, s0_scr, s1_scr, *, tt):
    assert PEER_HEADS == SUBLANES
    sub = lax.broadcasted_iota(I32, (SUBLANES, tt), 0)
    tops = [[], []]
    for h in range(PEER_HEADS):
        for p in range(2):
            lo = (2 * h + p) * (PEER_DKEY // 2)
            qhp = q_ref[:, lo:lo + PEER_DKEY // 2].astype(BF16)
            sc = lax.dot_general(sk_ref[2 * h + p], qhp, _NT, preferred_element_type=F32)
            (s0_scr if p == 0 else s1_scr)[h] = sc
            tops[p].append(_top16_rows(sc))
    a = [sum(jnp.where(sub == h, tops[0][h][k], 0.0) for h in range(PEER_HEADS)) for k in range(PEER_TOPK)]
    b = [sum(jnp.where(sub == h, tops[1][h][k], 0.0) for h in range(PEER_HEADS)) for k in range(PEER_TOPK)]
    cand = [a[i] + b[j] for i in range(PEER_TOPK) for j in range(PEER_TOPK) if (i + 1) * (j + 1) <= PEER_TOPK]
    cand = cand + [jnp.full((SUBLANES, tt), -jnp.inf, F32)] * (64 - len(cand))
    _bitonic_sort_desc(cand)
    tau = cand[PEER_TOPK - 1]
    z = sum(jnp.exp(cand[k] - cand[0]) for k in range(PEER_TOPK))
    half_inv_z = 0.5 / z
    nsel_of_rank = [sum(jnp.where(a[r] + b[k] >= tau, 1.0, 0.0) for k in range(PEER_TOPK)) for r in range(PEER_TOPK)]
    for h in range(PEER_HEADS):
        s0 = s0_scr[h]
        s1 = s1_scr[h]
        nsel = jnp.zeros((PEER_NKEYS, tt), F32)
        rank1 = jnp.full((PEER_NKEYS, tt), float(PEER_TOPK), F32)
        for r in reversed(range(PEER_TOPK)):
            nsel = jnp.where(s0 == a[r][h:h + 1, :], nsel_of_rank[r][h:h + 1, :], nsel)
            rank1 = jnp.where(s1 == b[r][h:h + 1, :], float(r), rank1)
        ns_ref[h] = nsel
        r1_ref[h] = rank1.astype(BF16)
        e0_ref[h] = jnp.exp(s0 - a[0][h:h + 1, :]) * half_inv_z[h:h + 1, :]
        e1_ref[h] = jnp.exp(s1 - b[0][h:h + 1, :]).astype(BF16)


def _peer_route(q, subkeys_bf16, tt=PEER_ROUTE_TOKENS):
    s = q.shape[0]
    big = lambda dt: jax.ShapeDtypeStruct((PEER_HEADS, PEER_NKEYS, s), dt)
    big_spec = pl.BlockSpec((PEER_HEADS, PEER_NKEYS, tt), lambda i: (0, 0, i))
    return pl.pallas_call(
        functools.partial(_peer_route_kernel, tt=tt),
        grid=(s // tt,),
        in_specs=[pl.BlockSpec((tt, PEER_HEADS * PEER_DKEY), lambda i: (i, 0)),
                  pl.BlockSpec((2 * PEER_HEADS, PEER_NKEYS, PEER_DKEY // 2), lambda i: (0, 0, 0))],
        out_specs=[big_spec, big_spec, big_spec, big_spec],
        out_shape=[big(BF16), big(BF16), big(F32), big(F32)],
        scratch_shapes=[pltpu.VMEM((PEER_HEADS, PEER_NKEYS, tt), F32),
                        pltpu.VMEM((PEER_HEADS, PEER_NKEYS, tt), F32)],
        compiler_params=_cparams(("arbitrary",)),
    )(q, subkeys_bf16)


def _peer_dense_kernel(ht_ref, u_ref, vt_ref, r1_ref, e1_ref, e0_ref, ns_ref,
                       x_ref, g_ref, gf_ref, o_ref, acc_scr, p_scr, *, rows_per_step, final_norm):
    k = pl.program_id(1)

    @pl.when(k == 0)
    def _():
        acc_scr[...] = jnp.zeros(acc_scr.shape, F32)

    tt = ht_ref.shape[1]
    pack_rows = BF16_PACK_ROWS
    zero = jnp.zeros((pack_rows, tt), BF16)

    def activations(c):
        return jnp.dot(u_ref[c * MXU_DEPTH:(c + 1) * MXU_DEPTH, :], ht_ref[...], preferred_element_type=F32)

    def expert_rows(ii, a):
        i = k * rows_per_step + ii
        def bcast(ref, h):
            half = jnp.broadcast_to(ref[h, pl.ds(i, 1), :], (pack_rows // 2, tt))
            return jnp.concatenate([half, half], axis=0).astype(BF16)
        e0_rows = [bcast(e0_ref, h) for h in range(PEER_HEADS)]
        ns_rows = [bcast(ns_ref, h) for h in range(PEER_HEADS)]
        gelu2 = (a * (1.0 + lax.erf(a * math.sqrt(0.5)))).astype(BF16)
        for g in range(PEER_NKEYS // pack_rows):
            rows = slice(g * pack_rows, (g + 1) * pack_rows)
            wgt = None
            for h in range(PEER_HEADS):
                term = jnp.where(r1_ref[h, rows, :] < ns_rows[h], e1_ref[h, rows, :], zero) * e0_rows[h]
                wgt = term if wgt is None else wgt + term
            p_scr[ii * PEER_NKEYS + g * pack_rows:ii * PEER_NKEYS + (g + 1) * pack_rows, :] = wgt * gelu2[rows, :]

    depth_rows = MXU_DEPTH // PEER_NKEYS
    n_chunks = rows_per_step // depth_rows
    act = activations(0)
    for c in range(n_chunks):
        act_next = activations(c + 1) if c + 1 < n_chunks else None
        for j in range(depth_rows):
            expert_rows(c * depth_rows + j, act[j * PEER_NKEYS:(j + 1) * PEER_NKEYS, :])
        act = act_next
    acc_scr[...] += jnp.dot(vt_ref[...], p_scr[...], preferred_element_type=F32)

    @pl.when(k == pl.num_programs(1) - 1)
    def _():
        y = x_ref[...] + g_ref[...] * acc_scr[...].T
        if final_norm:
            y = y * lax.rsqrt(jnp.mean(y * y, axis=-1, keepdims=True) + EPS) * gf_ref[...]
        o_ref[...] = y


def _peer_dense(ht_bf16, u_bf16, vt_bf16, rank1, e1, e0, nsel, x, gate, g_final, final_norm,
                tt=PEER_DENSE_TOKENS, rows_per_step=PEER_DENSE_KEY_ROWS):
    s, d = x.shape
    n_exp = u_bf16.shape[0]
    et = rows_per_step * PEER_NKEYS
    tok = lambda j, k: (0, 0, j)
    big_spec = pl.BlockSpec((PEER_HEADS, PEER_NKEYS, tt), tok)
    return pl.pallas_call(
        functools.partial(_peer_dense_kernel, rows_per_step=rows_per_step, final_norm=final_norm),
        grid=(s // tt, n_exp // et),
        in_specs=[pl.BlockSpec((d, tt), lambda j, k: (0, j)),
                  pl.BlockSpec((et, d), lambda j, k: (k, 0)),
                  pl.BlockSpec((d, et), lambda j, k: (0, k)),
                  big_spec, big_spec, big_spec, big_spec,
                  pl.BlockSpec((tt, d), lambda j, k: (j, 0)),
                  pl.BlockSpec((1, d), lambda j, k: (0, 0)),
                  pl.BlockSpec((1, d), lambda j, k: (0, 0))],
        out_specs=pl.BlockSpec((tt, d), lambda j, k: (j, 0)),
        out_shape=jax.ShapeDtypeStruct((s, d), F32),
        scratch_shapes=[pltpu.VMEM((d, tt), F32), pltpu.VMEM((et, tt), BF16)],
        compiler_params=_cparams(("arbitrary", "arbitrary")),
    )(ht_bf16, u_bf16, vt_bf16, rank1, e1, e0, nsel, x, gate.reshape(1, d), g_final.reshape(1, d))


def _heads_first(t, nh):
    s = t.shape[0]
    return jnp.transpose(t.reshape(s, nh, HEAD_DIM), (1, 0, 2))


def _mixer(h_proj, w_out_bf16, bias_tiles, x, gate):
    s = h_proj.shape[0]
    parts = []
    off = 0
    for size in PROJ_SIZES:
        parts.append(h_proj[:, off:off + size])
        off += size
    mq, mk, mv, dq, dk, dv, iq, ik, iw = parts
    scale = HEAD_DIM ** -0.5 * LOG2E
    dims_first = lambda a, nh: a.T.reshape(nh, HEAD_DIM, s).astype(BF16)

    mq_h, mk_h = _heads_first(mq, MOBA_HEADS), _heads_first(mk, MOBA_HEADS)
    block_sel = _moba_select(mq_h, _moba_kmean(mk_h))
    blk = jnp.arange(s, dtype=I32) // MOBA_BLOCK
    onehot = (blk[:, None] == jnp.arange(MOBA_MAX_BLOCKS, dtype=I32)[None, :]).astype(BF16)
    q_aug = jnp.concatenate([(mq_h * scale).astype(BF16), jnp.transpose(block_sel, (0, 2, 1)).astype(BF16)], axis=-1)
    k_aug = jnp.concatenate([mk_h.astype(BF16), jnp.broadcast_to(onehot, (MOBA_HEADS, s, MOBA_MAX_BLOCKS))], axis=-1)
    y_moba = _attention(q_aug, k_aug, dims_first(mv, MOBA_HEADS), bias_tiles, 0)

    ikf = ik.astype(F32)
    ik_n = ikf * lax.rsqrt(jnp.mean(ikf * ikf, axis=-1, keepdims=True) + EPS)
    topk = min(DSA_TOPK_MAX, s // 4)
    mask_bias = _dsa_select(iq.astype(BF16), iw * IDX_HEADS ** -0.5, ik_n.astype(BF16), topk)
    dq_h, dk_h = _heads_first(dq, DSA_HEADS), _heads_first(dk, DSA_HEADS)
    y_dsa = _attention((dq_h * scale).astype(BF16), dk_h.astype(BF16), dims_first(dv, DSA_HEADS),
                       bias_tiles, 1, mask_bias=mask_bias)

    return _out_proj(y_moba.reshape(MOBA_WIDTH, s), y_dsa.reshape(DSA_WIDTH, s), w_out_bf16, x, gate)


def kernel(x, c, w_ada, b_ada, norm_attn, norm_ffn, w_in, w_out, rel_bias,
           peer_wq, peer_subkeys, peer_u, peer_v, norm_final):
    batch, s, d = x.shape
    assert batch == 1 and s % (ATT_KEY_TILES_PER_STEP * ATT_TILE) == 0
    depth = w_ada.shape[0]
    x = x[0]
    mod = _ada_mod(c, w_ada, b_ada)
    bias_tiles = _bias_tiles(rel_bias, ATT_TILE)
    n_pad = -(-PROJ_WIDTH // LANES) * LANES
    for l in range(depth):
        sh1, sc1, g1, sh2, sc2, g2 = jnp.split(mod[l], 6)
        w_in_l = jnp.pad(w_in[l], ((0, 0), (0, n_pad - PROJ_WIDTH))).astype(BF16)
        _, proj = _norm_mod_matmul(x, norm_attn[l], sh1, sc1, w_in_l)
        x = _mixer(proj, w_out[l].astype(BF16), bias_tiles, x, g1)
        h2, q = _norm_mod_matmul(x, norm_ffn[l], sh2, sc2, peer_wq[l].astype(BF16))
        subkeys = peer_subkeys[l].reshape(2 * PEER_HEADS, PEER_NKEYS, PEER_DKEY // 2).astype(BF16)
        rank1, e1, e0, nsel = _peer_route(q, subkeys)
        x = _peer_dense(_transpose_cast(h2), _cast_bf16(peer_u, l), _transpose_cast(peer_v, l), rank1, e1, e0, nsel,
                        x, g2, norm_final, final_norm=(l == depth - 1))
    return x[None]
```

```python
import functools
import math

import jax
import jax.numpy as jnp
from jax import lax
from jax.experimental import pallas as pl
from jax.experimental.pallas import tpu as pltpu

F32 = jnp.float32
BF16 = jnp.bfloat16
I32 = jnp.int32

HEAD_DIM = 64
MOBA_HEADS = 8
DSA_HEADS = 8
N_HEADS = MOBA_HEADS + DSA_HEADS
MOBA_WIDTH = MOBA_HEADS * HEAD_DIM
DSA_WIDTH = DSA_HEADS * HEAD_DIM
MOBA_BLOCK = 256
MOBA_TOPK = 3
MOBA_MAX_BLOCKS = 64
IDX_HEADS = 4
IDX_DIM = 64
DSA_TOPK_MAX = 256
DSA_SORT_CHUNKS = 8
N_BUCKETS = 32
MAX_DISTANCE = 4096
PEER_HEADS = 8
PEER_NKEYS = 128
PEER_DKEY = 256
PEER_TOPK = 16
EPS = 1e-6
PROJ_SIZES = (MOBA_WIDTH, MOBA_WIDTH, MOBA_WIDTH, DSA_WIDTH, DSA_WIDTH, DSA_WIDTH,
              IDX_HEADS * IDX_DIM, IDX_DIM, IDX_HEADS)
PROJ_WIDTH = sum(PROJ_SIZES)
IDX_F32_START = sum(PROJ_SIZES[:7])

NEG = -1e30
LOG2E = math.log2(math.e)
ATT_TILE = MOBA_BLOCK
ATT_KEY_TILES_PER_STEP = 4
LANES = 128
SUBLANES = 8
BF16_PACK_ROWS = 2 * SUBLANES
MXU_DEPTH = 256
VMEM_LIMIT = 56 * 1024 * 1024
NETWORK_REGS = 32

ADA_COLS = 1536
PROJ_ROWS = 512
MOBA_SEL_QUERIES = 2048
PEER_ROUTE_TOKENS = 256
PEER_DENSE_TOKENS = 512
PEER_DENSE_KEY_ROWS = 16
TRANSPOSE_ROWS = 1024

_NT = (((1,), (1,)), ((), ()))


def _far_bias_tiles(tile):
    max_exact = N_BUCKETS // 2
    sat = int(math.ceil(max_exact * (MAX_DISTANCE / max_exact) ** ((N_BUCKETS - max_exact - 1) / (N_BUCKETS - max_exact)))) + 4
    return -(-(sat + tile - 1) // tile)


def _cparams(sem):
    return pltpu.CompilerParams(dimension_semantics=sem, vmem_limit_bytes=VMEM_LIMIT)


def _ada_kernel(c_ref, w_ref, b_ref, o_ref):
    c = c_ref[...]
    sc = c * jax.nn.sigmoid(c)
    o_ref[0] = jnp.dot(sc, w_ref[0], preferred_element_type=F32,
                       precision=lax.Precision.HIGHEST) + b_ref[0]


def _ada_mod(c, w_ada, b_ada):
    depth, d, n = w_ada.shape
    tn = ADA_COLS
    c8 = jnp.broadcast_to(c, (SUBLANES, d))
    out = pl.pallas_call(
        _ada_kernel,
        grid=(depth, n // tn),
        in_specs=[pl.BlockSpec((SUBLANES, d), lambda l, j: (0, 0)),
                  pl.BlockSpec((1, d, tn), lambda l, j: (l, 0, j)),
                  pl.BlockSpec((1, 1, tn), lambda l, j: (l, 0, j))],
        out_specs=pl.BlockSpec((1, SUBLANES, tn), lambda l, j: (l, 0, j)),
        out_shape=jax.ShapeDtypeStruct((depth, SUBLANES, n), F32),
        compiler_params=_cparams(("arbitrary", "arbitrary")),
    )(c8, w_ada, b_ada.reshape(depth, 1, n))
    return out[:, 0, :]


def _nmm_kernel(x_ref, g_ref, sh_ref, sc_ref, w_ref, h_ref, o_ref):
    x = x_ref[...]
    y = x * lax.rsqrt(jnp.mean(x * x, axis=-1, keepdims=True) + EPS) * g_ref[...]
    hb = (y * (1.0 + sc_ref[...]) + sh_ref[...]).astype(BF16)
    h_ref[...] = hb
    o_ref[...] = jnp.dot(hb, w_ref[...], preferred_element_type=F32)


def _norm_mod_matmul(x, g, shift, scale, w_bf16, tm=PROJ_ROWS):
    s, d = x.shape
    n = w_bf16.shape[1]
    row = lambda i: (0, 0)
    return pl.pallas_call(
        _nmm_kernel,
        grid=(s // tm,),
        in_specs=[pl.BlockSpec((tm, d), lambda i: (i, 0)),
                  pl.BlockSpec((1, d), row), pl.BlockSpec((1, d), row), pl.BlockSpec((1, d), row),
                  pl.BlockSpec((d, n), row)],
        out_specs=[pl.BlockSpec((tm, d), lambda i: (i, 0)),
                   pl.BlockSpec((tm, n), lambda i: (i, 0))],
        out_shape=[jax.ShapeDtypeStruct((s, d), BF16), jax.ShapeDtypeStruct((s, n), F32)],
        compiler_params=_cparams(("arbitrary",)),
    )(x, g.reshape(1, d), shift.reshape(1, d), scale.reshape(1, d), w_bf16)


def _mixer_proj_kernel(x_ref, g_ref, sh_ref, sc_ref, w_ref, cs_ref, ob_ref, oq_ref, oi_ref):
    x = x_ref[...]
    y = x * lax.rsqrt(jnp.mean(x * x, axis=-1, keepdims=True) + EPS) * g_ref[...]
    hb = (y * (1.0 + sc_ref[...]) + sh_ref[...]).astype(BF16)
    p = jnp.dot(hb, w_ref[...], preferred_element_type=F32)
    ob_ref[...] = (p * cs_ref[...]).astype(BF16)
    oq_ref[...] = p[:, :2 * MOBA_WIDTH]
    oi_ref[...] = p[:, IDX_F32_START:IDX_F32_START + LANES]


def _mixer_proj(x, g, shift, scale, w_bf16, col_scale, tm=PROJ_ROWS):
    s, d = x.shape
    n = w_bf16.shape[1]
    row = lambda i: (0, 0)
    rows = lambda i: (i, 0)
    return pl.pallas_call(
        _mixer_proj_kernel,
        grid=(s // tm,),
        in_specs=[pl.BlockSpec((tm, d), rows),
                  pl.BlockSpec((1, d), row), pl.BlockSpec((1, d), row), pl.BlockSpec((1, d), row),
                  pl.BlockSpec((d, n), row), pl.BlockSpec((1, n), row)],
        out_specs=[pl.BlockSpec((tm, n), rows), pl.BlockSpec((tm, 2 * MOBA_WIDTH), rows),
                   pl.BlockSpec((tm, LANES), rows)],
        out_shape=[jax.ShapeDtypeStruct((s, n), BF16), jax.ShapeDtypeStruct((s, 2 * MOBA_WIDTH), F32),
                   jax.ShapeDtypeStruct((s, LANES), F32)],
        compiler_params=_cparams(("arbitrary",)),
    )(x, g.reshape(1, d), shift.reshape(1, d), scale.reshape(1, d), w_bf16, col_scale.reshape(1, n))


def _t5_bucket(d):
    max_exact = N_BUCKETS // 2
    d = jnp.maximum(d, 0)
    df = jnp.maximum(d, 1).astype(F32)
    large = max_exact + (jnp.log(df / max_exact) / math.log(MAX_DISTANCE / max_exact)
                         * (N_BUCKETS - max_exact)).astype(I32)
    return jnp.where(d < max_exact, d, jnp.minimum(large, N_BUCKETS - 1))


def _bias_tile_kernel(tab_ref, o_ref, *, tile):
    h = pl.program_id(0)
    off = pl.program_id(1)
    b = lax.broadcasted_iota(I32, (tile, tile), 0)
    a = lax.broadcasted_iota(I32, (tile, tile), 1)
    d = off * tile + a - b
    bucket = _t5_bucket(d)
    val = lax.fori_loop(jnp.min(bucket), jnp.max(bucket) + 1,
                        lambda k, val: jnp.where(bucket == k, tab_ref[h, k], val),
                        jnp.zeros((tile, tile), F32))
    val = (val - tab_ref[h, N_BUCKETS - 1]) * LOG2E
    o_ref[0, 0] = jnp.where(d < 0, NEG, val)


def _bias_tiles(rel_bias, tile):
    n_near = _far_bias_tiles(tile)
    nh = rel_bias.shape[0]
    return pl.pallas_call(
        functools.partial(_bias_tile_kernel, tile=tile),
        grid=(nh, n_near + 1),
        in_specs=[pl.BlockSpec(memory_space=pltpu.SMEM)],
        out_specs=pl.BlockSpec((1, 1, tile, tile), lambda h, o: (h, o, 0, 0)),
        out_shape=jax.ShapeDtypeStruct((nh, n_near + 1, tile, tile), F32),
        compiler_params=_cparams(("arbitrary", "arbitrary")),
    )(rel_bias)


def _kmean_kernel(k_ref, o_ref, *, nb):
    k = k_ref[0]
    km = jnp.mean(k.reshape(nb, MOBA_BLOCK, HEAD_DIM), axis=1)
    o_ref[0] = jnp.zeros(o_ref.shape[1:], F32)
    o_ref[0, :nb, :] = km


def _moba_kmean(k_hsd):
    nh, s, dh = k_hsd.shape
    nb = s // MOBA_BLOCK
    return pl.pallas_call(
        functools.partial(_kmean_kernel, nb=nb),
        grid=(nh,),
        in_specs=[pl.BlockSpec((1, s, dh), lambda h: (h, 0, 0))],
        out_specs=pl.BlockSpec((1, MOBA_MAX_BLOCKS, dh), lambda h: (h, 0, 0)),
        out_shape=jax.ShapeDtypeStruct((nh, MOBA_MAX_BLOCKS, dh), F32),
        compiler_params=_cparams(("arbitrary",)),
    )(k_hsd)


def _moba_sel_kernel(q_ref, km_ref, o_ref, *, tq, n_sel):
    gate = lax.dot_general(km_ref[0], q_ref[0], _NT, preferred_element_type=F32,
                           precision=lax.Precision.HIGHEST)
    t = pl.program_id(1) * tq + lax.broadcasted_iota(I32, (MOBA_MAX_BLOCKS, tq), 1)
    b0 = t // MOBA_BLOCK
    n = lax.broadcasted_iota(I32, (MOBA_MAX_BLOCKS, tq), 0)
    avail = n < b0
    taken = n == b0
    for _ in range(n_sel):
        g = jnp.where(avail, gate, -jnp.inf)
        m = jnp.max(g, axis=0, keepdims=True)
        idx = jnp.min(jnp.where((g == m) & avail, n, 1 << 20), axis=0, keepdims=True)
        pick = n == idx
        taken = taken | pick
        avail = avail & jnp.logical_not(pick)
    o_ref[0] = jnp.where(taken, 0.0, NEG)


def _moba_select(q_hsd, kmean, tq=MOBA_SEL_QUERIES):
    nh, s, dh = q_hsd.shape
    nb = s // MOBA_BLOCK
    assert nb <= MOBA_MAX_BLOCKS
    n_sel = max(min(MOBA_TOPK, nb - 1), 1)
    tq = min(tq, s)
    return pl.pallas_call(
        functools.partial(_moba_sel_kernel, tq=tq, n_sel=n_sel),
        grid=(nh, s // tq),
        in_specs=[pl.BlockSpec((1, tq, dh), lambda h, i: (h, i, 0)),
                  pl.BlockSpec((1, MOBA_MAX_BLOCKS, dh), lambda h, i: (h, 0, 0))],
        out_specs=pl.BlockSpec((1, MOBA_MAX_BLOCKS, tq), lambda h, i: (h, 0, i)),
        out_shape=jax.ShapeDtypeStruct((nh, MOBA_MAX_BLOCKS, s), F32),
        compiler_params=_cparams(("arbitrary", "arbitrary")),
    )(q_hsd, kmean)


def _attn_kernel(qi_ref, kg_ref, *refs, nh, n_near, group, has_mask):
    q_ref, k_ref, vt_ref, bt_ref = refs[:4]
    mask_ref = refs[4] if has_mask else None
    o_ref, m_scr, acc_scr = refs[-3:]
    t = ATT_TILE
    step = pl.program_id(0)
    qi = qi_ref[step]
    kg = kg_ref[step]

    @pl.when(kg == 0)
    def _():
        m_scr[...] = jnp.full(m_scr.shape, NEG, F32)
        acc_scr[...] = jnp.zeros(acc_scr.shape, F32)

    def step_update(with_bias, n_tiles):
        m = [m_scr[h] for h in range(nh)]
        for g in range(n_tiles):
            keys = slice(g * t, (g + 1) * t)
            off = qi - (kg * group + g)
            bias_row = jnp.minimum(off, n_near)
            if has_mask:
                mb = mask_ref[0, g].astype(F32)
            scores = []
            for h in range(nh):
                s = lax.dot_general(k_ref[h, keys, :], q_ref[h], _NT, preferred_element_type=F32)
                if with_bias:
                    s = s + bt_ref[h, bias_row]
                if has_mask:
                    s = s + mb
                scores.append(s)
            probs, alphas = [], []
            for h in range(nh):
                m_new = jnp.maximum(m[h], jnp.max(scores[h], axis=0, keepdims=True))
                alpha = jnp.exp2(m[h] - m_new)
                p = jnp.exp2(scores[h] - m_new)
                m[h] = m_new
                probs.append(p.astype(BF16))
                alphas.append(alpha)
            for h in range(nh):
                acc_scr[h] = alphas[h] * acc_scr[h] + jnp.dot(vt_ref[h, :, keys], probs[h],
                                                              preferred_element_type=F32)
        for h in range(nh):
            m_scr[h] = m[h]

    all_far = qi - (kg * group + group - 1) >= n_near
    pl.when(all_far)(functools.partial(step_update, False, group))
    n_causal = jnp.minimum(qi - kg * group + 1, group)
    for n_tiles in range(1, group + 1):
        pl.when(jnp.logical_not(all_far) & (n_causal == n_tiles))(functools.partial(step_update, True, n_tiles))

    @pl.when(kg == qi // group)
    def _():
        for h in range(nh):
            o_ref[h] = acc_scr[h, :HEAD_DIM, :] / acc_scr[h, HEAD_DIM:HEAD_DIM + 1, :]


def _attention(q, k, vt, bias_tiles, head_group, mask_bias=None):
    nh, s, dk = q.shape
    vt = jnp.concatenate([vt, jnp.ones((nh, SUBLANES, s), vt.dtype)], axis=1)
    dv = vt.shape[1]
    t = ATT_TILE
    group = ATT_KEY_TILES_PER_STEP
    nq = s // t
    n_near = bias_tiles.shape[1] - 1
    qi_list = [i for i in range(nq) for _ in range(i // group + 1)]
    kg_list = [j for i in range(nq) for j in range(i // group + 1)]
    qi_arr = jnp.asarray(qi_list, I32)
    kg_arr = jnp.asarray(kg_list, I32)
    in_specs = [pl.BlockSpec((nh, t, dk), lambda st, qi, kg: (0, qi[st], 0)),
                pl.BlockSpec((nh, group * t, dk), lambda st, qi, kg: (0, kg[st], 0)),
                pl.BlockSpec((nh, dv, group * t), lambda st, qi, kg: (0, 0, kg[st])),
                pl.BlockSpec((nh, n_near + 1, t, t), lambda st, qi, kg: (head_group, 0, 0, 0),
                             pipeline_mode=pl.Buffered(1))]
    args = [q, k, vt, bias_tiles]
    if mask_bias is not None:
        in_specs.append(pl.BlockSpec((1, group, t, t), lambda st, qi, kg: (qi[st], kg[st], 0, 0)))
        args.append(mask_bias)
    grid_spec = pltpu.PrefetchScalarGridSpec(
        num_scalar_prefetch=2,
        grid=(len(qi_list),),
        in_specs=in_specs,
        out_specs=pl.BlockSpec((nh, HEAD_DIM, t), lambda st, qi, kg: (0, 0, qi[st])),
        scratch_shapes=[pltpu.VMEM((nh, 1, t), F32), pltpu.VMEM((nh, dv, t), F32)])
    return pl.pallas_call(
        functools.partial(_attn_kernel, nh=nh, n_near=n_near, group=group, has_mask=mask_bias is not None),
        grid_spec=grid_spec,
        out_shape=jax.ShapeDtypeStruct((nh, HEAD_DIM, s), F32),
        compiler_params=_cparams(("arbitrary",)),
    )(qi_arr, kg_arr, *args)


def _bitonic_stages(n, full_sort):
    levels = [2 ** p for p in range(1, n.bit_length())] if full_sort else [n]
    return [(k, k >> s) for k in levels for s in range(1, k.bit_length())]


def _plan_passes(stages, block):
    passes = []
    for k, j in stages:
        cross = j >= block
        if passes and passes[-1][2] == cross and (not cross or bin(passes[-1][0] | j).count("1") <= 3):
            passes[-1][0] |= j if cross else 0
            passes[-1][1].append((k, j))
        else:
            passes.append([j if cross else block - 1, [(k, j)], cross])
    return [(mask, sub) for mask, sub, _ in passes]


def _run_pass(load, store, n, mask, stages, regs=NETWORK_REGS):
    groups = {}
    for e in range(n):
        groups.setdefault(e & ~mask, []).append(e)
    groups = list(groups.values())
    per = max(1, regs // len(groups[0]))
    for g0 in range(0, len(groups), per):
        elems = sorted(e for grp in groups[g0:g0 + per] for e in grp)
        vals = {e: load(e) for e in elems}
        for k, j in stages:
            for e in elems:
                l = e ^ j
                if l > e:
                    hi = jnp.maximum(vals[e], vals[l])
                    lo = jnp.minimum(vals[e], vals[l])
                    vals[e], vals[l] = (hi, lo) if (e & k) == 0 else (lo, hi)
        for e in elems:
            store(e, vals[e])


def _sortable(x):
    return jnp.where(x == 0.0, 0.0, x)


def _dsa_sel_kernel(qi_ref, wt_ref, kn_ref, tri_ref, o_ref, keys_scr, seen_scr, work_scr, top_scr, *, tile, topk):
    n_chunks = kn_ref.shape[0]
    i = pl.program_id(0)
    n_live = i + 1
    groups_per_chunk = tile // SUBLANES
    causal = lax.broadcasted_iota(I32, (tile, tile), 0) <= lax.broadcasted_iota(I32, (tile, tile), 1)
    q_heads = [qi_ref[:, h * IDX_DIM:(h + 1) * IDX_DIM] for h in range(IDX_HEADS)]
    w_heads = [wt_ref[h:h + 1, :] for h in range(IDX_HEADS)]

    def score_chunk(c, diagonal):
        kc = kn_ref[c]
        dots = [lax.dot_general(kc, q_heads[h], _NT, preferred_element_type=F32) for h in range(IDX_HEADS)]
        sc = None
        for h in range(IDX_HEADS):
            term = w_heads[h] * jnp.maximum(dots[h], 0.0)
            sc = term if sc is None else sc + term
        if diagonal:
            sc = jnp.where(causal, sc, -jnp.inf)
        keys_scr[c] = _sortable(sc)

    def for_past_chunks(fn):
        def pair(p, carry):
            fn(2 * p)
            fn(2 * p + 1)
            return carry
        lax.fori_loop(0, i // 2, pair, 0)
        pl.when(i % 2 == 1)(lambda: fn(i - 1))

    for_past_chunks(lambda c: score_chunk(c, False))
    score_chunk(i, True)

    n_list = DSA_SORT_CHUNKS * groups_per_chunk
    n_batches = (n_live + DSA_SORT_CHUNKS - 1) // DSA_SORT_CHUNKS

    def pad_chunk(c, carry):
        keys_scr[c] = jnp.full((tile, tile), -jnp.inf, F32)
        return carry

    lax.fori_loop(n_live, n_batches * DSA_SORT_CHUNKS, pad_chunk, 0)
    top_scr[...] = jnp.full(top_scr.shape, -jnp.inf, F32)
    sort_passes = _plan_passes(_bitonic_stages(n_list, True), groups_per_chunk)
    merge_passes = _plan_passes(_bitonic_stages(n_list, False), groups_per_chunk)
    lane_halves = [slice(lo, lo + LANES) for lo in range(0, tile, LANES)]

    def run_network(passes, first_load, first_store, buf, lanes):
        def load(e):
            return buf[e, :, lanes]

        def store(e, v):
            buf[e, :, lanes] = v

        for n_pass, (mask, stages) in enumerate(passes):
            _run_pass(first_load if n_pass == 0 else load, first_store if n_pass == 0 else store,
                      n_list, mask, stages)

    def sort_batch(b, carry):
        for lanes in lane_halves:
            def load_keys(e, lanes=lanes):
                g = e % groups_per_chunk
                return keys_scr[b * DSA_SORT_CHUNKS + e // groups_per_chunk, SUBLANES * g:SUBLANES * (g + 1), lanes]

            def store_work(e, v, lanes=lanes):
                work_scr[e, :, lanes] = v

            run_network(sort_passes, load_keys, store_work, work_scr, lanes)

            def load_best(e, lanes=lanes):
                return jnp.maximum(top_scr[e, :, lanes], work_scr[n_list - 1 - e, :, lanes])

            def store_top(e, v, lanes=lanes):
                top_scr[e, :, lanes] = v

            run_network(merge_passes, load_best, store_top, top_scr, lanes)
        return carry

    lax.fori_loop(0, n_batches, sort_batch, 0)

    src, dst = top_scr, work_scr
    for shift in (4, 2, 1):
        for lanes in lane_halves:
            def load_pair(e, lanes=lanes, src=src, shift=shift):
                return jnp.maximum(src[e, :, lanes], pltpu.roll(src[n_list - 1 - e, :, lanes], shift, 0))

            def store_dst(e, v, lanes=lanes, dst=dst):
                dst[e, :, lanes] = v

            run_network(merge_passes, load_pair, store_dst, dst, lanes)
        src, dst = dst, src
    thr = src[topk - 1, 0:1, :]
    hits = [(src[e] > thr).astype(I32) for e in range(topk - 1)]
    while len(hits) > 1:
        hits = [hits[j] + hits[j + 1] if j + 1 < len(hits) else hits[j] for j in range(0, len(hits), 2)]
    n_gt = hits[0][0:1, :]
    need = (topk - n_gt).astype(F32)

    seen_scr[...] = jnp.zeros(seen_scr.shape, F32)

    def emit_chunk(c, diagonal):
        kc = keys_scr[c]
        eq = kc == thr
        eqf = eq.astype(F32)
        seen = seen_scr[...]
        seen_scr[...] = seen + jnp.sum(eqf, axis=0, keepdims=True)
        rank = seen + jnp.dot(tri_ref[...], eqf.astype(BF16), preferred_element_type=F32)
        sel = (kc > thr) | (eq & (rank < need))
        if diagonal:
            sel = sel & causal
        o_ref[0, c] = jnp.where(sel, 0.0, NEG).astype(BF16)

    for_past_chunks(lambda c: emit_chunk(c, False))
    emit_chunk(i, True)

    def fill_chunk(c, carry):
        o_ref[0, c] = jnp.full((tile, tile), NEG, BF16)
        return carry

    lax.fori_loop(n_live, n_chunks, fill_chunk, 0)


def _dsa_select(q_idx_bf16, w_idx, kn_bf16, topk):
    s = q_idx_bf16.shape[0]
    t = ATT_TILE
    n_chunks = s // t
    n_list = DSA_SORT_CHUNKS * (t // SUBLANES)
    assert topk <= n_list, "the sorting network keeps n_list best keys per query"
    kn3 = kn_bf16.reshape(n_chunks, t, IDX_DIM)
    wt = jnp.pad(w_idx.T, ((0, SUBLANES - IDX_HEADS), (0, 0)))
    tri = (lax.broadcasted_iota(I32, (t, t), 1) < lax.broadcasted_iota(I32, (t, t), 0)).astype(BF16)
    return pl.pallas_call(
        functools.partial(_dsa_sel_kernel, tile=t, topk=topk),
        grid=(n_chunks,),
        in_specs=[pl.BlockSpec((t, IDX_HEADS * IDX_DIM), lambda i: (i, 0)),
                  pl.BlockSpec((SUBLANES, t), lambda i: (0, i)),
                  pl.BlockSpec((n_chunks, t, IDX_DIM), lambda i: (0, 0, 0)),
                  pl.BlockSpec((t, t), lambda i: (0, 0))],
        out_specs=pl.BlockSpec((1, n_chunks, t, t), lambda i: (i, 0, 0, 0)),
        out_shape=jax.ShapeDtypeStruct((n_chunks, n_chunks, t, t), BF16),
        scratch_shapes=[pltpu.VMEM((max(n_chunks, DSA_SORT_CHUNKS), t, t), F32), pltpu.VMEM((1, t), F32),
                        pltpu.VMEM((n_list, SUBLANES, t), F32), pltpu.VMEM((n_list, SUBLANES, t), F32)],
        compiler_params=_cparams(("arbitrary",)),
    )(q_idx_bf16, wt, kn3, tri)


def _transpose_bf16(a_bf16):
    n = a_bf16.shape[1]
    eye = (lax.broadcasted_iota(I32, (n, n), 0) == lax.broadcasted_iota(I32, (n, n), 1)).astype(BF16)
    return lax.dot_general(eye, a_bf16, _NT, preferred_element_type=F32)


def _out_proj_kernel(ya_ref, yb_ref, w_ref, x_ref, g_ref, o_ref):
    ka = ya_ref.shape[0]
    ya = _transpose_bf16(ya_ref[...].astype(BF16)).astype(BF16)
    yb = _transpose_bf16(yb_ref[...].astype(BF16)).astype(BF16)
    mixed = (jnp.dot(ya, w_ref[:ka, :], preferred_element_type=F32)
             + jnp.dot(yb, w_ref[ka:, :], preferred_element_type=F32))
    o_ref[...] = x_ref[...] + g_ref[...] * mixed


def _out_proj(yt_a, yt_b, w_bf16, x, gate, tm=PROJ_ROWS):
    s, d = x.shape
    ka, kb = yt_a.shape[0], yt_b.shape[0]
    return pl.pallas_call(
        _out_proj_kernel,
        grid=(s // tm,),
        in_specs=[pl.BlockSpec((ka, tm), lambda i: (0, i)),
                  pl.BlockSpec((kb, tm), lambda i: (0, i)),
                  pl.BlockSpec((ka + kb, d), lambda i: (0, 0)),
                  pl.BlockSpec((tm, d), lambda i: (i, 0)),
                  pl.BlockSpec((1, d), lambda i: (0, 0))],
        out_specs=pl.BlockSpec((tm, d), lambda i: (i, 0)),
        out_shape=jax.ShapeDtypeStruct((s, d), F32),
        compiler_params=_cparams(("arbitrary",)),
    )(yt_a, yt_b, w_bf16, x, gate.reshape(1, d))


def _transpose_cast_kernel(v_ref, o_ref):
    o_ref[...] = _transpose_bf16(v_ref[0].astype(BF16)).astype(BF16)


def _transpose_cast(v, layer=0, tr=TRANSPOSE_ROWS):
    v = v if v.ndim == 3 else v[None]
    _, rows, d = v.shape
    return pl.pallas_call(
        _transpose_cast_kernel,
        grid=(rows // tr,),
        in_specs=[pl.BlockSpec((1, tr, d), lambda i: (layer, i, 0))],
        out_specs=pl.BlockSpec((d, tr), lambda i: (0, i)),
        out_shape=jax.ShapeDtypeStruct((d, rows), BF16),
        compiler_params=_cparams(("arbitrary",)),
    )(v)


def _cast_kernel(v_ref, o_ref):
    o_ref[...] = v_ref[0].astype(BF16)


def _cast_bf16(v, layer, tr=TRANSPOSE_ROWS):
    _, rows, d = v.shape
    return pl.pallas_call(
        _cast_kernel,
        grid=(rows // tr,),
        in_specs=[pl.BlockSpec((1, tr, d), lambda i: (layer, i, 0))],
        out_specs=pl.BlockSpec((tr, d), lambda i: (i, 0)),
        out_shape=jax.ShapeDtypeStruct((rows, d), BF16),
        compiler_params=_cparams(("arbitrary",)),
    )(v)


def _cmp_exchange(v, i, j):
    hi = jnp.maximum(v[i], v[j])
    lo = jnp.minimum(v[i], v[j])
    v[i], v[j] = hi, lo


def _bitonic_sort_desc(v):
    n = len(v)
    k = 2
    while k <= n:
        j = k // 2
        while j >= 1:
            for i in range(n):
                l = i ^ j
                if l > i:
                    if (i & k) == 0:
                        _cmp_exchange(v, i, l)
                    else:
                        _cmp_exchange(v, l, i)
            j //= 2
        k *= 2


def _bitonic_merge_desc(v):
    n = len(v)
    j = n // 2
    while j >= 1:
        for i in range(n):
            l = i ^ j
            if l > i:
                _cmp_exchange(v, i, l)
        j //= 2


def _top16_rows(x):
    tt = x.shape[1]
    v = [x[SUBLANES * i:SUBLANES * (i + 1), :] for i in range(PEER_NKEYS // SUBLANES)]
    _bitonic_sort_desc(v)
    for shift in (4, 2, 1):
        other = [pltpu.roll(a, shift, 0) for a in v]
        v = [jnp.maximum(v[i], other[len(v) - 1 - i]) for i in range(len(v))]
        _bitonic_merge_desc(v)
    return v


def _peer_route_kernel(q_ref, sk_ref, r1_ref, e1_ref, e0_ref, ns_ref, s0_scr, s1_scr, *, tt):
    assert PEER_HEADS == SUBLANES
    sub = lax.broadcasted_iota(I32, (SUBLANES, tt), 0)
    tops = [[], []]
    for h in range(PEER_HEADS):
        for p in range(2):
            lo = (2 * h + p) * (PEER_DKEY // 2)
            qhp = q_ref[:, lo:lo + PEER_DKEY // 2].astype(BF16)
            sc = lax.dot_general(sk_ref[2 * h + p], qhp, _NT, preferred_element_type=F32)
            (s0_scr if p == 0 else s1_scr)[h] = sc
            tops[p].append(_top16_rows(sc))
    a = [sum(jnp.where(sub == h, tops[0][h][k], 0.0) for h in range(PEER_HEADS)) for k in range(PEER_TOPK)]
    b = [sum(jnp.where(sub == h, tops[1][h][k], 0.0) for h in range(PEER_HEADS)) for k in range(PEER_TOPK)]
    cand = [a[i] + b[j] for i in range(PEER_TOPK) for j in range(PEER_TOPK) if (i + 1) * (j + 1) <= PEER_TOPK]
    cand = cand + [jnp.full((SUBLANES, tt), -jnp.inf, F32)] * (64 - len(cand))
    _bitonic_sort_desc(cand)
    tau = cand[PEER_TOPK - 1]
    z = sum(jnp.exp(cand[k] - cand[0]) for k in range(PEER_TOPK))
    half_inv_z = 0.5 / z
    nsel_of_rank = [sum(jnp.where(a[r] + b[k] >= tau, 1.0, 0.0) for k in range(PEER_TOPK)) for r in range(PEER_TOPK)]
    for h in range(PEER_HEADS):
        s0 = s0_scr[h]
        s1 = s1_scr[h]
        nsel = jnp.zeros((PEER_NKEYS, tt), F32)
        rank1 = jnp.full((PEER_NKEYS, tt), float(PEER_TOPK), F32)
        for r in reversed(range(PEER_TOPK)):
            nsel = jnp.where(s0 == a[r][h:h + 1, :], nsel_of_rank[r][h:h + 1, :], nsel)
            rank1 = jnp.where(s1 == b[r][h:h + 1, :], float(r), rank1)
        ns_ref[h] = nsel
        r1_ref[h] = rank1.astype(BF16)
        e0_ref[h] = jnp.exp(s0 - a[0][h:h + 1, :]) * half_inv_z[h:h + 1, :]
        e1_ref[h] = jnp.exp(s1 - b[0][h:h + 1, :]).astype(BF16)


def _peer_route(q, subkeys_bf16, tt=PEER_ROUTE_TOKENS):
    s = q.shape[0]
    big = lambda dt: jax.ShapeDtypeStruct((PEER_HEADS, PEER_NKEYS, s), dt)
    big_spec = pl.BlockSpec((PEER_HEADS, PEER_NKEYS, tt), lambda i: (0, 0, i))
    return pl.pallas_call(
        functools.partial(_peer_route_kernel, tt=tt),
        grid=(s // tt,),
        in_specs=[pl.BlockSpec((tt, PEER_HEADS * PEER_DKEY), lambda i: (i, 0)),
                  pl.BlockSpec((2 * PEER_HEADS, PEER_NKEYS, PEER_DKEY // 2), lambda i: (0, 0, 0))],
        out_specs=[big_spec, big_spec, big_spec, big_spec],
        out_shape=[big(BF16), big(BF16), big(F32), big(F32)],
        scratch_shapes=[pltpu.VMEM((PEER_HEADS, PEER_NKEYS, tt), F32),
                        pltpu.VMEM((PEER_HEADS, PEER_NKEYS, tt), F32)],
        compiler_params=_cparams(("arbitrary",)),
    )(q, subkeys_bf16)


def _peer_dense_kernel(ht_ref, u_ref, vt_ref, r1_ref, e1_ref, e0_ref, ns_ref,
                       x_ref, g_ref, gf_ref, o_ref, acc_scr, p_scr, *, rows_per_step, final_norm):
    k = pl.program_id(1)

    @pl.when(k == 0)
    def _():
        acc_scr[...] = jnp.zeros(acc_scr.shape, F32)

    tt = ht_ref.shape[1]
    pack_rows = BF16_PACK_ROWS
    zero = jnp.zeros((pack_rows, tt), BF16)

    def activations(c):
        return jnp.dot(u_ref[c * MXU_DEPTH:(c + 1) * MXU_DEPTH, :], ht_ref[...], preferred_element_type=F32)

    def expert_rows(ii, a):
        i = k * rows_per_step + ii
        def bcast(ref, h):
            half = jnp.broadcast_to(ref[h, pl.ds(i, 1), :], (pack_rows // 2, tt))
            return jnp.concatenate([half, half], axis=0).astype(BF16)
        e0_rows = [bcast(e0_ref, h) for h in range(PEER_HEADS)]
        ns_rows = [bcast(ns_ref, h) for h in range(PEER_HEADS)]
        gelu2 = (a * (1.0 + lax.erf(a * math.sqrt(0.5)))).astype(BF16)
        for g in range(PEER_NKEYS // pack_rows):
            rows = slice(g * pack_rows, (g + 1) * pack_rows)
            wgt = None
            for h in range(PEER_HEADS):
                term = jnp.where(r1_ref[h, rows, :] < ns_rows[h], e1_ref[h, rows, :], zero) * e0_rows[h]
                wgt = term if wgt is None else wgt + term
            p_scr[ii * PEER_NKEYS + g * pack_rows:ii * PEER_NKEYS + (g + 1) * pack_rows, :] = wgt * gelu2[rows, :]

    depth_rows = MXU_DEPTH // PEER_NKEYS
    n_chunks = rows_per_step // depth_rows
    act = activations(0)
    for c in range(n_chunks):
        act_next = activations(c + 1) if c + 1 < n_chunks else None
        for j in range(depth_rows):
            expert_rows(c * depth_rows + j, act[j * PEER_NKEYS:(j + 1) * PEER_NKEYS, :])
        act = act_next
    acc_scr[...] += jnp.dot(vt_ref[...], p_scr[...], preferred_element_type=F32)

    @pl.when(k == pl.num_programs(1) - 1)
    def _():
        y = x_ref[...] + g_ref[...] * acc_scr[...].T
        if final_norm:
            y = y * lax.rsqrt(jnp.mean(y * y, axis=-1, keepdims=True) + EPS) * gf_ref[...]
        o_ref[...] = y


def _peer_dense(ht_bf16, u_bf16, vt_bf16, rank1, e1, e0, nsel, x, gate, g_final, final_norm,
                tt=PEER_DENSE_TOKENS, rows_per_step=PEER_DENSE_KEY_ROWS):
    s, d = x.shape
    n_exp = u_bf16.shape[0]
    et = rows_per_step * PEER_NKEYS
    tok = lambda j, k: (0, 0, j)
    big_spec = pl.BlockSpec((PEER_HEADS, PEER_NKEYS, tt), tok)
    return pl.pallas_call(
        functools.partial(_peer_dense_kernel, rows_per_step=rows_per_step, final_norm=final_norm),
        grid=(s // tt, n_exp // et),
        in_specs=[pl.BlockSpec((d, tt), lambda j, k: (0, j)),
                  pl.BlockSpec((et, d), lambda j, k: (k, 0)),
                  pl.BlockSpec((d, et), lambda j, k: (0, k)),
                  big_spec, big_spec, big_spec, big_spec,
                  pl.BlockSpec((tt, d), lambda j, k: (j, 0)),
                  pl.BlockSpec((1, d), lambda j, k: (0, 0)),
                  pl.BlockSpec((1, d), lambda j, k: (0, 0))],
        out_specs=pl.BlockSpec((tt, d), lambda j, k: (j, 0)),
        out_shape=jax.ShapeDtypeStruct((s, d), F32),
        scratch_shapes=[pltpu.VMEM((d, tt), F32), pltpu.VMEM((et, tt), BF16)],
        compiler_params=_cparams(("arbitrary", "arbitrary")),
    )(ht_bf16, u_bf16, vt_bf16, rank1, e1, e0, nsel, x, gate.reshape(1, d), g_final.reshape(1, d))


def _heads_first(t, nh):
    s = t.shape[0]
    return jnp.transpose(t.reshape(s, nh, HEAD_DIM), (1, 0, 2))


def _query_col_scale(n_pad):
    cs = jnp.ones((n_pad,), F32)
    q_scale = HEAD_DIM ** -0.5 * LOG2E
    dsa_q = 3 * MOBA_WIDTH
    return cs.at[:MOBA_WIDTH].set(q_scale).at[dsa_q:dsa_q + DSA_WIDTH].set(q_scale)


def _mixer(proj_bf16, moba_qk_f32, idx_f32, w_out_bf16, bias_tiles, x, gate):
    s = proj_bf16.shape[0]
    parts = []
    off = 0
    for size in PROJ_SIZES[:7]:
        parts.append(proj_bf16[:, off:off + size])
        off += size
    mq_s, mk_b, mv_b, dq_s, dk_b, dv_b, iq_b = parts
    mq, mk = moba_qk_f32[:, :MOBA_WIDTH], moba_qk_f32[:, MOBA_WIDTH:]
    ik, iw = idx_f32[:, :IDX_DIM], idx_f32[:, IDX_DIM:IDX_DIM + IDX_HEADS]
    dims_first = lambda a, nh: a.T.reshape(nh, HEAD_DIM, s)

    block_sel = _moba_select(_heads_first(mq, MOBA_HEADS), _moba_kmean(_heads_first(mk, MOBA_HEADS)))
    blk = jnp.arange(s, dtype=I32) // MOBA_BLOCK
    onehot = (blk[:, None] == jnp.arange(MOBA_MAX_BLOCKS, dtype=I32)[None, :]).astype(BF16)
    q_aug = jnp.concatenate([_heads_first(mq_s, MOBA_HEADS), jnp.transpose(block_sel, (0, 2, 1)).astype(BF16)],
                            axis=-1)
    k_aug = jnp.concatenate([_heads_first(mk_b, MOBA_HEADS),
                             jnp.broadcast_to(onehot, (MOBA_HEADS, s, MOBA_MAX_BLOCKS))], axis=-1)
    y_moba = _attention(q_aug, k_aug, dims_first(mv_b, MOBA_HEADS), bias_tiles, 0)

    ik_n = ik * lax.rsqrt(jnp.mean(ik * ik, axis=-1, keepdims=True) + EPS)
    topk = min(DSA_TOPK_MAX, s // 4)
    mask_bias = _dsa_select(iq_b, iw * IDX_HEADS ** -0.5, ik_n.astype(BF16), topk)
    y_dsa = _attention(_heads_first(dq_s, DSA_HEADS), _heads_first(dk_b, DSA_HEADS), dims_first(dv_b, DSA_HEADS),
                       bias_tiles, 1, mask_bias=mask_bias)

    return _out_proj(y_moba.reshape(MOBA_WIDTH, s), y_dsa.reshape(DSA_WIDTH, s), w_out_bf16, x, gate)


def kernel(x, c, w_ada, b_ada, norm_attn, norm_ffn, w_in, w_out, rel_bias,
           peer_wq, peer_subkeys, peer_u, peer_v, norm_final):
    batch, s, d = x.shape
    assert batch == 1 and s % (ATT_KEY_TILES_PER_STEP * ATT_TILE) == 0
    depth = w_ada.shape[0]
    x = x[0]
    mod = _ada_mod(c, w_ada, b_ada)
    bias_tiles = _bias_tiles(rel_bias, ATT_TILE)
    n_pad = -(-PROJ_WIDTH // LANES) * LANES
    for l in range(depth):
        sh1, sc1, g1, sh2, sc2, g2 = jnp.split(mod[l], 6)
        w_in_l = jnp.pad(w_in[l], ((0, 0), (0, n_pad - PROJ_WIDTH))).astype(BF16)
        proj_bf16, moba_qk, idx_cols = _mixer_proj(x, norm_attn[l], sh1, sc1, w_in_l, _query_col_scale(n_pad))
        x = _mixer(proj_bf16, moba_qk, idx_cols, w_out[l].astype(BF16), bias_tiles, x, g1)
        h2, q = _norm_mod_matmul(x, norm_ffn[l], sh2, sc2, peer_wq[l].astype(BF16))
        subkeys = peer_subkeys[l].reshape(2 * PEER_HEADS, PEER_NKEYS, PEER_DKEY // 2).astype(BF16)
        rank1, e1, e0, nsel = _peer_route(q, subkeys)
        x = _peer_dense(_transpose_cast(h2), _cast_bf16(peer_u, l), _transpose_cast(peer_v, l), rank1, e1, e0, nsel,
                        x, g2, norm_final, final_norm=(l == depth - 1))
    return x[None]
```

```python
import functools
import math

import jax
import jax.numpy as jnp
from jax import lax
from jax.experimental import pallas as pl
from jax.experimental.pallas import tpu as pltpu

F32 = jnp.float32
BF16 = jnp.bfloat16
I32 = jnp.int32

HEAD_DIM = 64
MOBA_HEADS = 8
DSA_HEADS = 8
N_HEADS = MOBA_HEADS + DSA_HEADS
MOBA_WIDTH = MOBA_HEADS * HEAD_DIM
DSA_WIDTH = DSA_HEADS * HEAD_DIM
MOBA_BLOCK = 256
MOBA_TOPK = 3
MOBA_MAX_BLOCKS = 64
IDX_HEADS = 4
IDX_DIM = 64
DSA_TOPK_MAX = 256
DSA_SORT_CHUNKS = 8
N_BUCKETS = 32
MAX_DISTANCE = 4096
PEER_HEADS = 8
PEER_NKEYS = 128
PEER_DKEY = 256
PEER_TOPK = 16
EPS = 1e-6
PROJ_SIZES = (MOBA_WIDTH, MOBA_WIDTH, MOBA_WIDTH, DSA_WIDTH, DSA_WIDTH, DSA_WIDTH,
              IDX_HEADS * IDX_DIM, IDX_DIM, IDX_HEADS)
PROJ_WIDTH = sum(PROJ_SIZES)
IDX_F32_START = sum(PROJ_SIZES[:7])
MOBA_V_START = sum(PROJ_SIZES[:2])
DSA_V_START = sum(PROJ_SIZES[:5])

NEG = -1e30
LOG2E = math.log2(math.e)
ATT_TILE = MOBA_BLOCK
ATT_KEY_TILES_PER_STEP = 4
LANES = 128
SUBLANES = 8
BF16_PACK_ROWS = 2 * SUBLANES
MXU_DEPTH = 256
VMEM_LIMIT = 56 * 1024 * 1024
NETWORK_REGS = 32

ADA_COLS = 1536
PROJ_ROWS = 512
MOBA_SEL_QUERIES = 2048
PEER_ROUTE_TOKENS = 256
PEER_DENSE_TOKENS = 512
PEER_DENSE_KEY_ROWS = 16
TRANSPOSE_ROWS = 1024

_NT = (((1,), (1,)), ((), ()))


def _far_bias_tiles(tile):
    max_exact = N_BUCKETS // 2
    sat = int(math.ceil(max_exact * (MAX_DISTANCE / max_exact) ** ((N_BUCKETS - max_exact - 1) / (N_BUCKETS - max_exact)))) + 4
    return -(-(sat + tile - 1) // tile)


def _cparams(sem):
    return pltpu.CompilerParams(dimension_semantics=sem, vmem_limit_bytes=VMEM_LIMIT)


def _ada_kernel(c_ref, w_ref, b_ref, o_ref):
    c = c_ref[...]
    sc = c * jax.nn.sigmoid(c)
    o_ref[0] = jnp.dot(sc, w_ref[0], preferred_element_type=F32,
                       precision=lax.Precision.HIGHEST) + b_ref[0]


def _ada_mod(c, w_ada, b_ada):
    depth, d, n = w_ada.shape
    tn = ADA_COLS
    c8 = jnp.broadcast_to(c, (SUBLANES, d))
    out = pl.pallas_call(
        _ada_kernel,
        grid=(depth, n // tn),
        in_specs=[pl.BlockSpec((SUBLANES, d), lambda l, j: (0, 0)),
                  pl.BlockSpec((1, d, tn), lambda l, j: (l, 0, j)),
                  pl.BlockSpec((1, 1, tn), lambda l, j: (l, 0, j))],
        out_specs=pl.BlockSpec((1, SUBLANES, tn), lambda l, j: (l, 0, j)),
        out_shape=jax.ShapeDtypeStruct((depth, SUBLANES, n), F32),
        compiler_params=_cparams(("arbitrary", "arbitrary")),
    )(c8, w_ada, b_ada.reshape(depth, 1, n))
    return out[:, 0, :]


def _nmm_kernel(x_ref, g_ref, sh_ref, sc_ref, w_ref, h_ref, o_ref):
    x = x_ref[...]
    y = x * lax.rsqrt(jnp.mean(x * x, axis=-1, keepdims=True) + EPS) * g_ref[...]
    hb = (y * (1.0 + sc_ref[...]) + sh_ref[...]).astype(BF16)
    h_ref[...] = hb
    o_ref[...] = jnp.dot(hb, w_ref[...], preferred_element_type=F32)


def _norm_mod_matmul(x, g, shift, scale, w_bf16, tm=PROJ_ROWS):
    s, d = x.shape
    n = w_bf16.shape[1]
    row = lambda i: (0, 0)
    return pl.pallas_call(
        _nmm_kernel,
        grid=(s // tm,),
        in_specs=[pl.BlockSpec((tm, d), lambda i: (i, 0)),
                  pl.BlockSpec((1, d), row), pl.BlockSpec((1, d), row), pl.BlockSpec((1, d), row),
                  pl.BlockSpec((d, n), row)],
        out_specs=[pl.BlockSpec((tm, d), lambda i: (i, 0)),
                   pl.BlockSpec((tm, n), lambda i: (i, 0))],
        out_shape=[jax.ShapeDtypeStruct((s, d), BF16), jax.ShapeDtypeStruct((s, n), F32)],
        compiler_params=_cparams(("arbitrary",)),
    )(x, g.reshape(1, d), shift.reshape(1, d), scale.reshape(1, d), w_bf16)


def _mixer_proj_kernel(x_ref, g_ref, sh_ref, sc_ref, w_ref, cs_ref, ob_ref, oq_ref, oi_ref, mvt_ref, dvt_ref):
    x = x_ref[...]
    y = x * lax.rsqrt(jnp.mean(x * x, axis=-1, keepdims=True) + EPS) * g_ref[...]
    hb = (y * (1.0 + sc_ref[...]) + sh_ref[...]).astype(BF16)
    p = jnp.dot(hb, w_ref[...], preferred_element_type=F32)
    pb = (p * cs_ref[...]).astype(BF16)
    ob_ref[...] = pb
    oq_ref[...] = p[:, :2 * MOBA_WIDTH]
    oi_ref[...] = p[:, IDX_F32_START:IDX_F32_START + LANES]
    mvt_ref[...] = _transpose_bf16(pb[:, MOBA_V_START:MOBA_V_START + MOBA_WIDTH]).astype(BF16)
    dvt_ref[...] = _transpose_bf16(pb[:, DSA_V_START:DSA_V_START + DSA_WIDTH]).astype(BF16)


def _mixer_proj(x, g, shift, scale, w_bf16, col_scale, tm=PROJ_ROWS):
    s, d = x.shape
    n = w_bf16.shape[1]
    row = lambda i: (0, 0)
    rows = lambda i: (i, 0)
    return pl.pallas_call(
        _mixer_proj_kernel,
        grid=(s // tm,),
        in_specs=[pl.BlockSpec((tm, d), rows),
                  pl.BlockSpec((1, d), row), pl.BlockSpec((1, d), row), pl.BlockSpec((1, d), row),
                  pl.BlockSpec((d, n), row), pl.BlockSpec((1, n), row)],
        out_specs=[pl.BlockSpec((tm, n), rows), pl.BlockSpec((tm, 2 * MOBA_WIDTH), rows),
                   pl.BlockSpec((tm, LANES), rows),
                   pl.BlockSpec((MOBA_WIDTH, tm), lambda i: (0, i)), pl.BlockSpec((DSA_WIDTH, tm), lambda i: (0, i))],
        out_shape=[jax.ShapeDtypeStruct((s, n), BF16), jax.ShapeDtypeStruct((s, 2 * MOBA_WIDTH), F32),
                   jax.ShapeDtypeStruct((s, LANES), F32),
                   jax.ShapeDtypeStruct((MOBA_WIDTH, s), BF16), jax.ShapeDtypeStruct((DSA_WIDTH, s), BF16)],
        compiler_params=_cparams(("arbitrary",)),
    )(x, g.reshape(1, d), shift.reshape(1, d), scale.reshape(1, d), w_bf16, col_scale.reshape(1, n))


def _t5_bucket(d):
    max_exact = N_BUCKETS // 2
    d = jnp.maximum(d, 0)
    df = jnp.maximum(d, 1).astype(F32)
    large = max_exact + (jnp.log(df / max_exact) / math.log(MAX_DISTANCE / max_exact)
                         * (N_BUCKETS - max_exact)).astype(I32)
    return jnp.where(d < max_exact, d, jnp.minimum(large, N_BUCKETS - 1))


def _bias_tile_kernel(tab_ref, o_ref, *, tile):
    h = pl.program_id(0)
    off = pl.program_id(1)
    b = lax.broadcasted_iota(I32, (tile, tile), 0)
    a = lax.broadcasted_iota(I32, (tile, tile), 1)
    d = off * tile + a - b
    bucket = _t5_bucket(d)
    val = lax.fori_loop(jnp.min(bucket), jnp.max(bucket) + 1,
                        lambda k, val: jnp.where(bucket == k, tab_ref[h, k], val),
                        jnp.zeros((tile, tile), F32))
    val = (val - tab_ref[h, N_BUCKETS - 1]) * LOG2E
    o_ref[0, 0] = jnp.where(d < 0, NEG, val)


def _bias_tiles(rel_bias, tile):
    n_near = _far_bias_tiles(tile)
    nh = rel_bias.shape[0]
    return pl.pallas_call(
        functools.partial(_bias_tile_kernel, tile=tile),
        grid=(nh, n_near + 1),
        in_specs=[pl.BlockSpec(memory_space=pltpu.SMEM)],
        out_specs=pl.BlockSpec((1, 1, tile, tile), lambda h, o: (h, o, 0, 0)),
        out_shape=jax.ShapeDtypeStruct((nh, n_near + 1, tile, tile), F32),
        compiler_params=_cparams(("arbitrary", "arbitrary")),
    )(rel_bias)


def _kmean_kernel(k_ref, o_ref, *, nb):
    k = k_ref[0]
    km = jnp.mean(k.reshape(nb, MOBA_BLOCK, HEAD_DIM), axis=1)
    o_ref[0] = jnp.zeros(o_ref.shape[1:], F32)
    o_ref[0, :nb, :] = km


def _moba_kmean(k_hsd):
    nh, s, dh = k_hsd.shape
    nb = s // MOBA_BLOCK
    return pl.pallas_call(
        functools.partial(_kmean_kernel, nb=nb),
        grid=(nh,),
        in_specs=[pl.BlockSpec((1, s, dh), lambda h: (h, 0, 0))],
        out_specs=pl.BlockSpec((1, MOBA_MAX_BLOCKS, dh), lambda h: (h, 0, 0)),
        out_shape=jax.ShapeDtypeStruct((nh, MOBA_MAX_BLOCKS, dh), F32),
        compiler_params=_cparams(("arbitrary",)),
    )(k_hsd)


def _moba_sel_kernel(q_ref, km_ref, o_ref, *, tq, n_sel):
    gate = lax.dot_general(km_ref[0], q_ref[0], _NT, preferred_element_type=F32,
                           precision=lax.Precision.HIGHEST)
    t = pl.program_id(1) * tq + lax.broadcasted_iota(I32, (MOBA_MAX_BLOCKS, tq), 1)
    b0 = t // MOBA_BLOCK
    n = lax.broadcasted_iota(I32, (MOBA_MAX_BLOCKS, tq), 0)
    avail = n < b0
    taken = n == b0
    for _ in range(n_sel):
        g = jnp.where(avail, gate, -jnp.inf)
        m = jnp.max(g, axis=0, keepdims=True)
        idx = jnp.min(jnp.where((g == m) & avail, n, 1 << 20), axis=0, keepdims=True)
        pick = n == idx
        taken = taken | pick
        avail = avail & jnp.logical_not(pick)
    o_ref[0] = jnp.where(taken, 0.0, NEG)


def _moba_select(q_hsd, kmean, tq=MOBA_SEL_QUERIES):
    nh, s, dh = q_hsd.shape
    nb = s // MOBA_BLOCK
    assert nb <= MOBA_MAX_BLOCKS
    n_sel = max(min(MOBA_TOPK, nb - 1), 1)
    tq = min(tq, s)
    return pl.pallas_call(
        functools.partial(_moba_sel_kernel, tq=tq, n_sel=n_sel),
        grid=(nh, s // tq),
        in_specs=[pl.BlockSpec((1, tq, dh), lambda h, i: (h, i, 0)),
                  pl.BlockSpec((1, MOBA_MAX_BLOCKS, dh), lambda h, i: (h, 0, 0))],
        out_specs=pl.BlockSpec((1, MOBA_MAX_BLOCKS, tq), lambda h, i: (h, 0, i)),
        out_shape=jax.ShapeDtypeStruct((nh, MOBA_MAX_BLOCKS, s), F32),
        compiler_params=_cparams(("arbitrary", "arbitrary")),
    )(q_hsd, kmean)


def _attn_kernel(qi_ref, kg_ref, *refs, nh, n_near, group, has_mask):
    q_ref, k_ref, vt_ref, bt_ref = refs[:4]
    mask_ref = refs[4] if has_mask else None
    o_ref, m_scr, acc_scr = refs[-3:]
    t = ATT_TILE
    step = pl.program_id(0)
    qi = qi_ref[step]
    kg = kg_ref[step]

    @pl.when(kg == 0)
    def _():
        m_scr[...] = jnp.full(m_scr.shape, NEG, F32)
        acc_scr[...] = jnp.zeros(acc_scr.shape, F32)

    def step_update(with_bias, n_tiles):
        m = [m_scr[h] for h in range(nh)]
        for g in range(n_tiles):
            keys = slice(g * t, (g + 1) * t)
            off = qi - (kg * group + g)
            bias_row = jnp.minimum(off, n_near)
            if has_mask:
                mb = mask_ref[0, g].astype(F32)
            scores = []
            for h in range(nh):
                s = lax.dot_general(k_ref[h, keys, :], q_ref[h], _NT, preferred_element_type=F32)
                if with_bias:
                    s = s + bt_ref[h, bias_row]
                if has_mask:
                    s = s + mb
                scores.append(s)
            probs, alphas = [], []
            for h in range(nh):
                m_new = jnp.maximum(m[h], jnp.max(scores[h], axis=0, keepdims=True))
                alpha = jnp.exp2(m[h] - m_new)
                p = jnp.exp2(scores[h] - m_new)
                m[h] = m_new
                probs.append(p.astype(BF16))
                alphas.append(alpha)
            for h in range(nh):
                acc_scr[h] = alphas[h] * acc_scr[h] + jnp.dot(vt_ref[h, :, keys], probs[h],
                                                              preferred_element_type=F32)
        for h in range(nh):
            m_scr[h] = m[h]

    all_far = qi - (kg * group + group - 1) >= n_near
    pl.when(all_far)(functools.partial(step_update, False, group))
    n_causal = jnp.minimum(qi - kg * group + 1, group)
    for n_tiles in range(1, group + 1):
        pl.when(jnp.logical_not(all_far) & (n_causal == n_tiles))(functools.partial(step_update, True, n_tiles))

    @pl.when(kg == qi // group)
    def _():
        for h in range(nh):
            o_ref[h] = acc_scr[h, :HEAD_DIM, :] / acc_scr[h, HEAD_DIM:HEAD_DIM + 1, :]


def _attention(q, k, vt, bias_tiles, head_group, mask_bias=None):
    nh, s, dk = q.shape
    vt = jnp.concatenate([vt, jnp.ones((nh, SUBLANES, s), vt.dtype)], axis=1)
    dv = vt.shape[1]
    t = ATT_TILE
    group = ATT_KEY_TILES_PER_STEP
    nq = s // t
    n_near = bias_tiles.shape[1] - 1
    qi_list = [i for i in range(nq) for _ in range(i // group + 1)]
    kg_list = [j for i in range(nq) for j in range(i // group + 1)]
    qi_arr = jnp.asarray(qi_list, I32)
    kg_arr = jnp.asarray(kg_list, I32)
    in_specs = [pl.BlockSpec((nh, t, dk), lambda st, qi, kg: (0, qi[st], 0)),
                pl.BlockSpec((nh, group * t, dk), lambda st, qi, kg: (0, kg[st], 0)),
                pl.BlockSpec((nh, dv, group * t), lambda st, qi, kg: (0, 0, kg[st])),
                pl.BlockSpec((nh, n_near + 1, t, t), lambda st, qi, kg: (head_group, 0, 0, 0),
                             pipeline_mode=pl.Buffered(1))]
    args = [q, k, vt, bias_tiles]
    if mask_bias is not None:
        in_specs.append(pl.BlockSpec((1, group, t, t), lambda st, qi, kg: (qi[st], kg[st], 0, 0)))
        args.append(mask_bias)
    grid_spec = pltpu.PrefetchScalarGridSpec(
        num_scalar_prefetch=2,
        grid=(len(qi_list),),
        in_specs=in_specs,
        out_specs=pl.BlockSpec((nh, HEAD_DIM, t), lambda st, qi, kg: (0, 0, qi[st])),
        scratch_shapes=[pltpu.VMEM((nh, 1, t), F32), pltpu.VMEM((nh, dv, t), F32)])
    return pl.pallas_call(
        functools.partial(_attn_kernel, nh=nh, n_near=n_near, group=group, has_mask=mask_bias is not None),
        grid_spec=grid_spec,
        out_shape=jax.ShapeDtypeStruct((nh, HEAD_DIM, s), F32),
        compiler_params=_cparams(("arbitrary",)),
    )(qi_arr, kg_arr, *args)


def _bitonic_stages(n, full_sort):
    levels = [2 ** p for p in range(1, n.bit_length())] if full_sort else [n]
    return [(k, k >> s) for k in levels for s in range(1, k.bit_length())]


def _plan_passes(stages, block):
    passes = []
    for k, j in stages:
        cross = j >= block
        if passes and passes[-1][2] == cross and (not cross or bin(passes[-1][0] | j).count("1") <= 3):
            passes[-1][0] |= j if cross else 0
            passes[-1][1].append((k, j))
        else:
            passes.append([j if cross else block - 1, [(k, j)], cross])
    return [(mask, sub) for mask, sub, _ in passes]


def _run_pass(load, store, n, mask, stages, regs=NETWORK_REGS):
    groups = {}
    for e in range(n):
        groups.setdefault(e & ~mask, []).append(e)
    groups = list(groups.values())
    per = max(1, regs // len(groups[0]))
    for g0 in range(0, len(groups), per):
        elems = sorted(e for grp in groups[g0:g0 + per] for e in grp)
        vals = {e: load(e) for e in elems}
        for k, j in stages:
            for e in elems:
                l = e ^ j
                if l > e:
                    hi = jnp.maximum(vals[e], vals[l])
                    lo = jnp.minimum(vals[e], vals[l])
                    vals[e], vals[l] = (hi, lo) if (e & k) == 0 else (lo, hi)
        for e in elems:
            store(e, vals[e])


def _sortable(x):
    return jnp.where(x == 0.0, 0.0, x)


def _dsa_sel_kernel(qi_ref, wt_ref, kn_ref, tri_ref, o_ref, keys_scr, seen_scr, work_scr, top_scr, *, tile, topk):
    n_chunks = kn_ref.shape[0]
    i = pl.program_id(0)
    n_live = i + 1
    groups_per_chunk = tile // SUBLANES
    causal = lax.broadcasted_iota(I32, (tile, tile), 0) <= lax.broadcasted_iota(I32, (tile, tile), 1)
    q_heads = [qi_ref[:, h * IDX_DIM:(h + 1) * IDX_DIM] for h in range(IDX_HEADS)]
    w_heads = [wt_ref[h:h + 1, :] for h in range(IDX_HEADS)]

    def score_chunk(c, diagonal):
        kc = kn_ref[c]
        dots = [lax.dot_general(kc, q_heads[h], _NT, preferred_element_type=F32) for h in range(IDX_HEADS)]
        sc = None
        for h in range(IDX_HEADS):
            term = w_heads[h] * jnp.maximum(dots[h], 0.0)
            sc = term if sc is None else sc + term
        if diagonal:
            sc = jnp.where(causal, sc, -jnp.inf)
        keys_scr[c] = _sortable(sc)

    def for_past_chunks(fn):
        def pair(p, carry):
            fn(2 * p)
            fn(2 * p + 1)
            return carry
        lax.fori_loop(0, i // 2, pair, 0)
        pl.when(i % 2 == 1)(lambda: fn(i - 1))

    for_past_chunks(lambda c: score_chunk(c, False))
    score_chunk(i, True)

    n_list = DSA_SORT_CHUNKS * groups_per_chunk
    n_batches = (n_live + DSA_SORT_CHUNKS - 1) // DSA_SORT_CHUNKS

    def pad_chunk(c, carry):
        keys_scr[c] = jnp.full((tile, tile), -jnp.inf, F32)
        return carry

    lax.fori_loop(n_live, n_batches * DSA_SORT_CHUNKS, pad_chunk, 0)
    top_scr[...] = jnp.full(top_scr.shape, -jnp.inf, F32)
    sort_passes = _plan_passes(_bitonic_stages(n_list, True), groups_per_chunk)
    merge_passes = _plan_passes(_bitonic_stages(n_list, False), groups_per_chunk)
    lane_halves = [slice(lo, lo + LANES) for lo in range(0, tile, LANES)]

    def run_network(passes, first_load, first_store, buf, lanes):
        def load(e):
            return buf[e, :, lanes]

        def store(e, v):
            buf[e, :, lanes] = v

        for n_pass, (mask, stages) in enumerate(passes):
            _run_pass(first_load if n_pass == 0 else load, first_store if n_pass == 0 else store,
                      n_list, mask, stages)

    def sort_batch(b, carry):
        for lanes in lane_halves:
            def load_keys(e, lanes=lanes):
                g = e % groups_per_chunk
                return keys_scr[b * DSA_SORT_CHUNKS + e // groups_per_chunk, SUBLANES * g:SUBLANES * (g + 1), lanes]

            def store_work(e, v, lanes=lanes):
                work_scr[e, :, lanes] = v

            run_network(sort_passes, load_keys, store_work, work_scr, lanes)

            def load_best(e, lanes=lanes):
                return jnp.maximum(top_scr[e, :, lanes], work_scr[n_list - 1 - e, :, lanes])

            def store_top(e, v, lanes=lanes):
                top_scr[e, :, lanes] = v

            run_network(merge_passes, load_best, store_top, top_scr, lanes)
        return carry

    lax.fori_loop(0, n_batches, sort_batch, 0)

    src, dst = top_scr, work_scr
    for shift in (4, 2, 1):
        for lanes in lane_halves:
            def load_pair(e, lanes=lanes, src=src, shift=shift):
                return jnp.maximum(src[e, :, lanes], pltpu.roll(src[n_list - 1 - e, :, lanes], shift, 0))

            def store_dst(e, v, lanes=lanes, dst=dst):
                dst[e, :, lanes] = v

            run_network(merge_passes, load_pair, store_dst, dst, lanes)
        src, dst = dst, src
    thr = src[topk - 1, 0:1, :]
    hits = [(src[e] > thr).astype(I32) for e in range(topk - 1)]
    while len(hits) > 1:
        hits = [hits[j] + hits[j + 1] if j + 1 < len(hits) else hits[j] for j in range(0, len(hits), 2)]
    n_gt = hits[0][0:1, :]
    need = (topk - n_gt).astype(F32)

    seen_scr[...] = jnp.zeros(seen_scr.shape, F32)

    def emit_chunk(c, diagonal):
        kc = keys_scr[c]
        eq = kc == thr
        eqf = eq.astype(F32)
        seen = seen_scr[...]
        seen_scr[...] = seen + jnp.sum(eqf, axis=0, keepdims=True)
        rank = seen + jnp.dot(tri_ref[...], eqf.astype(BF16), preferred_element_type=F32)
        sel = (kc > thr) | (eq & (rank < need))
        if diagonal:
            sel = sel & causal
        o_ref[0, c] = jnp.where(sel, 0.0, NEG).astype(BF16)

    for_past_chunks(lambda c: emit_chunk(c, False))
    emit_chunk(i, True)

    def fill_chunk(c, carry):
        o_ref[0, c] = jnp.full((tile, tile), NEG, BF16)
        return carry

    lax.fori_loop(n_live, n_chunks, fill_chunk, 0)


def _dsa_select(q_idx_bf16, w_idx, kn_bf16, topk):
    s = q_idx_bf16.shape[0]
    t = ATT_TILE
    n_chunks = s // t
    n_list = DSA_SORT_CHUNKS * (t // SUBLANES)
    assert topk <= n_list, "the sorting network keeps n_list best keys per query"
    kn3 = kn_bf16.reshape(n_chunks, t, IDX_DIM)
    wt = jnp.pad(w_idx.T, ((0, SUBLANES - IDX_HEADS), (0, 0)))
    tri = (lax.broadcasted_iota(I32, (t, t), 1) < lax.broadcasted_iota(I32, (t, t), 0)).astype(BF16)
    return pl.pallas_call(
        functools.partial(_dsa_sel_kernel, tile=t, topk=topk),
        grid=(n_chunks,),
        in_specs=[pl.BlockSpec((t, IDX_HEADS * IDX_DIM), lambda i: (i, 0)),
                  pl.BlockSpec((SUBLANES, t), lambda i: (0, i)),
                  pl.BlockSpec((n_chunks, t, IDX_DIM), lambda i: (0, 0, 0)),
                  pl.BlockSpec((t, t), lambda i: (0, 0))],
        out_specs=pl.BlockSpec((1, n_chunks, t, t), lambda i: (i, 0, 0, 0)),
        out_shape=jax.ShapeDtypeStruct((n_chunks, n_chunks, t, t), BF16),
        scratch_shapes=[pltpu.VMEM((max(n_chunks, DSA_SORT_CHUNKS), t, t), F32), pltpu.VMEM((1, t), F32),
                        pltpu.VMEM((n_list, SUBLANES, t), F32), pltpu.VMEM((n_list, SUBLANES, t), F32)],
        compiler_params=_cparams(("arbitrary",)),
    )(q_idx_bf16, wt, kn3, tri)


def _transpose_bf16(a_bf16):
    n = a_bf16.shape[1]
    eye = (lax.broadcasted_iota(I32, (n, n), 0) == lax.broadcasted_iota(I32, (n, n), 1)).astype(BF16)
    return lax.dot_general(eye, a_bf16, _NT, preferred_element_type=F32)


def _out_proj_kernel(ya_ref, yb_ref, w_ref, x_ref, g_ref, o_ref):
    ka = ya_ref.shape[0]
    ya = _transpose_bf16(ya_ref[...].astype(BF16)).astype(BF16)
    yb = _transpose_bf16(yb_ref[...].astype(BF16)).astype(BF16)
    mixed = (jnp.dot(ya, w_ref[:ka, :], preferred_element_type=F32)
             + jnp.dot(yb, w_ref[ka:, :], preferred_element_type=F32))
    o_ref[...] = x_ref[...] + g_ref[...] * mixed


def _out_proj(yt_a, yt_b, w_bf16, x, gate, tm=PROJ_ROWS):
    s, d = x.shape
    ka, kb = yt_a.shape[0], yt_b.shape[0]
    return pl.pallas_call(
        _out_proj_kernel,
        grid=(s // tm,),
        in_specs=[pl.BlockSpec((ka, tm), lambda i: (0, i)),
                  pl.BlockSpec((kb, tm), lambda i: (0, i)),
                  pl.BlockSpec((ka + kb, d), lambda i: (0, 0)),
                  pl.BlockSpec((tm, d), lambda i: (i, 0)),
                  pl.BlockSpec((1, d), lambda i: (0, 0))],
        out_specs=pl.BlockSpec((tm, d), lambda i: (i, 0)),
        out_shape=jax.ShapeDtypeStruct((s, d), F32),
        compiler_params=_cparams(("arbitrary",)),
    )(yt_a, yt_b, w_bf16, x, gate.reshape(1, d))


def _transpose_cast_kernel(v_ref, o_ref):
    o_ref[...] = _transpose_bf16(v_ref[0].astype(BF16)).astype(BF16)


def _transpose_cast(v, layer=0, tr=TRANSPOSE_ROWS):
    v = v if v.ndim == 3 else v[None]
    _, rows, d = v.shape
    return pl.pallas_call(
        _transpose_cast_kernel,
        grid=(rows // tr,),
        in_specs=[pl.BlockSpec((1, tr, d), lambda i: (layer, i, 0))],
        out_specs=pl.BlockSpec((d, tr), lambda i: (0, i)),
        out_shape=jax.ShapeDtypeStruct((d, rows), BF16),
        compiler_params=_cparams(("arbitrary",)),
    )(v)


def _cast_kernel(v_ref, o_ref):
    o_ref[...] = v_ref[0].astype(BF16)


def _cast_bf16(v, layer, tr=TRANSPOSE_ROWS):
    _, rows, d = v.shape
    return pl.pallas_call(
        _cast_kernel,
        grid=(rows // tr,),
        in_specs=[pl.BlockSpec((1, tr, d), lambda i: (layer, i, 0))],
        out_specs=pl.BlockSpec((tr, d), lambda i: (i, 0)),
        out_shape=jax.ShapeDtypeStruct((rows, d), BF16),
        compiler_params=_cparams(("arbitrary",)),
    )(v)


def _cmp_exchange(v, i, j):
    hi = jnp.maximum(v[i], v[j])
    lo = jnp.minimum(v[i], v[j])
    v[i], v[j] = hi, lo


def _bitonic_sort_desc(v):
    n = len(v)
    k = 2
    while k <= n:
        j = k // 2
        while j >= 1:
            for i in range(n):
                l = i ^ j
                if l > i:
                    if (i & k) == 0:
                        _cmp_exchange(v, i, l)
                    else:
                        _cmp_exchange(v, l, i)
            j //= 2
        k *= 2


def _bitonic_merge_desc(v):
    n = len(v)
    j = n // 2
    while j >= 1:
        for i in range(n):
            l = i ^ j
            if l > i:
                _cmp_exchange(v, i, l)
        j //= 2


def _top16_rows(x):
    tt = x.shape[1]
    v = [x[SUBLANES * i:SUBLANES * (i + 1), :] for i in range(PEER_NKEYS // SUBLANES)]
    _bitonic_sort_desc(v)
    for shift in (4, 2, 1):
        other = [pltpu.roll(a, shift, 0) for a in v]
        v = [jnp.maximum(v[i], other[len(v) - 1 - i]) for i in range(len(v))]
        _bitonic_merge_desc(v)
    return v


def _peer_route_kernel(q_ref, sk_ref, r1_ref, e1_ref, e0_ref, ns_ref, s0_scr, s1_scr, *, tt):
    assert PEER_HEADS == SUBLANES
    sub = lax.broadcasted_iota(I32, (SUBLANES, tt), 0)
    tops = [[], []]
    for h in range(PEER_HEADS):
        for p in range(2):
            lo = (2 * h + p) * (PEER_DKEY // 2)
            qhp = q_ref[:, lo:lo + PEER_DKEY // 2].astype(BF16)
            sc = lax.dot_general(sk_ref[2 * h + p], qhp, _NT, preferred_element_type=F32)
            (s0_scr if p == 0 else s1_scr)[h] = sc
            tops[p].append(_top16_rows(sc))
    a = [sum(jnp.where(sub == h, tops[0][h][k], 0.0) for h in range(PEER_HEADS)) for k in range(PEER_TOPK)]
    b = [sum(jnp.where(sub == h, tops[1][h][k], 0.0) for h in range(PEER_HEADS)) for k in range(PEER_TOPK)]
    cand = [a[i] + b[j] for i in range(PEER_TOPK) for j in range(PEER_TOPK) if (i + 1) * (j + 1) <= PEER_TOPK]
    cand = cand + [jnp.full((SUBLANES, tt), -jnp.inf, F32)] * (64 - len(cand))
    _bitonic_sort_desc(cand)
    tau = cand[PEER_TOPK - 1]
    z = sum(jnp.exp(cand[k] - cand[0]) for k in range(PEER_TOPK))
    half_inv_z = 0.5 / z
    nsel_of_rank = [sum(jnp.where(a[r] + b[k] >= tau, 1.0, 0.0) for k in range(PEER_TOPK)) for r in range(PEER_TOPK)]
    for h in range(PEER_HEADS):
        s0 = s0_scr[h]
        s1 = s1_scr[h]
        nsel = jnp.zeros((PEER_NKEYS, tt), F32)
        rank1 = jnp.full((PEER_NKEYS, tt), float(PEER_TOPK), F32)
        for r in reversed(range(PEER_TOPK)):
            nsel = jnp.where(s0 == a[r][h:h + 1, :], nsel_of_rank[r][h:h + 1, :], nsel)
            rank1 = jnp.where(s1 == b[r][h:h + 1, :], float(r), rank1)
        ns_ref[h] = nsel
        r1_ref[h] = rank1.astype(BF16)
        e0_ref[h] = jnp.exp(s0 - a[0][h:h + 1, :]) * half_inv_z[h:h + 1, :]
        e1_ref[h] = jnp.exp(s1 - b[0][h:h + 1, :]).astype(BF16)


def _peer_route(q, subkeys_bf16, tt=PEER_ROUTE_TOKENS):
    s = q.shape[0]
    big = lambda dt: jax.ShapeDtypeStruct((PEER_HEADS, PEER_NKEYS, s), dt)
    big_spec = pl.BlockSpec((PEER_HEADS, PEER_NKEYS, tt), lambda i: (0, 0, i))
    return pl.pallas_call(
        functools.partial(_peer_route_kernel, tt=tt),
        grid=(s // tt,),
        in_specs=[pl.BlockSpec((tt, PEER_HEADS * PEER_DKEY), lambda i: (i, 0)),
                  pl.BlockSpec((2 * PEER_HEADS, PEER_NKEYS, PEER_DKEY // 2), lambda i: (0, 0, 0))],
        out_specs=[big_spec, big_spec, big_spec, big_spec],
        out_shape=[big(BF16), big(BF16), big(F32), big(F32)],
        scratch_shapes=[pltpu.VMEM((PEER_HEADS, PEER_NKEYS, tt), F32),
                        pltpu.VMEM((PEER_HEADS, PEER_NKEYS, tt), F32)],
        compiler_params=_cparams(("arbitrary",)),
    )(q, subkeys_bf16)


def _peer_dense_kernel(ht_ref, u_ref, vt_ref, r1_ref, e1_ref, e0_ref, ns_ref,
                       x_ref, g_ref, gf_ref, o_ref, acc_scr, p_scr, *, rows_per_step, final_norm):
    k = pl.program_id(1)

    @pl.when(k == 0)
    def _():
        acc_scr[...] = jnp.zeros(acc_scr.shape, F32)

    tt = ht_ref.shape[1]
    pack_rows = BF16_PACK_ROWS
    zero = jnp.zeros((pack_rows, tt), BF16)

    def activations(c):
        return jnp.dot(u_ref[c * MXU_DEPTH:(c + 1) * MXU_DEPTH, :], ht_ref[...], preferred_element_type=F32)

    def expert_rows(ii, a):
        i = k * rows_per_step + ii
        def bcast(ref, h):
            half = jnp.broadcast_to(ref[h, pl.ds(i, 1), :], (pack_rows // 2, tt))
            return jnp.concatenate([half, half], axis=0).astype(BF16)
        e0_rows = [bcast(e0_ref, h) for h in range(PEER_HEADS)]
        ns_rows = [bcast(ns_ref, h) for h in range(PEER_HEADS)]
        gelu2 = (a * (1.0 + lax.erf(a * math.sqrt(0.5)))).astype(BF16)
        for g in range(PEER_NKEYS // pack_rows):
            rows = slice(g * pack_rows, (g + 1) * pack_rows)
            wgt = None
            for h in range(PEER_HEADS):
                term = jnp.where(r1_ref[h, rows, :] < ns_rows[h], e1_ref[h, rows, :], zero) * e0_rows[h]
                wgt = term if wgt is None else wgt + term
            p_scr[ii * PEER_NKEYS + g * pack_rows:ii * PEER_NKEYS + (g + 1) * pack_rows, :] = wgt * gelu2[rows, :]

    depth_rows = MXU_DEPTH // PEER_NKEYS
    n_chunks = rows_per_step // depth_rows
    act = activations(0)
    for c in range(n_chunks):
        act_next = activations(c + 1) if c + 1 < n_chunks else None
        for j in range(depth_rows):
            expert_rows(c * depth_rows + j, act[j * PEER_NKEYS:(j + 1) * PEER_NKEYS, :])
        act = act_next
    acc_scr[...] += jnp.dot(vt_ref[...], p_scr[...], preferred_element_type=F32)

    @pl.when(k == pl.num_programs(1) - 1)
    def _():
        y = x_ref[...] + g_ref[...] * acc_scr[...].T
        if final_norm:
            y = y * lax.rsqrt(jnp.mean(y * y, axis=-1, keepdims=True) + EPS) * gf_ref[...]
        o_ref[...] = y


def _peer_dense(ht_bf16, u_bf16, vt_bf16, rank1, e1, e0, nsel, x, gate, g_final, final_norm,
                tt=PEER_DENSE_TOKENS, rows_per_step=PEER_DENSE_KEY_ROWS):
    s, d = x.shape
    n_exp = u_bf16.shape[0]
    et = rows_per_step * PEER_NKEYS
    tok = lambda j, k: (0, 0, j)
    big_spec = pl.BlockSpec((PEER_HEADS, PEER_NKEYS, tt), tok)
    return pl.pallas_call(
        functools.partial(_peer_dense_kernel, rows_per_step=rows_per_step, final_norm=final_norm),
        grid=(s // tt, n_exp // et),
        in_specs=[pl.BlockSpec((d, tt), lambda j, k: (0, j)),
                  pl.BlockSpec((et, d), lambda j, k: (k, 0)),
                  pl.BlockSpec((d, et), lambda j, k: (0, k)),
                  big_spec, big_spec, big_spec, big_spec,
                  pl.BlockSpec((tt, d), lambda j, k: (j, 0)),
                  pl.BlockSpec((1, d), lambda j, k: (0, 0)),
                  pl.BlockSpec((1, d), lambda j, k: (0, 0))],
        out_specs=pl.BlockSpec((tt, d), lambda j, k: (j, 0)),
        out_shape=jax.ShapeDtypeStruct((s, d), F32),
        scratch_shapes=[pltpu.VMEM((d, tt), F32), pltpu.VMEM((et, tt), BF16)],
        compiler_params=_cparams(("arbitrary", "arbitrary")),
    )(ht_bf16, u_bf16, vt_bf16, rank1, e1, e0, nsel, x, gate.reshape(1, d), g_final.reshape(1, d))


def _heads_first(t, nh):
    s = t.shape[0]
    return jnp.transpose(t.reshape(s, nh, HEAD_DIM), (1, 0, 2))


def _query_col_scale(n_pad):
    cs = jnp.ones((n_pad,), F32)
    q_scale = HEAD_DIM ** -0.5 * LOG2E
    dsa_q = 3 * MOBA_WIDTH
    return cs.at[:MOBA_WIDTH].set(q_scale).at[dsa_q:dsa_q + DSA_WIDTH].set(q_scale)


def _mixer(proj_bf16, moba_qk_f32, idx_f32, mvt, dvt, w_out_bf16, bias_tiles, x, gate):
    s = proj_bf16.shape[0]
    parts = []
    off = 0
    for size in PROJ_SIZES[:7]:
        parts.append(proj_bf16[:, off:off + size])
        off += size
    mq_s, mk_b, _, dq_s, dk_b, _, iq_b = parts
    mq, mk = moba_qk_f32[:, :MOBA_WIDTH], moba_qk_f32[:, MOBA_WIDTH:]
    ik, iw = idx_f32[:, :IDX_DIM], idx_f32[:, IDX_DIM:IDX_DIM + IDX_HEADS]

    block_sel = _moba_select(_heads_first(mq, MOBA_HEADS), _moba_kmean(_heads_first(mk, MOBA_HEADS)))
    blk = jnp.arange(s, dtype=I32) // MOBA_BLOCK
    onehot = (blk[:, None] == jnp.arange(MOBA_MAX_BLOCKS, dtype=I32)[None, :]).astype(BF16)
    q_aug = jnp.concatenate([_heads_first(mq_s, MOBA_HEADS), jnp.transpose(block_sel, (0, 2, 1)).astype(BF16)],
                            axis=-1)
    k_aug = jnp.concatenate([_heads_first(mk_b, MOBA_HEADS),
                             jnp.broadcast_to(onehot, (MOBA_HEADS, s, MOBA_MAX_BLOCKS))], axis=-1)
    y_moba = _attention(q_aug, k_aug, mvt.reshape(MOBA_HEADS, HEAD_DIM, s), bias_tiles, 0)

    ik_n = ik * lax.rsqrt(jnp.mean(ik * ik, axis=-1, keepdims=True) + EPS)
    topk = min(DSA_TOPK_MAX, s // 4)
    mask_bias = _dsa_select(iq_b, iw * IDX_HEADS ** -0.5, ik_n.astype(BF16), topk)
    y_dsa = _attention(_heads_first(dq_s, DSA_HEADS), _heads_first(dk_b, DSA_HEADS),
                       dvt.reshape(DSA_HEADS, HEAD_DIM, s), bias_tiles, 1, mask_bias=mask_bias)

    return _out_proj(y_moba.reshape(MOBA_WIDTH, s), y_dsa.reshape(DSA_WIDTH, s), w_out_bf16, x, gate)


def kernel(x, c, w_ada, b_ada, norm_attn, norm_ffn, w_in, w_out, rel_bias,
           peer_wq, peer_subkeys, peer_u, peer_v, norm_final):
    batch, s, d = x.shape
    assert batch == 1 and s % (ATT_KEY_TILES_PER_STEP * ATT_TILE) == 0
    depth = w_ada.shape[0]
    x = x[0]
    mod = _ada_mod(c, w_ada, b_ada)
    bias_tiles = _bias_tiles(rel_bias, ATT_TILE)
    n_pad = -(-PROJ_WIDTH // LANES) * LANES
    for l in range(depth):
        sh1, sc1, g1, sh2, sc2, g2 = jnp.split(mod[l], 6)
        w_in_l = jnp.pad(w_in[l], ((0, 0), (0, n_pad - PROJ_WIDTH))).astype(BF16)
        proj_bf16, moba_qk, idx_cols, mvt, dvt = _mixer_proj(x, norm_attn[l], sh1, sc1, w_in_l,
                                                             _query_col_scale(n_pad))
        x = _mixer(proj_bf16, moba_qk, idx_cols, mvt, dvt, w_out[l].astype(BF16), bias_tiles, x, g1)
        h2, q = _norm_mod_matmul(x, norm_ffn[l], sh2, sc2, peer_wq[l].astype(BF16))
        subkeys = peer_subkeys[l].reshape(2 * PEER_HEADS, PEER_NKEYS, PEER_DKEY // 2).astype(BF16)
        rank1, e1, e0, nsel = _peer_route(q, subkeys)
        x = _peer_dense(_transpose_cast(h2), _cast_bf16(peer_u, l), _transpose_cast(peer_v, l), rank1, e1, e0, nsel,
                        x, g2, norm_final, final_norm=(l == depth - 1))
    return x[None]
```
